```python
import math
import jax, jax.numpy as jnp
from jax import lax
import numpy as np

D_MODEL = 2048
BATCH = 16
SEQ = 2048
DEPTH = 1

MEM_LEN = 256
D_FF = 5632
POOL_GROUPS = 4
POOL_GROUP_DIM = 128
POOL_WIDTH = POOL_GROUPS * POOL_GROUP_DIM
POOL_WINDOWS = (2, 4, 8, 16)
FOX_HEADS = 16
FOX_HEAD_DIM = 64
FOX_WIDTH = FOX_HEADS * FOX_HEAD_DIM
MEM_HEADS = 4
MEM_HEAD_DIM = 128
MEM_WIDTH = MEM_HEADS * MEM_HEAD_DIM
N_BRANCHES = 3
GATE_WIDTH = N_BRANCHES * D_MODEL
Q_BLOCK = 128
EPS = 1e-6
IN_SPLITS = (POOL_WIDTH, FOX_WIDTH, FOX_WIDTH, FOX_WIDTH, FOX_HEADS, MEM_WIDTH, GATE_WIDTH)
IN_WIDTH = sum(IN_SPLITS)

kernel_name = "hybrid_pool_fox_memxattn_macaron"


def rmsnorm(x, g):
    xf = x.astype(jnp.float32)
    y = xf * lax.rsqrt(jnp.mean(xf * xf, axis=-1, keepdims=True) + EPS)
    return (y * g.astype(jnp.float32)).astype(x.dtype)


def swiglu_half_ffn(x, norm_g, w_gate_up, w_down):
    h = rmsnorm(x, norm_g)
    gate, up = jnp.split(h @ w_gate_up, 2, axis=-1)
    return 0.5 * ((jax.nn.silu(gate) * up) @ w_down)


def causal_window_mean(u, w):
    S = u.shape[1]
    cs = jnp.cumsum(u.astype(jnp.float32), axis=1)
    lagged = jnp.pad(cs, ((0, 0), (w, 0), (0, 0)))[:, :S]
    count = jnp.minimum(jnp.arange(1, S + 1), w).astype(jnp.float32)
    return ((cs - lagged) / count[None, :, None]).astype(u.dtype)


def pool_mixer(u, pool_w, pool_scale):
    B, S, _ = u.shape
    groups = u.reshape(B, S, POOL_GROUPS, POOL_GROUP_DIM)
    pooled = jnp.stack([causal_window_mean(groups[:, :, g], POOL_WINDOWS[g])
                        for g in range(POOL_GROUPS)], axis=2)
    mixed = jnp.einsum('bsgc,gcd->bsgd', pooled - groups, pool_w)
    return mixed.reshape(B, S, POOL_WIDTH) * pool_scale


def forgetting_attention(q, k, v, log_f):
    B, S, H, Dh = q.shape
    c = jnp.cumsum(log_f, axis=1).transpose(0, 2, 1)
    scale = Dh ** -0.5
    outs = []
    for i in range(S // Q_BLOCK):
        q0, q1 = i * Q_BLOCK, (i + 1) * Q_BLOCK
        logits = jnp.einsum('bqhd,bkhd->bhqk', q[:, q0:q1], k[:, :q1]).astype(jnp.float32) * scale
        logits = logits + c[:, :, q0:q1, None] - c[:, :, None, :q1]
        causal = (q0 + jnp.arange(Q_BLOCK))[:, None] >= jnp.arange(q1)[None, :]
        logits = jnp.where(causal[None, None], logits, -jnp.inf)
        p = jax.nn.softmax(logits, axis=-1).astype(v.dtype)
        outs.append(jnp.einsum('bhqk,bkhd->bqhd', p, v[:, :q1]))
    return jnp.concatenate(outs, axis=1)


def memory_attention(q, k, v):
    scale = q.shape[-1] ** -0.5
    logits = jnp.einsum('bshd,bmhd->bhsm', q, k).astype(jnp.float32) * scale
    p = jax.nn.softmax(logits, axis=-1).astype(v.dtype)
    return jnp.einsum('bhsm,bmhd->bshd', p, v)


def _fwd_setup_inputs(seed: int = 0) -> dict:
    key = jax.random.key(seed)
    ks = jax.random.split(key, 24)
    nrm = lambda k, shape, fan_in: jax.random.normal(k, shape, jnp.float32) * fan_in ** -0.5
    gain = lambda k, shape: 1.0 + 0.1 * jax.random.normal(k, shape, jnp.float32)
    L = DEPTH
    return {
        "x": jax.random.normal(ks[0], (BATCH, SEQ, D_MODEL), jnp.float32),
        "mem": jax.random.normal(ks[1], (BATCH, MEM_LEN, D_MODEL), jnp.float32),
        "ffn1_norm": gain(ks[2], (L, D_MODEL)),
        "ffn1_w_gate_up": nrm(ks[3], (L, D_MODEL, 2 * D_FF), D_MODEL),
        "ffn1_w_down": nrm(ks[4], (L, D_FF, D_MODEL), D_FF),
        "mix_norm": gain(ks[5], (L, D_MODEL)),
        "mem_norm": gain(ks[6], (L, D_MODEL)),
        "w_in": nrm(ks[7], (L, D_MODEL, IN_WIDTH), D_MODEL),
        "b_forget": 2.0 + 0.1 * jax.random.normal(ks[8], (L, FOX_HEADS), jnp.float32),
        "pool_w": nrm(ks[9], (L, POOL_GROUPS, POOL_GROUP_DIM, POOL_GROUP_DIM), POOL_GROUP_DIM),
        "pool_scale": gain(ks[10], (L, POOL_WIDTH)),
        "w_pool_up": nrm(ks[11], (L, POOL_WIDTH, D_MODEL), POOL_WIDTH),
        "fox_q_norm": gain(ks[12], (L, FOX_HEAD_DIM)),
        "fox_k_norm": gain(ks[13], (L, FOX_HEAD_DIM)),
        "w_fox_o": nrm(ks[14], (L, FOX_WIDTH, D_MODEL), FOX_WIDTH),
        "w_mem_kv": nrm(ks[15], (L, D_MODEL, 2 * MEM_WIDTH), D_MODEL),
        "mem_q_norm": gain(ks[16], (L, MEM_HEAD_DIM)),
        "mem_k_norm": gain(ks[17], (L, MEM_HEAD_DIM)),
        "w_mem_o": nrm(ks[18], (L, MEM_WIDTH, D_MODEL), MEM_WIDTH),
        "w_out": nrm(ks[19], (L, D_MODEL, D_MODEL), D_MODEL),
        "ffn2_norm": gain(ks[20], (L, D_MODEL)),
        "ffn2_w_gate_up": nrm(ks[21], (L, D_MODEL, 2 * D_FF), D_MODEL),
        "ffn2_w_down": nrm(ks[22], (L, D_FF, D_MODEL), D_FF),
    }


def _fwd_reference(x, mem, ffn1_norm, ffn1_w_gate_up, ffn1_w_down, mix_norm, mem_norm, w_in,
              b_forget, pool_w, pool_scale, w_pool_up, fox_q_norm, fox_k_norm, w_fox_o,
              w_mem_kv, mem_q_norm, mem_k_norm, w_mem_o, w_out,
              ffn2_norm, ffn2_w_gate_up, ffn2_w_down):
    B, S, _ = x.shape
    M = mem.shape[1]
    split_idx = list(np.cumsum(IN_SPLITS)[:-1])
    for l in range(DEPTH):
        x = x + swiglu_half_ffn(x, ffn1_norm[l], ffn1_w_gate_up[l], ffn1_w_down[l])

        h = rmsnorm(x, mix_norm[l])
        u_pool, q_f, k_f, v_f, f_logit, q_m, gate_logit = jnp.split(h @ w_in[l], split_idx, axis=-1)

        y_pool = pool_mixer(u_pool, pool_w[l], pool_scale[l]) @ w_pool_up[l]

        q_f = rmsnorm(q_f.reshape(B, S, FOX_HEADS, FOX_HEAD_DIM), fox_q_norm[l])
        k_f = rmsnorm(k_f.reshape(B, S, FOX_HEADS, FOX_HEAD_DIM), fox_k_norm[l])
        v_f = v_f.reshape(B, S, FOX_HEADS, FOX_HEAD_DIM)
        log_f = jax.nn.log_sigmoid(f_logit.astype(jnp.float32) + b_forget[l].astype(jnp.float32))
        y_fox = forgetting_attention(q_f, k_f, v_f, log_f).reshape(B, S, FOX_WIDTH) @ w_fox_o[l]

        k_m, v_m = jnp.split(rmsnorm(mem, mem_norm[l]) @ w_mem_kv[l], 2, axis=-1)
        q_m = rmsnorm(q_m.reshape(B, S, MEM_HEADS, MEM_HEAD_DIM), mem_q_norm[l])
        k_m = rmsnorm(k_m.reshape(B, M, MEM_HEADS, MEM_HEAD_DIM), mem_k_norm[l])
        v_m = v_m.reshape(B, M, MEM_HEADS, MEM_HEAD_DIM)
        y_mem = memory_attention(q_m, k_m, v_m).reshape(B, S, MEM_WIDTH) @ w_mem_o[l]

        g_pool, g_fox, g_mem = jnp.split(jax.nn.sigmoid(gate_logit), N_BRANCHES, axis=-1)
        merged = g_pool * y_pool + g_fox * y_fox + g_mem * y_mem
        x = x + merged @ w_out[l]

        x = x + swiglu_half_ffn(x, ffn2_norm[l], ffn2_w_gate_up[l], ffn2_w_down[l])
    return x


import jax as _jax
import jax.numpy as _jnp

TWIN_FORMAT = 'train_step'
FWD_PARAMS = ['x', 'mem', 'ffn1_norm', 'ffn1_w_gate_up', 'ffn1_w_down', 'mix_norm', 'mem_norm', 'w_in', 'b_forget', 'pool_w', 'pool_scale', 'w_pool_up', 'fox_q_norm', 'fox_k_norm', 'w_fox_o', 'w_mem_kv', 'mem_q_norm', 'mem_k_norm', 'w_mem_o', 'w_out', 'ffn2_norm', 'ffn2_w_gate_up', 'ffn2_w_down']
TWIN_WEIGHTS = ['ffn1_norm', 'ffn1_w_gate_up', 'ffn1_w_down', 'mix_norm', 'mem_norm', 'w_in', 'b_forget', 'pool_w', 'pool_scale', 'w_pool_up', 'fox_q_norm', 'fox_k_norm', 'w_fox_o', 'w_mem_kv', 'mem_q_norm', 'mem_k_norm', 'w_mem_o', 'w_out', 'ffn2_norm', 'ffn2_w_gate_up', 'ffn2_w_down']
TWIN_DIFF_INPUT = 'x'
TWIN_INPUTS = ['x', 'mem', 'ffn1_norm', 'ffn1_w_gate_up', 'ffn1_w_down', 'mix_norm', 'mem_norm', 'w_in', 'b_forget', 'pool_w', 'pool_scale', 'w_pool_up', 'fox_q_norm', 'fox_k_norm', 'w_fox_o', 'w_mem_kv', 'mem_q_norm', 'mem_k_norm', 'w_mem_o', 'w_out', 'ffn2_norm', 'ffn2_w_gate_up', 'ffn2_w_down', 'loss_target', 'm_ffn1_norm', 'm_ffn1_w_gate_up', 'm_ffn1_w_down', 'm_mix_norm', 'm_mem_norm', 'm_w_in', 'm_b_forget', 'm_pool_w', 'm_pool_scale', 'm_w_pool_up', 'm_fox_q_norm', 'm_fox_k_norm', 'm_w_fox_o', 'm_w_mem_kv', 'm_mem_q_norm', 'm_mem_k_norm', 'm_w_mem_o', 'm_w_out', 'm_ffn2_norm', 'm_ffn2_w_gate_up', 'm_ffn2_w_down', 'v_ffn1_norm', 'v_ffn1_w_gate_up', 'v_ffn1_w_down', 'v_mix_norm', 'v_mem_norm', 'v_w_in', 'v_b_forget', 'v_pool_w', 'v_pool_scale', 'v_w_pool_up', 'v_fox_q_norm', 'v_fox_k_norm', 'v_w_fox_o', 'v_w_mem_kv', 'v_mem_q_norm', 'v_mem_k_norm', 'v_w_mem_o', 'v_w_out', 'v_ffn2_norm', 'v_ffn2_w_gate_up', 'v_ffn2_w_down']
TWIN_OUTPUTS = ['loss', 'grad_x', 'grad_ffn1_norm', 'grad_ffn1_w_gate_up', 'grad_ffn1_w_down', 'grad_mix_norm', 'grad_mem_norm', 'grad_w_in', 'grad_b_forget', 'grad_pool_w', 'grad_pool_scale', 'grad_w_pool_up', 'grad_fox_q_norm', 'grad_fox_k_norm', 'grad_w_fox_o', 'grad_w_mem_kv', 'grad_mem_q_norm', 'grad_mem_k_norm', 'grad_w_mem_o', 'grad_w_out', 'grad_ffn2_norm', 'grad_ffn2_w_gate_up', 'grad_ffn2_w_down', 'delta_ffn1_norm', 'delta_ffn1_w_gate_up', 'delta_ffn1_w_down', 'delta_mix_norm', 'delta_mem_norm', 'delta_w_in', 'delta_b_forget', 'delta_pool_w', 'delta_pool_scale', 'delta_w_pool_up', 'delta_fox_q_norm', 'delta_fox_k_norm', 'delta_w_fox_o', 'delta_w_mem_kv', 'delta_mem_q_norm', 'delta_mem_k_norm', 'delta_w_mem_o', 'delta_w_out', 'delta_ffn2_norm', 'delta_ffn2_w_gate_up', 'delta_ffn2_w_down', 'new_m_ffn1_norm', 'new_m_ffn1_w_gate_up', 'new_m_ffn1_w_down', 'new_m_mix_norm', 'new_m_mem_norm', 'new_m_w_in', 'new_m_b_forget', 'new_m_pool_w', 'new_m_pool_scale', 'new_m_w_pool_up', 'new_m_fox_q_norm', 'new_m_fox_k_norm', 'new_m_w_fox_o', 'new_m_w_mem_kv', 'new_m_mem_q_norm', 'new_m_mem_k_norm', 'new_m_w_mem_o', 'new_m_w_out', 'new_m_ffn2_norm', 'new_m_ffn2_w_gate_up', 'new_m_ffn2_w_down', 'new_v_ffn1_norm', 'new_v_ffn1_w_gate_up', 'new_v_ffn1_w_down', 'new_v_mix_norm', 'new_v_mem_norm', 'new_v_w_in', 'new_v_b_forget', 'new_v_pool_w', 'new_v_pool_scale', 'new_v_w_pool_up', 'new_v_fox_q_norm', 'new_v_fox_k_norm', 'new_v_w_fox_o', 'new_v_w_mem_kv', 'new_v_mem_q_norm', 'new_v_mem_k_norm', 'new_v_w_mem_o', 'new_v_w_out', 'new_v_ffn2_norm', 'new_v_ffn2_w_gate_up', 'new_v_ffn2_w_down']
TWIN_LEAF_KINDS = {'loss': 'loss', 'grad_x': 'grad_x', 'grad_ffn1_norm': 'grad_w', 'grad_ffn1_w_gate_up': 'grad_w', 'grad_ffn1_w_down': 'grad_w', 'grad_mix_norm': 'grad_w', 'grad_mem_norm': 'grad_w', 'grad_w_in': 'grad_w', 'grad_b_forget': 'grad_w', 'grad_pool_w': 'grad_w', 'grad_pool_scale': 'grad_w', 'grad_w_pool_up': 'grad_w', 'grad_fox_q_norm': 'grad_w', 'grad_fox_k_norm': 'grad_w', 'grad_w_fox_o': 'grad_w', 'grad_w_mem_kv': 'grad_w', 'grad_mem_q_norm': 'grad_w', 'grad_mem_k_norm': 'grad_w', 'grad_w_mem_o': 'grad_w', 'grad_w_out': 'grad_w', 'grad_ffn2_norm': 'grad_w', 'grad_ffn2_w_gate_up': 'grad_w', 'grad_ffn2_w_down': 'grad_w', 'delta_ffn1_norm': 'delta_w', 'delta_ffn1_w_gate_up': 'delta_w', 'delta_ffn1_w_down': 'delta_w', 'delta_mix_norm': 'delta_w', 'delta_mem_norm': 'delta_w', 'delta_w_in': 'delta_w', 'delta_b_forget': 'delta_w', 'delta_pool_w': 'delta_w', 'delta_pool_scale': 'delta_w', 'delta_w_pool_up': 'delta_w', 'delta_fox_q_norm': 'delta_w', 'delta_fox_k_norm': 'delta_w', 'delta_w_fox_o': 'delta_w', 'delta_w_mem_kv': 'delta_w', 'delta_mem_q_norm': 'delta_w', 'delta_mem_k_norm': 'delta_w', 'delta_w_mem_o': 'delta_w', 'delta_w_out': 'delta_w', 'delta_ffn2_norm': 'delta_w', 'delta_ffn2_w_gate_up': 'delta_w', 'delta_ffn2_w_down': 'delta_w', 'new_m_ffn1_norm': 'new_m', 'new_m_ffn1_w_gate_up': 'new_m', 'new_m_ffn1_w_down': 'new_m', 'new_m_mix_norm': 'new_m', 'new_m_mem_norm': 'new_m', 'new_m_w_in': 'new_m', 'new_m_b_forget': 'new_m', 'new_m_pool_w': 'new_m', 'new_m_pool_scale': 'new_m', 'new_m_w_pool_up': 'new_m', 'new_m_fox_q_norm': 'new_m', 'new_m_fox_k_norm': 'new_m', 'new_m_w_fox_o': 'new_m', 'new_m_w_mem_kv': 'new_m', 'new_m_mem_q_norm': 'new_m', 'new_m_mem_k_norm': 'new_m', 'new_m_w_mem_o': 'new_m', 'new_m_w_out': 'new_m', 'new_m_ffn2_norm': 'new_m', 'new_m_ffn2_w_gate_up': 'new_m', 'new_m_ffn2_w_down': 'new_m', 'new_v_ffn1_norm': 'new_v', 'new_v_ffn1_w_gate_up': 'new_v', 'new_v_ffn1_w_down': 'new_v', 'new_v_mix_norm': 'new_v', 'new_v_mem_norm': 'new_v', 'new_v_w_in': 'new_v', 'new_v_b_forget': 'new_v', 'new_v_pool_w': 'new_v', 'new_v_pool_scale': 'new_v', 'new_v_w_pool_up': 'new_v', 'new_v_fox_q_norm': 'new_v', 'new_v_fox_k_norm': 'new_v', 'new_v_w_fox_o': 'new_v', 'new_v_w_mem_kv': 'new_v', 'new_v_mem_q_norm': 'new_v', 'new_v_mem_k_norm': 'new_v', 'new_v_w_mem_o': 'new_v', 'new_v_w_out': 'new_v', 'new_v_ffn2_norm': 'new_v', 'new_v_ffn2_w_gate_up': 'new_v', 'new_v_ffn2_w_down': 'new_v'}


def _forward(args):
    return _fwd_reference(*[args[k] for k in FWD_PARAMS])


def _output_shape():
    out = _jax.eval_shape(lambda: _forward(_fwd_setup_inputs(0)))
    return out.shape, out.dtype

N_MICROBATCH = 1
ADAM_LR = 0.001
ADAM_B1 = 0.9
ADAM_B2 = 0.999
ADAM_EPS = 1e-08
ADAM_WD = 0.01
ADAM_STEP = 10
PER_EXAMPLE_BATCH_AXIS = {'x': 0, 'mem': 0, 'loss_target': 0}
SHARED_INPUTS = []
_WEIGHT_DTYPES = {'ffn1_norm': _jnp.float32, 'ffn1_w_gate_up': _jnp.float32, 'ffn1_w_down': _jnp.float32, 'mix_norm': _jnp.float32, 'mem_norm': _jnp.float32, 'w_in': _jnp.float32, 'b_forget': _jnp.float32, 'pool_w': _jnp.float32, 'pool_scale': _jnp.float32, 'w_pool_up': _jnp.float32, 'fox_q_norm': _jnp.float32, 'fox_k_norm': _jnp.float32, 'w_fox_o': _jnp.float32, 'w_mem_kv': _jnp.float32, 'mem_q_norm': _jnp.float32, 'mem_k_norm': _jnp.float32, 'w_mem_o': _jnp.float32, 'w_out': _jnp.float32, 'ffn2_norm': _jnp.float32, 'ffn2_w_gate_up': _jnp.float32, 'ffn2_w_down': _jnp.float32}
MOMENT_SCALE = {'ffn1_norm': 3.074473e+00, 'ffn1_w_gate_up': 3.900988e-02, 'ffn1_w_down': 6.563474e-02, 'mix_norm': 5.093631e+00, 'mem_norm': 4.383903e-02, 'w_in': 1.003495e-01, 'b_forget': 3.120285e+01, 'pool_w': 1.317001e+00, 'pool_scale': 1.513970e+01, 'w_pool_up': 2.734049e-01, 'fox_q_norm': 1.026171e+01, 'fox_k_norm': 1.007320e+01, 'w_fox_o': 4.295142e-02, 'w_mem_kv': 2.904274e-02, 'mem_q_norm': 7.052072e-01, 'mem_k_norm': 7.149200e-01, 'w_mem_o': 1.406559e-02, 'w_out': 2.035126e-01, 'ffn2_norm': 3.197083e+00, 'ffn2_w_gate_up': 3.113529e-02, 'ffn2_w_down': 5.235428e-02}


def _to_microbatches(a, axis):
    t = _jnp.moveaxis(a, axis, 0)
    t = t.reshape((N_MICROBATCH, t.shape[0] // N_MICROBATCH) + t.shape[1:])
    return _jnp.moveaxis(t, 1, axis + 1)


def setup_inputs(seed: int = 0) -> dict:
    inp = _fwd_setup_inputs(seed)
    key = _jax.random.fold_in(_jax.random.key(seed), 7919)
    shape, _ = _output_shape()
    out = dict(inp)
    out["loss_target"] = _jax.random.normal(_jax.random.fold_in(key, 0), shape, _jnp.float32)
    for i, name in enumerate(TWIN_WEIGHTS):
        w = inp[name].astype(_jnp.float32)
        if MOMENT_SCALE is None:
            s = _jnp.sqrt(_jnp.mean(_jnp.square(w)) + 1e-30)
        else:
            s = MOMENT_SCALE[name]
        km, kv = _jax.random.split(_jax.random.fold_in(key, i + 1))
        out[name] = w
        out["m_" + name] = s * _jax.random.normal(km, w.shape, _jnp.float32)
        out["v_" + name] = (s * s) * _jax.random.uniform(kv, w.shape, _jnp.float32, 0.5, 1.5)
    if N_MICROBATCH > 1:
        for name, axis in PER_EXAMPLE_BATCH_AXIS.items():
            out[name] = _to_microbatches(out[name], axis)
    return {'x': out['x'], 'mem': out['mem'], 'ffn1_norm': out['ffn1_norm'], 'ffn1_w_gate_up': out['ffn1_w_gate_up'], 'ffn1_w_down': out['ffn1_w_down'], 'mix_norm': out['mix_norm'], 'mem_norm': out['mem_norm'], 'w_in': out['w_in'], 'b_forget': out['b_forget'], 'pool_w': out['pool_w'], 'pool_scale': out['pool_scale'], 'w_pool_up': out['w_pool_up'], 'fox_q_norm': out['fox_q_norm'], 'fox_k_norm': out['fox_k_norm'], 'w_fox_o': out['w_fox_o'], 'w_mem_kv': out['w_mem_kv'], 'mem_q_norm': out['mem_q_norm'], 'mem_k_norm': out['mem_k_norm'], 'w_mem_o': out['w_mem_o'], 'w_out': out['w_out'], 'ffn2_norm': out['ffn2_norm'], 'ffn2_w_gate_up': out['ffn2_w_gate_up'], 'ffn2_w_down': out['ffn2_w_down'], 'loss_target': out['loss_target'], 'm_ffn1_norm': out['m_ffn1_norm'], 'm_ffn1_w_gate_up': out['m_ffn1_w_gate_up'], 'm_ffn1_w_down': out['m_ffn1_w_down'], 'm_mix_norm': out['m_mix_norm'], 'm_mem_norm': out['m_mem_norm'], 'm_w_in': out['m_w_in'], 'm_b_forget': out['m_b_forget'], 'm_pool_w': out['m_pool_w'], 'm_pool_scale': out['m_pool_scale'], 'm_w_pool_up': out['m_w_pool_up'], 'm_fox_q_norm': out['m_fox_q_norm'], 'm_fox_k_norm': out['m_fox_k_norm'], 'm_w_fox_o': out['m_w_fox_o'], 'm_w_mem_kv': out['m_w_mem_kv'], 'm_mem_q_norm': out['m_mem_q_norm'], 'm_mem_k_norm': out['m_mem_k_norm'], 'm_w_mem_o': out['m_w_mem_o'], 'm_w_out': out['m_w_out'], 'm_ffn2_norm': out['m_ffn2_norm'], 'm_ffn2_w_gate_up': out['m_ffn2_w_gate_up'], 'm_ffn2_w_down': out['m_ffn2_w_down'], 'v_ffn1_norm': out['v_ffn1_norm'], 'v_ffn1_w_gate_up': out['v_ffn1_w_gate_up'], 'v_ffn1_w_down': out['v_ffn1_w_down'], 'v_mix_norm': out['v_mix_norm'], 'v_mem_norm': out['v_mem_norm'], 'v_w_in': out['v_w_in'], 'v_b_forget': out['v_b_forget'], 'v_pool_w': out['v_pool_w'], 'v_pool_scale': out['v_pool_scale'], 'v_w_pool_up': out['v_w_pool_up'], 'v_fox_q_norm': out['v_fox_q_norm'], 'v_fox_k_norm': out['v_fox_k_norm'], 'v_w_fox_o': out['v_w_fox_o'], 'v_w_mem_kv': out['v_w_mem_kv'], 'v_mem_q_norm': out['v_mem_q_norm'], 'v_mem_k_norm': out['v_mem_k_norm'], 'v_w_mem_o': out['v_w_mem_o'], 'v_w_out': out['v_w_out'], 'v_ffn2_norm': out['v_ffn2_norm'], 'v_ffn2_w_gate_up': out['v_ffn2_w_gate_up'], 'v_ffn2_w_down': out['v_ffn2_w_down']}


def _loss(weights, diff, rest, loss_target):
    with _jax.named_scope("forward"):
        args = {**rest, TWIN_DIFF_INPUT: diff, **{k: w.astype(_WEIGHT_DTYPES[k]) for k, w in weights.items()}}
        y = _forward(args)
    with _jax.named_scope("loss_head"):
        err = _jnp.square(y.astype(_jnp.float32) - loss_target)
        return 0.5 * _jnp.sum(_jnp.mean(err, axis=-1)) if err.ndim else 0.5 * err


def _adamw(w, g, m, v):
    m = ADAM_B1 * m + (1.0 - ADAM_B1) * g
    v = ADAM_B2 * v + (1.0 - ADAM_B2) * _jnp.square(g)
    m_hat = m / (1.0 - ADAM_B1 ** ADAM_STEP)
    v_hat = v / (1.0 - ADAM_B2 ** ADAM_STEP)
    delta = -ADAM_LR * (m_hat / (_jnp.sqrt(v_hat) + ADAM_EPS) + ADAM_WD * w)
    return delta, m, v


def reference(x, mem, ffn1_norm, ffn1_w_gate_up, ffn1_w_down, mix_norm, mem_norm, w_in, b_forget, pool_w, pool_scale, w_pool_up, fox_q_norm, fox_k_norm, w_fox_o, w_mem_kv, mem_q_norm, mem_k_norm, w_mem_o, w_out, ffn2_norm, ffn2_w_gate_up, ffn2_w_down, loss_target, m_ffn1_norm, m_ffn1_w_gate_up, m_ffn1_w_down, m_mix_norm, m_mem_norm, m_w_in, m_b_forget, m_pool_w, m_pool_scale, m_w_pool_up, m_fox_q_norm, m_fox_k_norm, m_w_fox_o, m_w_mem_kv, m_mem_q_norm, m_mem_k_norm, m_w_mem_o, m_w_out, m_ffn2_norm, m_ffn2_w_gate_up, m_ffn2_w_down, v_ffn1_norm, v_ffn1_w_gate_up, v_ffn1_w_down, v_mix_norm, v_mem_norm, v_w_in, v_b_forget, v_pool_w, v_pool_scale, v_w_pool_up, v_fox_q_norm, v_fox_k_norm, v_w_fox_o, v_w_mem_kv, v_mem_q_norm, v_mem_k_norm, v_w_mem_o, v_w_out, v_ffn2_norm, v_ffn2_w_gate_up, v_ffn2_w_down):
    given = dict(x=x, mem=mem, ffn1_norm=ffn1_norm, ffn1_w_gate_up=ffn1_w_gate_up, ffn1_w_down=ffn1_w_down, mix_norm=mix_norm, mem_norm=mem_norm, w_in=w_in, b_forget=b_forget, pool_w=pool_w, pool_scale=pool_scale, w_pool_up=w_pool_up, fox_q_norm=fox_q_norm, fox_k_norm=fox_k_norm, w_fox_o=w_fox_o, w_mem_kv=w_mem_kv, mem_q_norm=mem_q_norm, mem_k_norm=mem_k_norm, w_mem_o=w_mem_o, w_out=w_out, ffn2_norm=ffn2_norm, ffn2_w_gate_up=ffn2_w_gate_up, ffn2_w_down=ffn2_w_down, loss_target=loss_target, m_ffn1_norm=m_ffn1_norm, m_ffn1_w_gate_up=m_ffn1_w_gate_up, m_ffn1_w_down=m_ffn1_w_down, m_mix_norm=m_mix_norm, m_mem_norm=m_mem_norm, m_w_in=m_w_in, m_b_forget=m_b_forget, m_pool_w=m_pool_w, m_pool_scale=m_pool_scale, m_w_pool_up=m_w_pool_up, m_fox_q_norm=m_fox_q_norm, m_fox_k_norm=m_fox_k_norm, m_w_fox_o=m_w_fox_o, m_w_mem_kv=m_w_mem_kv, m_mem_q_norm=m_mem_q_norm, m_mem_k_norm=m_mem_k_norm, m_w_mem_o=m_w_mem_o, m_w_out=m_w_out, m_ffn2_norm=m_ffn2_norm, m_ffn2_w_gate_up=m_ffn2_w_gate_up, m_ffn2_w_down=m_ffn2_w_down, v_ffn1_norm=v_ffn1_norm, v_ffn1_w_gate_up=v_ffn1_w_gate_up, v_ffn1_w_down=v_ffn1_w_down, v_mix_norm=v_mix_norm, v_mem_norm=v_mem_norm, v_w_in=v_w_in, v_b_forget=v_b_forget, v_pool_w=v_pool_w, v_pool_scale=v_pool_scale, v_w_pool_up=v_w_pool_up, v_fox_q_norm=v_fox_q_norm, v_fox_k_norm=v_fox_k_norm, v_w_fox_o=v_w_fox_o, v_w_mem_kv=v_w_mem_kv, v_mem_q_norm=v_mem_q_norm, v_mem_k_norm=v_mem_k_norm, v_w_mem_o=v_w_mem_o, v_w_out=v_w_out, v_ffn2_norm=v_ffn2_norm, v_ffn2_w_gate_up=v_ffn2_w_gate_up, v_ffn2_w_down=v_ffn2_w_down)
    weights = {n: given[n] for n in TWIN_WEIGHTS}
    shared = {n: given[n] for n in SHARED_INPUTS}
    per_example = {n: given[n] for n in ['x', 'mem']}
    grad_fn = _jax.value_and_grad(_loss, argnums=(0, 1))

    def one_microbatch(ex, loss_target):
        ex = dict(ex)
        diff = ex.pop(TWIN_DIFF_INPUT)
        return grad_fn(weights, diff, {**shared, **ex}, loss_target)

    if N_MICROBATCH == 1:
        loss, (grad_w, grad_x) = one_microbatch(per_example, given["loss_target"])
    else:
        def body(carry, xs):
            loss_sum, grad_sum = carry
            l_k, (gw_k, gx_k) = one_microbatch(xs[0], xs[1])
            with _jax.named_scope("update"):
                return (loss_sum + l_k, _jax.tree.map(_jnp.add, grad_sum, gw_k)), gx_k

        init = (_jnp.zeros((), _jnp.float32), _jax.tree.map(_jnp.zeros_like, weights))
        (loss, grad_w), grad_x = _jax.lax.scan(body, init, (per_example, given["loss_target"]))
    with _jax.named_scope("update"):
        delta_w, new_m, new_v = {}, {}, {}
        for n in TWIN_WEIGHTS:
            delta_w[n], new_m[n], new_v[n] = _adamw(weights[n], grad_w[n], given["m_" + n], given["v_" + n])
    return (loss, grad_x, *[grad_w[n] for n in TWIN_WEIGHTS], *[delta_w[n] for n in TWIN_WEIGHTS],
            *[new_m[n] for n in TWIN_WEIGHTS], *[new_v[n] for n in TWIN_WEIGHTS])
```

```python
import functools

import jax
import jax.numpy as jnp
from jax import lax
from jax.experimental import pallas as pl
from jax.experimental.pallas import tpu as pltpu

F32 = jnp.float32
BF16 = jnp.bfloat16
MESH = pl.DeviceIdType.MESH

N_DEV = 8
EPS = 1e-6
FOX_HEADS = 16
FOX_HEAD_DIM = 64
FOX_WIDTH = FOX_HEADS * FOX_HEAD_DIM
MEM_HEADS = 4
MEM_HEAD_DIM = 128
MEM_WIDTH = MEM_HEADS * MEM_HEAD_DIM
POOL_GROUPS = 4
POOL_GROUP_DIM = 128
POOL_WIDTH = POOL_GROUPS * POOL_GROUP_DIM
POOL_WINDOWS = (2, 4, 8, 16)
LANES = 128
F_PAD = LANES

ADAM_LR = 0.001
ADAM_B1 = 0.9
ADAM_B2 = 0.999
ADAM_EPS = 1e-08
ADAM_WD = 0.01
ADAM_STEP = 10

VMEM_LIMIT = 56 * 1024 * 1024
NEG = -1e30

ANY = pl.BlockSpec(memory_space=pl.ANY)


def _params(sem=None):
    return pltpu.CompilerParams(dimension_semantics=sem, vmem_limit_bytes=VMEM_LIMIT)


def _tile(dim, cap, align):
    best = None
    t = align
    while t <= min(dim, cap):
        if dim % t == 0:
            best = t
        t += align
    return dim if best is None else best


def _sigmoid(x):
    return 1.0 / (1.0 + jnp.exp(-x))


_DIMS = {"nn": (((1,), (0,)), ((), ())), "nt": (((1,), (1,)), ((), ())), "tn": (((0,), (0,)), ((), ()))}


def _mm(a, b, mode, name, out_dtype=F32, scale=1.0, res=None, caps=None):
    if mode == "nn":
        (m, k), (k2, n) = a.shape, b.shape
    elif mode == "nt":
        (m, k), (n, k2) = a.shape, b.shape
    else:
        (k, m), (k2, n) = a.shape, b.shape
    assert k == k2, (name, a.shape, b.shape)
    cm, cn, ck = caps or ((1024, 512, 2048) if k <= 2048 else (1024, 1024, 1024))
    tm, tn, tk = _tile(m, cm, LANES), _tile(n, cn, LANES), _tile(k, ck, LANES)
    nk = k // tk
    if mode == "nn":
        a_spec = pl.BlockSpec((tm, tk), lambda i, j, kk: (i, kk))
        b_spec = pl.BlockSpec((tk, tn), lambda i, j, kk: (kk, j))
    elif mode == "nt":
        a_spec = pl.BlockSpec((tm, tk), lambda i, j, kk: (i, kk))
        b_spec = pl.BlockSpec((tn, tk), lambda i, j, kk: (j, kk))
    else:
        a_spec = pl.BlockSpec((tk, tm), lambda i, j, kk: (kk, i))
        b_spec = pl.BlockSpec((tk, tn), lambda i, j, kk: (kk, j))
    o_spec = pl.BlockSpec((tm, tn), lambda i, j, kk: (i, j))
    in_specs = [a_spec, b_spec] + ([o_spec] if res is not None else [])
    dims = _DIMS[mode]

    def kern(*refs):
        a_ref, b_ref = refs[0], refs[1]
        res_ref = refs[2] if res is not None else None
        o_ref = refs[3] if res is not None else refs[2]
        prod = lax.dot_general(a_ref[...].astype(BF16), b_ref[...].astype(BF16), dims,
                               preferred_element_type=F32)

        def finish(acc):
            r = acc if scale == 1.0 else scale * acc
            if res_ref is not None:
                r = res_ref[...] + r
            o_ref[...] = r.astype(out_dtype)

        if nk == 1:
            finish(prod)
        else:
            acc_ref = refs[-1]
            kk = pl.program_id(2)

            @pl.when(kk == 0)
            def _():
                acc_ref[...] = prod

            @pl.when(kk > 0)
            def _():
                acc_ref[...] += prod

            @pl.when(kk == nk - 1)
            def _():
                finish(acc_ref[...])

    args = (a, b) + ((res,) if res is not None else ())
    return pl.pallas_call(
        kern, name=name, grid=(m // tm, n // tn, nk), in_specs=in_specs, out_specs=o_spec,
        out_shape=jax.ShapeDtypeStruct((m, n), out_dtype),
        scratch_shapes=[pltpu.VMEM((tm, tn), F32)] if nk > 1 else [],
        compiler_params=_params(("parallel", "parallel", "arbitrary")),
    )(*args)


def _rowwise(body, ins, params, outs, accs, name, tm):
    rows = ins[0][0].shape[0]
    assert rows % tm == 0, (name, rows, tm)
    in_specs = []
    for arr, off, width in ins:
        assert off % width == 0 and arr.shape[0] == rows, (name, arr.shape, off, width)
        in_specs.append(pl.BlockSpec((tm, width), functools.partial(lambda i, c: (i, c), c=off // width)))
    for p in params:
        in_specs.append(pl.BlockSpec(p.shape, functools.partial(lambda i, nd: (0,) * nd, nd=p.ndim)))
    out_specs = [pl.BlockSpec((tm, w), lambda i: (i, 0)) for w, _ in outs]
    out_specs += [pl.BlockSpec(s, functools.partial(lambda i, nd: (0,) * nd, nd=len(s))) for s, _ in accs]
    out_shape = [jax.ShapeDtypeStruct((rows, w), d) for w, d in outs]
    out_shape += [jax.ShapeDtypeStruct(s, d) for s, d in accs]
    n_in, n_par, n_out = len(ins), len(params), len(outs)

    def kern(*refs):
        vals = [r[...] for r in refs[:n_in + n_par]]
        res = body(*vals)
        out_refs = refs[n_in + n_par:]
        for r, v in zip(out_refs[:n_out], res[:n_out]):
            r[...] = v.astype(r.dtype)
        first = pl.program_id(0) == 0
        for r, v in zip(out_refs[n_out:], res[n_out:]):
            @pl.when(first)
            def _(r=r, v=v):
                r[...] = v.astype(r.dtype)

            @pl.when(jnp.logical_not(first))
            def _(r=r, v=v):
                r[...] += v.astype(r.dtype)

    result = pl.pallas_call(
        kern, name=name, grid=(rows // tm,), in_specs=in_specs, out_specs=out_specs, out_shape=out_shape,
        compiler_params=_params(("arbitrary",) if accs else ("parallel",)),
    )(*[a for a, _, _ in ins], *params)
    return result


def _whole(x):
    return (x, 0, x.shape[1])


def _rms_fwd(x, gain, name):
    def body(xv, g):
        r = lax.rsqrt(jnp.mean(xv * xv, axis=-1, keepdims=True) + EPS)
        return ((xv * r) * g,)

    return _rowwise(body, [_whole(x)], [gain], [(x.shape[1], BF16)], [], name, _tile(x.shape[0], 256, 16))[0]


def _rms_bwd(x, gain, dh, dres, name):
    d = x.shape[1]

    def body(*vals):
        if dres is None:
            xv, dhv, g = vals
        else:
            xv, dhv, drv, g = vals
        r = lax.rsqrt(jnp.mean(xv * xv, axis=-1, keepdims=True) + EPS)
        xh = xv * r
        w = dhv * g
        dx = r * (w - xh * jnp.mean(w * xh, axis=-1, keepdims=True))
        if dres is not None:
            dx = drv + dx
        return dx, jnp.sum(dhv * xh, axis=0, keepdims=True)

    ins = [_whole(x), _whole(dh)] + ([_whole(dres)] if dres is not None else [])
    return _rowwise(body, ins, [gain], [(d, F32)], [((1, d), F32)], name, _tile(x.shape[0], 256, 16))


def _swiglu_fwd(gu, name):
    f = gu.shape[1] // 2

    def body(g, u):
        return ((g * _sigmoid(g)) * u,)

    return _rowwise(body, [(gu, 0, f), (gu, f, f)], [], [(f, BF16)], [], name, _tile(gu.shape[0], 128, 16))[0]


def _swiglu_bwd(gu, da, name):
    f = gu.shape[1] // 2

    def body(g, u, dav):
        s = _sigmoid(g)
        dg = dav * u * (s * (1.0 + g * (1.0 - s)))
        du = dav * (g * s)
        return (jnp.concatenate([dg, du], axis=1),)

    return _rowwise(body, [(gu, 0, f), (gu, f, f), _whole(da)], [], [(2 * f, BF16)], [], name,
                    _tile(gu.shape[0], 128, 16))[0]


def _loss_head(y, target, name):
    d = y.shape[1]

    def body(yv, tv):
        e = yv - tv
        part = jnp.sum(jnp.sum(e * e, axis=1, keepdims=True), axis=0, keepdims=True)
        return e / d, jnp.broadcast_to((0.5 / d) * part, (1, LANES))

    return _rowwise(body, [_whole(y), _whole(target)], [], [(d, F32)], [((1, LANES), F32)], name,
                    _tile(y.shape[0], 256, 16))


def _head_mean(v, head_dim):
    cols = []
    lane = lax.broadcasted_iota(jnp.int32, (1, LANES), 1)
    for j in range(v.shape[1] // LANES):
        blk = v[:, j * LANES:(j + 1) * LANES]
        if head_dim == LANES:
            m = jnp.sum(blk, axis=-1, keepdims=True)
            cols.append(jnp.broadcast_to(m, blk.shape))
        else:
            lo = jnp.sum(jnp.where(lane < head_dim, blk, 0.0), axis=-1, keepdims=True)
            hi = jnp.sum(jnp.where(lane >= head_dim, blk, 0.0), axis=-1, keepdims=True)
            cols.append(jnp.where(lane < head_dim, lo, hi))
    return jnp.concatenate(cols, axis=1) / head_dim


def _head_norm(xv, g, head_dim):
    r = lax.rsqrt(_head_mean(xv * xv, head_dim) + EPS)
    return (xv * r) * g


def _head_norm_bwd(xv, g, dy, head_dim):
    r = lax.rsqrt(_head_mean(xv * xv, head_dim) + EPS)
    xh = xv * r
    w = dy * g
    dx = r * (w - xh * _head_mean(w * xh, head_dim))
    return dx, jnp.sum(dy * xh, axis=0, keepdims=True)


def _log_sigmoid(x):
    return jnp.minimum(x, 0.0) - jnp.log(1.0 + jnp.exp(-jnp.abs(x)))


def _shift_rows(v, sh, down):
    n = v.shape[0]
    row = lax.broadcasted_iota(jnp.int32, (n, 1), 0)
    if down:
        return jnp.where(row >= sh, pltpu.roll(v, sh, 0), 0.0)
    return jnp.where(row < n - sh, pltpu.roll(v, n - sh, 0), 0.0)


def _scan_rows(v, down):
    sh = 1
    while sh < v.shape[0]:
        v = v + _shift_rows(v, sh, down)
        sh *= 2
    return v


def _window_sum(v, steps, down):
    for s in range(steps):
        v = v + _shift_rows(v, 2 ** s, down)
    return v


def _seq_call(body, ins, params, outs, accs, name, seq, batch):
    return _rowwise(body, ins, params, outs, accs, name, seq)


def _fox_fwd(qn, kn, vb, cks, batch, seq, name):
    t, width = qn.shape
    pairs = width // LANES
    tq = _tile(seq, 512, LANES)
    nq = seq // tq
    scale = FOX_HEAD_DIM ** -0.5

    def kern(q_ref, k_ref, v_ref, c_ref, o_ref, lse_ref):
        qi = pl.program_id(2)
        lane = lax.broadcasted_iota(jnp.int32, (1, LANES), 1)
        rowi = lax.broadcasted_iota(jnp.int32, (tq, tq), 0)
        coli = lax.broadcasted_iota(jnp.int32, (tq, tq), 1)
        q_all = q_ref[...]
        o_heads, lse_heads = [], []
        for h in range(2):
            hm = (lane < FOX_HEAD_DIM) if h == 0 else (lane >= FOX_HEAD_DIM)
            q = jnp.where(hm, q_all, jnp.zeros_like(q_all))

            def step(j, carry, h=h, q=q):
                m, l, acc = carry
                start = pl.multiple_of(j * tq, tq)
                k = k_ref[pl.ds(start, tq), :]
                v = v_ref[pl.ds(start, tq), :]
                s = lax.dot_general(q, k, _DIMS["nt"], preferred_element_type=F32) * scale
                s = s - c_ref[0, h:h + 1, pl.ds(start, tq)]
                s = jnp.where(qi * tq + rowi >= j * tq + coli, s, NEG)
                m_new = jnp.maximum(m, jnp.max(s, axis=-1, keepdims=True))
                alpha = jnp.exp(m - m_new)
                p = jnp.exp(s - m_new)
                l = alpha * l + jnp.sum(p, axis=-1, keepdims=True)
                acc = alpha * acc + lax.dot_general(p.astype(BF16), v, _DIMS["nn"], preferred_element_type=F32)
                return m_new, l, acc

            init = (jnp.full((tq, 1), NEG, F32), jnp.zeros((tq, 1), F32), jnp.zeros((tq, LANES), F32))
            m, l, acc = lax.fori_loop(0, qi + 1, step, init)
            o_heads.append(acc / l)
            lse_heads.append(jnp.broadcast_to(m + jnp.log(l), (tq, LANES)))
        o_ref[...] = jnp.where(lane < FOX_HEAD_DIM, o_heads[0], o_heads[1])
        lse_ref[...] = jnp.where(lane < FOX_HEAD_DIM, lse_heads[0], lse_heads[1])

    q_spec = pl.BlockSpec((tq, LANES), lambda b, hp, qi: (b * nq + qi, hp))
    kv_spec = pl.BlockSpec((seq, LANES), lambda b, hp, qi: (b, hp))
    c_spec = pl.BlockSpec((1, 2, seq), lambda b, hp, qi: (b * pairs + hp, 0, 0))
    return pl.pallas_call(
        kern, name=name, grid=(batch, pairs, nq), in_specs=[q_spec, kv_spec, kv_spec, c_spec],
        out_specs=[q_spec, q_spec],
        out_shape=[jax.ShapeDtypeStruct((t, width), F32), jax.ShapeDtypeStruct((t, width), F32)],
        compiler_params=_params(("parallel", "parallel", "arbitrary")),
    )(qn, kn, vb, cks)


def _fox_bwd(qn, kn, vb, cks, o, lse, do, batch, seq, name):
    t, width = qn.shape
    pairs = width // LANES
    tk = _tile(seq, 512, LANES)
    nk = seq // tk
    scale = FOX_HEAD_DIM ** -0.5

    def kern(k_ref, v_ref, q_ref, c_ref, o_ref, lse_ref, do_ref, dk_ref, dv_ref, dq_ref, dc_ref, dcq_ref):
        kj = pl.program_id(2)
        lane = lax.broadcasted_iota(jnp.int32, (1, LANES), 1)
        rowi = lax.broadcasted_iota(jnp.int32, (tk, tk), 0)
        coli = lax.broadcasted_iota(jnp.int32, (tk, tk), 1)

        @pl.when(kj == 0)
        def _():
            dq_ref[...] = jnp.zeros_like(dq_ref)
            dcq_ref[...] = jnp.zeros_like(dcq_ref)

        k_all = k_ref[...]
        v_all = v_ref[...]
        kstart = pl.multiple_of(kj * tk, tk)
        dk_heads, dv_heads = [], []
        for h in range(2):
            hm = (lane < FOX_HEAD_DIM) if h == 0 else (lane >= FOX_HEAD_DIM)
            kh = jnp.where(hm, k_all, jnp.zeros_like(k_all))
            vh = jnp.where(hm, v_all, jnp.zeros_like(v_all))
            c_row = c_ref[0, h:h + 1, pl.ds(kstart, tk)]

            def step(qi, carry, h=h, hm=hm, kh=kh, vh=vh, c_row=c_row):
                dk_acc, dv_acc, dc_acc = carry
                start = pl.multiple_of(qi * tk, tk)
                q = q_ref[pl.ds(start, tk), :]
                dov = do_ref[pl.ds(start, tk), :]
                ov = o_ref[pl.ds(start, tk), :]
                lse_col = jnp.max(jnp.where(hm, lse_ref[pl.ds(start, tk), :], NEG), axis=-1, keepdims=True)
                dob = jnp.where(hm, dov, 0.0).astype(BF16)
                dcol = jnp.sum(dob.astype(F32) * ov, axis=-1, keepdims=True)
                s = lax.dot_general(q, kh, _DIMS["nt"], preferred_element_type=F32) * scale - c_row
                p = jnp.where(qi * tk + rowi >= kj * tk + coli, jnp.exp(s - lse_col), 0.0)
                dp = lax.dot_general(dob, vh, _DIMS["nt"], preferred_element_type=F32)
                ds = p * (dp - dcol)
                dsb = (ds * scale).astype(BF16)
                dv_acc = dv_acc + lax.dot_general(p.astype(BF16), dob, _DIMS["tn"], preferred_element_type=F32)
                dk_acc = dk_acc + lax.dot_general(dsb, q, _DIMS["tn"], preferred_element_type=F32)
                dq_part = lax.dot_general(dsb, kh, _DIMS["nn"], preferred_element_type=F32)
                dq_ref[pl.ds(start, tk), :] += dq_part
                dcq_ref[pl.ds(start, tk), :] += jnp.where(hm, jnp.sum(ds, axis=-1, keepdims=True), 0.0)
                dc_acc = dc_acc - jnp.sum(ds, axis=0, keepdims=True)
                return dk_acc, dv_acc, dc_acc

            init = (jnp.zeros((tk, LANES), F32), jnp.zeros((tk, LANES), F32), jnp.zeros((1, tk), F32))
            dk_acc, dv_acc, dc_acc = lax.fori_loop(kj, nk, step, init)
            dk_heads.append(dk_acc)
            dv_heads.append(dv_acc)
            dc_ref[0, h:h + 1, pl.ds(kstart, tk)] = dc_acc
        dk_ref[...] = jnp.where(lane < FOX_HEAD_DIM, dk_heads[0], dk_heads[1])
        dv_ref[...] = jnp.where(lane < FOX_HEAD_DIM, dv_heads[0], dv_heads[1]).astype(BF16)

    kv_spec = pl.BlockSpec((tk, LANES), lambda b, hp, kj: (b * nk + kj, hp))
    full_spec = pl.BlockSpec((seq, LANES), lambda b, hp, kj: (b, hp))
    c_spec = pl.BlockSpec((1, 2, seq), lambda b, hp, kj: (b * pairs + hp, 0, 0))
    return pl.pallas_call(
        kern, name=name, grid=(batch, pairs, nk),
        in_specs=[kv_spec, kv_spec, full_spec, c_spec, full_spec, full_spec, full_spec],
        out_specs=[kv_spec, kv_spec, full_spec, c_spec, full_spec],
        out_shape=[jax.ShapeDtypeStruct((t, width), F32), jax.ShapeDtypeStruct((t, width), BF16),
                   jax.ShapeDtypeStruct((t, width), F32), jax.ShapeDtypeStruct(cks.shape, F32),
                   jax.ShapeDtypeStruct((t, width), F32)],
        compiler_params=_params(("parallel", "parallel", "arbitrary")),
    )(kn, vb, qn, cks, o, lse, do)


def _mem_fwd(qn, kn, vb, batch, seq, mlen, name):
    t, width = qn.shape
    heads = width // LANES
    tq = _tile(seq, 512, LANES)
    nq = seq // tq
    scale = MEM_HEAD_DIM ** -0.5

    def kern(q_ref, k_ref, v_ref, o_ref):
        s = lax.dot_general(q_ref[...], k_ref[...], _DIMS["nt"], preferred_element_type=F32) * scale
        e = jnp.exp(s - jnp.max(s, axis=-1, keepdims=True))
        p = e / jnp.sum(e, axis=-1, keepdims=True)
        o_ref[...] = lax.dot_general(p.astype(BF16), v_ref[...], _DIMS["nn"], preferred_element_type=F32)

    q_spec = pl.BlockSpec((tq, LANES), lambda b, h, qi: (b * nq + qi, h))
    kv_spec = pl.BlockSpec((mlen, LANES), lambda b, h, qi: (b, h))
    return pl.pallas_call(
        kern, name=name, grid=(batch, heads, nq), in_specs=[q_spec, kv_spec, kv_spec], out_specs=q_spec,
        out_shape=jax.ShapeDtypeStruct((t, width), F32),
        compiler_params=_params(("parallel", "parallel", "parallel")),
    )(qn, kn, vb)


def _mem_bwd(qn, kn, vb, do, batch, seq, mlen, name):
    t, width = qn.shape
    heads = width // LANES
    tq = _tile(seq, 512, LANES)
    nq = seq // tq
    scale = MEM_HEAD_DIM ** -0.5

    def kern(q_ref, k_ref, v_ref, do_ref, dq_ref, dk_ref, dv_ref):
        qi = pl.program_id(2)
        q, k, v = q_ref[...], k_ref[...], v_ref[...]
        dob = do_ref[...].astype(BF16)
        s = lax.dot_general(q, k, _DIMS["nt"], preferred_element_type=F32) * scale
        e = jnp.exp(s - jnp.max(s, axis=-1, keepdims=True))
        p = e / jnp.sum(e, axis=-1, keepdims=True)
        dp = lax.dot_general(dob, v, _DIMS["nt"], preferred_element_type=F32)
        ds = p * (dp - jnp.sum(p * dp, axis=-1, keepdims=True))
        dsb = (ds * scale).astype(BF16)
        dq_ref[...] = lax.dot_general(dsb, k, _DIMS["nn"], preferred_element_type=F32)
        dk = lax.dot_general(dsb, q, _DIMS["tn"], preferred_element_type=F32)
        dv = lax.dot_general(p.astype(BF16), dob, _DIMS["tn"], preferred_element_type=F32)

        @pl.when(qi == 0)
        def _():
            dk_ref[...] = dk
            dv_ref[...] = dv

        @pl.when(qi > 0)
        def _():
            dk_ref[...] += dk
            dv_ref[...] += dv

    q_spec = pl.BlockSpec((tq, LANES), lambda b, h, qi: (b * nq + qi, h))
    kv_spec = pl.BlockSpec((mlen, LANES), lambda b, h, qi: (b, h))
    return pl.pallas_call(
        kern, name=name, grid=(batch, heads, nq), in_specs=[q_spec, kv_spec, kv_spec, q_spec],
        out_specs=[q_spec, kv_spec, kv_spec],
        out_shape=[jax.ShapeDtypeStruct((t, width), F32), jax.ShapeDtypeStruct(kn.shape, F32),
                   jax.ShapeDtypeStruct(kn.shape, F32)],
        compiler_params=_params(("parallel", "parallel", "arbitrary")),
    )(qn, kn, vb, do)


def _position():
    return lax.axis_index("x"), lax.axis_index("y"), lax.axis_index("c")


def _all_gather(shards, name, in_vmem=False):
    n = len(shards)

    def body(*refs):
        x_refs, out_refs = refs[:n], refs[n:2 * n]
        send_sems, recv_sems, local_sems = refs[2 * n:]
        x, y, c = _position()
        me, sibling = (x, y, c), (x, y, 1 - c)
        chips = [(1 - x, y), (x, 1 - y), (1 - x, 1 - y)]

        def copy(i, k, block, to, src=None):
            dst = out_refs[i].at[4 * block[0] + 2 * block[1] + block[2]]
            return pltpu.make_async_remote_copy(
                src_ref=dst if src is None else src, dst_ref=dst, send_sem=send_sems.at[7 * i + k],
                recv_sem=recv_sems.at[7 * i + k], device_id=to, device_id_type=MESH)

        mine, first, passed = [], [], []
        for i in range(n):
            cp = pltpu.make_async_copy(x_refs[i], out_refs[i].at[4 * x + 2 * y + c], local_sems.at[i])
            cp.start()
            mine.append(cp)
            first.append(copy(i, 0, me, sibling, src=x_refs[i]))
            first += [copy(i, 1 + j, me, (*chip, c), src=x_refs[i]) for j, chip in enumerate(chips)]
        for cp in first:
            cp.start()
        for j, chip in enumerate(chips):
            for i in range(n):
                copy(i, 1 + j, (*chip, c), me).wait_recv()
                fwd = copy(i, 4 + j, (*chip, c), sibling)
                fwd.start()
                passed.append(fwd)
        for i in range(n):
            copy(i, 0, sibling, me).wait_recv()
        for j, chip in enumerate(chips):
            for i in range(n):
                copy(i, 4 + j, (*chip, 1 - c), me).wait_recv()
        for cp in first + passed:
            cp.wait_send()
        for cp in mine:
            cp.wait()

    spec = pl.BlockSpec(memory_space=pltpu.VMEM) if in_vmem else ANY
    return pl.pallas_call(
        body, name=name, in_specs=[spec] * n, out_specs=[spec] * n,
        out_shape=[jax.ShapeDtypeStruct((N_DEV,) + s.shape, s.dtype) for s in shards],
        scratch_shapes=[pltpu.SemaphoreType.DMA((7 * n,)), pltpu.SemaphoreType.DMA((7 * n,)),
                        pltpu.SemaphoreType.DMA((n,))],
    )(*shards)


def _rs_sibling_exchange(partials, name):
    n = len(partials)

    def body(*refs):
        g_refs, l_refs = refs[:n], refs[n:2 * n]
        send_sems, recv_sems = refs[2 * n:]
        x, y, c = _position()
        copies = []
        for i in range(n):
            for q in range(4):
                copies.append(pltpu.make_async_remote_copy(
                    src_ref=g_refs[i].at[2 * q + (1 - c)], dst_ref=l_refs[i].at[q],
                    send_sem=send_sems.at[4 * i + q], recv_sem=recv_sems.at[4 * i + q],
                    device_id=(x, y, 1 - c), device_id_type=MESH))
        for cp in copies:
            cp.start()
        for cp in copies:
            cp.wait()

    return pl.pallas_call(
        body, name=name, in_specs=[ANY] * n, out_specs=[ANY] * n,
        out_shape=[jax.ShapeDtypeStruct((4,) + g.shape[1:], g.dtype) for g in partials],
        scratch_shapes=[pltpu.SemaphoreType.DMA((4 * n,)), pltpu.SemaphoreType.DMA((4 * n,))],
    )(*partials)


def _rs_chip_exchange(halves, name):
    n = len(halves)

    def body(*refs):
        h_refs, l_refs = refs[:n], refs[n:2 * n]
        send_sems, recv_sems = refs[2 * n:]
        x, y, c = _position()
        chips = [(1 - x, y), (x, 1 - y), (1 - x, 1 - y)]
        copies = []
        for i in range(n):
            for r, (px, py) in enumerate(chips):
                copies.append(pltpu.make_async_remote_copy(
                    src_ref=h_refs[i].at[2 * px + py], dst_ref=l_refs[i].at[r],
                    send_sem=send_sems.at[3 * i + r], recv_sem=recv_sems.at[3 * i + r],
                    device_id=(px, py, c), device_id_type=MESH))
        for cp in copies:
            cp.start()
        for cp in copies:
            cp.wait()

    return pl.pallas_call(
        body, name=name, in_specs=[ANY] * n, out_specs=[ANY] * n,
        out_shape=[jax.ShapeDtypeStruct((3,) + h.shape[1:], h.dtype) for h in halves],
        scratch_shapes=[pltpu.SemaphoreType.DMA((3 * n,)), pltpu.SemaphoreType.DMA((3 * n,))],
    )(*halves)


def _rs_pair_add(partial, landed, core, name):
    _, r, c = partial.shape
    tm = _tile(r, 256, 8)

    def kern(core_ref, g_ref, l_ref, o_ref):
        o_ref[...] = g_ref[...] + l_ref[...]

    grid_spec = pltpu.PrefetchScalarGridSpec(
        num_scalar_prefetch=1, grid=(4, r // tm),
        in_specs=[pl.BlockSpec((1, tm, c), lambda q, i, cr: (2 * q + cr[0], i, 0)),
                  pl.BlockSpec((1, tm, c), lambda q, i, cr: (q, i, 0))],
        out_specs=pl.BlockSpec((1, tm, c), lambda q, i, cr: (q, i, 0)))
    return pl.pallas_call(
        kern, name=name, grid_spec=grid_spec, out_shape=jax.ShapeDtypeStruct((4, r, c), F32),
        compiler_params=_params(("parallel", "parallel")),
    )(core, partial, landed)


def _adam_math(g, w, m, v):
    m = ADAM_B1 * m + (1.0 - ADAM_B1) * g
    v = ADAM_B2 * v + (1.0 - ADAM_B2) * (g * g)
    m_hat = m / (1.0 - ADAM_B1 ** ADAM_STEP)
    v_hat = v / (1.0 - ADAM_B2 ** ADAM_STEP)
    delta = -ADAM_LR * (m_hat / (jnp.sqrt(v_hat) + ADAM_EPS) + ADAM_WD * w)
    return delta, m, v


def _rs_finish_adam(halves, landed, chip, w, m, v, name):
    _, r, c = halves.shape
    tm = _tile(r, 128, 8)

    def kern(chip_ref, h_ref, l_ref, w_ref, m_ref, v_ref, g_out, d_out, m_out, v_out):
        g = ((h_ref[0] + l_ref[0]) + l_ref[1]) + l_ref[2]
        delta, m_new, v_new = _adam_math(g, w_ref[...], m_ref[...], v_ref[...])
        g_out[...] = g
        d_out[...] = delta
        m_out[...] = m_new
        v_out[...] = v_new

    flat = pl.BlockSpec((tm, c), lambda i, cr: (i, 0))
    grid_spec = pltpu.PrefetchScalarGridSpec(
        num_scalar_prefetch=1, grid=(r // tm,),
        in_specs=[pl.BlockSpec((1, tm, c), lambda i, cr: (cr[0], i, 0)),
                  pl.BlockSpec((3, tm, c), lambda i, cr: (0, i, 0)), flat, flat, flat],
        out_specs=[flat] * 4)
    return pl.pallas_call(
        kern, name=name, grid_spec=grid_spec, out_shape=[jax.ShapeDtypeStruct((r, c), F32)] * 4,
        compiler_params=_params(("parallel",)),
    )(chip, halves, landed, w, m, v)


def _allreduce_adam(gathered, w, m, v, name):
    _, r, c = gathered.shape

    def kern(a_ref, w_ref, m_ref, v_ref, g_out, d_out, m_out, v_out):
        g = a_ref[0]
        for j in range(1, N_DEV):
            g = g + a_ref[j]
        delta, m_new, v_new = _adam_math(g, w_ref[...], m_ref[...], v_ref[...])
        g_out[...] = g
        d_out[...] = delta
        m_out[...] = m_new
        v_out[...] = v_new

    return pl.pallas_call(
        kern, name=name, out_shape=[jax.ShapeDtypeStruct((r, c), F32)] * 4,
        compiler_params=_params(),
    )(gathered, w, m, v)


def _cols_from_blocks(g):
    n, k, nb = g.shape
    return jnp.transpose(g, (1, 0, 2)).reshape(k, n * nb)


def _blocks_from_cols(w):
    k, n = w.shape
    return jnp.transpose(w.reshape(k, N_DEV, n // N_DEV), (1, 0, 2))


def _pack_small(parts):
    flat = []
    for p in parts:
        v = p.reshape(-1)
        flat.append(jnp.pad(v, (0, (-v.shape[0]) % (8 * LANES))))
    return jnp.concatenate(flat).reshape(-1, LANES)


def _unpack_small(buf, like):
    out, pos = [], 0
    flat = buf.reshape(-1)
    for p in like:
        size = p.size
        out.append(flat[pos:pos + size].reshape(p.shape))
        pos += size + (-size) % (8 * LANES)
    return out


def kernel(x, mem, ffn1_norm, ffn1_w_gate_up, ffn1_w_down, mix_norm, mem_norm, w_in, b_forget, pool_w, pool_scale, w_pool_up, fox_q_norm, fox_k_norm, w_fox_o, w_mem_kv, mem_q_norm, mem_k_norm, w_mem_o, w_out, ffn2_norm, ffn2_w_gate_up, ffn2_w_down, loss_target, m_ffn1_norm, m_ffn1_w_gate_up, m_ffn1_w_down, m_mix_norm, m_mem_norm, m_w_in, m_b_forget, m_pool_w, m_pool_scale, m_w_pool_up, m_fox_q_norm, m_fox_k_norm, m_w_fox_o, m_w_mem_kv, m_mem_q_norm, m_mem_k_norm, m_w_mem_o, m_w_out, m_ffn2_norm, m_ffn2_w_gate_up, m_ffn2_w_down, v_ffn1_norm, v_ffn1_w_gate_up, v_ffn1_w_down, v_mix_norm, v_mem_norm, v_w_in, v_b_forget, v_pool_w, v_pool_scale, v_w_pool_up, v_fox_q_norm, v_fox_k_norm, v_w_fox_o, v_w_mem_kv, v_mem_q_norm, v_mem_k_norm, v_w_mem_o, v_w_out, v_ffn2_norm, v_ffn2_w_gate_up, v_ffn2_w_down):
    names = ["ffn1_norm", "ffn1_w_gate_up", "ffn1_w_down", "mix_norm", "mem_norm", "w_in", "b_forget", "pool_w",
             "pool_scale", "w_pool_up", "fox_q_norm", "fox_k_norm", "w_fox_o", "w_mem_kv", "mem_q_norm",
             "mem_k_norm", "w_mem_o", "w_out", "ffn2_norm", "ffn2_w_gate_up", "ffn2_w_down"]
    w_args = [ffn1_norm, ffn1_w_gate_up, ffn1_w_down, mix_norm, mem_norm, w_in, b_forget, pool_w, pool_scale,
              w_pool_up, fox_q_norm, fox_k_norm, w_fox_o, w_mem_kv, mem_q_norm, mem_k_norm, w_mem_o, w_out,
              ffn2_norm, ffn2_w_gate_up, ffn2_w_down]
    m_args = [m_ffn1_norm, m_ffn1_w_gate_up, m_ffn1_w_down, m_mix_norm, m_mem_norm, m_w_in, m_b_forget, m_pool_w,
              m_pool_scale, m_w_pool_up, m_fox_q_norm, m_fox_k_norm, m_w_fox_o, m_w_mem_kv, m_mem_q_norm,
              m_mem_k_norm, m_w_mem_o, m_w_out, m_ffn2_norm, m_ffn2_w_gate_up, m_ffn2_w_down]
    v_args = [v_ffn1_norm, v_ffn1_w_gate_up, v_ffn1_w_down, v_mix_norm, v_mem_norm, v_w_in, v_b_forget, v_pool_w,
              v_pool_scale, v_w_pool_up, v_fox_q_norm, v_fox_k_norm, v_w_fox_o, v_w_mem_kv, v_mem_q_norm,
              v_mem_k_norm, v_w_mem_o, v_w_out, v_ffn2_norm, v_ffn2_w_gate_up, v_ffn2_w_down]
    W = dict(zip(names, w_args))
    M = dict(zip(names, m_args))
    V = dict(zip(names, v_args))

    batch, seq, d = x.shape
    mlen = mem.shape[1]
    t = batch * seq
    d_ff = ffn1_w_down.shape[1] * N_DEV
    gate_w = 3 * d
    z_gate, z_q = 0, gate_w
    z_k, z_v = z_q + FOX_WIDTH, z_q + 2 * FOX_WIDTH
    z_u = z_q + 3 * FOX_WIDTH
    z_qm = z_u + POOL_WIDTH
    z_f = z_qm + MEM_WIDTH
    z_width = z_f + F_PAD

    x2d = x.reshape(t, d)
    mem2d = mem.reshape(batch * mlen, d)
    tgt2d = loss_target.reshape(t, d)

    big = ["ffn1_w_gate_up", "ffn1_w_down", "w_in", "w_pool_up", "w_fox_o", "w_mem_kv", "w_mem_o", "w_out",
           "ffn2_w_gate_up", "ffn2_w_down"]
    col_sharded = {"ffn1_w_gate_up", "ffn2_w_gate_up", "w_in", "w_pool_up", "w_fox_o", "w_mem_o"}
    shards = [W[n][0].astype(BF16) for n in big]
    gathered = dict(zip(big, _all_gather(shards, "ag_weights")))
    full = {}
    for n in big:
        g = gathered[n]
        full[n] = _cols_from_blocks(g) if n in col_sharded else g.reshape(-1, g.shape[2])
    o_u, o_q, o_k, o_v = 0, POOL_WIDTH, POOL_WIDTH + FOX_WIDTH, POOL_WIDTH + 2 * FOX_WIDTH
    o_f = o_v + FOX_WIDTH
    o_qm = o_f + FOX_HEADS
    o_g = o_qm + MEM_WIDTH
    wi = full["w_in"]
    w_in_pad = jnp.concatenate(
        [wi[:, o_g:o_g + gate_w], wi[:, o_q:o_q + 3 * FOX_WIDTH], wi[:, o_u:o_u + POOL_WIDTH],
         wi[:, o_qm:o_qm + MEM_WIDTH], wi[:, o_f:o_f + FOX_HEADS], jnp.zeros((d, F_PAD - FOX_HEADS), BF16)], axis=1)

    def ffn_fwd(xin, gain, wgu, wd, tag):
        h = _rms_fwd(xin, gain, tag + "_rms")
        gu = _mm(h, wgu, "nn", tag + "_gu")
        a = _swiglu_fwd(gu, tag + "_swiglu")
        xout = _mm(a, wd, "nn", tag + "_down", scale=0.5, res=xin)
        return xout, (xin, h, gu, a)

    def ffn_bwd(dxo, saved, gain, wgu, wd, tag):
        xin, h, gu, a = saved
        da = _mm(dxo, wd, "nt", tag + "_da", scale=0.5)
        d_wd = _mm(a, dxo, "tn", tag + "_dwd", scale=0.5)
        dgu = _swiglu_bwd(gu, da, tag + "_dswiglu")
        d_wgu = _mm(h, dgu, "tn", tag + "_dwgu")
        dh = _mm(dgu, wgu, "nt", tag + "_dh")
        dxin, d_gain = _rms_bwd(xin, gain, dh, dxo, tag + "_drms")
        return dxin, d_gain, d_wgu, d_wd

    x1, saved1 = ffn_fwd(x2d, W["ffn1_norm"], full["ffn1_w_gate_up"], full["ffn1_w_down"], "ffn1")

    h2 = _rms_fwd(x1, W["mix_norm"], "mix_rms")
    z = _mm(h2, w_in_pad, "nn", "mix_in")

    pool_w_b = W["pool_w"][0].astype(BF16)

    def pool_fwd_body(u, pw, ps):
        row = lax.broadcasted_iota(jnp.int32, (seq, 1), 0)
        diffs, mixed = [], []
        for g in range(POOL_GROUPS):
            ug = u[:, g * POOL_GROUP_DIM:(g + 1) * POOL_GROUP_DIM]
            cnt = jnp.minimum(row + 1, POOL_WINDOWS[g]).astype(F32)
            diff = _window_sum(ug, g + 1, True) / cnt - ug
            diffs.append(diff)
            mixed.append(lax.dot_general(diff.astype(BF16), pw[g], _DIMS["nn"], preferred_element_type=F32))
        diffs = jnp.concatenate(diffs, axis=1)
        mixed = jnp.concatenate(mixed, axis=1)
        return mixed * ps, diffs, mixed

    ypp, pool_diff, pool_mixed = _rowwise(
        pool_fwd_body, [(z, z_u, POOL_WIDTH)], [pool_w_b, W["pool_scale"]],
        [(POOL_WIDTH, BF16), (POOL_WIDTH, BF16), (POOL_WIDTH, F32)], [], "pool_fwd", seq)
    y_pool = _mm(ypp, full["w_pool_up"], "nn", "pool_up")

    gq = jnp.tile(W["fox_q_norm"], (1, FOX_HEADS))
    gk = jnp.tile(W["fox_k_norm"], (1, FOX_HEADS))
    b_pad = jnp.pad(W["b_forget"], ((0, 0), (0, F_PAD - FOX_HEADS)))

    def fox_prep_body(q, k, v, f, gqv, gkv, bv):
        return (_head_norm(q, gqv, FOX_HEAD_DIM), _head_norm(k, gkv, FOX_HEAD_DIM), v, _log_sigmoid(f + bv))

    tm_e = _tile(t, 256, 16)
    qn, kn, vb, logf = _rowwise(
        fox_prep_body, [(z, z_q, FOX_WIDTH), (z, z_k, FOX_WIDTH), (z, z_v, FOX_WIDTH), (z, z_f, F_PAD)],
        [gq, gk, b_pad], [(FOX_WIDTH, BF16), (FOX_WIDTH, BF16), (FOX_WIDTH, BF16), (F_PAD, F32)], [], "fox_prep", tm_e)
    csum = _rowwise(lambda v: (_scan_rows(v, True),), [_whole(logf)], [], [(F_PAD, F32)], [], "fox_cumsum", seq)[0]
    cks = jnp.transpose(csum.reshape(batch, seq, F_PAD)[:, :, :FOX_HEADS], (0, 2, 1)).reshape(
        batch * FOX_HEADS // 2, 2, seq)
    o_fox, lse = _fox_fwd(qn, kn, vb, cks, batch, seq, "fox_fwd")
    y_fox = _mm(o_fox, full["w_fox_o"], "nn", "fox_o")

    memn = _rms_fwd(mem2d, W["mem_norm"], "mem_rms")
    kv = _mm(memn, full["w_mem_kv"], "nn", "mem_kv")
    gqm = jnp.tile(W["mem_q_norm"], (1, MEM_HEADS))
    gkm = jnp.tile(W["mem_k_norm"], (1, MEM_HEADS))
    qmn = _rowwise(lambda q, g: (_head_norm(q, g, MEM_HEAD_DIM),), [(z, z_qm, MEM_WIDTH)], [gqm],
                   [(MEM_WIDTH, BF16)], [], "memq_prep", tm_e)[0]
    kmn, vmb = _rowwise(lambda k, v, g: (_head_norm(k, g, MEM_HEAD_DIM), v),
                        [(kv, 0, MEM_WIDTH), (kv, MEM_WIDTH, MEM_WIDTH)], [gkm],
                        [(MEM_WIDTH, BF16), (MEM_WIDTH, BF16)], [], "memk_prep", _tile(batch * mlen, 256, 16))
    o_mem = _mem_fwd(qmn, kmn, vmb, batch, seq, mlen, "mem_fwd")
    y_mem = _mm(o_mem, full["w_mem_o"], "nn", "mem_o")

    def gate_fwd_body(gp, gf, gm, yp, yf, ym):
        return ((_sigmoid(gp) * yp + _sigmoid(gf) * yf) + _sigmoid(gm) * ym,)

    tm_g = _tile(t, 128, 16)
    merged = _rowwise(gate_fwd_body, [(z, 0, d), (z, d, d), (z, 2 * d, d), _whole(y_pool), _whole(y_fox), _whole(y_mem)],
                      [], [(d, BF16)], [], "gate_fwd", tm_g)[0]
    x2 = _mm(merged, full["w_out"], "nn", "mix_out", res=x1)

    x3, saved2 = ffn_fwd(x2, W["ffn2_norm"], full["ffn2_w_gate_up"], full["ffn2_w_down"], "ffn2")
    dy, loss_part = _loss_head(x3, tgt2d, "loss")
    loss = lax.psum(loss_part[0, 0], ("x", "y", "c"))

    G = {}
    dx2, G["ffn2_norm"], G["ffn2_w_gate_up"], G["ffn2_w_down"] = ffn_bwd(
        dy, saved2, W["ffn2_norm"], full["ffn2_w_gate_up"], full["ffn2_w_down"], "ffn2")

    dmerged = _mm(dx2, full["w_out"], "nt", "mix_out_dx")
    G["w_out"] = _mm(merged, dx2, "tn", "mix_out_dw")

    def gate_bwd_body(gp, gf, gm, yp, yf, ym, dm):
        outs_dl, outs_dy = [], []
        for gl, yv in ((gp, yp), (gf, yf), (gm, ym)):
            s = _sigmoid(gl)
            outs_dl.append((dm * yv) * (s * (1.0 - s)))
            outs_dy.append(dm * s)
        return (jnp.concatenate(outs_dl, axis=1), *outs_dy)

    dz_gate, dy_pool, dy_fox, dy_mem = _rowwise(
        gate_bwd_body, [(z, 0, d), (z, d, d), (z, 2 * d, d), _whole(y_pool), _whole(y_fox), _whole(y_mem), _whole(dmerged)],
        [], [(gate_w, BF16), (d, BF16), (d, BF16), (d, BF16)], [], "gate_bwd", tm_g)

    dypp = _mm(dy_pool, full["w_pool_up"], "nt", "pool_up_dx")
    G["w_pool_up"] = _mm(ypp, dy_pool, "tn", "pool_up_dw")

    def pool_bwd_body(dyv, mixed, diff, pw, ps):
        row = lax.broadcasted_iota(jnp.int32, (seq, 1), 0)
        d_scale = jnp.sum(dyv * mixed, axis=0, keepdims=True)
        dmix = (dyv * ps).astype(BF16)
        du, dpw = [], []
        for g in range(POOL_GROUPS):
            sl = slice(g * POOL_GROUP_DIM, (g + 1) * POOL_GROUP_DIM)
            dmg = dmix[:, sl]
            ddiff = lax.dot_general(dmg, pw[g], _DIMS["nt"], preferred_element_type=F32)
            dpw.append(lax.dot_general(diff[:, sl], dmg, _DIMS["tn"], preferred_element_type=F32))
            cnt = jnp.minimum(row + 1, POOL_WINDOWS[g]).astype(F32)
            du.append(_window_sum(ddiff / cnt, g + 1, False) - ddiff)
        return jnp.concatenate(du, axis=1), d_scale, jnp.concatenate(dpw, axis=0)

    dz_u, G["pool_scale"], d_pool_w = _rowwise(
        pool_bwd_body, [_whole(dypp), _whole(pool_mixed), _whole(pool_diff)], [pool_w_b, W["pool_scale"]],
        [(POOL_WIDTH, BF16)], [((1, POOL_WIDTH), F32), ((POOL_WIDTH, POOL_GROUP_DIM), F32)],
        "pool_bwd", seq)
    G["pool_w"] = d_pool_w.reshape(1, POOL_GROUPS, POOL_GROUP_DIM, POOL_GROUP_DIM)

    do_fox = _mm(dy_fox, full["w_fox_o"], "nt", "fox_o_dx")
    G["w_fox_o"] = _mm(o_fox, dy_fox, "tn", "fox_o_dw")
    dkn, dz_v, dqn, dcks, dcq = _fox_bwd(qn, kn, vb, cks, o_fox, lse, do_fox, batch, seq, "fox_bwd")
    dcs = jnp.transpose(dcks.reshape(batch, FOX_HEADS, seq), (0, 2, 1)).reshape(t, FOX_HEADS)
    dcs = jnp.pad(dcs, ((0, 0), (0, F_PAD - FOX_HEADS)))
    dcq = jnp.pad(dcq.reshape(t, FOX_HEADS, FOX_HEAD_DIM)[:, :, 0], ((0, 0), (0, F_PAD - FOX_HEADS)))

    def fox_f_bwd_body(dc, dc_rows, f, bv):
        lane = lax.broadcasted_iota(jnp.int32, (1, F_PAD), 1)
        dlogf = _scan_rows(dc + dc_rows, False)
        df = jnp.where(lane < FOX_HEADS, dlogf * _sigmoid(-(f + bv)), 0.0)
        return df, jnp.sum(df, axis=0, keepdims=True)

    dz_f, db_pad = _rowwise(fox_f_bwd_body, [_whole(dcs), _whole(dcq), (z, z_f, F_PAD)], [b_pad], [(F_PAD, BF16)],
                            [((1, F_PAD), F32)], "fox_f_bwd", seq)
    G["b_forget"] = db_pad[:, :FOX_HEADS]

    def fox_qk_bwd_body(q, k, dq, dk, gqv, gkv):
        dqr, dgq = _head_norm_bwd(q, gqv, dq, FOX_HEAD_DIM)
        dkr, dgk = _head_norm_bwd(k, gkv, dk, FOX_HEAD_DIM)
        return dqr, dkr, dgq, dgk

    dz_q, dz_k, dgq_t, dgk_t = _rowwise(
        fox_qk_bwd_body, [(z, z_q, FOX_WIDTH), (z, z_k, FOX_WIDTH), _whole(dqn), _whole(dkn)], [gq, gk],
        [(FOX_WIDTH, BF16), (FOX_WIDTH, BF16)], [((1, FOX_WIDTH), F32), ((1, FOX_WIDTH), F32)], "fox_qk_bwd", tm_e)
    G["fox_q_norm"] = jnp.sum(dgq_t.reshape(FOX_HEADS, FOX_HEAD_DIM), axis=0, keepdims=True)
    G["fox_k_norm"] = jnp.sum(dgk_t.reshape(FOX_HEADS, FOX_HEAD_DIM), axis=0, keepdims=True)

    do_mem = _mm(dy_mem, full["w_mem_o"], "nt", "mem_o_dx")
    G["w_mem_o"] = _mm(o_mem, dy_mem, "tn", "mem_o_dw")
    dqmn, dkmn, dvm = _mem_bwd(qmn, kmn, vmb, do_mem, batch, seq, mlen, "mem_bwd")
    dz_qm, dgqm_t = _rowwise(lambda q, dq, g: _head_norm_bwd(q, g, dq, MEM_HEAD_DIM),
                             [(z, z_qm, MEM_WIDTH), _whole(dqmn)], [gqm], [(MEM_WIDTH, BF16)],
                             [((1, MEM_WIDTH), F32)], "memq_bwd", tm_e)

    def memk_bwd_body(k, dk, dv, g):
        dkr, dg = _head_norm_bwd(k, g, dk, MEM_HEAD_DIM)
        return jnp.concatenate([dkr, dv], axis=1), dg

    dkv, dgkm_t = _rowwise(memk_bwd_body, [(kv, 0, MEM_WIDTH), _whole(dkmn), _whole(dvm)], [gkm],
                           [(2 * MEM_WIDTH, BF16)], [((1, MEM_WIDTH), F32)], "memk_bwd", _tile(batch * mlen, 256, 16))
    G["mem_q_norm"] = jnp.sum(dgqm_t.reshape(MEM_HEADS, MEM_HEAD_DIM), axis=0, keepdims=True)
    G["mem_k_norm"] = jnp.sum(dgkm_t.reshape(MEM_HEADS, MEM_HEAD_DIM), axis=0, keepdims=True)
    G["w_mem_kv"] = _mm(memn, dkv, "tn", "mem_kv_dw")
    dmemn = _mm(dkv, full["w_mem_kv"], "nt", "mem_kv_dx")
    _, G["mem_norm"] = _rms_bwd(mem2d, W["mem_norm"], dmemn, None, "mem_drms")

    dz = jnp.concatenate([dz_gate, dz_q, dz_k, dz_v, dz_u, dz_qm, dz_f], axis=1)
    d_w_in_pad = _mm(h2, dz, "tn", "mix_in_dw")
    dh2 = _mm(dz, w_in_pad, "nt", "mix_in_dx")
    dx1, G["mix_norm"] = _rms_bwd(x1, W["mix_norm"], dh2, dx2, "mix_drms")
    p = d_w_in_pad
    G["w_in"] = jnp.concatenate(
        [p[:, z_u:z_u + POOL_WIDTH], p[:, z_q:z_q + 3 * FOX_WIDTH], p[:, z_f:z_f + FOX_HEADS],
         p[:, z_qm:z_qm + MEM_WIDTH], p[:, z_gate:z_gate + gate_w]], axis=1)

    dx0, G["ffn1_norm"], G["ffn1_w_gate_up"], G["ffn1_w_down"] = ffn_bwd(
        dx1, saved1, W["ffn1_norm"], full["ffn1_w_gate_up"], full["ffn1_w_down"], "ffn1")
    grad_x = dx0.reshape(batch, seq, d)

    cx, cy, cc = _position()
    core = jnp.reshape(cc, (1,)).astype(jnp.int32)
    chip = jnp.reshape(2 * cx + cy, (1,)).astype(jnp.int32)
    partials = []
    for n in big:
        g = G[n]
        partials.append(_blocks_from_cols(g) if n in col_sharded else g.reshape(N_DEV, g.shape[0] // N_DEV, g.shape[1]))
    from_core = _rs_sibling_exchange(partials, "rs_d2d")
    halves = [_rs_pair_add(pt, ld, core, "rs_add_" + n) for n, pt, ld in zip(big, partials, from_core)]
    from_chips = _rs_chip_exchange(halves, "rs_ici")
    out_g, out_d, out_m, out_v = {}, {}, {}, {}
    for n, hv, ld in zip(big, halves, from_chips):
        res = _rs_finish_adam(hv, ld, chip, W[n][0], M[n][0], V[n][0], "adam_" + n)
        out_g[n], out_d[n], out_m[n], out_v[n] = [r[None] for r in res]

    small = [n for n in names if n not in big]
    g_small = _pack_small([G[n].reshape(W[n].shape) for n in small])
    all_small = _all_gather([g_small], "ag_small_grads", in_vmem=True)[0]
    res = _allreduce_adam(all_small, _pack_small([W[n] for n in small]), _pack_small([M[n] for n in small]),
                          _pack_small([V[n] for n in small]), "adam_small")
    like = [W[n] for n in small]
    for dst, buf in zip((out_g, out_d, out_m, out_v), res):
        for n, a in zip(small, _unpack_small(buf, like)):
            dst[n] = a

    return (loss, grad_x, *[out_g[n] for n in names], *[out_d[n] for n in names],
            *[out_m[n] for n in names], *[out_v[n] for n in names])
```

```python
import functools

import jax
import jax.numpy as jnp
from jax import lax
from jax.experimental import pallas as pl
from jax.experimental.pallas import tpu as pltpu

F32 = jnp.float32
BF16 = jnp.bfloat16
MESH = pl.DeviceIdType.MESH

N_DEV = 8
EPS = 1e-6
FOX_HEADS = 16
FOX_HEAD_DIM = 64
FOX_WIDTH = FOX_HEADS * FOX_HEAD_DIM
MEM_HEADS = 4
MEM_HEAD_DIM = 128
MEM_WIDTH = MEM_HEADS * MEM_HEAD_DIM
POOL_GROUPS = 4
POOL_GROUP_DIM = 128
POOL_WIDTH = POOL_GROUPS * POOL_GROUP_DIM
POOL_WINDOWS = (2, 4, 8, 16)
LANES = 128
F_PAD = LANES

ADAM_LR = 0.001
ADAM_B1 = 0.9
ADAM_B2 = 0.999
ADAM_EPS = 1e-08
ADAM_WD = 0.01
ADAM_STEP = 10

VMEM_LIMIT = 56 * 1024 * 1024
NEG = -1e30

ANY = pl.BlockSpec(memory_space=pl.ANY)


def _params(sem=None):
    return pltpu.CompilerParams(dimension_semantics=sem, vmem_limit_bytes=VMEM_LIMIT)


MXU_DIM = 256


def _tile(dim, cap, align):
    best = None
    t = align
    while t <= min(dim, cap):
        if dim % t == 0:
            best = t
        t += align
    return dim if best is None else best


def _mxu_tile(dim, cap):
    t = _tile(dim, cap, MXU_DIM)
    return t if t % MXU_DIM == 0 and t <= cap else _tile(dim, cap, LANES)


class _Sems:
    def __init__(self, send, recv, local):
        self.send, self.recv, self.local = send, recv, local
        self.n_remote = self.n_local = 0

    def remote(self):
        i = self.n_remote
        self.n_remote += 1
        return self.send.at[i], self.recv.at[i]

    def one_local(self):
        i = self.n_local
        self.n_local += 1
        return self.local.at[i]


class _Comm:
    def __init__(self, srcs, outs, n_remote, n_local, plan, in_place=False):
        self.srcs, self.outs, self.n_remote, self.n_local = list(srcs), list(outs), n_remote, n_local
        self.plan, self.in_place = plan, in_place

    def aliases(self, n_in, n_out):
        return {n_in + i: n_out + i for i in range(len(self.srcs))} if self.in_place else {}

    def scratch(self):
        return [pltpu.SemaphoreType.DMA((self.n_remote,)), pltpu.SemaphoreType.DMA((self.n_remote,)),
                pltpu.SemaphoreType.DMA((max(self.n_local, 1),))]

    def run(self, src_refs, out_refs, sem_refs, first, last):
        def copies():
            return self.plan(list(src_refs), list(out_refs), _Sems(*sem_refs))

        return (lambda: _when(first, lambda: [cp.start() for cp in copies()]),
                lambda: _when(last, lambda: [cp.wait() for cp in copies()]))


def _when(cond, fn):
    @pl.when(cond)
    def _():
        fn()


def _join(*comms):
    comms = [c for c in comms if c is not None]
    assert all(not c.in_place for c in comms)

    def plan(src_refs, out_refs, sems):
        copies, si, oi = [], 0, 0
        for c in comms:
            copies += c.plan(src_refs[si:si + len(c.srcs)], out_refs[oi:oi + len(c.outs)], sems)
            si += len(c.srcs)
            oi += len(c.outs)
        return copies

    return _Comm(sum((c.srcs for c in comms), []), sum((c.outs for c in comms), []),
                 sum(c.n_remote for c in comms), sum(c.n_local for c in comms), plan)


def _split_refs(refs, n_in, n_out, comm):
    if comm is None:
        return refs[:n_in], refs[n_in:n_in + n_out], refs[n_in + n_out:], (), (), ()
    ns, no = len(comm.srcs), len(comm.outs)
    ins = refs[:n_in]
    srcs = refs[n_in:n_in + ns]
    outs = refs[n_in + ns:n_in + ns + n_out]
    couts = refs[n_in + ns + n_out:n_in + ns + n_out + no]
    rest = refs[n_in + ns + n_out + no:]
    return ins, outs, rest[:-3], srcs, couts, rest[-3:]


def _sigmoid(x):
    return 1.0 / (1.0 + jnp.exp(-x))


_DIMS = {"nn": (((1,), (0,)), ((), ())), "nt": (((1,), (1,)), ((), ())), "tn": (((0,), (0,)), ((), ()))}


def _mm(a, b, mode, name, out_dtype=F32, scale=1.0, res=None, caps=None, b_blocks=False, out_blocks=False,
        comm=None):
    nblk = N_DEV
    if b_blocks:
        _, r0, c0 = b.shape
        b_shape = (r0, nblk * c0)
    else:
        b_shape = b.shape
    if mode == "nn":
        (m, k), (k2, n) = a.shape, b_shape
    elif mode == "nt":
        (m, k), (n, k2) = a.shape, b_shape
    else:
        (k, m), (k2, n) = a.shape, b_shape
    assert k == k2, (name, a.shape, b.shape)
    cm, cn, ck = caps or ((1024, 512, 2048) if k <= 2048 else (1024, 1024, 1024))
    n_unit = n // nblk if (out_blocks or (b_blocks and mode == "nn")) else n
    k_unit = k // nblk if (b_blocks and mode == "nt") else k
    tm, tn, tk = _mxu_tile(m, cm), _mxu_tile(n_unit, cn), _mxu_tile(k_unit, ck)
    nk = k // tk
    npb, kpb = n_unit // tn, k_unit // tk
    if mode == "nn":
        a_spec = pl.BlockSpec((tm, tk), lambda i, j, kk: (i, kk))
        b_spec = pl.BlockSpec((tk, tn), lambda i, j, kk: (kk, j))
        if b_blocks:
            b_spec = pl.BlockSpec((None, tk, tn), lambda i, j, kk: (j // npb, kk, j % npb))
    elif mode == "nt":
        a_spec = pl.BlockSpec((tm, tk), lambda i, j, kk: (i, kk))
        b_spec = pl.BlockSpec((tn, tk), lambda i, j, kk: (j, kk))
        if b_blocks:
            b_spec = pl.BlockSpec((None, tn, tk), lambda i, j, kk: (kk // kpb, j, kk % kpb))
    else:
        assert not b_blocks
        a_spec = pl.BlockSpec((tk, tm), lambda i, j, kk: (kk, i))
        b_spec = pl.BlockSpec((tk, tn), lambda i, j, kk: (kk, j))
    if out_blocks:
        assert res is None
        o_spec = pl.BlockSpec((None, tm, tn), lambda i, j, kk: (j // npb, i, j % npb))
        o_shape = jax.ShapeDtypeStruct((nblk, m, n // nblk), out_dtype)
    else:
        o_spec = pl.BlockSpec((tm, tn), lambda i, j, kk: (i, j))
        o_shape = jax.ShapeDtypeStruct((m, n), out_dtype)
    in_specs = [a_spec, b_spec] + ([o_spec] if res is not None else [])
    n_in = len(in_specs)
    dims = _DIMS[mode]
    gm, gn = m // tm, n // tn

    def kern(*refs):
        ins, outs, scratch, c_src, c_out, c_sem = _split_refs(refs, n_in, 1, comm)
        a_ref, b_ref = ins[0], ins[1]
        res_ref = ins[2] if res is not None else None
        o_ref = outs[0]
        if comm is not None:
            i, j, kq = pl.program_id(0), pl.program_id(1), pl.program_id(2)
            first = jnp.logical_and(jnp.logical_and(i == 0, j == 0), kq == 0)
            last = jnp.logical_and(jnp.logical_and(i == gm - 1, j == gn - 1), kq == nk - 1)
            start_comm, wait_comm = comm.run(c_src, c_out, c_sem, first, last)
            start_comm()
        prod = lax.dot_general(a_ref[...].astype(BF16), b_ref[...].astype(BF16), dims,
                               preferred_element_type=F32)

        def finish(acc):
            r = acc if scale == 1.0 else scale * acc
            if res_ref is not None:
                r = res_ref[...] + r
            o_ref[...] = r.astype(out_dtype)

        if nk == 1:
            finish(prod)
        else:
            acc_ref = scratch[0]
            kk = pl.program_id(2)

            @pl.when(kk == 0)
            def _():
                acc_ref[...] = prod

            @pl.when(kk > 0)
            def _():
                acc_ref[...] += prod

            @pl.when(kk == nk - 1)
            def _():
                finish(acc_ref[...])
        if comm is not None:
            wait_comm()

    args = (a, b) + ((res,) if res is not None else ())
    scratch_shapes = [pltpu.VMEM((tm, tn), F32)] if nk > 1 else []
    if comm is None:
        return pl.pallas_call(
            kern, name=name, grid=(gm, gn, nk), in_specs=in_specs, out_specs=o_spec, out_shape=o_shape,
            scratch_shapes=scratch_shapes, compiler_params=_params(("parallel", "parallel", "arbitrary")),
        )(*args)
    res_all = pl.pallas_call(
        kern, name=name, grid=(gm, gn, nk), in_specs=in_specs + [ANY] * len(comm.srcs),
        out_specs=[o_spec] + [ANY] * len(comm.outs), out_shape=[o_shape] + comm.outs,
        scratch_shapes=scratch_shapes + comm.scratch(), input_output_aliases=comm.aliases(n_in, 1),
        compiler_params=_params(("arbitrary", "arbitrary", "arbitrary")),
    )(*args, *comm.srcs)
    return res_all[0], list(res_all[1:])


def _rowwise(body, ins, params, outs, accs, name, tm, comm=None):
    rows = ins[0][0].shape[0]
    assert rows % tm == 0, (name, rows, tm)
    in_specs = []
    for arr, off, width in ins:
        assert off % width == 0 and arr.shape[0] == rows, (name, arr.shape, off, width)
        in_specs.append(pl.BlockSpec((tm, width), functools.partial(lambda i, c: (i, c), c=off // width)))
    for p in params:
        in_specs.append(pl.BlockSpec(p.shape, functools.partial(lambda i, nd: (0,) * nd, nd=p.ndim)))
    out_specs = [pl.BlockSpec((tm, w), lambda i: (i, 0)) for w, _ in outs]
    out_specs += [pl.BlockSpec(s, functools.partial(lambda i, nd: (0,) * nd, nd=len(s))) for s, _ in accs]
    out_shape = [jax.ShapeDtypeStruct((rows, w), d) for w, d in outs]
    out_shape += [jax.ShapeDtypeStruct(s, d) for s, d in accs]
    n_in, n_par, n_out = len(ins), len(params), len(outs)

    steps = rows // tm

    def kern(*refs):
        in_refs, out_refs, _, c_src, c_out, c_sem = _split_refs(refs, n_in + n_par, n_out + len(accs), comm)
        first = pl.program_id(0) == 0
        if comm is not None:
            start_comm, wait_comm = comm.run(c_src, c_out, c_sem, first, pl.program_id(0) == steps - 1)
            start_comm()
        res = body(*[r[...] for r in in_refs])
        for r, v in zip(out_refs[:n_out], res[:n_out]):
            r[...] = v.astype(r.dtype)
        for r, v in zip(out_refs[n_out:], res[n_out:]):
            @pl.when(first)
            def _(r=r, v=v):
                r[...] = v.astype(r.dtype)

            @pl.when(jnp.logical_not(first))
            def _(r=r, v=v):
                r[...] += v.astype(r.dtype)
        if comm is not None:
            wait_comm()

    args = [a for a, _, _ in ins] + list(params)
    if comm is None:
        return pl.pallas_call(
            kern, name=name, grid=(steps,), in_specs=in_specs, out_specs=out_specs, out_shape=out_shape,
            compiler_params=_params(("arbitrary",) if accs else ("parallel",)),
        )(*args)
    res_all = pl.pallas_call(
        kern, name=name, grid=(steps,), in_specs=in_specs + [ANY] * len(comm.srcs),
        out_specs=out_specs + [ANY] * len(comm.outs), out_shape=out_shape + comm.outs,
        scratch_shapes=comm.scratch(), input_output_aliases=comm.aliases(len(args), len(out_shape)),
        compiler_params=_params(("arbitrary",)),
    )(*args, *comm.srcs)
    return list(res_all[:len(out_shape)]), list(res_all[len(out_shape):])


def _whole(x):
    return (x, 0, x.shape[1])


def _first(res, comm):
    return res[0] if comm is None else (res[0][0], res[1])


def _rms_fwd(x, gain, name, comm=None):
    def body(xv, g):
        r = lax.rsqrt(jnp.mean(xv * xv, axis=-1, keepdims=True) + EPS)
        return ((xv * r) * g,)

    return _first(_rowwise(body, [_whole(x)], [gain], [(x.shape[1], BF16)], [], name, _tile(x.shape[0], 256, 16),
                           comm=comm), comm)


def _rms_bwd(x, gain, dh, dres, name, comm=None):
    d = x.shape[1]

    def body(*vals):
        if dres is None:
            xv, dhv, g = vals
        else:
            xv, dhv, drv, g = vals
        r = lax.rsqrt(jnp.mean(xv * xv, axis=-1, keepdims=True) + EPS)
        xh = xv * r
        w = dhv * g
        dx = r * (w - xh * jnp.mean(w * xh, axis=-1, keepdims=True))
        if dres is not None:
            dx = drv + dx
        return dx, jnp.sum(dhv * xh, axis=0, keepdims=True)

    ins = [_whole(x), _whole(dh)] + ([_whole(dres)] if dres is not None else [])
    return _rowwise(body, ins, [gain], [(d, F32)], [((1, d), F32)], name, _tile(x.shape[0], 256, 16), comm=comm)


def _swiglu_fwd(gu, name, comm=None):
    f = gu.shape[1] // 2

    def body(g, u):
        return ((g * _sigmoid(g)) * u,)

    return _first(_rowwise(body, [(gu, 0, f), (gu, f, f)], [], [(f, BF16)], [], name, _tile(gu.shape[0], 128, 16),
                           comm=comm), comm)


def _swiglu_bwd(gu, da, name, comm=None):
    f = gu.shape[1] // 2

    def body(g, u, dav):
        s = _sigmoid(g)
        dg = dav * u * (s * (1.0 + g * (1.0 - s)))
        du = dav * (g * s)
        return (jnp.concatenate([dg, du], axis=1),)

    return _first(_rowwise(body, [(gu, 0, f), (gu, f, f), _whole(da)], [], [(2 * f, BF16)], [], name,
                           _tile(gu.shape[0], 128, 16), comm=comm), comm)


def _loss_head(y, target, name):
    d = y.shape[1]

    def body(yv, tv):
        e = yv - tv
        part = jnp.sum(jnp.sum(e * e, axis=1, keepdims=True), axis=0, keepdims=True)
        return e / d, jnp.broadcast_to((0.5 / d) * part, (1, LANES))

    return _rowwise(body, [_whole(y), _whole(target)], [], [(d, F32)], [((1, LANES), F32)], name,
                    _tile(y.shape[0], 256, 16))


def _head_mean(v, head_dim):
    cols = []
    lane = lax.broadcasted_iota(jnp.int32, (1, LANES), 1)
    for j in range(v.shape[1] // LANES):
        blk = v[:, j * LANES:(j + 1) * LANES]
        if head_dim == LANES:
            m = jnp.sum(blk, axis=-1, keepdims=True)
            cols.append(jnp.broadcast_to(m, blk.shape))
        else:
            lo = jnp.sum(jnp.where(lane < head_dim, blk, 0.0), axis=-1, keepdims=True)
            hi = jnp.sum(jnp.where(lane >= head_dim, blk, 0.0), axis=-1, keepdims=True)
            cols.append(jnp.where(lane < head_dim, lo, hi))
    return jnp.concatenate(cols, axis=1) / head_dim


def _head_norm(xv, g, head_dim):
    r = lax.rsqrt(_head_mean(xv * xv, head_dim) + EPS)
    return (xv * r) * g


def _head_norm_bwd(xv, g, dy, head_dim):
    r = lax.rsqrt(_head_mean(xv * xv, head_dim) + EPS)
    xh = xv * r
    w = dy * g
    dx = r * (w - xh * _head_mean(w * xh, head_dim))
    return dx, jnp.sum(dy * xh, axis=0, keepdims=True)


def _log_sigmoid(x):
    return jnp.minimum(x, 0.0) - jnp.log(1.0 + jnp.exp(-jnp.abs(x)))


def _shift_rows(v, sh, down):
    n = v.shape[0]
    row = lax.broadcasted_iota(jnp.int32, (n, 1), 0)
    if down:
        return jnp.where(row >= sh, pltpu.roll(v, sh, 0), 0.0)
    return jnp.where(row < n - sh, pltpu.roll(v, n - sh, 0), 0.0)


def _scan_rows(v, down):
    sh = 1
    while sh < v.shape[0]:
        v = v + _shift_rows(v, sh, down)
        sh *= 2
    return v


def _window_sum(v, steps, down):
    for s in range(steps):
        v = v + _shift_rows(v, 2 ** s, down)
    return v


def _fox_fwd(qn, kn, vb, cks, batch, seq, name, comm=None):
    t, width = qn.shape
    pairs = width // LANES
    tq = _tile(seq, 512, LANES)
    nq = seq // tq
    scale = FOX_HEAD_DIM ** -0.5

    def kern(*refs):
        (q_ref, k_ref, v_ref, c_ref), (o_ref, lse_ref), _, c_src, c_out, c_sem = _split_refs(refs, 4, 2, comm)
        qi = pl.program_id(2)
        if comm is not None:
            b_id, p_id = pl.program_id(0), pl.program_id(1)
            first = jnp.logical_and(jnp.logical_and(b_id == 0, p_id == 0), qi == 0)
            last = jnp.logical_and(jnp.logical_and(b_id == batch - 1, p_id == pairs - 1), qi == nq - 1)
            start_comm, wait_comm = comm.run(c_src, c_out, c_sem, first, last)
            start_comm()
        lane = lax.broadcasted_iota(jnp.int32, (1, LANES), 1)
        rowi = lax.broadcasted_iota(jnp.int32, (tq, tq), 0)
        coli = lax.broadcasted_iota(jnp.int32, (tq, tq), 1)
        q_all = q_ref[...]
        o_heads, lse_heads = [], []
        for h in range(2):
            hm = (lane < FOX_HEAD_DIM) if h == 0 else (lane >= FOX_HEAD_DIM)
            q = jnp.where(hm, q_all, jnp.zeros_like(q_all))

            def step(j, carry, h=h, q=q):
                m, l, acc = carry
                start = pl.multiple_of(j * tq, tq)
                k = k_ref[pl.ds(start, tq), :]
                v = v_ref[pl.ds(start, tq), :]
                s = lax.dot_general(q, k, _DIMS["nt"], preferred_element_type=F32) * scale
                s = s - c_ref[0, h:h + 1, pl.ds(start, tq)]
                s = jnp.where(qi * tq + rowi >= j * tq + coli, s, NEG)
                m_new = jnp.maximum(m, jnp.max(s, axis=-1, keepdims=True))
                alpha = jnp.exp(m - m_new)
                p = jnp.exp(s - m_new)
                l = alpha * l + jnp.sum(p, axis=-1, keepdims=True)
                acc = alpha * acc + lax.dot_general(p.astype(BF16), v, _DIMS["nn"], preferred_element_type=F32)
                return m_new, l, acc

            init = (jnp.full((tq, 1), NEG, F32), jnp.zeros((tq, 1), F32), jnp.zeros((tq, LANES), F32))
            m, l, acc = lax.fori_loop(0, qi + 1, step, init)
            o_heads.append(acc / l)
            lse_heads.append(jnp.broadcast_to(m + jnp.log(l), (tq, LANES)))
        o_ref[...] = jnp.where(lane < FOX_HEAD_DIM, o_heads[0], o_heads[1])
        lse_ref[...] = jnp.where(lane < FOX_HEAD_DIM, lse_heads[0], lse_heads[1])
        if comm is not None:
            wait_comm()

    q_spec = pl.BlockSpec((tq, LANES), lambda b, hp, qi: (b * nq + qi, hp))
    kv_spec = pl.BlockSpec((seq, LANES), lambda b, hp, qi: (b, hp))
    c_spec = pl.BlockSpec((1, 2, seq), lambda b, hp, qi: (b * pairs + hp, 0, 0))
    in_specs = [q_spec, kv_spec, kv_spec, c_spec]
    out_shape = [jax.ShapeDtypeStruct((t, width), F32), jax.ShapeDtypeStruct((t, width), F32)]
    if comm is None:
        return pl.pallas_call(
            kern, name=name, grid=(batch, pairs, nq), in_specs=in_specs, out_specs=[q_spec, q_spec],
            out_shape=out_shape, compiler_params=_params(("parallel", "parallel", "arbitrary")),
        )(qn, kn, vb, cks)
    res_all = pl.pallas_call(
        kern, name=name, grid=(batch, pairs, nq), in_specs=in_specs + [ANY] * len(comm.srcs),
        out_specs=[q_spec, q_spec] + [ANY] * len(comm.outs), out_shape=out_shape + comm.outs,
        scratch_shapes=comm.scratch(), compiler_params=_params(("arbitrary", "arbitrary", "arbitrary")),
    )(qn, kn, vb, cks, *comm.srcs)
    return list(res_all[:2]), list(res_all[2:])


def _fox_bwd(qn, kn, vb, cks, o, lse, do, batch, seq, name, comm=None):
    t, width = qn.shape
    pairs = width // LANES
    tk = _tile(seq, 512, LANES)
    nk = seq // tk
    scale = FOX_HEAD_DIM ** -0.5

    def kern(*refs):
        ins, outs, _, c_src, c_out, c_sem = _split_refs(refs, 7, 5, comm)
        k_ref, v_ref, q_ref, c_ref, o_ref, lse_ref, do_ref = ins
        dk_ref, dv_ref, dq_ref, dc_ref, dcq_ref = outs
        kj = pl.program_id(2)
        if comm is not None:
            b_id, p_id = pl.program_id(0), pl.program_id(1)
            first = jnp.logical_and(jnp.logical_and(b_id == 0, p_id == 0), kj == 0)
            last = jnp.logical_and(jnp.logical_and(b_id == batch - 1, p_id == pairs - 1), kj == nk - 1)
            start_comm, wait_comm = comm.run(c_src, c_out, c_sem, first, last)
            start_comm()
        lane = lax.broadcasted_iota(jnp.int32, (1, LANES), 1)
        rowi = lax.broadcasted_iota(jnp.int32, (tk, tk), 0)
        coli = lax.broadcasted_iota(jnp.int32, (tk, tk), 1)

        @pl.when(kj == 0)
        def _():
            dq_ref[...] = jnp.zeros_like(dq_ref)
            dcq_ref[...] = jnp.zeros_like(dcq_ref)

        k_all = k_ref[...]
        v_all = v_ref[...]
        kstart = pl.multiple_of(kj * tk, tk)
        dk_heads, dv_heads = [], []
        for h in range(2):
            hm = (lane < FOX_HEAD_DIM) if h == 0 else (lane >= FOX_HEAD_DIM)
            kh = jnp.where(hm, k_all, jnp.zeros_like(k_all))
            vh = jnp.where(hm, v_all, jnp.zeros_like(v_all))
            c_row = c_ref[0, h:h + 1, pl.ds(kstart, tk)]

            def step(qi, carry, h=h, hm=hm, kh=kh, vh=vh, c_row=c_row):
                dk_acc, dv_acc, dc_acc = carry
                start = pl.multiple_of(qi * tk, tk)
                q = q_ref[pl.ds(start, tk), :]
                dov = do_ref[pl.ds(start, tk), :]
                ov = o_ref[pl.ds(start, tk), :]
                lse_col = jnp.max(jnp.where(hm, lse_ref[pl.ds(start, tk), :], NEG), axis=-1, keepdims=True)
                dob = jnp.where(hm, dov, 0.0).astype(BF16)
                dcol = jnp.sum(dob.astype(F32) * ov, axis=-1, keepdims=True)
                s = lax.dot_general(q, kh, _DIMS["nt"], preferred_element_type=F32) * scale - c_row
                p = jnp.where(qi * tk + rowi >= kj * tk + coli, jnp.exp(s - lse_col), 0.0)
                dp = lax.dot_general(dob, vh, _DIMS["nt"], preferred_element_type=F32)
                ds = p * (dp - dcol)
                dsb = (ds * scale).astype(BF16)
                dv_acc = dv_acc + lax.dot_general(p.astype(BF16), dob, _DIMS["tn"], preferred_element_type=F32)
                dk_acc = dk_acc + lax.dot_general(dsb, q, _DIMS["tn"], preferred_element_type=F32)
                dq_part = lax.dot_general(dsb, kh, _DIMS["nn"], preferred_element_type=F32)
                dq_ref[pl.ds(start, tk), :] += dq_part
                dcq_ref[pl.ds(start, tk), :] += jnp.where(hm, jnp.sum(ds, axis=-1, keepdims=True), 0.0)
                dc_acc = dc_acc - jnp.sum(ds, axis=0, keepdims=True)
                return dk_acc, dv_acc, dc_acc

            init = (jnp.zeros((tk, LANES), F32), jnp.zeros((tk, LANES), F32), jnp.zeros((1, tk), F32))
            dk_acc, dv_acc, dc_acc = lax.fori_loop(kj, nk, step, init)
            dk_heads.append(dk_acc)
            dv_heads.append(dv_acc)
            dc_ref[0, h:h + 1, pl.ds(kstart, tk)] = dc_acc
        dk_ref[...] = jnp.where(lane < FOX_HEAD_DIM, dk_heads[0], dk_heads[1])
        dv_ref[...] = jnp.where(lane < FOX_HEAD_DIM, dv_heads[0], dv_heads[1]).astype(BF16)
        if comm is not None:
            wait_comm()

    kv_spec = pl.BlockSpec((tk, LANES), lambda b, hp, kj: (b * nk + kj, hp))
    full_spec = pl.BlockSpec((seq, LANES), lambda b, hp, kj: (b, hp))
    c_spec = pl.BlockSpec((1, 2, seq), lambda b, hp, kj: (b * pairs + hp, 0, 0))
    in_specs = [kv_spec, kv_spec, full_spec, c_spec, full_spec, full_spec, full_spec]
    out_specs = [kv_spec, kv_spec, full_spec, c_spec, full_spec]
    out_shape = [jax.ShapeDtypeStruct((t, width), F32), jax.ShapeDtypeStruct((t, width), BF16),
                 jax.ShapeDtypeStruct((t, width), F32), jax.ShapeDtypeStruct(cks.shape, F32),
                 jax.ShapeDtypeStruct((t, width), F32)]
    if comm is None:
        return pl.pallas_call(
            kern, name=name, grid=(batch, pairs, nk), in_specs=in_specs, out_specs=out_specs, out_shape=out_shape,
            compiler_params=_params(("parallel", "parallel", "arbitrary")),
        )(kn, vb, qn, cks, o, lse, do)
    res_all = pl.pallas_call(
        kern, name=name, grid=(batch, pairs, nk), in_specs=in_specs + [ANY] * len(comm.srcs),
        out_specs=out_specs + [ANY] * len(comm.outs), out_shape=out_shape + comm.outs,
        scratch_shapes=comm.scratch(), compiler_params=_params(("arbitrary", "arbitrary", "arbitrary")),
    )(kn, vb, qn, cks, o, lse, do, *comm.srcs)
    return list(res_all[:5]), list(res_all[5:])


def _mem_fwd(qn, kn, vb, batch, seq, mlen, name):
    t, width = qn.shape
    heads = width // LANES
    tq = _tile(seq, 512, LANES)
    nq = seq // tq
    scale = MEM_HEAD_DIM ** -0.5

    def kern(q_ref, k_ref, v_ref, o_ref):
        s = lax.dot_general(q_ref[...], k_ref[...], _DIMS["nt"], preferred_element_type=F32) * scale
        e = jnp.exp(s - jnp.max(s, axis=-1, keepdims=True))
        p = e / jnp.sum(e, axis=-1, keepdims=True)
        o_ref[...] = lax.dot_general(p.astype(BF16), v_ref[...], _DIMS["nn"], preferred_element_type=F32)

    q_spec = pl.BlockSpec((tq, LANES), lambda b, h, qi: (b * nq + qi, h))
    kv_spec = pl.BlockSpec((mlen, LANES), lambda b, h, qi: (b, h))
    return pl.pallas_call(
        kern, name=name, grid=(batch, heads, nq), in_specs=[q_spec, kv_spec, kv_spec], out_specs=q_spec,
        out_shape=jax.ShapeDtypeStruct((t, width), F32),
        compiler_params=_params(("parallel", "parallel", "parallel")),
    )(qn, kn, vb)


def _mem_bwd(qn, kn, vb, do, batch, seq, mlen, name):
    t, width = qn.shape
    heads = width // LANES
    tq = _tile(seq, 512, LANES)
    nq = seq // tq
    scale = MEM_HEAD_DIM ** -0.5

    def kern(q_ref, k_ref, v_ref, do_ref, dq_ref, dk_ref, dv_ref):
        qi = pl.program_id(2)
        q, k, v = q_ref[...], k_ref[...], v_ref[...]
        dob = do_ref[...].astype(BF16)
        s = lax.dot_general(q, k, _DIMS["nt"], preferred_element_type=F32) * scale
        e = jnp.exp(s - jnp.max(s, axis=-1, keepdims=True))
        p = e / jnp.sum(e, axis=-1, keepdims=True)
        dp = lax.dot_general(dob, v, _DIMS["nt"], preferred_element_type=F32)
        ds = p * (dp - jnp.sum(p * dp, axis=-1, keepdims=True))
        dsb = (ds * scale).astype(BF16)
        dq_ref[...] = lax.dot_general(dsb, k, _DIMS["nn"], preferred_element_type=F32)
        dk = lax.dot_general(dsb, q, _DIMS["tn"], preferred_element_type=F32)
        dv = lax.dot_general(p.astype(BF16), dob, _DIMS["tn"], preferred_element_type=F32)

        @pl.when(qi == 0)
        def _():
            dk_ref[...] = dk
            dv_ref[...] = dv

        @pl.when(qi > 0)
        def _():
            dk_ref[...] += dk
            dv_ref[...] += dv

    q_spec = pl.BlockSpec((tq, LANES), lambda b, h, qi: (b * nq + qi, h))
    kv_spec = pl.BlockSpec((mlen, LANES), lambda b, h, qi: (b, h))
    return pl.pallas_call(
        kern, name=name, grid=(batch, heads, nq), in_specs=[q_spec, kv_spec, kv_spec, q_spec],
        out_specs=[q_spec, kv_spec, kv_spec],
        out_shape=[jax.ShapeDtypeStruct((t, width), F32), jax.ShapeDtypeStruct(kn.shape, F32),
                   jax.ShapeDtypeStruct(kn.shape, F32)],
        compiler_params=_params(("parallel", "parallel", "arbitrary")),
    )(qn, kn, vb, do)


def _position():
    return lax.axis_index("x"), lax.axis_index("y"), lax.axis_index("c")


def _other_chips(x, y):
    return [(1 - x, y), (x, 1 - y), (1 - x, 1 - y)]


def _remote(src, dst, sems, to):
    send, recv = sems.remote()
    return pltpu.make_async_remote_copy(src_ref=src, dst_ref=dst, send_sem=send, recv_sem=recv,
                                        device_id=to, device_id_type=MESH)


def _gather_first(shards):
    n = len(shards)

    def plan(src, out, sems):
        x, y, c = _position()
        me = 4 * x + 2 * y + c
        peers = [(x, y, 1 - c)] + [(px, py, c) for px, py in _other_chips(x, y)]
        copies = []
        for i in range(n):
            copies.append(pltpu.make_async_copy(src[i], out[i].at[me], sems.one_local()))
            copies += [_remote(src[i], out[i].at[me], sems, to) for to in peers]
        return copies

    outs = [jax.ShapeDtypeStruct((N_DEV,) + s.shape, s.dtype) for s in shards]
    return _Comm(shards, outs, 4 * n, n, plan)


def _gather_second(bufs):
    n = len(bufs)

    def plan(src, out, sems):
        x, y, c = _position()
        copies = []
        for i in range(n):
            for px, py in _other_chips(x, y):
                blk = out[i].at[4 * px + 2 * py + c]
                copies.append(_remote(blk, blk, sems, (x, y, 1 - c)))
        return copies

    outs = [jax.ShapeDtypeStruct(b.shape, b.dtype) for b in bufs]
    return _Comm(bufs, outs, 3 * n, 0, plan, in_place=True)


def _scatter_first(partials):
    n = len(partials)

    def plan(src, out, sems):
        x, y, c = _position()
        return [_remote(src[i].at[2 * q + (1 - c)], out[i].at[q], sems, (x, y, 1 - c))
                for i in range(n) for q in range(4)]

    outs = [jax.ShapeDtypeStruct((4,) + g.shape[1:], g.dtype) for g in partials]
    return _Comm(partials, outs, 4 * n, 0, plan)


def _scatter_second(halves):
    n = len(halves)

    def plan(src, out, sems):
        x, y, c = _position()
        return [_remote(src[i].at[2 * px + py], out[i].at[r], sems, (px, py, c))
                for i in range(n) for r, (px, py) in enumerate(_other_chips(x, y))]

    outs = [jax.ShapeDtypeStruct((3,) + h.shape[1:], h.dtype) for h in halves]
    return _Comm(halves, outs, 3 * n, 0, plan)


def _comm_call(comm, name):
    ns, no = len(comm.srcs), len(comm.outs)

    def body(*refs):
        copies = comm.plan(list(refs[:ns]), list(refs[ns:ns + no]), _Sems(*refs[ns + no:]))
        for cp in copies:
            cp.start()
        for cp in copies:
            cp.wait()

    return pl.pallas_call(
        body, name=name, in_specs=[ANY] * ns, out_specs=[ANY] * no, out_shape=comm.outs,
        scratch_shapes=comm.scratch(), input_output_aliases=comm.aliases(0, 0),
    )(*comm.srcs)


def _all_gather_vmem(shard, name):
    first = _gather_first([shard])

    def body(x_ref, out_ref, send, recv, local):
        sems = _Sems(send, recv, local)
        copies = first.plan([x_ref], [out_ref], sems)
        for cp in copies:
            cp.start()
        x, y, c = _position()
        passed = []
        for j, (px, py) in enumerate(_other_chips(x, y)):
            copies[2 + j].wait_recv()
            blk = out_ref.at[4 * px + 2 * py + c]
            fwd = _remote(blk, blk, sems, (x, y, 1 - c))
            fwd.start()
            passed.append(fwd)
        copies[0].wait()
        copies[1].wait()
        for cp in copies[2:]:
            cp.wait_send()
        for cp in passed:
            cp.wait()

    vm = pl.BlockSpec(memory_space=pltpu.VMEM)
    return pl.pallas_call(
        body, name=name, in_specs=[vm], out_specs=vm,
        out_shape=jax.ShapeDtypeStruct((N_DEV,) + shard.shape, shard.dtype),
        scratch_shapes=[pltpu.SemaphoreType.DMA((7,)), pltpu.SemaphoreType.DMA((7,)), pltpu.SemaphoreType.DMA((1,))],
    )(shard)


def _rs_pair_add(partial, landed, where, name):
    _, r, c = partial.shape
    tm = _tile(r, 256, 16)

    def kern(where_ref, g_ref, l_ref, own_ref, hb_ref):
        s = g_ref[...] + l_ref[...]
        hb_ref[...] = s.astype(BF16)

        @pl.when(pl.program_id(1) == where_ref[1])
        def _():
            own_ref[...] = s

    grid_spec = pltpu.PrefetchScalarGridSpec(
        num_scalar_prefetch=1, grid=(r // tm, 4),
        in_specs=[pl.BlockSpec((None, tm, c), lambda i, q, wr: (2 * q + wr[0], i, 0)),
                  pl.BlockSpec((None, tm, c), lambda i, q, wr: (q, i, 0))],
        out_specs=[pl.BlockSpec((tm, c), lambda i, q, wr: (i, 0)),
                   pl.BlockSpec((None, tm, c), lambda i, q, wr: (q, i, 0))])
    return pl.pallas_call(
        kern, name=name, grid_spec=grid_spec,
        out_shape=[jax.ShapeDtypeStruct((r, c), F32), jax.ShapeDtypeStruct((4, r, c), BF16)],
        compiler_params=_params(("parallel", "arbitrary")),
    )(where, partial, landed)


def _adam_math(g, w, m, v):
    m = ADAM_B1 * m + (1.0 - ADAM_B1) * g
    v = ADAM_B2 * v + (1.0 - ADAM_B2) * (g * g)
    m_hat = m / (1.0 - ADAM_B1 ** ADAM_STEP)
    v_hat = v / (1.0 - ADAM_B2 ** ADAM_STEP)
    delta = -ADAM_LR * (m_hat / (jnp.sqrt(v_hat) + ADAM_EPS) + ADAM_WD * w)
    return delta, m, v


def _rs_finish_adam(own, landed, w, m, v, name, comm=None):
    r, c = own.shape
    tm = _tile(r, 128, 16)
    steps = r // tm

    def kern(*refs):
        ins, outs, _, c_src, c_out, c_sem = _split_refs(refs, 5, 4, comm)
        h_ref, l_ref, w_ref, m_ref, v_ref = ins
        if comm is not None:
            start_comm, wait_comm = comm.run(c_src, c_out, c_sem, pl.program_id(0) == 0,
                                             pl.program_id(0) == steps - 1)
            start_comm()
        g = ((h_ref[...] + l_ref[0].astype(F32)) + l_ref[1].astype(F32)) + l_ref[2].astype(F32)
        delta, m_new, v_new = _adam_math(g, w_ref[...], m_ref[...], v_ref[...])
        for ref, val in zip(outs, (g, delta, m_new, v_new)):
            ref[...] = val
        if comm is not None:
            wait_comm()

    flat = pl.BlockSpec((tm, c), lambda i: (i, 0))
    in_specs = [flat, pl.BlockSpec((3, tm, c), lambda i: (0, i, 0)), flat, flat, flat]
    out_shape = [jax.ShapeDtypeStruct((r, c), F32)] * 4
    if comm is None:
        return pl.pallas_call(kern, name=name, grid=(steps,), in_specs=in_specs, out_specs=[flat] * 4,
                              out_shape=out_shape, compiler_params=_params(("parallel",)))(own, landed, w, m, v)
    res_all = pl.pallas_call(
        kern, name=name, grid=(steps,), in_specs=in_specs + [ANY] * len(comm.srcs),
        out_specs=[flat] * 4 + [ANY] * len(comm.outs), out_shape=out_shape + comm.outs,
        scratch_shapes=comm.scratch(), compiler_params=_params(("arbitrary",)),
    )(own, landed, w, m, v, *comm.srcs)
    return list(res_all[:4]), list(res_all[4:])


def _allreduce_adam(gathered, w, m, v, name):
    _, r, c = gathered.shape

    def kern(a_ref, w_ref, m_ref, v_ref, g_out, d_out, m_out, v_out):
        g = a_ref[0]
        for j in range(1, N_DEV):
            g = g + a_ref[j]
        delta, m_new, v_new = _adam_math(g, w_ref[...], m_ref[...], v_ref[...])
        g_out[...] = g
        d_out[...] = delta
        m_out[...] = m_new
        v_out[...] = v_new

    return pl.pallas_call(
        kern, name=name, out_shape=[jax.ShapeDtypeStruct((r, c), F32)] * 4,
        compiler_params=_params(),
    )(gathered, w, m, v)


def _cols_from_blocks(g):
    n, k, nb = g.shape
    return jnp.transpose(g, (1, 0, 2)).reshape(k, n * nb)


def _blocks_from_cols(w):
    k, n = w.shape
    return jnp.transpose(w.reshape(k, N_DEV, n // N_DEV), (1, 0, 2))


def _pack_small(parts):
    flat = []
    for p in parts:
        v = p.reshape(-1)
        flat.append(jnp.pad(v, (0, (-v.shape[0]) % (8 * LANES))))
    return jnp.concatenate(flat).reshape(-1, LANES)


def _unpack_small(buf, like):
    out, pos = [], 0
    flat = buf.reshape(-1)
    for p in like:
        size = p.size
        out.append(flat[pos:pos + size].reshape(p.shape))
        pos += size + (-size) % (8 * LANES)
    return out


def kernel(x, mem, ffn1_norm, ffn1_w_gate_up, ffn1_w_down, mix_norm, mem_norm, w_in, b_forget, pool_w, pool_scale, w_pool_up, fox_q_norm, fox_k_norm, w_fox_o, w_mem_kv, mem_q_norm, mem_k_norm, w_mem_o, w_out, ffn2_norm, ffn2_w_gate_up, ffn2_w_down, loss_target, m_ffn1_norm, m_ffn1_w_gate_up, m_ffn1_w_down, m_mix_norm, m_mem_norm, m_w_in, m_b_forget, m_pool_w, m_pool_scale, m_w_pool_up, m_fox_q_norm, m_fox_k_norm, m_w_fox_o, m_w_mem_kv, m_mem_q_norm, m_mem_k_norm, m_w_mem_o, m_w_out, m_ffn2_norm, m_ffn2_w_gate_up, m_ffn2_w_down, v_ffn1_norm, v_ffn1_w_gate_up, v_ffn1_w_down, v_mix_norm, v_mem_norm, v_w_in, v_b_forget, v_pool_w, v_pool_scale, v_w_pool_up, v_fox_q_norm, v_fox_k_norm, v_w_fox_o, v_w_mem_kv, v_mem_q_norm, v_mem_k_norm, v_w_mem_o, v_w_out, v_ffn2_norm, v_ffn2_w_gate_up, v_ffn2_w_down):
    names = ["ffn1_norm", "ffn1_w_gate_up", "ffn1_w_down", "mix_norm", "mem_norm", "w_in", "b_forget", "pool_w",
             "pool_scale", "w_pool_up", "fox_q_norm", "fox_k_norm", "w_fox_o", "w_mem_kv", "mem_q_norm",
             "mem_k_norm", "w_mem_o", "w_out", "ffn2_norm", "ffn2_w_gate_up", "ffn2_w_down"]
    w_args = [ffn1_norm, ffn1_w_gate_up, ffn1_w_down, mix_norm, mem_norm, w_in, b_forget, pool_w, pool_scale,
              w_pool_up, fox_q_norm, fox_k_norm, w_fox_o, w_mem_kv, mem_q_norm, mem_k_norm, w_mem_o, w_out,
              ffn2_norm, ffn2_w_gate_up, ffn2_w_down]
    m_args = [m_ffn1_norm, m_ffn1_w_gate_up, m_ffn1_w_down, m_mix_norm, m_mem_norm, m_w_in, m_b_forget, m_pool_w,
              m_pool_scale, m_w_pool_up, m_fox_q_norm, m_fox_k_norm, m_w_fox_o, m_w_mem_kv, m_mem_q_norm,
              m_mem_k_norm, m_w_mem_o, m_w_out, m_ffn2_norm, m_ffn2_w_gate_up, m_ffn2_w_down]
    v_args = [v_ffn1_norm, v_ffn1_w_gate_up, v_ffn1_w_down, v_mix_norm, v_mem_norm, v_w_in, v_b_forget, v_pool_w,
              v_pool_scale, v_w_pool_up, v_fox_q_norm, v_fox_k_norm, v_w_fox_o, v_w_mem_kv, v_mem_q_norm,
              v_mem_k_norm, v_w_mem_o, v_w_out, v_ffn2_norm, v_ffn2_w_gate_up, v_ffn2_w_down]
    W = dict(zip(names, w_args))
    M = dict(zip(names, m_args))
    V = dict(zip(names, v_args))

    batch, seq, d = x.shape
    mlen = mem.shape[1]
    t = batch * seq
    gate_w = 3 * d
    z_gate, z_q = 0, gate_w
    z_k, z_v = z_q + FOX_WIDTH, z_q + 2 * FOX_WIDTH
    z_u = z_q + 3 * FOX_WIDTH
    z_qm = z_u + POOL_WIDTH
    z_f = z_qm + MEM_WIDTH
    z_width = -(-(z_f + F_PAD) // 512) * 512

    x2d = x.reshape(t, d)
    mem2d = mem.reshape(batch * mlen, d)
    tgt2d = loss_target.reshape(t, d)

    big = ["ffn1_w_gate_up", "ffn1_w_down", "w_in", "w_pool_up", "w_fox_o", "w_mem_kv", "w_mem_o", "w_out",
           "ffn2_w_gate_up", "ffn2_w_down"]
    col_sharded = {"ffn1_w_gate_up", "ffn2_w_gate_up", "w_in", "w_pool_up", "w_fox_o", "w_mem_o"}
    mixer_small = ["w_pool_up", "w_fox_o", "w_mem_kv", "w_mem_o", "w_out"]
    shard = {n: W[n][0].astype(BF16) for n in big}

    def rows_of(g):
        return g.reshape(-1, g.shape[2])

    gu_caps_nn, gu_caps_nt, gu_caps_tn = (1024, 1408, 2048), (1024, 1024, 1408), (1024, 1408, 1024)

    (wgu1,) = _comm_call(_gather_first([shard["ffn1_w_gate_up"]]), "ag_first")
    h1, (wgu1,) = _rms_fwd(x2d, W["ffn1_norm"], "ffn1_rms", comm=_gather_second([wgu1]))
    group_a = ["ffn1_w_down"] + mixer_small
    gu1, bufs_a = _mm(h1, wgu1, "nn", "ffn1_gu", caps=gu_caps_nn, b_blocks=True,
                      comm=_gather_first([shard[n] for n in group_a]))
    a1, bufs_a = _swiglu_fwd(gu1, "ffn1_swiglu", comm=_gather_second(bufs_a))
    full = {n: (_cols_from_blocks(g) if n in col_sharded else rows_of(g)) for n, g in zip(group_a, bufs_a)}
    x1, (w_in_g,) = _mm(a1, full["ffn1_w_down"], "nn", "ffn1_down", scale=0.5, res=x2d,
                        comm=_gather_first([shard["w_in"]]))
    h2, (w_in_g,) = _rms_fwd(x1, W["mix_norm"], "mix_rms", comm=_gather_second([w_in_g]))

    o_u, o_q, o_v = 0, POOL_WIDTH, POOL_WIDTH + 2 * FOX_WIDTH
    o_f = o_v + FOX_WIDTH
    o_qm = o_f + FOX_HEADS
    o_g = o_qm + MEM_WIDTH
    wi = _cols_from_blocks(w_in_g)
    w_in_pad = jnp.concatenate(
        [wi[:, o_g:o_g + gate_w], wi[:, o_q:o_q + 3 * FOX_WIDTH], wi[:, o_u:o_u + POOL_WIDTH],
         wi[:, o_qm:o_qm + MEM_WIDTH], wi[:, o_f:o_f + FOX_HEADS],
         jnp.zeros((d, z_width - z_f - FOX_HEADS), BF16)], axis=1)
    z, (wgu2,) = _mm(h2, w_in_pad, "nn", "mix_in", comm=_gather_first([shard["ffn2_w_gate_up"]]))

    pool_w_b = W["pool_w"][0].astype(BF16)

    def pool_fwd_body(u, pw, ps):
        row = lax.broadcasted_iota(jnp.int32, (seq, 1), 0)
        diffs, mixed = [], []
        for g in range(POOL_GROUPS):
            ug = u[:, g * POOL_GROUP_DIM:(g + 1) * POOL_GROUP_DIM]
            cnt = jnp.minimum(row + 1, POOL_WINDOWS[g]).astype(F32)
            diff = _window_sum(ug, g + 1, True) / cnt - ug
            diffs.append(diff)
            mixed.append(lax.dot_general(diff.astype(BF16), pw[g], _DIMS["nn"], preferred_element_type=F32))
        diffs = jnp.concatenate(diffs, axis=1)
        mixed = jnp.concatenate(mixed, axis=1)
        return mixed * ps, diffs, mixed

    ypp, pool_diff, pool_mixed = _rowwise(
        pool_fwd_body, [(z, z_u, POOL_WIDTH)], [pool_w_b, W["pool_scale"]],
        [(POOL_WIDTH, BF16), (POOL_WIDTH, BF16), (POOL_WIDTH, F32)], [], "pool_fwd", seq)
    y_pool = _mm(ypp, full["w_pool_up"], "nn", "pool_up")

    gq = jnp.tile(W["fox_q_norm"], (1, FOX_HEADS))
    gk = jnp.tile(W["fox_k_norm"], (1, FOX_HEADS))
    b_pad = jnp.pad(W["b_forget"], ((0, 0), (0, F_PAD - FOX_HEADS)))

    def fox_prep_body(q, k, v, f, gqv, gkv, bv):
        return (_head_norm(q, gqv, FOX_HEAD_DIM), _head_norm(k, gkv, FOX_HEAD_DIM), v, _log_sigmoid(f + bv))

    tm_e = _tile(t, 256, 16)
    (qn, kn, vb, logf), (wgu2,) = _rowwise(
        fox_prep_body, [(z, z_q, FOX_WIDTH), (z, z_k, FOX_WIDTH), (z, z_v, FOX_WIDTH), (z, z_f, F_PAD)],
        [gq, gk, b_pad], [(FOX_WIDTH, BF16), (FOX_WIDTH, BF16), (FOX_WIDTH, BF16), (F_PAD, F32)], [], "fox_prep", tm_e,
        comm=_gather_second([wgu2]))
    csum = _rowwise(lambda v: (_scan_rows(v, True),), [_whole(logf)], [], [(F_PAD, F32)], [], "fox_cumsum", seq)[0]
    cks = jnp.transpose(csum.reshape(batch, seq, F_PAD)[:, :, :FOX_HEADS], (0, 2, 1)).reshape(
        batch * FOX_HEADS // 2, 2, seq)
    (o_fox, lse), (wd2_g,) = _fox_fwd(qn, kn, vb, cks, batch, seq, "fox_fwd",
                                      comm=_gather_first([shard["ffn2_w_down"]]))
    y_fox = _mm(o_fox, full["w_fox_o"], "nn", "fox_o")

    memn = _rms_fwd(mem2d, W["mem_norm"], "mem_rms")
    kv = _mm(memn, full["w_mem_kv"], "nn", "mem_kv")
    gqm = jnp.tile(W["mem_q_norm"], (1, MEM_HEADS))
    gkm = jnp.tile(W["mem_k_norm"], (1, MEM_HEADS))
    qmn = _rowwise(lambda q, g: (_head_norm(q, g, MEM_HEAD_DIM),), [(z, z_qm, MEM_WIDTH)], [gqm],
                   [(MEM_WIDTH, BF16)], [], "memq_prep", tm_e)[0]
    kmn, vmb = _rowwise(lambda k, v, g: (_head_norm(k, g, MEM_HEAD_DIM), v),
                        [(kv, 0, MEM_WIDTH), (kv, MEM_WIDTH, MEM_WIDTH)], [gkm],
                        [(MEM_WIDTH, BF16), (MEM_WIDTH, BF16)], [], "memk_prep", _tile(batch * mlen, 256, 16))
    o_mem = _mem_fwd(qmn, kmn, vmb, batch, seq, mlen, "mem_fwd")
    y_mem = _mm(o_mem, full["w_mem_o"], "nn", "mem_o")

    def gate_fwd_body(gp, gf, gm, yp, yf, ym):
        return ((_sigmoid(gp) * yp + _sigmoid(gf) * yf) + _sigmoid(gm) * ym,)

    tm_g = _tile(t, 128, 16)
    (merged,), (wd2_g,) = _rowwise(
        gate_fwd_body, [(z, 0, d), (z, d, d), (z, 2 * d, d), _whole(y_pool), _whole(y_fox), _whole(y_mem)],
        [], [(d, BF16)], [], "gate_fwd", tm_g, comm=_gather_second([wd2_g]))
    wd2 = rows_of(wd2_g)
    x2 = _mm(merged, full["w_out"], "nn", "mix_out", res=x1)

    h3 = _rms_fwd(x2, W["ffn2_norm"], "ffn2_rms")
    gu2 = _mm(h3, wgu2, "nn", "ffn2_gu", caps=gu_caps_nn, b_blocks=True)
    a2 = _swiglu_fwd(gu2, "ffn2_swiglu")
    x3 = _mm(a2, wd2, "nn", "ffn2_down", scale=0.5, res=x2)
    dy, loss_part = _loss_head(x3, tgt2d, "loss")
    loss = lax.psum(loss_part[0, 0], ("x", "y", "c"))

    cx, cy, cc = _position()
    where = jnp.stack([cc, 2 * cx + cy]).astype(jnp.int32)
    G, own, landed = {}, {}, {}

    def row_blocks(g):
        return g.reshape(N_DEV, g.shape[0] // N_DEV, g.shape[1])

    def pair_add(n, partial, from_core):
        own[n], chip_sums = _rs_pair_add(partial, from_core, where, "rs_add_" + n)
        return chip_sums

    da2 = _mm(dy, wd2, "nt", "ffn2_da", scale=0.5)
    p_wd2 = row_blocks(_mm(a2, dy, "tn", "ffn2_dwd", scale=0.5))
    dgu2, (l_wd2,) = _swiglu_bwd(gu2, da2, "ffn2_dswiglu", comm=_scatter_first([p_wd2]))
    s_wd2 = pair_add("ffn2_w_down", p_wd2, l_wd2)
    p_wgu2, (landed["ffn2_w_down"],) = _mm(h3, dgu2, "tn", "ffn2_dwgu", caps=gu_caps_tn, out_blocks=True,
                                           comm=_scatter_second([s_wd2]))
    dh3, (l_wgu2,) = _mm(dgu2, wgu2, "nt", "ffn2_dh", caps=gu_caps_nt, b_blocks=True,
                         comm=_scatter_first([p_wgu2]))
    s_wgu2 = pair_add("ffn2_w_gate_up", p_wgu2, l_wgu2)
    dx2, G["ffn2_norm"] = _rms_bwd(x2, W["ffn2_norm"], dh3, dy, "ffn2_drms")

    dmerged = _mm(dx2, full["w_out"], "nt", "mix_out_dx")
    P = {"w_out": row_blocks(_mm(merged, dx2, "tn", "mix_out_dw"))}

    def gate_bwd_body(gp, gf, gm, yp, yf, ym, dm):
        outs_dl, outs_dy = [], []
        for gl, yv in ((gp, yp), (gf, yf), (gm, ym)):
            s = _sigmoid(gl)
            outs_dl.append((dm * yv) * (s * (1.0 - s)))
            outs_dy.append(dm * s)
        return (jnp.concatenate(outs_dl, axis=1), *outs_dy)

    dz_gate, dy_pool, dy_fox, dy_mem = _rowwise(
        gate_bwd_body, [(z, 0, d), (z, d, d), (z, 2 * d, d), _whole(y_pool), _whole(y_fox), _whole(y_mem), _whole(dmerged)],
        [], [(gate_w, BF16), (d, BF16), (d, BF16), (d, BF16)], [], "gate_bwd", tm_g)

    dypp = _mm(dy_pool, full["w_pool_up"], "nt", "pool_up_dx")
    P["w_pool_up"] = _blocks_from_cols(_mm(ypp, dy_pool, "tn", "pool_up_dw"))

    def pool_bwd_body(dyv, mixed, diff, pw, ps):
        row = lax.broadcasted_iota(jnp.int32, (seq, 1), 0)
        d_scale = jnp.sum(dyv * mixed, axis=0, keepdims=True)
        dmix = (dyv * ps).astype(BF16)
        du, dpw = [], []
        for g in range(POOL_GROUPS):
            sl = slice(g * POOL_GROUP_DIM, (g + 1) * POOL_GROUP_DIM)
            dmg = dmix[:, sl]
            ddiff = lax.dot_general(dmg, pw[g], _DIMS["nt"], preferred_element_type=F32)
            dpw.append(lax.dot_general(diff[:, sl], dmg, _DIMS["tn"], preferred_element_type=F32))
            cnt = jnp.minimum(row + 1, POOL_WINDOWS[g]).astype(F32)
            du.append(_window_sum(ddiff / cnt, g + 1, False) - ddiff)
        return jnp.concatenate(du, axis=1), d_scale, jnp.concatenate(dpw, axis=0)

    dz_u, G["pool_scale"], d_pool_w = _rowwise(
        pool_bwd_body, [_whole(dypp), _whole(pool_mixed), _whole(pool_diff)], [pool_w_b, W["pool_scale"]],
        [(POOL_WIDTH, BF16)], [((1, POOL_WIDTH), F32), ((POOL_WIDTH, POOL_GROUP_DIM), F32)],
        "pool_bwd", seq)
    G["pool_w"] = d_pool_w.reshape(1, POOL_GROUPS, POOL_GROUP_DIM, POOL_GROUP_DIM)

    do_fox = _mm(dy_fox, full["w_fox_o"], "nt", "fox_o_dx")
    P["w_fox_o"] = _blocks_from_cols(_mm(o_fox, dy_fox, "tn", "fox_o_dw"))
    (dkn, dz_v, dqn, dcks, dcq), (landed["ffn2_w_gate_up"],) = _fox_bwd(
        qn, kn, vb, cks, o_fox, lse, do_fox, batch, seq, "fox_bwd", comm=_scatter_second([s_wgu2]))
    dcs = jnp.transpose(dcks.reshape(batch, FOX_HEADS, seq), (0, 2, 1)).reshape(t, FOX_HEADS)
    dcs = jnp.pad(dcs, ((0, 0), (0, F_PAD - FOX_HEADS)))
    dcq = jnp.pad(dcq.reshape(t, FOX_HEADS, FOX_HEAD_DIM)[:, :, 0], ((0, 0), (0, F_PAD - FOX_HEADS)))

    def fox_f_bwd_body(dc, dc_rows, f, bv):
        lane = lax.broadcasted_iota(jnp.int32, (1, F_PAD), 1)
        dlogf = _scan_rows(dc + dc_rows, False)
        df = jnp.where(lane < FOX_HEADS, dlogf * _sigmoid(-(f + bv)), 0.0)
        return df, jnp.sum(df, axis=0, keepdims=True)

    dz_f, db_pad = _rowwise(fox_f_bwd_body, [_whole(dcs), _whole(dcq), (z, z_f, F_PAD)], [b_pad], [(F_PAD, BF16)],
                            [((1, F_PAD), F32)], "fox_f_bwd", seq)
    G["b_forget"] = db_pad[:, :FOX_HEADS]

    def fox_qk_bwd_body(q, k, dq, dk, gqv, gkv):
        dqr, dgq = _head_norm_bwd(q, gqv, dq, FOX_HEAD_DIM)
        dkr, dgk = _head_norm_bwd(k, gkv, dk, FOX_HEAD_DIM)
        return dqr, dkr, dgq, dgk

    dz_q, dz_k, dgq_t, dgk_t = _rowwise(
        fox_qk_bwd_body, [(z, z_q, FOX_WIDTH), (z, z_k, FOX_WIDTH), _whole(dqn), _whole(dkn)], [gq, gk],
        [(FOX_WIDTH, BF16), (FOX_WIDTH, BF16)], [((1, FOX_WIDTH), F32), ((1, FOX_WIDTH), F32)], "fox_qk_bwd", tm_e)
    G["fox_q_norm"] = jnp.sum(dgq_t.reshape(FOX_HEADS, FOX_HEAD_DIM), axis=0, keepdims=True)
    G["fox_k_norm"] = jnp.sum(dgk_t.reshape(FOX_HEADS, FOX_HEAD_DIM), axis=0, keepdims=True)

    do_mem = _mm(dy_mem, full["w_mem_o"], "nt", "mem_o_dx")
    P["w_mem_o"] = _blocks_from_cols(_mm(o_mem, dy_mem, "tn", "mem_o_dw"))
    dqmn, dkmn, dvm = _mem_bwd(qmn, kmn, vmb, do_mem, batch, seq, mlen, "mem_bwd")
    dz_qm, dgqm_t = _rowwise(lambda q, dq, g: _head_norm_bwd(q, g, dq, MEM_HEAD_DIM),
                             [(z, z_qm, MEM_WIDTH), _whole(dqmn)], [gqm], [(MEM_WIDTH, BF16)],
                             [((1, MEM_WIDTH), F32)], "memq_bwd", tm_e)

    def memk_bwd_body(k, dk, dv, g):
        dkr, dg = _head_norm_bwd(k, g, dk, MEM_HEAD_DIM)
        return jnp.concatenate([dkr, dv], axis=1), dg

    dkv, dgkm_t = _rowwise(memk_bwd_body, [(kv, 0, MEM_WIDTH), _whole(dkmn), _whole(dvm)], [gkm],
                           [(2 * MEM_WIDTH, BF16)], [((1, MEM_WIDTH), F32)], "memk_bwd", _tile(batch * mlen, 256, 16))
    G["mem_q_norm"] = jnp.sum(dgqm_t.reshape(MEM_HEADS, MEM_HEAD_DIM), axis=0, keepdims=True)
    G["mem_k_norm"] = jnp.sum(dgkm_t.reshape(MEM_HEADS, MEM_HEAD_DIM), axis=0, keepdims=True)
    P["w_mem_kv"] = row_blocks(_mm(memn, dkv, "tn", "mem_kv_dw"))
    dmemn = _mm(dkv, full["w_mem_kv"], "nt", "mem_kv_dx")
    _, G["mem_norm"] = _rms_bwd(mem2d, W["mem_norm"], dmemn, None, "mem_drms")

    dz = jnp.concatenate([dz_gate, dz_q, dz_k, dz_v, dz_u, dz_qm, dz_f,
                          jnp.zeros((t, z_width - z_f - F_PAD), BF16)], axis=1)
    d_w_in_pad, l_small = _mm(h2, dz, "tn", "mix_in_dw", comm=_scatter_first([P[n] for n in mixer_small]))
    s_small = [pair_add(n, P[n], l) for n, l in zip(mixer_small, l_small)]
    p = d_w_in_pad
    p_w_in = _blocks_from_cols(jnp.concatenate(
        [p[:, z_u:z_u + POOL_WIDTH], p[:, z_q:z_q + 3 * FOX_WIDTH], p[:, z_f:z_f + FOX_HEADS],
         p[:, z_qm:z_qm + MEM_WIDTH], p[:, z_gate:z_gate + gate_w]], axis=1))
    dh2, rest = _mm(dz, w_in_pad, "nt", "mix_in_dx", caps=(1024, 1024, 1792),
                    comm=_join(_scatter_second(s_small), _scatter_first([p_w_in])))
    for n, l in zip(mixer_small, rest[:len(mixer_small)]):
        landed[n] = l
    s_w_in = pair_add("w_in", p_w_in, rest[-1])
    dx1, G["mix_norm"] = _rms_bwd(x1, W["mix_norm"], dh2, dx2, "mix_drms")

    wd1 = full["ffn1_w_down"]
    da1 = _mm(dx1, wd1, "nt", "ffn1_da", scale=0.5)
    dgu1 = _swiglu_bwd(gu1, da1, "ffn1_dswiglu")
    p_wgu1, (landed["w_in"],) = _mm(h1, dgu1, "tn", "ffn1_dwgu", caps=gu_caps_tn, out_blocks=True,
                                    comm=_scatter_second([s_w_in]))
    d_wd1, (l_wgu1,) = _mm(a1, dx1, "tn", "ffn1_dwd", scale=0.5, comm=_scatter_first([p_wgu1]))
    p_wd1 = row_blocks(d_wd1)
    s_wgu1 = pair_add("ffn1_w_gate_up", p_wgu1, l_wgu1)
    dh1, (landed["ffn1_w_gate_up"],) = _mm(dgu1, wgu1, "nt", "ffn1_dh", caps=gu_caps_nt, b_blocks=True,
                                           comm=_scatter_second([s_wgu1]))
    (dx0, G["ffn1_norm"]), (l_wd1,) = _rms_bwd(x2d, W["ffn1_norm"], dh1, dx1, "ffn1_drms",
                                               comm=_scatter_first([p_wd1]))
    s_wd1 = pair_add("ffn1_w_down", p_wd1, l_wd1)
    grad_x = dx0.reshape(batch, seq, d)

    out_g, out_d, out_m, out_v = {}, {}, {}, {}

    def adam(n, comm=None):
        res = _rs_finish_adam(own[n], landed[n], W[n][0], M[n][0], V[n][0], "adam_" + n, comm=comm)
        extra = None
        if comm is not None:
            res, extra = res
        out_g[n], out_d[n], out_m[n], out_v[n] = [r[None] for r in res]
        return extra

    (landed["ffn1_w_down"],) = adam("ffn2_w_gate_up", comm=_scatter_second([s_wd1]))
    for n in ["ffn2_w_down", "w_in"] + mixer_small + ["ffn1_w_gate_up", "ffn1_w_down"]:
        adam(n)

    small = [n for n in names if n not in big]
    g_small = _pack_small([G[n].reshape(W[n].shape) for n in small])
    all_small = _all_gather_vmem(g_small, "ag_small_grads")
    res = _allreduce_adam(all_small, _pack_small([W[n] for n in small]), _pack_small([M[n] for n in small]),
                          _pack_small([V[n] for n in small]), "adam_small")
    like = [W[n] for n in small]
    for dst, buf in zip((out_g, out_d, out_m, out_v), res):
        for n, a in zip(small, _unpack_small(buf, like)):
            dst[n] = a

    return (loss, grad_x, *[out_g[n] for n in names], *[out_d[n] for n in names],
            *[out_m[n] for n in names], *[out_v[n] for n in names])
```

```python
import functools

import jax
import jax.numpy as jnp
from jax import lax
from jax.experimental import pallas as pl
from jax.experimental.pallas import tpu as pltpu

F32 = jnp.float32
BF16 = jnp.bfloat16
MESH = pl.DeviceIdType.MESH

N_DEV = 8
EPS = 1e-6
FOX_HEADS = 16
FOX_HEAD_DIM = 64
FOX_WIDTH = FOX_HEADS * FOX_HEAD_DIM
MEM_HEADS = 4
MEM_HEAD_DIM = 128
MEM_WIDTH = MEM_HEADS * MEM_HEAD_DIM
POOL_GROUPS = 4
POOL_GROUP_DIM = 128
POOL_WIDTH = POOL_GROUPS * POOL_GROUP_DIM
POOL_WINDOWS = (2, 4, 8, 16)
LANES = 128
F_PAD = LANES

ADAM_LR = 0.001
ADAM_B1 = 0.9
ADAM_B2 = 0.999
ADAM_EPS = 1e-08
ADAM_WD = 0.01
ADAM_STEP = 10

VMEM_LIMIT = 56 * 1024 * 1024
NEG = -1e30

ANY = pl.BlockSpec(memory_space=pl.ANY)


def _params(sem=None):
    return pltpu.CompilerParams(dimension_semantics=sem, vmem_limit_bytes=VMEM_LIMIT)


MXU_DIM = 256


def _tile(dim, cap, align):
    best = None
    t = align
    while t <= min(dim, cap):
        if dim % t == 0:
            best = t
        t += align
    return dim if best is None else best


def _mxu_tile(dim, cap):
    wide, fine = _tile(dim, cap, MXU_DIM), _tile(dim, cap, LANES)
    whole_widths = wide % MXU_DIM == 0 and wide <= cap
    return wide if whole_widths and fine < 2 * wide else fine


class _Sems:
    def __init__(self, send, recv, local):
        self.send, self.recv, self.local = send, recv, local
        self.n_remote = self.n_local = 0

    def remote(self):
        i = self.n_remote
        self.n_remote += 1
        return self.send.at[i], self.recv.at[i]

    def one_local(self):
        i = self.n_local
        self.n_local += 1
        return self.local.at[i]


class _Comm:
    def __init__(self, srcs, outs, n_remote, n_local, plan, in_place=False):
        self.srcs, self.outs, self.n_remote, self.n_local = list(srcs), list(outs), n_remote, n_local
        self.plan, self.in_place = plan, in_place

    def aliases(self, n_in, n_out):
        return {n_in + i: n_out + i for i in range(len(self.srcs))} if self.in_place else {}

    def scratch(self):
        return [pltpu.SemaphoreType.DMA((self.n_remote,)), pltpu.SemaphoreType.DMA((self.n_remote,)),
                pltpu.SemaphoreType.DMA((max(self.n_local, 1),))]

    def run(self, src_refs, out_refs, sem_refs, first, last):
        def copies():
            return self.plan(list(src_refs), list(out_refs), _Sems(*sem_refs))

        return (lambda: _when(first, lambda: [cp.start() for cp in copies()]),
                lambda: _when(last, lambda: [cp.wait() for cp in copies()]))


def _when(cond, fn):
    @pl.when(cond)
    def _():
        fn()


def _join(*comms):
    comms = [c for c in comms if c is not None]
    assert all(not c.in_place for c in comms)

    def plan(src_refs, out_refs, sems):
        copies, si, oi = [], 0, 0
        for c in comms:
            copies += c.plan(src_refs[si:si + len(c.srcs)], out_refs[oi:oi + len(c.outs)], sems)
            si += len(c.srcs)
            oi += len(c.outs)
        return copies

    return _Comm(sum((c.srcs for c in comms), []), sum((c.outs for c in comms), []),
                 sum(c.n_remote for c in comms), sum(c.n_local for c in comms), plan)


def _split_refs(refs, n_in, n_out, comm):
    if comm is None:
        return refs[:n_in], refs[n_in:n_in + n_out], refs[n_in + n_out:], (), (), ()
    ns, no = len(comm.srcs), len(comm.outs)
    ins = refs[:n_in]
    srcs = refs[n_in:n_in + ns]
    outs = refs[n_in + ns:n_in + ns + n_out]
    couts = refs[n_in + ns + n_out:n_in + ns + n_out + no]
    rest = refs[n_in + ns + n_out + no:]
    return ins, outs, rest[:-3], srcs, couts, rest[-3:]


def _sigmoid(x):
    return 1.0 / (1.0 + jnp.exp(-x))


_DIMS = {"nn": (((1,), (0,)), ((), ())), "nt": (((1,), (1,)), ((), ())), "tn": (((0,), (0,)), ((), ()))}


def _mm(a, b, mode, name, out_dtype=F32, scale=1.0, res=None, caps=None, b_blocks=False, out_blocks=False,
        comm=None):
    nblk = N_DEV
    if b_blocks:
        _, r0, c0 = b.shape
        b_shape = (r0, nblk * c0)
    else:
        b_shape = b.shape
    if mode == "nn":
        (m, k), (k2, n) = a.shape, b_shape
    elif mode == "nt":
        (m, k), (n, k2) = a.shape, b_shape
    else:
        (k, m), (k2, n) = a.shape, b_shape
    assert k == k2, (name, a.shape, b.shape)
    cm, cn, ck = caps or ((1024, 512, 2048) if k <= 2048 else (1024, 1024, 2048))
    n_unit = n // nblk if (out_blocks or (b_blocks and mode == "nn")) else n
    k_unit = k // nblk if (b_blocks and mode == "nt") else k
    tm, tn, tk = _mxu_tile(m, cm), _mxu_tile(n_unit, cn), _mxu_tile(k_unit, ck)
    nk = k // tk
    npb, kpb = n_unit // tn, k_unit // tk
    if mode == "nn":
        a_spec = pl.BlockSpec((tm, tk), lambda i, j, kk: (i, kk))
        b_spec = pl.BlockSpec((tk, tn), lambda i, j, kk: (kk, j))
        if b_blocks:
            b_spec = pl.BlockSpec((None, tk, tn), lambda i, j, kk: (j // npb, kk, j % npb))
    elif mode == "nt":
        a_spec = pl.BlockSpec((tm, tk), lambda i, j, kk: (i, kk))
        b_spec = pl.BlockSpec((tn, tk), lambda i, j, kk: (j, kk))
        if b_blocks:
            b_spec = pl.BlockSpec((None, tn, tk), lambda i, j, kk: (kk // kpb, j, kk % kpb))
    else:
        assert not b_blocks
        a_spec = pl.BlockSpec((tk, tm), lambda i, j, kk: (kk, i))
        b_spec = pl.BlockSpec((tk, tn), lambda i, j, kk: (kk, j))
    if out_blocks:
        assert res is None
        o_spec = pl.BlockSpec((None, tm, tn), lambda i, j, kk: (j // npb, i, j % npb))
        o_shape = jax.ShapeDtypeStruct((nblk, m, n // nblk), out_dtype)
    else:
        o_spec = pl.BlockSpec((tm, tn), lambda i, j, kk: (i, j))
        o_shape = jax.ShapeDtypeStruct((m, n), out_dtype)
    in_specs = [a_spec, b_spec] + ([o_spec] if res is not None else [])
    n_in = len(in_specs)
    dims = _DIMS[mode]
    gm, gn = m // tm, n // tn

    def kern(*refs):
        ins, outs, scratch, c_src, c_out, c_sem = _split_refs(refs, n_in, 1, comm)
        a_ref, b_ref = ins[0], ins[1]
        res_ref = ins[2] if res is not None else None
        o_ref = outs[0]
        if comm is not None:
            i, j, kq = pl.program_id(0), pl.program_id(1), pl.program_id(2)
            first = jnp.logical_and(jnp.logical_and(i == 0, j == 0), kq == 0)
            last = jnp.logical_and(jnp.logical_and(i == gm - 1, j == gn - 1), kq == nk - 1)
            start_comm, wait_comm = comm.run(c_src, c_out, c_sem, first, last)
            start_comm()
        a_tile = a_ref[...].astype(BF16)
        if scale != 1.0:
            a_tile = a_tile * scale

        def product():
            return lax.dot_general(a_tile, b_ref[...].astype(BF16), dims, preferred_element_type=F32)

        if nk == 1:
            r = product()
            if res_ref is not None:
                r = res_ref[...] + r
            o_ref[...] = r.astype(out_dtype)
        else:
            acc_ref = scratch[0] if scratch else o_ref
            kk = pl.program_id(2)

            @pl.when(kk == 0)
            def _():
                acc_ref[...] = jnp.zeros_like(acc_ref) if res_ref is None else res_ref[...]

            acc_ref[...] += product()
            if scratch:
                @pl.when(kk == nk - 1)
                def _():
                    o_ref[...] = acc_ref[...].astype(out_dtype)
        if comm is not None:
            wait_comm()

    assert scale in (1.0, 0.5)
    args = (a, b) + ((res,) if res is not None else ())
    scratch_shapes = [pltpu.VMEM((tm, tn), F32)] if (nk > 1 and out_dtype != F32) else []
    if comm is None:
        return pl.pallas_call(
            kern, name=name, grid=(gm, gn, nk), in_specs=in_specs, out_specs=o_spec, out_shape=o_shape,
            scratch_shapes=scratch_shapes, compiler_params=_params(("parallel", "parallel", "arbitrary")),
        )(*args)
    res_all = pl.pallas_call(
        kern, name=name, grid=(gm, gn, nk), in_specs=in_specs + [ANY] * len(comm.srcs),
        out_specs=[o_spec] + [ANY] * len(comm.outs), out_shape=[o_shape] + comm.outs,
        scratch_shapes=scratch_shapes + comm.scratch(), input_output_aliases=comm.aliases(n_in, 1),
        compiler_params=_params(("arbitrary", "arbitrary", "arbitrary")),
    )(*args, *comm.srcs)
    return res_all[0], list(res_all[1:])


def _rowwise(body, ins, params, outs, accs, name, tm, comm=None):
    rows = ins[0][0].shape[0]
    assert rows % tm == 0, (name, rows, tm)
    in_specs = []
    for arr, off, width in ins:
        assert off % width == 0 and arr.shape[0] == rows, (name, arr.shape, off, width)
        in_specs.append(pl.BlockSpec((tm, width), functools.partial(lambda i, c: (i, c), c=off // width)))
    for p in params:
        in_specs.append(pl.BlockSpec(p.shape, functools.partial(lambda i, nd: (0,) * nd, nd=p.ndim)))
    out_specs = [pl.BlockSpec((tm, w), lambda i: (i, 0)) for w, _ in outs]
    out_specs += [pl.BlockSpec(s, functools.partial(lambda i, nd: (0,) * nd, nd=len(s))) for s, _ in accs]
    out_shape = [jax.ShapeDtypeStruct((rows, w), d) for w, d in outs]
    out_shape += [jax.ShapeDtypeStruct(s, d) for s, d in accs]
    n_in, n_par, n_out = len(ins), len(params), len(outs)

    steps = rows // tm

    def kern(*refs):
        in_refs, out_refs, _, c_src, c_out, c_sem = _split_refs(refs, n_in + n_par, n_out + len(accs), comm)
        first = pl.program_id(0) == 0
        if comm is not None:
            start_comm, wait_comm = comm.run(c_src, c_out, c_sem, first, pl.program_id(0) == steps - 1)
            start_comm()
        res = body(*[r[...] for r in in_refs])
        for r, v in zip(out_refs[:n_out], res[:n_out]):
            r[...] = v.astype(r.dtype)
        for r, v in zip(out_refs[n_out:], res[n_out:]):
            @pl.when(first)
            def _(r=r, v=v):
                r[...] = v.astype(r.dtype)

            @pl.when(jnp.logical_not(first))
            def _(r=r, v=v):
                r[...] += v.astype(r.dtype)
        if comm is not None:
            wait_comm()

    args = [a for a, _, _ in ins] + list(params)
    if comm is None:
        return pl.pallas_call(
            kern, name=name, grid=(steps,), in_specs=in_specs, out_specs=out_specs, out_shape=out_shape,
            compiler_params=_params(("arbitrary",) if accs else ("parallel",)),
        )(*args)
    res_all = pl.pallas_call(
        kern, name=name, grid=(steps,), in_specs=in_specs + [ANY] * len(comm.srcs),
        out_specs=out_specs + [ANY] * len(comm.outs), out_shape=out_shape + comm.outs,
        scratch_shapes=comm.scratch(), input_output_aliases=comm.aliases(len(args), len(out_shape)),
        compiler_params=_params(("arbitrary",)),
    )(*args, *comm.srcs)
    return list(res_all[:len(out_shape)]), list(res_all[len(out_shape):])


def _whole(x):
    return (x, 0, x.shape[1])


def _first(res, comm):
    return res[0] if comm is None else (res[0][0], res[1])


def _rms_fwd(x, gain, name, comm=None):
    def body(xv, g):
        r = lax.rsqrt(jnp.mean(xv * xv, axis=-1, keepdims=True) + EPS)
        return ((xv * r) * g,)

    return _first(_rowwise(body, [_whole(x)], [gain], [(x.shape[1], BF16)], [], name, _tile(x.shape[0], 256, 16),
                           comm=comm), comm)


def _rms_bwd(x, gain, dh, dres, name, comm=None):
    d = x.shape[1]

    def body(*vals):
        if dres is None:
            xv, dhv, g = vals
        else:
            xv, dhv, drv, g = vals
        r = lax.rsqrt(jnp.mean(xv * xv, axis=-1, keepdims=True) + EPS)
        xh = xv * r
        w = dhv * g
        dx = r * (w - xh * jnp.mean(w * xh, axis=-1, keepdims=True))
        if dres is not None:
            dx = drv + dx
        return dx, dx, jnp.sum(dhv * xh, axis=0, keepdims=True)

    ins = [_whole(x), _whole(dh)] + ([_whole(dres)] if dres is not None else [])
    return _rowwise(body, ins, [gain], [(d, F32), (d, BF16)], [((1, d), F32)], name, _tile(x.shape[0], 256, 16),
                    comm=comm)


def _swiglu_fwd(gu, name, comm=None):
    f = gu.shape[1] // 2

    def body(g, u):
        return ((g * _sigmoid(g)) * u,)

    return _first(_rowwise(body, [(gu, 0, f), (gu, f, f)], [], [(f, BF16)], [], name, _tile(gu.shape[0], 128, 16),
                           comm=comm), comm)


def _swiglu_bwd(gu, da, name, comm=None):
    f = gu.shape[1] // 2

    def body(g, u, dav):
        s = _sigmoid(g)
        dg = dav * u * (s * (1.0 + g * (1.0 - s)))
        du = dav * (g * s)
        return (jnp.concatenate([dg, du], axis=1),)

    return _first(_rowwise(body, [(gu, 0, f), (gu, f, f), _whole(da)], [], [(2 * f, BF16)], [], name,
                           _tile(gu.shape[0], 128, 16), comm=comm), comm)


def _loss_head(y, target, name):
    d = y.shape[1]

    def body(yv, tv):
        e = yv - tv
        part = jnp.sum(jnp.sum(e * e, axis=1, keepdims=True), axis=0, keepdims=True)
        return e / d, e / d, jnp.broadcast_to((0.5 / d) * part, (1, LANES))

    return _rowwise(body, [_whole(y), _whole(target)], [], [(d, F32), (d, BF16)], [((1, LANES), F32)], name,
                    _tile(y.shape[0], 256, 16))


def _head_mean(v, head_dim):
    cols = []
    lane = lax.broadcasted_iota(jnp.int32, (1, LANES), 1)
    for j in range(v.shape[1] // LANES):
        blk = v[:, j * LANES:(j + 1) * LANES]
        if head_dim == LANES:
            m = jnp.sum(blk, axis=-1, keepdims=True)
            cols.append(jnp.broadcast_to(m, blk.shape))
        else:
            lo = jnp.sum(jnp.where(lane < head_dim, blk, 0.0), axis=-1, keepdims=True)
            hi = jnp.sum(jnp.where(lane >= head_dim, blk, 0.0), axis=-1, keepdims=True)
            cols.append(jnp.where(lane < head_dim, lo, hi))
    return jnp.concatenate(cols, axis=1) / head_dim


def _head_norm(xv, g, head_dim):
    r = lax.rsqrt(_head_mean(xv * xv, head_dim) + EPS)
    return (xv * r) * g


def _head_norm_bwd(xv, g, dy, head_dim):
    r = lax.rsqrt(_head_mean(xv * xv, head_dim) + EPS)
    xh = xv * r
    w = dy * g
    dx = r * (w - xh * _head_mean(w * xh, head_dim))
    return dx, jnp.sum(dy * xh, axis=0, keepdims=True)


def _log_sigmoid(x):
    return jnp.minimum(x, 0.0) - jnp.log(1.0 + jnp.exp(-jnp.abs(x)))


def _shift_rows(v, sh, down):
    n = v.shape[0]
    row = lax.broadcasted_iota(jnp.int32, (n, 1), 0)
    if down:
        return jnp.where(row >= sh, pltpu.roll(v, sh, 0), 0.0)
    return jnp.where(row < n - sh, pltpu.roll(v, n - sh, 0), 0.0)


def _scan_rows(v, down):
    sh = 1
    while sh < v.shape[0]:
        v = v + _shift_rows(v, sh, down)
        sh *= 2
    return v


def _window_sum(v, steps, down):
    for s in range(steps):
        v = v + _shift_rows(v, 2 ** s, down)
    return v


def _fox_fwd(qn, kn, vb, cks, batch, seq, name, comm=None):
    t, width = qn.shape
    pairs = width // LANES
    tq = _tile(seq, 512, LANES)
    nq = seq // tq
    scale = FOX_HEAD_DIM ** -0.5

    def kern(*refs):
        (q_ref, k_ref, v_ref, c_ref), (o_ref, lse_ref), _, c_src, c_out, c_sem = _split_refs(refs, 4, 2, comm)
        qi = pl.program_id(2)
        if comm is not None:
            b_id, p_id = pl.program_id(0), pl.program_id(1)
            first = jnp.logical_and(jnp.logical_and(b_id == 0, p_id == 0), qi == 0)
            last = jnp.logical_and(jnp.logical_and(b_id == batch - 1, p_id == pairs - 1), qi == nq - 1)
            start_comm, wait_comm = comm.run(c_src, c_out, c_sem, first, last)
            start_comm()
        lane = lax.broadcasted_iota(jnp.int32, (1, LANES), 1)
        rowi = lax.broadcasted_iota(jnp.int32, (tq, tq), 0)
        coli = lax.broadcasted_iota(jnp.int32, (tq, tq), 1)
        q_all = q_ref[...]
        o_heads, lse_heads = [], []
        for h in range(2):
            hm = (lane < FOX_HEAD_DIM) if h == 0 else (lane >= FOX_HEAD_DIM)
            q = jnp.where(hm, q_all, jnp.zeros_like(q_all))

            def step(j, carry, h=h, q=q):
                m, l, acc = carry
                start = pl.multiple_of(j * tq, tq)
                k = k_ref[pl.ds(start, tq), :]
                v = v_ref[pl.ds(start, tq), :]
                s = lax.dot_general(q, k, _DIMS["nt"], preferred_element_type=F32) * scale
                s = s - c_ref[0, h:h + 1, pl.ds(start, tq)]
                s = jnp.where(qi * tq + rowi >= j * tq + coli, s, NEG)
                m_new = jnp.maximum(m, jnp.max(s, axis=-1, keepdims=True))
                alpha = jnp.exp(m - m_new)
                p = jnp.exp(s - m_new)
                l = alpha * l + jnp.sum(p, axis=-1, keepdims=True)
                acc = alpha * acc + lax.dot_general(p.astype(BF16), v, _DIMS["nn"], preferred_element_type=F32)
                return m_new, l, acc

            init = (jnp.full((tq, 1), NEG, F32), jnp.zeros((tq, 1), F32), jnp.zeros((tq, LANES), F32))
            m, l, acc = lax.fori_loop(0, qi + 1, step, init)
            o_heads.append(acc / l)
            lse_heads.append(jnp.broadcast_to(m + jnp.log(l), (tq, LANES)))
        o_ref[...] = jnp.where(lane < FOX_HEAD_DIM, o_heads[0], o_heads[1])
        lse_ref[...] = jnp.where(lane < FOX_HEAD_DIM, lse_heads[0], lse_heads[1])
        if comm is not None:
            wait_comm()

    q_spec = pl.BlockSpec((tq, LANES), lambda b, hp, qi: (b * nq + qi, hp))
    kv_spec = pl.BlockSpec((seq, LANES), lambda b, hp, qi: (b, hp))
    c_spec = pl.BlockSpec((1, 2, seq), lambda b, hp, qi: (b * pairs + hp, 0, 0))
    in_specs = [q_spec, kv_spec, kv_spec, c_spec]
    out_shape = [jax.ShapeDtypeStruct((t, width), F32), jax.ShapeDtypeStruct((t, width), F32)]
    if comm is None:
        return pl.pallas_call(
            kern, name=name, grid=(batch, pairs, nq), in_specs=in_specs, out_specs=[q_spec, q_spec],
            out_shape=out_shape, compiler_params=_params(("parallel", "parallel", "arbitrary")),
        )(qn, kn, vb, cks)
    res_all = pl.pallas_call(
        kern, name=name, grid=(batch, pairs, nq), in_specs=in_specs + [ANY] * len(comm.srcs),
        out_specs=[q_spec, q_spec] + [ANY] * len(comm.outs), out_shape=out_shape + comm.outs,
        scratch_shapes=comm.scratch(), compiler_params=_params(("arbitrary", "arbitrary", "arbitrary")),
    )(qn, kn, vb, cks, *comm.srcs)
    return list(res_all[:2]), list(res_all[2:])


def _fox_bwd(qn, kn, vb, cks, o, lse, do, batch, seq, name, comm=None):
    t, width = qn.shape
    pairs = width // LANES
    tk = _tile(seq, 512, LANES)
    nk = seq // tk
    scale = FOX_HEAD_DIM ** -0.5

    def kern(*refs):
        ins, outs, _, c_src, c_out, c_sem = _split_refs(refs, 7, 5, comm)
        k_ref, v_ref, q_ref, c_ref, o_ref, lse_ref, do_ref = ins
        dk_ref, dv_ref, dq_ref, dc_ref, dcq_ref = outs
        kj = pl.program_id(2)
        if comm is not None:
            b_id, p_id = pl.program_id(0), pl.program_id(1)
            first = jnp.logical_and(jnp.logical_and(b_id == 0, p_id == 0), kj == 0)
            last = jnp.logical_and(jnp.logical_and(b_id == batch - 1, p_id == pairs - 1), kj == nk - 1)
            start_comm, wait_comm = comm.run(c_src, c_out, c_sem, first, last)
            start_comm()
        lane = lax.broadcasted_iota(jnp.int32, (1, LANES), 1)
        rowi = lax.broadcasted_iota(jnp.int32, (tk, tk), 0)
        coli = lax.broadcasted_iota(jnp.int32, (tk, tk), 1)

        @pl.when(kj == 0)
        def _():
            dq_ref[...] = jnp.zeros_like(dq_ref)
            dcq_ref[...] = jnp.zeros_like(dcq_ref)

        k_all = k_ref[...]
        v_all = v_ref[...]
        kstart = pl.multiple_of(kj * tk, tk)
        dk_heads, dv_heads = [], []
        for h in range(2):
            hm = (lane < FOX_HEAD_DIM) if h == 0 else (lane >= FOX_HEAD_DIM)
            kh = jnp.where(hm, k_all, jnp.zeros_like(k_all))
            vh = jnp.where(hm, v_all, jnp.zeros_like(v_all))
            c_row = c_ref[0, h:h + 1, pl.ds(kstart, tk)]

            def step(qi, carry, h=h, hm=hm, kh=kh, vh=vh, c_row=c_row):
                dk_acc, dv_acc, dc_acc = carry
                start = pl.multiple_of(qi * tk, tk)
                q = q_ref[pl.ds(start, tk), :]
                dov = do_ref[pl.ds(start, tk), :]
                ov = o_ref[pl.ds(start, tk), :]
                lse_col = jnp.max(jnp.where(hm, lse_ref[pl.ds(start, tk), :], NEG), axis=-1, keepdims=True)
                dob = jnp.where(hm, dov, 0.0).astype(BF16)
                dcol = jnp.sum(dob.astype(F32) * ov, axis=-1, keepdims=True)
                s = lax.dot_general(q, kh, _DIMS["nt"], preferred_element_type=F32) * scale - c_row
                p = jnp.where(qi * tk + rowi >= kj * tk + coli, jnp.exp(s - lse_col), 0.0)
                dp = lax.dot_general(dob, vh, _DIMS["nt"], preferred_element_type=F32)
                ds = p * (dp - dcol)
                dsb = (ds * scale).astype(BF16)
                dv_acc = dv_acc + lax.dot_general(p.astype(BF16), dob, _DIMS["tn"], preferred_element_type=F32)
                dk_acc = dk_acc + lax.dot_general(dsb, q, _DIMS["tn"], preferred_element_type=F32)
                dq_part = lax.dot_general(dsb, kh, _DIMS["nn"], preferred_element_type=F32)
                dq_ref[pl.ds(start, tk), :] += dq_part
                dcq_ref[pl.ds(start, tk), :] += jnp.where(hm, jnp.sum(ds, axis=-1, keepdims=True), 0.0)
                dc_acc = dc_acc - jnp.sum(ds, axis=0, keepdims=True)
                return dk_acc, dv_acc, dc_acc

            init = (jnp.zeros((tk, LANES), F32), jnp.zeros((tk, LANES), F32), jnp.zeros((1, tk), F32))
            dk_acc, dv_acc, dc_acc = lax.fori_loop(kj, nk, step, init)
            dk_heads.append(dk_acc)
            dv_heads.append(dv_acc)
            dc_ref[0, h:h + 1, pl.ds(kstart, tk)] = dc_acc
        dk_ref[...] = jnp.where(lane < FOX_HEAD_DIM, dk_heads[0], dk_heads[1])
        dv_ref[...] = jnp.where(lane < FOX_HEAD_DIM, dv_heads[0], dv_heads[1]).astype(BF16)
        if comm is not None:
            wait_comm()

    kv_spec = pl.BlockSpec((tk, LANES), lambda b, hp, kj: (b * nk + kj, hp))
    full_spec = pl.BlockSpec((seq, LANES), lambda b, hp, kj: (b, hp))
    c_spec = pl.BlockSpec((1, 2, seq), lambda b, hp, kj: (b * pairs + hp, 0, 0))
    in_specs = [kv_spec, kv_spec, full_spec, c_spec, full_spec, full_spec, full_spec]
    out_specs = [kv_spec, kv_spec, full_spec, c_spec, full_spec]
    out_shape = [jax.ShapeDtypeStruct((t, width), F32), jax.ShapeDtypeStruct((t, width), BF16),
                 jax.ShapeDtypeStruct((t, width), F32), jax.ShapeDtypeStruct(cks.shape, F32),
                 jax.ShapeDtypeStruct((t, width), F32)]
    if comm is None:
        return pl.pallas_call(
            kern, name=name, grid=(batch, pairs, nk), in_specs=in_specs, out_specs=out_specs, out_shape=out_shape,
            compiler_params=_params(("parallel", "parallel", "arbitrary")),
        )(kn, vb, qn, cks, o, lse, do)
    res_all = pl.pallas_call(
        kern, name=name, grid=(batch, pairs, nk), in_specs=in_specs + [ANY] * len(comm.srcs),
        out_specs=out_specs + [ANY] * len(comm.outs), out_shape=out_shape + comm.outs,
        scratch_shapes=comm.scratch(), compiler_params=_params(("arbitrary", "arbitrary", "arbitrary")),
    )(kn, vb, qn, cks, o, lse, do, *comm.srcs)
    return list(res_all[:5]), list(res_all[5:])


def _mem_fwd(qn, kn, vb, batch, seq, mlen, name):
    t, width = qn.shape
    heads = width // LANES
    tq = _tile(seq, 512, LANES)
    nq = seq // tq
    scale = MEM_HEAD_DIM ** -0.5

    def kern(q_ref, k_ref, v_ref, o_ref):
        s = lax.dot_general(q_ref[...], k_ref[...], _DIMS["nt"], preferred_element_type=F32) * scale
        e = jnp.exp(s - jnp.max(s, axis=-1, keepdims=True))
        p = e / jnp.sum(e, axis=-1, keepdims=True)
        o_ref[...] = lax.dot_general(p.astype(BF16), v_ref[...], _DIMS["nn"], preferred_element_type=F32)

    q_spec = pl.BlockSpec((tq, LANES), lambda b, h, qi: (b * nq + qi, h))
    kv_spec = pl.BlockSpec((mlen, LANES), lambda b, h, qi: (b, h))
    return pl.pallas_call(
        kern, name=name, grid=(batch, heads, nq), in_specs=[q_spec, kv_spec, kv_spec], out_specs=q_spec,
        out_shape=jax.ShapeDtypeStruct((t, width), F32),
        compiler_params=_params(("parallel", "parallel", "parallel")),
    )(qn, kn, vb)


def _mem_bwd(qn, kn, vb, do, batch, seq, mlen, name):
    t, width = qn.shape
    heads = width // LANES
    tq = _tile(seq, 512, LANES)
    nq = seq // tq
    scale = MEM_HEAD_DIM ** -0.5

    def kern(q_ref, k_ref, v_ref, do_ref, dq_ref, dk_ref, dv_ref):
        qi = pl.program_id(2)
        q, k, v = q_ref[...], k_ref[...], v_ref[...]
        dob = do_ref[...].astype(BF16)
        s = lax.dot_general(q, k, _DIMS["nt"], preferred_element_type=F32) * scale
        e = jnp.exp(s - jnp.max(s, axis=-1, keepdims=True))
        p = e / jnp.sum(e, axis=-1, keepdims=True)
        dp = lax.dot_general(dob, v, _DIMS["nt"], preferred_element_type=F32)
        ds = p * (dp - jnp.sum(p * dp, axis=-1, keepdims=True))
        dsb = (ds * scale).astype(BF16)
        dq_ref[...] = lax.dot_general(dsb, k, _DIMS["nn"], preferred_element_type=F32)
        dk = lax.dot_general(dsb, q, _DIMS["tn"], preferred_element_type=F32)
        dv = lax.dot_general(p.astype(BF16), dob, _DIMS["tn"], preferred_element_type=F32)

        @pl.when(qi == 0)
        def _():
            dk_ref[...] = dk
            dv_ref[...] = dv

        @pl.when(qi > 0)
        def _():
            dk_ref[...] += dk
            dv_ref[...] += dv

    q_spec = pl.BlockSpec((tq, LANES), lambda b, h, qi: (b * nq + qi, h))
    kv_spec = pl.BlockSpec((mlen, LANES), lambda b, h, qi: (b, h))
    return pl.pallas_call(
        kern, name=name, grid=(batch, heads, nq), in_specs=[q_spec, kv_spec, kv_spec, q_spec],
        out_specs=[q_spec, kv_spec, kv_spec],
        out_shape=[jax.ShapeDtypeStruct((t, width), F32), jax.ShapeDtypeStruct(kn.shape, F32),
                   jax.ShapeDtypeStruct(kn.shape, F32)],
        compiler_params=_params(("parallel", "parallel", "arbitrary")),
    )(qn, kn, vb, do)


def _position():
    return lax.axis_index("x"), lax.axis_index("y"), lax.axis_index("c")


def _other_chips(x, y):
    return [(1 - x, y), (x, 1 - y), (1 - x, 1 - y)]


def _remote(src, dst, sems, to):
    send, recv = sems.remote()
    return pltpu.make_async_remote_copy(src_ref=src, dst_ref=dst, send_sem=send, recv_sem=recv,
                                        device_id=to, device_id_type=MESH)


def _gather_first(shards):
    n = len(shards)

    def plan(src, out, sems):
        x, y, c = _position()
        me = 4 * x + 2 * y + c
        peers = [(x, y, 1 - c)] + [(px, py, c) for px, py in _other_chips(x, y)]
        copies = []
        for i in range(n):
            copies.append(pltpu.make_async_copy(src[i], out[i].at[me], sems.one_local()))
            copies += [_remote(src[i], out[i].at[me], sems, to) for to in peers]
        return copies

    outs = [jax.ShapeDtypeStruct((N_DEV,) + s.shape, s.dtype) for s in shards]
    return _Comm(shards, outs, 4 * n, n, plan)


def _gather_second(bufs):
    n = len(bufs)

    def plan(src, out, sems):
        x, y, c = _position()
        copies = []
        for i in range(n):
            for px, py in _other_chips(x, y):
                blk = out[i].at[4 * px + 2 * py + c]
                copies.append(_remote(blk, blk, sems, (x, y, 1 - c)))
        return copies

    outs = [jax.ShapeDtypeStruct(b.shape, b.dtype) for b in bufs]
    return _Comm(bufs, outs, 3 * n, 0, plan, in_place=True)


def _scatter_first(partials):
    n = len(partials)

    def plan(src, out, sems):
        x, y, c = _position()
        return [_remote(src[i].at[2 * q + (1 - c)], out[i].at[q], sems, (x, y, 1 - c))
                for i in range(n) for q in range(4)]

    outs = [jax.ShapeDtypeStruct((4,) + g.shape[1:], g.dtype) for g in partials]
    return _Comm(partials, outs, 4 * n, 0, plan)


def _scatter_second(halves):
    n = len(halves)

    def plan(src, out, sems):
        x, y, c = _position()
        return [_remote(src[i].at[2 * px + py], out[i].at[r], sems, (px, py, c))
                for i in range(n) for r, (px, py) in enumerate(_other_chips(x, y))]

    outs = [jax.ShapeDtypeStruct((3,) + h.shape[1:], h.dtype) for h in halves]
    return _Comm(halves, outs, 3 * n, 0, plan)


def _comm_call(comm, name):
    ns, no = len(comm.srcs), len(comm.outs)

    def body(*refs):
        copies = comm.plan(list(refs[:ns]), list(refs[ns:ns + no]), _Sems(*refs[ns + no:]))
        for cp in copies:
            cp.start()
        for cp in copies:
            cp.wait()

    return pl.pallas_call(
        body, name=name, in_specs=[ANY] * ns, out_specs=[ANY] * no, out_shape=comm.outs,
        scratch_shapes=comm.scratch(), input_output_aliases=comm.aliases(0, 0),
    )(*comm.srcs)


def _all_gather_vmem(shard, name):
    first = _gather_first([shard])

    def body(x_ref, out_ref, send, recv, local):
        sems = _Sems(send, recv, local)
        copies = first.plan([x_ref], [out_ref], sems)
        for cp in copies:
            cp.start()
        x, y, c = _position()
        passed = []
        for j, (px, py) in enumerate(_other_chips(x, y)):
            copies[2 + j].wait_recv()
            blk = out_ref.at[4 * px + 2 * py + c]
            fwd = _remote(blk, blk, sems, (x, y, 1 - c))
            fwd.start()
            passed.append(fwd)
        copies[0].wait()
        copies[1].wait()
        for cp in copies[2:]:
            cp.wait_send()
        for cp in passed:
            cp.wait()

    vm = pl.BlockSpec(memory_space=pltpu.VMEM)
    return pl.pallas_call(
        body, name=name, in_specs=[vm], out_specs=vm,
        out_shape=jax.ShapeDtypeStruct((N_DEV,) + shard.shape, shard.dtype),
        scratch_shapes=[pltpu.SemaphoreType.DMA((7,)), pltpu.SemaphoreType.DMA((7,)), pltpu.SemaphoreType.DMA((1,))],
    )(shard)


def _rs_pair_add(partial, landed, where, name):
    _, r, c = partial.shape
    tm = _tile(r, 256, 16)

    def kern(where_ref, g_ref, l_ref, own_ref, hb_ref):
        s = g_ref[...] + l_ref[...]
        hb_ref[...] = s.astype(BF16)

        @pl.when(pl.program_id(1) == where_ref[1])
        def _():
            own_ref[...] = s

    grid_spec = pltpu.PrefetchScalarGridSpec(
        num_scalar_prefetch=1, grid=(r // tm, 4),
        in_specs=[pl.BlockSpec((None, tm, c), lambda i, q, wr: (2 * q + wr[0], i, 0)),
                  pl.BlockSpec((None, tm, c), lambda i, q, wr: (q, i, 0))],
        out_specs=[pl.BlockSpec((tm, c), lambda i, q, wr: (i, 0)),
                   pl.BlockSpec((None, tm, c), lambda i, q, wr: (q, i, 0))])
    return pl.pallas_call(
        kern, name=name, grid_spec=grid_spec,
        out_shape=[jax.ShapeDtypeStruct((r, c), F32), jax.ShapeDtypeStruct((4, r, c), BF16)],
        compiler_params=_params(("parallel", "arbitrary")),
    )(where, partial, landed)


def _adam_math(g, w, m, v):
    m = ADAM_B1 * m + (1.0 - ADAM_B1) * g
    v = ADAM_B2 * v + (1.0 - ADAM_B2) * (g * g)
    m_hat = m / (1.0 - ADAM_B1 ** ADAM_STEP)
    v_hat = v / (1.0 - ADAM_B2 ** ADAM_STEP)
    delta = -ADAM_LR * (m_hat / (jnp.sqrt(v_hat) + ADAM_EPS) + ADAM_WD * w)
    return delta, m, v


def _rs_finish_adam(own, landed, w, m, v, name, comm=None):
    r, c = own.shape
    tm = _tile(r, 128, 16)
    steps = r // tm

    def kern(*refs):
        ins, outs, _, c_src, c_out, c_sem = _split_refs(refs, 5, 4, comm)
        h_ref, l_ref, w_ref, m_ref, v_ref = ins
        if comm is not None:
            start_comm, wait_comm = comm.run(c_src, c_out, c_sem, pl.program_id(0) == 0,
                                             pl.program_id(0) == steps - 1)
            start_comm()
        g = ((h_ref[...] + l_ref[0].astype(F32)) + l_ref[1].astype(F32)) + l_ref[2].astype(F32)
        delta, m_new, v_new = _adam_math(g, w_ref[...], m_ref[...], v_ref[...])
        for ref, val in zip(outs, (g, delta, m_new, v_new)):
            ref[...] = val
        if comm is not None:
            wait_comm()

    flat = pl.BlockSpec((tm, c), lambda i: (i, 0))
    in_specs = [flat, pl.BlockSpec((3, tm, c), lambda i: (0, i, 0)), flat, flat, flat]
    out_shape = [jax.ShapeDtypeStruct((r, c), F32)] * 4
    if comm is None:
        return pl.pallas_call(kern, name=name, grid=(steps,), in_specs=in_specs, out_specs=[flat] * 4,
                              out_shape=out_shape, compiler_params=_params(("parallel",)))(own, landed, w, m, v)
    res_all = pl.pallas_call(
        kern, name=name, grid=(steps,), in_specs=in_specs + [ANY] * len(comm.srcs),
        out_specs=[flat] * 4 + [ANY] * len(comm.outs), out_shape=out_shape + comm.outs,
        scratch_shapes=comm.scratch(), compiler_params=_params(("arbitrary",)),
    )(own, landed, w, m, v, *comm.srcs)
    return list(res_all[:4]), list(res_all[4:])


def _allreduce_adam(gathered, w, m, v, name):
    _, r, c = gathered.shape

    def kern(a_ref, w_ref, m_ref, v_ref, g_out, d_out, m_out, v_out):
        g = a_ref[0]
        for j in range(1, N_DEV):
            g = g + a_ref[j]
        delta, m_new, v_new = _adam_math(g, w_ref[...], m_ref[...], v_ref[...])
        g_out[...] = g
        d_out[...] = delta
        m_out[...] = m_new
        v_out[...] = v_new

    return pl.pallas_call(
        kern, name=name, out_shape=[jax.ShapeDtypeStruct((r, c), F32)] * 4,
        compiler_params=_params(),
    )(gathered, w, m, v)


def _cols_from_blocks(g):
    n, k, nb = g.shape
    return jnp.transpose(g, (1, 0, 2)).reshape(k, n * nb)


def _blocks_from_cols(w):
    k, n = w.shape
    return jnp.transpose(w.reshape(k, N_DEV, n // N_DEV), (1, 0, 2))


def _pack_small(parts):
    flat = []
    for p in parts:
        v = p.reshape(-1)
        flat.append(jnp.pad(v, (0, (-v.shape[0]) % (8 * LANES))))
    return jnp.concatenate(flat).reshape(-1, LANES)


def _unpack_small(buf, like):
    out, pos = [], 0
    flat = buf.reshape(-1)
    for p in like:
        size = p.size
        out.append(flat[pos:pos + size].reshape(p.shape))
        pos += size + (-size) % (8 * LANES)
    return out


def kernel(x, mem, ffn1_norm, ffn1_w_gate_up, ffn1_w_down, mix_norm, mem_norm, w_in, b_forget, pool_w, pool_scale, w_pool_up, fox_q_norm, fox_k_norm, w_fox_o, w_mem_kv, mem_q_norm, mem_k_norm, w_mem_o, w_out, ffn2_norm, ffn2_w_gate_up, ffn2_w_down, loss_target, m_ffn1_norm, m_ffn1_w_gate_up, m_ffn1_w_down, m_mix_norm, m_mem_norm, m_w_in, m_b_forget, m_pool_w, m_pool_scale, m_w_pool_up, m_fox_q_norm, m_fox_k_norm, m_w_fox_o, m_w_mem_kv, m_mem_q_norm, m_mem_k_norm, m_w_mem_o, m_w_out, m_ffn2_norm, m_ffn2_w_gate_up, m_ffn2_w_down, v_ffn1_norm, v_ffn1_w_gate_up, v_ffn1_w_down, v_mix_norm, v_mem_norm, v_w_in, v_b_forget, v_pool_w, v_pool_scale, v_w_pool_up, v_fox_q_norm, v_fox_k_norm, v_w_fox_o, v_w_mem_kv, v_mem_q_norm, v_mem_k_norm, v_w_mem_o, v_w_out, v_ffn2_norm, v_ffn2_w_gate_up, v_ffn2_w_down):
    names = ["ffn1_norm", "ffn1_w_gate_up", "ffn1_w_down", "mix_norm", "mem_norm", "w_in", "b_forget", "pool_w",
             "pool_scale", "w_pool_up", "fox_q_norm", "fox_k_norm", "w_fox_o", "w_mem_kv", "mem_q_norm",
             "mem_k_norm", "w_mem_o", "w_out", "ffn2_norm", "ffn2_w_gate_up", "ffn2_w_down"]
    w_args = [ffn1_norm, ffn1_w_gate_up, ffn1_w_down, mix_norm, mem_norm, w_in, b_forget, pool_w, pool_scale,
              w_pool_up, fox_q_norm, fox_k_norm, w_fox_o, w_mem_kv, mem_q_norm, mem_k_norm, w_mem_o, w_out,
              ffn2_norm, ffn2_w_gate_up, ffn2_w_down]
    m_args = [m_ffn1_norm, m_ffn1_w_gate_up, m_ffn1_w_down, m_mix_norm, m_mem_norm, m_w_in, m_b_forget, m_pool_w,
              m_pool_scale, m_w_pool_up, m_fox_q_norm, m_fox_k_norm, m_w_fox_o, m_w_mem_kv, m_mem_q_norm,
              m_mem_k_norm, m_w_mem_o, m_w_out, m_ffn2_norm, m_ffn2_w_gate_up, m_ffn2_w_down]
    v_args = [v_ffn1_norm, v_ffn1_w_gate_up, v_ffn1_w_down, v_mix_norm, v_mem_norm, v_w_in, v_b_forget, v_pool_w,
              v_pool_scale, v_w_pool_up, v_fox_q_norm, v_fox_k_norm, v_w_fox_o, v_w_mem_kv, v_mem_q_norm,
              v_mem_k_norm, v_w_mem_o, v_w_out, v_ffn2_norm, v_ffn2_w_gate_up, v_ffn2_w_down]
    W = dict(zip(names, w_args))
    M = dict(zip(names, m_args))
    V = dict(zip(names, v_args))

    batch, seq, d = x.shape
    mlen = mem.shape[1]
    t = batch * seq
    gate_w = 3 * d
    z_gate, z_q = 0, gate_w
    z_k, z_v = z_q + FOX_WIDTH, z_q + 2 * FOX_WIDTH
    z_u = z_q + 3 * FOX_WIDTH
    z_qm = z_u + POOL_WIDTH
    z_f = z_qm + MEM_WIDTH
    z_width = -(-(z_f + F_PAD) // 512) * 512

    x2d = x.reshape(t, d)
    mem2d = mem.reshape(batch * mlen, d)
    tgt2d = loss_target.reshape(t, d)

    big = ["ffn1_w_gate_up", "ffn1_w_down", "w_in", "w_pool_up", "w_fox_o", "w_mem_kv", "w_mem_o", "w_out",
           "ffn2_w_gate_up", "ffn2_w_down"]
    col_sharded = {"ffn1_w_gate_up", "ffn2_w_gate_up", "w_in", "w_pool_up", "w_fox_o", "w_mem_o"}
    mixer_small = ["w_pool_up", "w_fox_o", "w_mem_kv", "w_mem_o", "w_out"]
    shard = {n: W[n][0].astype(BF16) for n in big}

    def rows_of(g):
        return g.reshape(-1, g.shape[2])

    gu_caps_nn, gu_caps_nt, gu_caps_tn = (1024, 1408, 2048), (1024, 1024, 1408), (1024, 1408, 2048)
    down_caps, dwd_caps = (1024, 1024, 2816), (1408, 1024, 2048)

    (wgu1,) = _comm_call(_gather_first([shard["ffn1_w_gate_up"]]), "ag_first")
    h1, (wgu1,) = _rms_fwd(x2d, W["ffn1_norm"], "ffn1_rms", comm=_gather_second([wgu1]))
    group_a = ["ffn1_w_down"] + mixer_small
    gu1, bufs_a = _mm(h1, wgu1, "nn", "ffn1_gu", caps=gu_caps_nn, b_blocks=True,
                      comm=_gather_first([shard[n] for n in group_a]))
    a1, bufs_a = _swiglu_fwd(gu1, "ffn1_swiglu", comm=_gather_second(bufs_a))
    full = {n: (_cols_from_blocks(g) if n in col_sharded else rows_of(g)) for n, g in zip(group_a, bufs_a)}
    x1, (w_in_g,) = _mm(a1, full["ffn1_w_down"], "nn", "ffn1_down", scale=0.5, res=x2d, caps=down_caps,
                        comm=_gather_first([shard["w_in"]]))
    h2, (w_in_g,) = _rms_fwd(x1, W["mix_norm"], "mix_rms", comm=_gather_second([w_in_g]))

    o_u, o_q, o_v = 0, POOL_WIDTH, POOL_WIDTH + 2 * FOX_WIDTH
    o_f = o_v + FOX_WIDTH
    o_qm = o_f + FOX_HEADS
    o_g = o_qm + MEM_WIDTH
    wi = _cols_from_blocks(w_in_g)
    w_in_pad = jnp.concatenate(
        [wi[:, o_g:o_g + gate_w], wi[:, o_q:o_q + 3 * FOX_WIDTH], wi[:, o_u:o_u + POOL_WIDTH],
         wi[:, o_qm:o_qm + MEM_WIDTH], wi[:, o_f:o_f + FOX_HEADS],
         jnp.zeros((d, z_width - z_f - FOX_HEADS), BF16)], axis=1)
    z, (wgu2,) = _mm(h2, w_in_pad, "nn", "mix_in", comm=_gather_first([shard["ffn2_w_gate_up"]]))

    pool_w_b = W["pool_w"][0].astype(BF16)

    def pool_fwd_body(u, pw, ps):
        row = lax.broadcasted_iota(jnp.int32, (seq, 1), 0)
        diffs, mixed = [], []
        for g in range(POOL_GROUPS):
            ug = u[:, g * POOL_GROUP_DIM:(g + 1) * POOL_GROUP_DIM]
            cnt = jnp.minimum(row + 1, POOL_WINDOWS[g]).astype(F32)
            diff = _window_sum(ug, g + 1, True) / cnt - ug
            diffs.append(diff)
            mixed.append(lax.dot_general(diff.astype(BF16), pw[g], _DIMS["nn"], preferred_element_type=F32))
        diffs = jnp.concatenate(diffs, axis=1)
        mixed = jnp.concatenate(mixed, axis=1)
        return mixed * ps, diffs, mixed

    ypp, pool_diff, pool_mixed = _rowwise(
        pool_fwd_body, [(z, z_u, POOL_WIDTH)], [pool_w_b, W["pool_scale"]],
        [(POOL_WIDTH, BF16), (POOL_WIDTH, BF16), (POOL_WIDTH, F32)], [], "pool_fwd", seq)
    y_pool = _mm(ypp, full["w_pool_up"], "nn", "pool_up")

    gq = jnp.tile(W["fox_q_norm"], (1, FOX_HEADS))
    gk = jnp.tile(W["fox_k_norm"], (1, FOX_HEADS))
    b_pad = jnp.pad(W["b_forget"], ((0, 0), (0, F_PAD - FOX_HEADS)))

    def fox_prep_body(q, k, v, f, gqv, gkv, bv):
        return (_head_norm(q, gqv, FOX_HEAD_DIM), _head_norm(k, gkv, FOX_HEAD_DIM), v, _log_sigmoid(f + bv))

    tm_e = _tile(t, 256, 16)
    (qn, kn, vb, logf), (wgu2,) = _rowwise(
        fox_prep_body, [(z, z_q, FOX_WIDTH), (z, z_k, FOX_WIDTH), (z, z_v, FOX_WIDTH), (z, z_f, F_PAD)],
        [gq, gk, b_pad], [(FOX_WIDTH, BF16), (FOX_WIDTH, BF16), (FOX_WIDTH, BF16), (F_PAD, F32)], [], "fox_prep", tm_e,
        comm=_gather_second([wgu2]))
    csum = _rowwise(lambda v: (_scan_rows(v, True),), [_whole(logf)], [], [(F_PAD, F32)], [], "fox_cumsum", seq)[0]
    cks = jnp.transpose(csum.reshape(batch, seq, F_PAD)[:, :, :FOX_HEADS], (0, 2, 1)).reshape(
        batch * FOX_HEADS // 2, 2, seq)
    (o_fox, lse), (wd2_g,) = _fox_fwd(qn, kn, vb, cks, batch, seq, "fox_fwd",
                                      comm=_gather_first([shard["ffn2_w_down"]]))
    y_fox = _mm(o_fox, full["w_fox_o"], "nn", "fox_o")

    memn = _rms_fwd(mem2d, W["mem_norm"], "mem_rms")
    kv = _mm(memn, full["w_mem_kv"], "nn", "mem_kv")
    gqm = jnp.tile(W["mem_q_norm"], (1, MEM_HEADS))
    gkm = jnp.tile(W["mem_k_norm"], (1, MEM_HEADS))
    qmn = _rowwise(lambda q, g: (_head_norm(q, g, MEM_HEAD_DIM),), [(z, z_qm, MEM_WIDTH)], [gqm],
                   [(MEM_WIDTH, BF16)], [], "memq_prep", tm_e)[0]
    kmn, vmb = _rowwise(lambda k, v, g: (_head_norm(k, g, MEM_HEAD_DIM), v),
                        [(kv, 0, MEM_WIDTH), (kv, MEM_WIDTH, MEM_WIDTH)], [gkm],
                        [(MEM_WIDTH, BF16), (MEM_WIDTH, BF16)], [], "memk_prep", _tile(batch * mlen, 256, 16))
    o_mem = _mem_fwd(qmn, kmn, vmb, batch, seq, mlen, "mem_fwd")
    y_mem = _mm(o_mem, full["w_mem_o"], "nn", "mem_o")

    def gate_fwd_body(gp, gf, gm, yp, yf, ym):
        return ((_sigmoid(gp) * yp + _sigmoid(gf) * yf) + _sigmoid(gm) * ym,)

    tm_g = _tile(t, 128, 16)
    (merged,), (wd2_g,) = _rowwise(
        gate_fwd_body, [(z, 0, d), (z, d, d), (z, 2 * d, d), _whole(y_pool), _whole(y_fox), _whole(y_mem)],
        [], [(d, BF16)], [], "gate_fwd", tm_g, comm=_gather_second([wd2_g]))
    wd2 = rows_of(wd2_g)
    x2 = _mm(merged, full["w_out"], "nn", "mix_out", res=x1)

    h3 = _rms_fwd(x2, W["ffn2_norm"], "ffn2_rms")
    gu2 = _mm(h3, wgu2, "nn", "ffn2_gu", caps=gu_caps_nn, b_blocks=True)
    a2 = _swiglu_fwd(gu2, "ffn2_swiglu")
    x3 = _mm(a2, wd2, "nn", "ffn2_down", scale=0.5, res=x2, caps=down_caps)
    dy, dy_b, loss_part = _loss_head(x3, tgt2d, "loss")
    loss = lax.psum(loss_part[0, 0], ("x", "y", "c"))

    cx, cy, cc = _position()
    where = jnp.stack([cc, 2 * cx + cy]).astype(jnp.int32)
    G, own, landed = {}, {}, {}

    def row_blocks(g):
        return g.reshape(N_DEV, g.shape[0] // N_DEV, g.shape[1])

    def pair_add(n, partial, from_core):
        own[n], chip_sums = _rs_pair_add(partial, from_core, where, "rs_add_" + n)
        return chip_sums

    da2 = _mm(dy_b, wd2, "nt", "ffn2_da", scale=0.5)
    p_wd2 = row_blocks(_mm(a2, dy_b, "tn", "ffn2_dwd", scale=0.5, caps=dwd_caps))
    dgu2, (l_wd2,) = _swiglu_bwd(gu2, da2, "ffn2_dswiglu", comm=_scatter_first([p_wd2]))
    s_wd2 = pair_add("ffn2_w_down", p_wd2, l_wd2)
    p_wgu2, (landed["ffn2_w_down"],) = _mm(h3, dgu2, "tn", "ffn2_dwgu", caps=gu_caps_tn, out_blocks=True,
                                           comm=_scatter_second([s_wd2]))
    dh3, (l_wgu2,) = _mm(dgu2, wgu2, "nt", "ffn2_dh", caps=gu_caps_nt, b_blocks=True,
                         comm=_scatter_first([p_wgu2]))
    s_wgu2 = pair_add("ffn2_w_gate_up", p_wgu2, l_wgu2)
    dx2, dx2_b, G["ffn2_norm"] = _rms_bwd(x2, W["ffn2_norm"], dh3, dy, "ffn2_drms")

    dmerged = _mm(dx2_b, full["w_out"], "nt", "mix_out_dx")
    P = {"w_out": row_blocks(_mm(merged, dx2_b, "tn", "mix_out_dw"))}

    def gate_bwd_body(gp, gf, gm, yp, yf, ym, dm):
        outs_dl, outs_dy = [], []
        for gl, yv in ((gp, yp), (gf, yf), (gm, ym)):
            s = _sigmoid(gl)
            outs_dl.append((dm * yv) * (s * (1.0 - s)))
            outs_dy.append(dm * s)
        return (jnp.concatenate(outs_dl, axis=1), *outs_dy)

    dz_gate, dy_pool, dy_fox, dy_mem = _rowwise(
        gate_bwd_body, [(z, 0, d), (z, d, d), (z, 2 * d, d), _whole(y_pool), _whole(y_fox), _whole(y_mem), _whole(dmerged)],
        [], [(gate_w, BF16), (d, BF16), (d, BF16), (d, BF16)], [], "gate_bwd", tm_g)

    dypp = _mm(dy_pool, full["w_pool_up"], "nt", "pool_up_dx")
    P["w_pool_up"] = _blocks_from_cols(_mm(ypp, dy_pool, "tn", "pool_up_dw"))

    def pool_bwd_body(dyv, mixed, diff, pw, ps):
        row = lax.broadcasted_iota(jnp.int32, (seq, 1), 0)
        d_scale = jnp.sum(dyv * mixed, axis=0, keepdims=True)
        dmix = (dyv * ps).astype(BF16)
        du, dpw = [], []
        for g in range(POOL_GROUPS):
            sl = slice(g * POOL_GROUP_DIM, (g + 1) * POOL_GROUP_DIM)
            dmg = dmix[:, sl]
            ddiff = lax.dot_general(dmg, pw[g], _DIMS["nt"], preferred_element_type=F32)
            dpw.append(lax.dot_general(diff[:, sl], dmg, _DIMS["tn"], preferred_element_type=F32))
            cnt = jnp.minimum(row + 1, POOL_WINDOWS[g]).astype(F32)
            du.append(_window_sum(ddiff / cnt, g + 1, False) - ddiff)
        return jnp.concatenate(du, axis=1), d_scale, jnp.concatenate(dpw, axis=0)

    dz_u, G["pool_scale"], d_pool_w = _rowwise(
        pool_bwd_body, [_whole(dypp), _whole(pool_mixed), _whole(pool_diff)], [pool_w_b, W["pool_scale"]],
        [(POOL_WIDTH, BF16)], [((1, POOL_WIDTH), F32), ((POOL_WIDTH, POOL_GROUP_DIM), F32)],
        "pool_bwd", seq)
    G["pool_w"] = d_pool_w.reshape(1, POOL_GROUPS, POOL_GROUP_DIM, POOL_GROUP_DIM)

    do_fox = _mm(dy_fox, full["w_fox_o"], "nt", "fox_o_dx")
    P["w_fox_o"] = _blocks_from_cols(_mm(o_fox, dy_fox, "tn", "fox_o_dw"))
    (dkn, dz_v, dqn, dcks, dcq), (landed["ffn2_w_gate_up"],) = _fox_bwd(
        qn, kn, vb, cks, o_fox, lse, do_fox, batch, seq, "fox_bwd", comm=_scatter_second([s_wgu2]))
    dcs = jnp.transpose(dcks.reshape(batch, FOX_HEADS, seq), (0, 2, 1)).reshape(t, FOX_HEADS)
    dcs = jnp.pad(dcs, ((0, 0), (0, F_PAD - FOX_HEADS)))
    dcq = jnp.pad(dcq.reshape(t, FOX_HEADS, FOX_HEAD_DIM)[:, :, 0], ((0, 0), (0, F_PAD - FOX_HEADS)))

    def fox_f_bwd_body(dc, dc_rows, f, bv):
        lane = lax.broadcasted_iota(jnp.int32, (1, F_PAD), 1)
        dlogf = _scan_rows(dc + dc_rows, False)
        df = jnp.where(lane < FOX_HEADS, dlogf * _sigmoid(-(f + bv)), 0.0)
        return df, jnp.sum(df, axis=0, keepdims=True)

    dz_f, db_pad = _rowwise(fox_f_bwd_body, [_whole(dcs), _whole(dcq), (z, z_f, F_PAD)], [b_pad], [(F_PAD, BF16)],
                            [((1, F_PAD), F32)], "fox_f_bwd", seq)
    G["b_forget"] = db_pad[:, :FOX_HEADS]

    def fox_qk_bwd_body(q, k, dq, dk, gqv, gkv):
        dqr, dgq = _head_norm_bwd(q, gqv, dq, FOX_HEAD_DIM)
        dkr, dgk = _head_norm_bwd(k, gkv, dk, FOX_HEAD_DIM)
        return dqr, dkr, dgq, dgk

    dz_q, dz_k, dgq_t, dgk_t = _rowwise(
        fox_qk_bwd_body, [(z, z_q, FOX_WIDTH), (z, z_k, FOX_WIDTH), _whole(dqn), _whole(dkn)], [gq, gk],
        [(FOX_WIDTH, BF16), (FOX_WIDTH, BF16)], [((1, FOX_WIDTH), F32), ((1, FOX_WIDTH), F32)], "fox_qk_bwd", tm_e)
    G["fox_q_norm"] = jnp.sum(dgq_t.reshape(FOX_HEADS, FOX_HEAD_DIM), axis=0, keepdims=True)
    G["fox_k_norm"] = jnp.sum(dgk_t.reshape(FOX_HEADS, FOX_HEAD_DIM), axis=0, keepdims=True)

    do_mem = _mm(dy_mem, full["w_mem_o"], "nt", "mem_o_dx")
    P["w_mem_o"] = _blocks_from_cols(_mm(o_mem, dy_mem, "tn", "mem_o_dw"))
    dqmn, dkmn, dvm = _mem_bwd(qmn, kmn, vmb, do_mem, batch, seq, mlen, "mem_bwd")
    dz_qm, dgqm_t = _rowwise(lambda q, dq, g: _head_norm_bwd(q, g, dq, MEM_HEAD_DIM),
                             [(z, z_qm, MEM_WIDTH), _whole(dqmn)], [gqm], [(MEM_WIDTH, BF16)],
                             [((1, MEM_WIDTH), F32)], "memq_bwd", tm_e)

    def memk_bwd_body(k, dk, dv, g):
        dkr, dg = _head_norm_bwd(k, g, dk, MEM_HEAD_DIM)
        return jnp.concatenate([dkr, dv], axis=1), dg

    dkv, dgkm_t = _rowwise(memk_bwd_body, [(kv, 0, MEM_WIDTH), _whole(dkmn), _whole(dvm)], [gkm],
                           [(2 * MEM_WIDTH, BF16)], [((1, MEM_WIDTH), F32)], "memk_bwd", _tile(batch * mlen, 256, 16))
    G["mem_q_norm"] = jnp.sum(dgqm_t.reshape(MEM_HEADS, MEM_HEAD_DIM), axis=0, keepdims=True)
    G["mem_k_norm"] = jnp.sum(dgkm_t.reshape(MEM_HEADS, MEM_HEAD_DIM), axis=0, keepdims=True)
    P["w_mem_kv"] = row_blocks(_mm(memn, dkv, "tn", "mem_kv_dw"))
    dmemn = _mm(dkv, full["w_mem_kv"], "nt", "mem_kv_dx")
    _, _, G["mem_norm"] = _rms_bwd(mem2d, W["mem_norm"], dmemn, None, "mem_drms")

    dz = jnp.concatenate([dz_gate, dz_q, dz_k, dz_v, dz_u, dz_qm, dz_f,
                          jnp.zeros((t, z_width - z_f - F_PAD), BF16)], axis=1)
    d_w_in_pad, l_small = _mm(h2, dz, "tn", "mix_in_dw", comm=_scatter_first([P[n] for n in mixer_small]))
    s_small = [pair_add(n, P[n], l) for n, l in zip(mixer_small, l_small)]
    p = d_w_in_pad
    p_w_in = _blocks_from_cols(jnp.concatenate(
        [p[:, z_u:z_u + POOL_WIDTH], p[:, z_q:z_q + 3 * FOX_WIDTH], p[:, z_f:z_f + FOX_HEADS],
         p[:, z_qm:z_qm + MEM_WIDTH], p[:, z_gate:z_gate + gate_w]], axis=1))
    dh2, rest = _mm(dz, w_in_pad, "nt", "mix_in_dx", caps=(1024, 1024, 1792),
                    comm=_join(_scatter_second(s_small), _scatter_first([p_w_in])))
    for n, l in zip(mixer_small, rest[:len(mixer_small)]):
        landed[n] = l
    s_w_in = pair_add("w_in", p_w_in, rest[-1])
    dx1, dx1_b, G["mix_norm"] = _rms_bwd(x1, W["mix_norm"], dh2, dx2, "mix_drms")

    wd1 = full["ffn1_w_down"]
    da1 = _mm(dx1_b, wd1, "nt", "ffn1_da", scale=0.5)
    dgu1 = _swiglu_bwd(gu1, da1, "ffn1_dswiglu")
    p_wgu1, (landed["w_in"],) = _mm(h1, dgu1, "tn", "ffn1_dwgu", caps=gu_caps_tn, out_blocks=True,
                                    comm=_scatter_second([s_w_in]))
    d_wd1, (l_wgu1,) = _mm(a1, dx1_b, "tn", "ffn1_dwd", scale=0.5, caps=dwd_caps, comm=_scatter_first([p_wgu1]))
    p_wd1 = row_blocks(d_wd1)
    s_wgu1 = pair_add("ffn1_w_gate_up", p_wgu1, l_wgu1)
    dh1, (landed["ffn1_w_gate_up"],) = _mm(dgu1, wgu1, "nt", "ffn1_dh", caps=gu_caps_nt, b_blocks=True,
                                           comm=_scatter_second([s_wgu1]))
    (dx0, _, G["ffn1_norm"]), (l_wd1,) = _rms_bwd(x2d, W["ffn1_norm"], dh1, dx1, "ffn1_drms",
                                               comm=_scatter_first([p_wd1]))
    s_wd1 = pair_add("ffn1_w_down", p_wd1, l_wd1)
    grad_x = dx0.reshape(batch, seq, d)

    out_g, out_d, out_m, out_v = {}, {}, {}, {}

    def adam(n, comm=None):
        res = _rs_finish_adam(own[n], landed[n], W[n][0], M[n][0], V[n][0], "adam_" + n, comm=comm)
        extra = None
        if comm is not None:
            res, extra = res
        out_g[n], out_d[n], out_m[n], out_v[n] = [r[None] for r in res]
        return extra

    (landed["ffn1_w_down"],) = adam("ffn2_w_gate_up", comm=_scatter_second([s_wd1]))
    for n in ["ffn2_w_down", "w_in"] + mixer_small + ["ffn1_w_gate_up", "ffn1_w_down"]:
        adam(n)

    small = [n for n in names if n not in big]
    g_small = _pack_small([G[n].reshape(W[n].shape) for n in small])
    all_small = _all_gather_vmem(g_small, "ag_small_grads")
    res = _allreduce_adam(all_small, _pack_small([W[n] for n in small]), _pack_small([M[n] for n in small]),
                          _pack_small([V[n] for n in small]), "adam_small")
    like = [W[n] for n in small]
    for dst, buf in zip((out_g, out_d, out_m, out_v), res):
        for n, a in zip(small, _unpack_small(buf, like)):
            dst[n] = a

    return (loss, grad_x, *[out_g[n] for n in names], *[out_d[n] for n in names],
            *[out_m[n] for n in names], *[out_v[n] for n in names])
```

```python
import functools

import jax
import jax.numpy as jnp
from jax import lax
from jax.experimental import pallas as pl
from jax.experimental.pallas import tpu as pltpu

F32 = jnp.float32
BF16 = jnp.bfloat16
MESH = pl.DeviceIdType.MESH

N_DEV = 8
EPS = 1e-6
FOX_HEADS = 16
FOX_HEAD_DIM = 64
FOX_WIDTH = FOX_HEADS * FOX_HEAD_DIM
MEM_HEADS = 4
MEM_HEAD_DIM = 128
MEM_WIDTH = MEM_HEADS * MEM_HEAD_DIM
POOL_GROUPS = 4
POOL_GROUP_DIM = 128
POOL_WIDTH = POOL_GROUPS * POOL_GROUP_DIM
POOL_WINDOWS = (2, 4, 8, 16)
LANES = 128
F_PAD = LANES

ADAM_LR = 0.001
ADAM_B1 = 0.9
ADAM_B2 = 0.999
ADAM_EPS = 1e-08
ADAM_WD = 0.01
ADAM_STEP = 10

VMEM_LIMIT = 56 * 1024 * 1024
NEG = -1e30

ANY = pl.BlockSpec(memory_space=pl.ANY)


def _params(sem=None):
    return pltpu.CompilerParams(dimension_semantics=sem, vmem_limit_bytes=VMEM_LIMIT)


MXU_DIM = 256


def _tile(dim, cap, align):
    best = None
    t = align
    while t <= min(dim, cap):
        if dim % t == 0:
            best = t
        t += align
    return dim if best is None else best


def _mxu_tile(dim, cap):
    wide, fine = _tile(dim, cap, MXU_DIM), _tile(dim, cap, LANES)
    whole_widths = wide % MXU_DIM == 0 and wide <= cap
    return wide if whole_widths and fine < 2 * wide else fine


class _Sems:
    def __init__(self, send, recv, local):
        self.send, self.recv, self.local = send, recv, local
        self.n_remote = self.n_local = 0

    def remote(self):
        i = self.n_remote
        self.n_remote += 1
        return self.send.at[i], self.recv.at[i]

    def one_local(self):
        i = self.n_local
        self.n_local += 1
        return self.local.at[i]


class _Comm:
    def __init__(self, srcs, outs, n_remote, n_local, plan, in_place=False):
        self.srcs, self.outs, self.n_remote, self.n_local = list(srcs), list(outs), n_remote, n_local
        self.plan, self.in_place = plan, in_place

    def aliases(self, n_in, n_out):
        return {n_in + i: n_out + i for i in range(len(self.srcs))} if self.in_place else {}

    def scratch(self):
        return [pltpu.SemaphoreType.DMA((self.n_remote,)), pltpu.SemaphoreType.DMA((self.n_remote,)),
                pltpu.SemaphoreType.DMA((max(self.n_local, 1),))]

    def run(self, src_refs, out_refs, sem_refs, first, last):
        def copies():
            return self.plan(list(src_refs), list(out_refs), _Sems(*sem_refs))

        return (lambda: _when(first, lambda: [cp.start() for cp in copies()]),
                lambda: _when(last, lambda: [cp.wait() for cp in copies()]))


def _when(cond, fn):
    @pl.when(cond)
    def _():
        fn()


def _join(*comms):
    comms = [c for c in comms if c is not None]
    assert all(not c.in_place for c in comms)

    def plan(src_refs, out_refs, sems):
        copies, si, oi = [], 0, 0
        for c in comms:
            copies += c.plan(src_refs[si:si + len(c.srcs)], out_refs[oi:oi + len(c.outs)], sems)
            si += len(c.srcs)
            oi += len(c.outs)
        return copies

    return _Comm(sum((c.srcs for c in comms), []), sum((c.outs for c in comms), []),
                 sum(c.n_remote for c in comms), sum(c.n_local for c in comms), plan)


def _split_refs(refs, n_in, n_out, comm):
    if comm is None:
        return refs[:n_in], refs[n_in:n_in + n_out], refs[n_in + n_out:], (), (), ()
    ns, no = len(comm.srcs), len(comm.outs)
    ins = refs[:n_in]
    srcs = refs[n_in:n_in + ns]
    outs = refs[n_in + ns:n_in + ns + n_out]
    couts = refs[n_in + ns + n_out:n_in + ns + n_out + no]
    rest = refs[n_in + ns + n_out + no:]
    return ins, outs, rest[:-3], srcs, couts, rest[-3:]


def _sigmoid(x):
    return 1.0 / (1.0 + jnp.exp(-x))


_DIMS = {"nn": (((1,), (0,)), ((), ())), "nt": (((1,), (1,)), ((), ())), "tn": (((0,), (0,)), ((), ()))}


def _mm(a, b, mode, name, out_dtype=F32, scale=1.0, res=None, caps=None, b_blocks=False, out_blocks=False,
        comm=None):
    nblk = N_DEV
    if b_blocks:
        _, r0, c0 = b.shape
        b_shape = (r0, nblk * c0)
    else:
        b_shape = b.shape
    if mode == "nn":
        (m, k), (k2, n) = a.shape, b_shape
    elif mode == "nt":
        (m, k), (n, k2) = a.shape, b_shape
    else:
        (k, m), (k2, n) = a.shape, b_shape
    assert k == k2, (name, a.shape, b.shape)
    cm, cn, ck = caps or ((1024, 512, 2048) if k <= 2048 else (1024, 1024, 2048))
    n_unit = n // nblk if (out_blocks or (b_blocks and mode == "nn")) else n
    k_unit = k // nblk if (b_blocks and mode == "nt") else k
    tm, tn, tk = _mxu_tile(m, cm), _mxu_tile(n_unit, cn), _mxu_tile(k_unit, ck)
    nk = k // tk
    npb, kpb = n_unit // tn, k_unit // tk
    if mode == "nn":
        a_spec = pl.BlockSpec((tm, tk), lambda i, j, kk: (i, kk))
        b_spec = pl.BlockSpec((tk, tn), lambda i, j, kk: (kk, j))
        if b_blocks:
            b_spec = pl.BlockSpec((None, tk, tn), lambda i, j, kk: (j // npb, kk, j % npb))
    elif mode == "nt":
        a_spec = pl.BlockSpec((tm, tk), lambda i, j, kk: (i, kk))
        b_spec = pl.BlockSpec((tn, tk), lambda i, j, kk: (j, kk))
        if b_blocks:
            b_spec = pl.BlockSpec((None, tn, tk), lambda i, j, kk: (kk // kpb, j, kk % kpb))
    else:
        assert not b_blocks
        a_spec = pl.BlockSpec((tk, tm), lambda i, j, kk: (kk, i))
        b_spec = pl.BlockSpec((tk, tn), lambda i, j, kk: (kk, j))
    if out_blocks:
        assert res is None
        o_spec = pl.BlockSpec((None, tm, tn), lambda i, j, kk: (j // npb, i, j % npb))
        o_shape = jax.ShapeDtypeStruct((nblk, m, n // nblk), out_dtype)
    else:
        o_spec = pl.BlockSpec((tm, tn), lambda i, j, kk: (i, j))
        o_shape = jax.ShapeDtypeStruct((m, n), out_dtype)
    in_specs = [a_spec, b_spec] + ([o_spec] if res is not None else [])
    n_in = len(in_specs)
    dims = _DIMS[mode]
    gm, gn = m // tm, n // tn

    def kern(*refs):
        ins, outs, scratch, c_src, c_out, c_sem = _split_refs(refs, n_in, 1, comm)
        a_ref, b_ref = ins[0], ins[1]
        res_ref = ins[2] if res is not None else None
        o_ref = outs[0]
        if comm is not None:
            i, j, kq = pl.program_id(0), pl.program_id(1), pl.program_id(2)
            first = jnp.logical_and(jnp.logical_and(i == 0, j == 0), kq == 0)
            last = jnp.logical_and(jnp.logical_and(i == gm - 1, j == gn - 1), kq == nk - 1)
            start_comm, wait_comm = comm.run(c_src, c_out, c_sem, first, last)
            start_comm()
        a_tile = a_ref[...].astype(BF16)
        if scale != 1.0:
            a_tile = a_tile * scale

        def product():
            return lax.dot_general(a_tile, b_ref[...].astype(BF16), dims, preferred_element_type=F32)

        if nk == 1:
            r = product()
            if res_ref is not None:
                r = res_ref[...] + r
            o_ref[...] = r.astype(out_dtype)
        else:
            acc_ref = scratch[0] if scratch else o_ref
            kk = pl.program_id(2)

            @pl.when(kk == 0)
            def _():
                acc_ref[...] = jnp.zeros_like(acc_ref) if res_ref is None else res_ref[...]

            acc_ref[...] += product()
            if scratch:
                @pl.when(kk == nk - 1)
                def _():
                    o_ref[...] = acc_ref[...].astype(out_dtype)
        if comm is not None:
            wait_comm()

    assert scale in (1.0, 0.5)
    args = (a, b) + ((res,) if res is not None else ())
    scratch_shapes = [pltpu.VMEM((tm, tn), F32)] if (nk > 1 and out_dtype != F32) else []
    if comm is None:
        return pl.pallas_call(
            kern, name=name, grid=(gm, gn, nk), in_specs=in_specs, out_specs=o_spec, out_shape=o_shape,
            scratch_shapes=scratch_shapes, compiler_params=_params(("parallel", "parallel", "arbitrary")),
        )(*args)
    res_all = pl.pallas_call(
        kern, name=name, grid=(gm, gn, nk), in_specs=in_specs + [ANY] * len(comm.srcs),
        out_specs=[o_spec] + [ANY] * len(comm.outs), out_shape=[o_shape] + comm.outs,
        scratch_shapes=scratch_shapes + comm.scratch(), input_output_aliases=comm.aliases(n_in, 1),
        compiler_params=_params(("arbitrary", "arbitrary", "arbitrary")),
    )(*args, *comm.srcs)
    return res_all[0], list(res_all[1:])


def _rowwise(body, ins, params, outs, accs, name, tm, comm=None):
    rows = ins[0][0].shape[0]
    assert rows % tm == 0, (name, rows, tm)
    in_specs = []
    for arr, off, width in ins:
        assert off % width == 0 and arr.shape[0] == rows, (name, arr.shape, off, width)
        in_specs.append(pl.BlockSpec((tm, width), functools.partial(lambda i, c: (i, c), c=off // width)))
    for p in params:
        in_specs.append(pl.BlockSpec(p.shape, functools.partial(lambda i, nd: (0,) * nd, nd=p.ndim)))
    out_specs = [pl.BlockSpec((tm, w), lambda i: (i, 0)) for w, _ in outs]
    out_specs += [pl.BlockSpec(s, functools.partial(lambda i, nd: (0,) * nd, nd=len(s))) for s, _ in accs]
    out_shape = [jax.ShapeDtypeStruct((rows, w), d) for w, d in outs]
    out_shape += [jax.ShapeDtypeStruct(s, d) for s, d in accs]
    n_in, n_par, n_out = len(ins), len(params), len(outs)

    steps = rows // tm

    def kern(*refs):
        in_refs, out_refs, _, c_src, c_out, c_sem = _split_refs(refs, n_in + n_par, n_out + len(accs), comm)
        first = pl.program_id(0) == 0
        if comm is not None:
            start_comm, wait_comm = comm.run(c_src, c_out, c_sem, first, pl.program_id(0) == steps - 1)
            start_comm()
        res = body(*[r[...] for r in in_refs])
        for r, v in zip(out_refs[:n_out], res[:n_out]):
            r[...] = v.astype(r.dtype)
        for r, v in zip(out_refs[n_out:], res[n_out:]):
            @pl.when(first)
            def _(r=r, v=v):
                r[...] = v.astype(r.dtype)

            @pl.when(jnp.logical_not(first))
            def _(r=r, v=v):
                r[...] += v.astype(r.dtype)
        if comm is not None:
            wait_comm()

    args = [a for a, _, _ in ins] + list(params)
    if comm is None:
        return pl.pallas_call(
            kern, name=name, grid=(steps,), in_specs=in_specs, out_specs=out_specs, out_shape=out_shape,
            compiler_params=_params(("arbitrary",) if accs else ("parallel",)),
        )(*args)
    res_all = pl.pallas_call(
        kern, name=name, grid=(steps,), in_specs=in_specs + [ANY] * len(comm.srcs),
        out_specs=out_specs + [ANY] * len(comm.outs), out_shape=out_shape + comm.outs,
        scratch_shapes=comm.scratch(), input_output_aliases=comm.aliases(len(args), len(out_shape)),
        compiler_params=_params(("arbitrary",)),
    )(*args, *comm.srcs)
    return list(res_all[:len(out_shape)]), list(res_all[len(out_shape):])


def _whole(x):
    return (x, 0, x.shape[1])


def _first(res, comm):
    return res[0] if comm is None else (res[0][0], res[1])


def _rms_fwd(x, gain, name, comm=None):
    def body(xv, g):
        r = lax.rsqrt(jnp.mean(xv * xv, axis=-1, keepdims=True) + EPS)
        return ((xv * r) * g,)

    return _first(_rowwise(body, [_whole(x)], [gain], [(x.shape[1], BF16)], [], name, _tile(x.shape[0], 256, 16),
                           comm=comm), comm)


def _rms_bwd(x, gain, dh, dres, name, comm=None):
    d = x.shape[1]

    def body(*vals):
        if dres is None:
            xv, dhv, g = vals
        else:
            xv, dhv, drv, g = vals
        r = lax.rsqrt(jnp.mean(xv * xv, axis=-1, keepdims=True) + EPS)
        xh = xv * r
        w = dhv * g
        dx = r * (w - xh * jnp.mean(w * xh, axis=-1, keepdims=True))
        if dres is not None:
            dx = drv + dx
        return dx, dx, jnp.sum(dhv * xh, axis=0, keepdims=True)

    ins = [_whole(x), _whole(dh)] + ([_whole(dres)] if dres is not None else [])
    return _rowwise(body, ins, [gain], [(d, F32), (d, BF16)], [((1, d), F32)], name, _tile(x.shape[0], 256, 16),
                    comm=comm)


def _swiglu_fwd(gu, name, comm=None):
    f = gu.shape[1] // 2

    def body(g, u):
        return ((g * _sigmoid(g)) * u,)

    return _first(_rowwise(body, [(gu, 0, f), (gu, f, f)], [], [(f, BF16)], [], name, _tile(gu.shape[0], 128, 16),
                           comm=comm), comm)


def _swiglu_bwd(gu, da, name, comm=None):
    f = gu.shape[1] // 2

    def body(g, u, dav):
        s = _sigmoid(g)
        dg = dav * u * (s * (1.0 + g * (1.0 - s)))
        du = dav * (g * s)
        return (jnp.concatenate([dg, du], axis=1),)

    return _first(_rowwise(body, [(gu, 0, f), (gu, f, f), _whole(da)], [], [(2 * f, BF16)], [], name,
                           _tile(gu.shape[0], 128, 16), comm=comm), comm)


def _loss_head(y, target, name):
    d = y.shape[1]

    def body(yv, tv):
        e = yv - tv
        part = jnp.sum(jnp.sum(e * e, axis=1, keepdims=True), axis=0, keepdims=True)
        return e / d, e / d, jnp.broadcast_to((0.5 / d) * part, (1, LANES))

    return _rowwise(body, [_whole(y), _whole(target)], [], [(d, F32), (d, BF16)], [((1, LANES), F32)], name,
                    _tile(y.shape[0], 256, 16))


def _head_mean(v, head_dim):
    cols = []
    lane = lax.broadcasted_iota(jnp.int32, (1, LANES), 1)
    for j in range(v.shape[1] // LANES):
        blk = v[:, j * LANES:(j + 1) * LANES]
        if head_dim == LANES:
            m = jnp.sum(blk, axis=-1, keepdims=True)
            cols.append(jnp.broadcast_to(m, blk.shape))
        else:
            lo = jnp.sum(jnp.where(lane < head_dim, blk, 0.0), axis=-1, keepdims=True)
            hi = jnp.sum(jnp.where(lane >= head_dim, blk, 0.0), axis=-1, keepdims=True)
            cols.append(jnp.where(lane < head_dim, lo, hi))
    return jnp.concatenate(cols, axis=1) / head_dim


def _head_norm(xv, g, head_dim):
    r = lax.rsqrt(_head_mean(xv * xv, head_dim) + EPS)
    return (xv * r) * g


def _head_norm_bwd(xv, g, dy, head_dim):
    r = lax.rsqrt(_head_mean(xv * xv, head_dim) + EPS)
    xh = xv * r
    w = dy * g
    dx = r * (w - xh * _head_mean(w * xh, head_dim))
    return dx, jnp.sum(dy * xh, axis=0, keepdims=True)


def _log_sigmoid(x):
    return jnp.minimum(x, 0.0) - jnp.log(1.0 + jnp.exp(-jnp.abs(x)))


def _shift_rows(v, sh, down):
    n = v.shape[0]
    row = lax.broadcasted_iota(jnp.int32, (n, 1), 0)
    if down:
        return jnp.where(row >= sh, pltpu.roll(v, sh, 0), 0.0)
    return jnp.where(row < n - sh, pltpu.roll(v, n - sh, 0), 0.0)


def _scan_rows(v, down):
    sh = 1
    while sh < v.shape[0]:
        v = v + _shift_rows(v, sh, down)
        sh *= 2
    return v


def _window_sum(v, steps, down):
    for s in range(steps):
        v = v + _shift_rows(v, 2 ** s, down)
    return v


def _fox_fwd(qn, kn, vb, cks, batch, seq, name, comm=None):
    t, width = qn.shape
    pairs = width // LANES
    tq = _tile(seq, 512, LANES)
    nq = seq // tq

    def kern(*refs):
        (q_ref, k_ref, v_ref, c_ref), (o_ref, lse_ref), _, c_src, c_out, c_sem = _split_refs(refs, 4, 2, comm)
        qi = pl.program_id(2)
        if comm is not None:
            b_id, p_id = pl.program_id(0), pl.program_id(1)
            first = jnp.logical_and(jnp.logical_and(b_id == 0, p_id == 0), qi == 0)
            last = jnp.logical_and(jnp.logical_and(b_id == batch - 1, p_id == pairs - 1), qi == nq - 1)
            start_comm, wait_comm = comm.run(c_src, c_out, c_sem, first, last)
            start_comm()
        lane = lax.broadcasted_iota(jnp.int32, (1, LANES), 1)
        rowi = lax.broadcasted_iota(jnp.int32, (tq, tq), 0)
        coli = lax.broadcasted_iota(jnp.int32, (tq, tq), 1)
        q_all = q_ref[...]
        o_heads, lse_heads = [], []
        for h in range(2):
            hm = (lane < FOX_HEAD_DIM) if h == 0 else (lane >= FOX_HEAD_DIM)
            q = jnp.where(hm, q_all, jnp.zeros_like(q_all))

            def step(j, carry, h=h, q=q, diagonal=False):
                m, l, acc = carry
                start = pl.multiple_of(j * tq, tq)
                k = k_ref[pl.ds(start, tq), :]
                v = v_ref[pl.ds(start, tq), :]
                s = lax.dot_general(q, k, _DIMS["nt"], preferred_element_type=F32)
                s = s - c_ref[0, h:h + 1, pl.ds(start, tq)]
                if diagonal:
                    s = jnp.where(rowi >= coli, s, NEG)
                m_new = jnp.maximum(m, jnp.max(s, axis=-1, keepdims=True))
                alpha = jnp.exp(m - m_new)
                p = jnp.exp(s - m_new)
                l = alpha * l + jnp.sum(p, axis=-1, keepdims=True)
                acc = alpha * acc + lax.dot_general(p.astype(BF16), v, _DIMS["nn"], preferred_element_type=F32)
                return m_new, l, acc

            init = (jnp.full((tq, 1), NEG, F32), jnp.zeros((tq, 1), F32), jnp.zeros((tq, LANES), F32))
            m, l, acc = step(qi, lax.fori_loop(0, qi, step, init), diagonal=True)
            o_heads.append(acc / l)
            lse_heads.append(jnp.broadcast_to(m + jnp.log(l), (tq, LANES)))
        o_ref[...] = jnp.where(lane < FOX_HEAD_DIM, o_heads[0], o_heads[1])
        lse_ref[...] = jnp.where(lane < FOX_HEAD_DIM, lse_heads[0], lse_heads[1])
        if comm is not None:
            wait_comm()

    q_spec = pl.BlockSpec((tq, LANES), lambda b, hp, qi: (b * nq + qi, hp))
    kv_spec = pl.BlockSpec((seq, LANES), lambda b, hp, qi: (b, hp))
    c_spec = pl.BlockSpec((1, 2, seq), lambda b, hp, qi: (b * pairs + hp, 0, 0))
    in_specs = [q_spec, kv_spec, kv_spec, c_spec]
    out_shape = [jax.ShapeDtypeStruct((t, width), F32), jax.ShapeDtypeStruct((t, width), F32)]
    if comm is None:
        return pl.pallas_call(
            kern, name=name, grid=(batch, pairs, nq), in_specs=in_specs, out_specs=[q_spec, q_spec],
            out_shape=out_shape, compiler_params=_params(("parallel", "parallel", "arbitrary")),
        )(qn, kn, vb, cks)
    res_all = pl.pallas_call(
        kern, name=name, grid=(batch, pairs, nq), in_specs=in_specs + [ANY] * len(comm.srcs),
        out_specs=[q_spec, q_spec] + [ANY] * len(comm.outs), out_shape=out_shape + comm.outs,
        scratch_shapes=comm.scratch(), compiler_params=_params(("arbitrary", "arbitrary", "arbitrary")),
    )(qn, kn, vb, cks, *comm.srcs)
    return list(res_all[:2]), list(res_all[2:])


def _fox_bwd(qn, kn, vb, cks, o, lse, do, batch, seq, name, comm=None):
    t, width = qn.shape
    pairs = width // LANES
    tk = _tile(seq, 512, LANES)
    nk = seq // tk
    scale = FOX_HEAD_DIM ** -0.5

    def kern(*refs):
        ins, outs, _, c_src, c_out, c_sem = _split_refs(refs, 7, 5, comm)
        k_ref, v_ref, q_ref, c_ref, o_ref, lse_ref, do_ref = ins
        dk_ref, dv_ref, dq_ref, dc_ref, dcq_ref = outs
        kj = pl.program_id(2)
        if comm is not None:
            b_id, p_id = pl.program_id(0), pl.program_id(1)
            first = jnp.logical_and(jnp.logical_and(b_id == 0, p_id == 0), kj == 0)
            last = jnp.logical_and(jnp.logical_and(b_id == batch - 1, p_id == pairs - 1), kj == nk - 1)
            start_comm, wait_comm = comm.run(c_src, c_out, c_sem, first, last)
            start_comm()
        lane = lax.broadcasted_iota(jnp.int32, (1, LANES), 1)
        rowi = lax.broadcasted_iota(jnp.int32, (tk, tk), 0)
        coli = lax.broadcasted_iota(jnp.int32, (tk, tk), 1)

        @pl.when(kj == 0)
        def _():
            dq_ref[...] = jnp.zeros_like(dq_ref)
            dcq_ref[...] = jnp.zeros_like(dcq_ref)

        k_all = k_ref[...]
        v_all = v_ref[...]
        kstart = pl.multiple_of(kj * tk, tk)
        dk_heads, dv_heads = [], []
        for h in range(2):
            hm = (lane < FOX_HEAD_DIM) if h == 0 else (lane >= FOX_HEAD_DIM)
            kh = jnp.where(hm, k_all, jnp.zeros_like(k_all))
            vh = jnp.where(hm, v_all, jnp.zeros_like(v_all))
            c_row = c_ref[0, h:h + 1, pl.ds(kstart, tk)]

            def step(qi, carry, h=h, hm=hm, kh=kh, vh=vh, c_row=c_row, diagonal=False):
                dk_acc, dv_acc, dc_acc = carry
                start = pl.multiple_of(qi * tk, tk)
                q = q_ref[pl.ds(start, tk), :]
                dov = do_ref[pl.ds(start, tk), :]
                ov = o_ref[pl.ds(start, tk), :]
                lse_col = jnp.max(jnp.where(hm, lse_ref[pl.ds(start, tk), :], NEG), axis=-1, keepdims=True)
                dob = jnp.where(hm, dov, 0.0).astype(BF16)
                dcol = jnp.sum(dob.astype(F32) * ov, axis=-1, keepdims=True)
                s = lax.dot_general(q, kh, _DIMS["nt"], preferred_element_type=F32) - c_row
                p = jnp.exp(s - lse_col)
                if diagonal:
                    p = jnp.where(rowi >= coli, p, 0.0)
                dp = lax.dot_general(dob, vh, _DIMS["nt"], preferred_element_type=F32)
                ds = p * (dp - dcol)
                ds_b = ds.astype(BF16)
                dv_acc = dv_acc + lax.dot_general(p.astype(BF16), dob, _DIMS["tn"], preferred_element_type=F32)
                dk_acc = dk_acc + lax.dot_general(ds_b, q, _DIMS["tn"], preferred_element_type=F32)
                dq_part = lax.dot_general(ds_b * scale, kh, _DIMS["nn"], preferred_element_type=F32)
                dq_ref[pl.ds(start, tk), :] += dq_part
                dcq_ref[pl.ds(start, tk), :] += jnp.where(hm, jnp.sum(ds, axis=-1, keepdims=True), 0.0)
                dc_acc = dc_acc - jnp.sum(ds, axis=0, keepdims=True)
                return dk_acc, dv_acc, dc_acc

            init = (jnp.zeros((tk, LANES), F32), jnp.zeros((tk, LANES), F32), jnp.zeros((1, tk), F32))
            dk_acc, dv_acc, dc_acc = lax.fori_loop(kj + 1, nk, step, step(kj, init, diagonal=True))
            dk_heads.append(dk_acc)
            dv_heads.append(dv_acc)
            dc_ref[0, h:h + 1, pl.ds(kstart, tk)] = dc_acc
        dk_ref[...] = jnp.where(lane < FOX_HEAD_DIM, dk_heads[0], dk_heads[1])
        dv_ref[...] = jnp.where(lane < FOX_HEAD_DIM, dv_heads[0], dv_heads[1]).astype(BF16)
        if comm is not None:
            wait_comm()

    kv_spec = pl.BlockSpec((tk, LANES), lambda b, hp, kj: (b * nk + kj, hp))
    full_spec = pl.BlockSpec((seq, LANES), lambda b, hp, kj: (b, hp))
    c_spec = pl.BlockSpec((1, 2, seq), lambda b, hp, kj: (b * pairs + hp, 0, 0))
    in_specs = [kv_spec, kv_spec, full_spec, c_spec, full_spec, full_spec, full_spec]
    out_specs = [kv_spec, kv_spec, full_spec, c_spec, full_spec]
    out_shape = [jax.ShapeDtypeStruct((t, width), F32), jax.ShapeDtypeStruct((t, width), BF16),
                 jax.ShapeDtypeStruct((t, width), F32), jax.ShapeDtypeStruct(cks.shape, F32),
                 jax.ShapeDtypeStruct((t, width), F32)]
    if comm is None:
        return pl.pallas_call(
            kern, name=name, grid=(batch, pairs, nk), in_specs=in_specs, out_specs=out_specs, out_shape=out_shape,
            compiler_params=_params(("parallel", "parallel", "arbitrary")),
        )(kn, vb, qn, cks, o, lse, do)
    res_all = pl.pallas_call(
        kern, name=name, grid=(batch, pairs, nk), in_specs=in_specs + [ANY] * len(comm.srcs),
        out_specs=out_specs + [ANY] * len(comm.outs), out_shape=out_shape + comm.outs,
        scratch_shapes=comm.scratch(), compiler_params=_params(("arbitrary", "arbitrary", "arbitrary")),
    )(kn, vb, qn, cks, o, lse, do, *comm.srcs)
    return list(res_all[:5]), list(res_all[5:])


def _mem_fwd(qn, kn, vb, batch, seq, mlen, name):
    t, width = qn.shape
    heads = width // LANES
    tq = _tile(seq, 512, LANES)
    nq = seq // tq
    scale = MEM_HEAD_DIM ** -0.5

    def kern(q_ref, k_ref, v_ref, o_ref):
        s = lax.dot_general(q_ref[...], k_ref[...], _DIMS["nt"], preferred_element_type=F32) * scale
        e = jnp.exp(s - jnp.max(s, axis=-1, keepdims=True))
        p = e / jnp.sum(e, axis=-1, keepdims=True)
        o_ref[...] = lax.dot_general(p.astype(BF16), v_ref[...], _DIMS["nn"], preferred_element_type=F32)

    q_spec = pl.BlockSpec((tq, LANES), lambda b, h, qi: (b * nq + qi, h))
    kv_spec = pl.BlockSpec((mlen, LANES), lambda b, h, qi: (b, h))
    return pl.pallas_call(
        kern, name=name, grid=(batch, heads, nq), in_specs=[q_spec, kv_spec, kv_spec], out_specs=q_spec,
        out_shape=jax.ShapeDtypeStruct((t, width), F32),
        compiler_params=_params(("parallel", "parallel", "parallel")),
    )(qn, kn, vb)


def _mem_bwd(qn, kn, vb, do, batch, seq, mlen, name):
    t, width = qn.shape
    heads = width // LANES
    tq = _tile(seq, 512, LANES)
    nq = seq // tq
    scale = MEM_HEAD_DIM ** -0.5

    def kern(q_ref, k_ref, v_ref, do_ref, dq_ref, dk_ref, dv_ref):
        qi = pl.program_id(2)
        q, k, v = q_ref[...], k_ref[...], v_ref[...]
        dob = do_ref[...].astype(BF16)
        s = lax.dot_general(q, k, _DIMS["nt"], preferred_element_type=F32) * scale
        e = jnp.exp(s - jnp.max(s, axis=-1, keepdims=True))
        p = e / jnp.sum(e, axis=-1, keepdims=True)
        dp = lax.dot_general(dob, v, _DIMS["nt"], preferred_element_type=F32)
        ds = p * (dp - jnp.sum(p * dp, axis=-1, keepdims=True))
        dsb = (ds * scale).astype(BF16)
        dq_ref[...] = lax.dot_general(dsb, k, _DIMS["nn"], preferred_element_type=F32)
        dk = lax.dot_general(dsb, q, _DIMS["tn"], preferred_element_type=F32)
        dv = lax.dot_general(p.astype(BF16), dob, _DIMS["tn"], preferred_element_type=F32)

        @pl.when(qi == 0)
        def _():
            dk_ref[...] = dk
            dv_ref[...] = dv

        @pl.when(qi > 0)
        def _():
            dk_ref[...] += dk
            dv_ref[...] += dv

    q_spec = pl.BlockSpec((tq, LANES), lambda b, h, qi: (b * nq + qi, h))
    kv_spec = pl.BlockSpec((mlen, LANES), lambda b, h, qi: (b, h))
    return pl.pallas_call(
        kern, name=name, grid=(batch, heads, nq), in_specs=[q_spec, kv_spec, kv_spec, q_spec],
        out_specs=[q_spec, kv_spec, kv_spec],
        out_shape=[jax.ShapeDtypeStruct((t, width), F32), jax.ShapeDtypeStruct(kn.shape, F32),
                   jax.ShapeDtypeStruct(kn.shape, F32)],
        compiler_params=_params(("parallel", "parallel", "arbitrary")),
    )(qn, kn, vb, do)


def _position():
    return lax.axis_index("x"), lax.axis_index("y"), lax.axis_index("c")


def _other_chips(x, y):
    return [(1 - x, y), (x, 1 - y), (1 - x, 1 - y)]


def _remote(src, dst, sems, to):
    send, recv = sems.remote()
    return pltpu.make_async_remote_copy(src_ref=src, dst_ref=dst, send_sem=send, recv_sem=recv,
                                        device_id=to, device_id_type=MESH)


def _gather_first(shards):
    n = len(shards)

    def plan(src, out, sems):
        x, y, c = _position()
        me = 4 * x + 2 * y + c
        peers = [(x, y, 1 - c)] + [(px, py, c) for px, py in _other_chips(x, y)]
        copies = []
        for i in range(n):
            copies.append(pltpu.make_async_copy(src[i], out[i].at[me], sems.one_local()))
            copies += [_remote(src[i], out[i].at[me], sems, to) for to in peers]
        return copies

    outs = [jax.ShapeDtypeStruct((N_DEV,) + s.shape, s.dtype) for s in shards]
    return _Comm(shards, outs, 4 * n, n, plan)


def _gather_second(bufs):
    n = len(bufs)

    def plan(src, out, sems):
        x, y, c = _position()
        copies = []
        for i in range(n):
            for px, py in _other_chips(x, y):
                blk = out[i].at[4 * px + 2 * py + c]
                copies.append(_remote(blk, blk, sems, (x, y, 1 - c)))
        return copies

    outs = [jax.ShapeDtypeStruct(b.shape, b.dtype) for b in bufs]
    return _Comm(bufs, outs, 3 * n, 0, plan, in_place=True)


def _scatter_first(partials):
    n = len(partials)

    def plan(src, out, sems):
        x, y, c = _position()
        return [_remote(src[i].at[2 * q + (1 - c)], out[i].at[q], sems, (x, y, 1 - c))
                for i in range(n) for q in range(4)]

    outs = [jax.ShapeDtypeStruct((4,) + g.shape[1:], g.dtype) for g in partials]
    return _Comm(partials, outs, 4 * n, 0, plan)


def _scatter_second(halves):
    n = len(halves)

    def plan(src, out, sems):
        x, y, c = _position()
        return [_remote(src[i].at[2 * px + py], out[i].at[r], sems, (px, py, c))
                for i in range(n) for r, (px, py) in enumerate(_other_chips(x, y))]

    outs = [jax.ShapeDtypeStruct((3,) + h.shape[1:], h.dtype) for h in halves]
    return _Comm(halves, outs, 3 * n, 0, plan)


def _comm_call(comm, name):
    ns, no = len(comm.srcs), len(comm.outs)

    def body(*refs):
        copies = comm.plan(list(refs[:ns]), list(refs[ns:ns + no]), _Sems(*refs[ns + no:]))
        for cp in copies:
            cp.start()
        for cp in copies:
            cp.wait()

    return pl.pallas_call(
        body, name=name, in_specs=[ANY] * ns, out_specs=[ANY] * no, out_shape=comm.outs,
        scratch_shapes=comm.scratch(), input_output_aliases=comm.aliases(0, 0),
    )(*comm.srcs)


def _all_gather_vmem(shard, name):
    first = _gather_first([shard])

    def body(x_ref, out_ref, send, recv, local):
        sems = _Sems(send, recv, local)
        copies = first.plan([x_ref], [out_ref], sems)
        for cp in copies:
            cp.start()
        x, y, c = _position()
        passed = []
        for j, (px, py) in enumerate(_other_chips(x, y)):
            copies[2 + j].wait_recv()
            blk = out_ref.at[4 * px + 2 * py + c]
            fwd = _remote(blk, blk, sems, (x, y, 1 - c))
            fwd.start()
            passed.append(fwd)
        copies[0].wait()
        copies[1].wait()
        for cp in copies[2:]:
            cp.wait_send()
        for cp in passed:
            cp.wait()

    vm = pl.BlockSpec(memory_space=pltpu.VMEM)
    return pl.pallas_call(
        body, name=name, in_specs=[vm], out_specs=vm,
        out_shape=jax.ShapeDtypeStruct((N_DEV,) + shard.shape, shard.dtype),
        scratch_shapes=[pltpu.SemaphoreType.DMA((7,)), pltpu.SemaphoreType.DMA((7,)), pltpu.SemaphoreType.DMA((1,))],
    )(shard)


def _rs_pair_add(partial, landed, where, name):
    _, r, c = partial.shape
    tm = _tile(r, 256, 16)

    def kern(where_ref, g_ref, l_ref, own_ref, hb_ref):
        s = g_ref[...] + l_ref[...]
        hb_ref[...] = s.astype(BF16)

        @pl.when(pl.program_id(1) == where_ref[1])
        def _():
            own_ref[...] = s

    grid_spec = pltpu.PrefetchScalarGridSpec(
        num_scalar_prefetch=1, grid=(r // tm, 4),
        in_specs=[pl.BlockSpec((None, tm, c), lambda i, q, wr: (2 * q + wr[0], i, 0)),
                  pl.BlockSpec((None, tm, c), lambda i, q, wr: (q, i, 0))],
        out_specs=[pl.BlockSpec((tm, c), lambda i, q, wr: (i, 0)),
                   pl.BlockSpec((None, tm, c), lambda i, q, wr: (q, i, 0))])
    return pl.pallas_call(
        kern, name=name, grid_spec=grid_spec,
        out_shape=[jax.ShapeDtypeStruct((r, c), F32), jax.ShapeDtypeStruct((4, r, c), BF16)],
        compiler_params=_params(("parallel", "arbitrary")),
    )(where, partial, landed)


def _adam_math(g, w, m, v):
    m = ADAM_B1 * m + (1.0 - ADAM_B1) * g
    v = ADAM_B2 * v + (1.0 - ADAM_B2) * (g * g)
    m_hat = m / (1.0 - ADAM_B1 ** ADAM_STEP)
    v_hat = v / (1.0 - ADAM_B2 ** ADAM_STEP)
    delta = -ADAM_LR * (m_hat / (jnp.sqrt(v_hat) + ADAM_EPS) + ADAM_WD * w)
    return delta, m, v


def _rs_finish_adam(own, landed, w, m, v, name):
    r, c = own.shape
    tm = _tile(r, 128, 16)

    def kern(h_ref, l_ref, w_ref, m_ref, v_ref, *outs):
        g = ((h_ref[...] + l_ref[0].astype(F32)) + l_ref[1].astype(F32)) + l_ref[2].astype(F32)
        delta, m_new, v_new = _adam_math(g, w_ref[...], m_ref[...], v_ref[...])
        for ref, val in zip(outs, (g, delta, m_new, v_new)):
            ref[...] = val

    flat = pl.BlockSpec((tm, c), lambda i: (i, 0))
    return pl.pallas_call(
        kern, name=name, grid=(r // tm,), in_specs=[flat, pl.BlockSpec((3, tm, c), lambda i: (0, i, 0)), flat, flat, flat],
        out_specs=[flat] * 4, out_shape=[jax.ShapeDtypeStruct((r, c), F32)] * 4,
        compiler_params=_params(("parallel",)))(own, landed, w, m, v)


def _allreduce_adam(gathered, w, m, v, name):
    _, r, c = gathered.shape

    def kern(a_ref, w_ref, m_ref, v_ref, g_out, d_out, m_out, v_out):
        g = a_ref[0]
        for j in range(1, N_DEV):
            g = g + a_ref[j]
        delta, m_new, v_new = _adam_math(g, w_ref[...], m_ref[...], v_ref[...])
        g_out[...] = g
        d_out[...] = delta
        m_out[...] = m_new
        v_out[...] = v_new

    return pl.pallas_call(
        kern, name=name, out_shape=[jax.ShapeDtypeStruct((r, c), F32)] * 4,
        compiler_params=_params(),
    )(gathered, w, m, v)


def _cols_from_blocks(g):
    n, k, nb = g.shape
    return jnp.transpose(g, (1, 0, 2)).reshape(k, n * nb)


def _blocks_from_cols(w):
    k, n = w.shape
    return jnp.transpose(w.reshape(k, N_DEV, n // N_DEV), (1, 0, 2))


def _pack_small(parts):
    flat = []
    for p in parts:
        v = p.reshape(-1)
        flat.append(jnp.pad(v, (0, (-v.shape[0]) % (8 * LANES))))
    return jnp.concatenate(flat).reshape(-1, LANES)


def _unpack_small(buf, like):
    out, pos = [], 0
    flat = buf.reshape(-1)
    for p in like:
        size = p.size
        out.append(flat[pos:pos + size].reshape(p.shape))
        pos += size + (-size) % (8 * LANES)
    return out


def kernel(x, mem, ffn1_norm, ffn1_w_gate_up, ffn1_w_down, mix_norm, mem_norm, w_in, b_forget, pool_w, pool_scale, w_pool_up, fox_q_norm, fox_k_norm, w_fox_o, w_mem_kv, mem_q_norm, mem_k_norm, w_mem_o, w_out, ffn2_norm, ffn2_w_gate_up, ffn2_w_down, loss_target, m_ffn1_norm, m_ffn1_w_gate_up, m_ffn1_w_down, m_mix_norm, m_mem_norm, m_w_in, m_b_forget, m_pool_w, m_pool_scale, m_w_pool_up, m_fox_q_norm, m_fox_k_norm, m_w_fox_o, m_w_mem_kv, m_mem_q_norm, m_mem_k_norm, m_w_mem_o, m_w_out, m_ffn2_norm, m_ffn2_w_gate_up, m_ffn2_w_down, v_ffn1_norm, v_ffn1_w_gate_up, v_ffn1_w_down, v_mix_norm, v_mem_norm, v_w_in, v_b_forget, v_pool_w, v_pool_scale, v_w_pool_up, v_fox_q_norm, v_fox_k_norm, v_w_fox_o, v_w_mem_kv, v_mem_q_norm, v_mem_k_norm, v_w_mem_o, v_w_out, v_ffn2_norm, v_ffn2_w_gate_up, v_ffn2_w_down):
    names = ["ffn1_norm", "ffn1_w_gate_up", "ffn1_w_down", "mix_norm", "mem_norm", "w_in", "b_forget", "pool_w",
             "pool_scale", "w_pool_up", "fox_q_norm", "fox_k_norm", "w_fox_o", "w_mem_kv", "mem_q_norm",
             "mem_k_norm", "w_mem_o", "w_out", "ffn2_norm", "ffn2_w_gate_up", "ffn2_w_down"]
    w_args = [ffn1_norm, ffn1_w_gate_up, ffn1_w_down, mix_norm, mem_norm, w_in, b_forget, pool_w, pool_scale,
              w_pool_up, fox_q_norm, fox_k_norm, w_fox_o, w_mem_kv, mem_q_norm, mem_k_norm, w_mem_o, w_out,
              ffn2_norm, ffn2_w_gate_up, ffn2_w_down]
    m_args = [m_ffn1_norm, m_ffn1_w_gate_up, m_ffn1_w_down, m_mix_norm, m_mem_norm, m_w_in, m_b_forget, m_pool_w,
              m_pool_scale, m_w_pool_up, m_fox_q_norm, m_fox_k_norm, m_w_fox_o, m_w_mem_kv, m_mem_q_norm,
              m_mem_k_norm, m_w_mem_o, m_w_out, m_ffn2_norm, m_ffn2_w_gate_up, m_ffn2_w_down]
    v_args = [v_ffn1_norm, v_ffn1_w_gate_up, v_ffn1_w_down, v_mix_norm, v_mem_norm, v_w_in, v_b_forget, v_pool_w,
              v_pool_scale, v_w_pool_up, v_fox_q_norm, v_fox_k_norm, v_w_fox_o, v_w_mem_kv, v_mem_q_norm,
              v_mem_k_norm, v_w_mem_o, v_w_out, v_ffn2_norm, v_ffn2_w_gate_up, v_ffn2_w_down]
    W = dict(zip(names, w_args))
    M = dict(zip(names, m_args))
    V = dict(zip(names, v_args))

    batch, seq, d = x.shape
    mlen = mem.shape[1]
    t = batch * seq
    gate_w = 3 * d
    z_gate, z_q = 0, gate_w
    z_k, z_v = z_q + FOX_WIDTH, z_q + 2 * FOX_WIDTH
    z_u = z_q + 3 * FOX_WIDTH
    z_qm = z_u + POOL_WIDTH
    z_f = z_qm + MEM_WIDTH
    z_width = -(-(z_f + F_PAD) // 512) * 512

    x2d = x.reshape(t, d)
    mem2d = mem.reshape(batch * mlen, d)
    tgt2d = loss_target.reshape(t, d)

    big = ["ffn1_w_gate_up", "ffn1_w_down", "w_in", "w_pool_up", "w_fox_o", "w_mem_kv", "w_mem_o", "w_out",
           "ffn2_w_gate_up", "ffn2_w_down"]
    col_sharded = {"ffn1_w_gate_up", "ffn2_w_gate_up", "w_in", "w_pool_up", "w_fox_o", "w_mem_o"}
    mixer_small = ["w_pool_up", "w_fox_o", "w_mem_kv", "w_mem_o", "w_out"]
    shard = {n: W[n][0].astype(BF16) for n in big}

    def rows_of(g):
        return g.reshape(-1, g.shape[2])

    gu_caps_nn, gu_caps_nt, gu_caps_tn = (1024, 1408, 2048), (1024, 1024, 1408), (1024, 1408, 2048)
    down_caps, dwd_caps = (1024, 1024, 2816), (1408, 1024, 2048)

    (wgu1,) = _comm_call(_gather_first([shard["ffn1_w_gate_up"]]), "ag_first")
    h1, (wgu1,) = _rms_fwd(x2d, W["ffn1_norm"], "ffn1_rms", comm=_gather_second([wgu1]))
    w_in_top, w_in_bot = shard["w_in"][:d // 2], shard["w_in"][d // 2:]
    gu1, bufs = _mm(h1, wgu1, "nn", "ffn1_gu", caps=gu_caps_nn, b_blocks=True,
                    comm=_gather_first([shard["ffn1_w_down"], w_in_top]))
    a1, (wd1_g, w_in_top_g) = _swiglu_fwd(gu1, "ffn1_swiglu", comm=_gather_second(bufs))
    wd1 = rows_of(wd1_g)
    x1, (w_in_bot_g,) = _mm(a1, wd1, "nn", "ffn1_down", scale=0.5, res=x2d, caps=down_caps,
                            comm=_gather_first([w_in_bot]))
    h2, (w_in_bot_g,) = _rms_fwd(x1, W["mix_norm"], "mix_rms", comm=_gather_second([w_in_bot_g]))

    o_u, o_q, o_v = 0, POOL_WIDTH, POOL_WIDTH + 2 * FOX_WIDTH
    o_f = o_v + FOX_WIDTH
    o_qm = o_f + FOX_HEADS
    o_g = o_qm + MEM_WIDTH
    wi = jnp.concatenate([_cols_from_blocks(w_in_top_g), _cols_from_blocks(w_in_bot_g)], axis=0)
    w_in_pad = jnp.concatenate(
        [wi[:, o_g:o_g + gate_w], wi[:, o_q:o_q + 3 * FOX_WIDTH], wi[:, o_u:o_u + POOL_WIDTH],
         wi[:, o_qm:o_qm + MEM_WIDTH], wi[:, o_f:o_f + FOX_HEADS],
         jnp.zeros((d, z_width - z_f - FOX_HEADS), BF16)], axis=1)
    group_b = mixer_small + ["ffn2_w_down"]
    z, bufs = _mm(h2, w_in_pad, "nn", "mix_in", comm=_gather_first([shard[n] for n in group_b]))

    pool_w_b = W["pool_w"][0].astype(BF16)

    def pool_fwd_body(u, pw, ps):
        row = lax.broadcasted_iota(jnp.int32, (seq, 1), 0)
        diffs, mixed = [], []
        for g in range(POOL_GROUPS):
            ug = u[:, g * POOL_GROUP_DIM:(g + 1) * POOL_GROUP_DIM]
            cnt = jnp.minimum(row + 1, POOL_WINDOWS[g]).astype(F32)
            diff = _window_sum(ug, g + 1, True) / cnt - ug
            diffs.append(diff)
            mixed.append(lax.dot_general(diff.astype(BF16), pw[g], _DIMS["nn"], preferred_element_type=F32))
        diffs = jnp.concatenate(diffs, axis=1)
        mixed = jnp.concatenate(mixed, axis=1)
        return mixed * ps, diffs, mixed

    ypp, pool_diff, pool_mixed = _rowwise(
        pool_fwd_body, [(z, z_u, POOL_WIDTH)], [pool_w_b, W["pool_scale"]],
        [(POOL_WIDTH, BF16), (POOL_WIDTH, BF16), (POOL_WIDTH, F32)], [], "pool_fwd", seq)

    gq = jnp.tile(W["fox_q_norm"], (1, FOX_HEADS))
    gk = jnp.tile(W["fox_k_norm"], (1, FOX_HEADS))
    b_pad = jnp.pad(W["b_forget"], ((0, 0), (0, F_PAD - FOX_HEADS)))

    def fox_prep_body(q, k, v, f, gqv, gkv, bv):
        qn_scaled = _head_norm(q, gqv, FOX_HEAD_DIM).astype(BF16) * FOX_HEAD_DIM ** -0.5
        return qn_scaled, _head_norm(k, gkv, FOX_HEAD_DIM), v, _log_sigmoid(f + bv)

    tm_e = _tile(t, 256, 16)
    (qn, kn, vb, logf), bufs = _rowwise(
        fox_prep_body, [(z, z_q, FOX_WIDTH), (z, z_k, FOX_WIDTH), (z, z_v, FOX_WIDTH), (z, z_f, F_PAD)],
        [gq, gk, b_pad], [(FOX_WIDTH, BF16), (FOX_WIDTH, BF16), (FOX_WIDTH, BF16), (F_PAD, F32)], [], "fox_prep", tm_e,
        comm=_gather_second(bufs))
    full = {n: (_cols_from_blocks(g) if n in col_sharded else rows_of(g)) for n, g in zip(group_b, bufs)}
    wd2 = full["ffn2_w_down"]
    y_pool = _mm(ypp, full["w_pool_up"], "nn", "pool_up")
    csum = _rowwise(lambda v: (_scan_rows(v, True),), [_whole(logf)], [], [(F_PAD, F32)], [], "fox_cumsum", seq)[0]
    cks = jnp.transpose(csum.reshape(batch, seq, F_PAD)[:, :, :FOX_HEADS], (0, 2, 1)).reshape(
        batch * FOX_HEADS // 2, 2, seq)
    (o_fox, lse), (wgu2,) = _fox_fwd(qn, kn, vb, cks, batch, seq, "fox_fwd",
                                     comm=_gather_first([shard["ffn2_w_gate_up"]]))
    y_fox = _mm(o_fox, full["w_fox_o"], "nn", "fox_o")

    memn = _rms_fwd(mem2d, W["mem_norm"], "mem_rms")
    kv = _mm(memn, full["w_mem_kv"], "nn", "mem_kv")
    gqm = jnp.tile(W["mem_q_norm"], (1, MEM_HEADS))
    gkm = jnp.tile(W["mem_k_norm"], (1, MEM_HEADS))
    qmn = _rowwise(lambda q, g: (_head_norm(q, g, MEM_HEAD_DIM),), [(z, z_qm, MEM_WIDTH)], [gqm],
                   [(MEM_WIDTH, BF16)], [], "memq_prep", tm_e)[0]
    kmn, vmb = _rowwise(lambda k, v, g: (_head_norm(k, g, MEM_HEAD_DIM), v),
                        [(kv, 0, MEM_WIDTH), (kv, MEM_WIDTH, MEM_WIDTH)], [gkm],
                        [(MEM_WIDTH, BF16), (MEM_WIDTH, BF16)], [], "memk_prep", _tile(batch * mlen, 256, 16))
    o_mem = _mem_fwd(qmn, kmn, vmb, batch, seq, mlen, "mem_fwd")
    y_mem = _mm(o_mem, full["w_mem_o"], "nn", "mem_o")

    def gate_fwd_body(gp, gf, gm, yp, yf, ym):
        return ((_sigmoid(gp) * yp + _sigmoid(gf) * yf) + _sigmoid(gm) * ym,)

    tm_g = _tile(t, 128, 16)
    (merged,), (wgu2,) = _rowwise(
        gate_fwd_body, [(z, 0, d), (z, d, d), (z, 2 * d, d), _whole(y_pool), _whole(y_fox), _whole(y_mem)],
        [], [(d, BF16)], [], "gate_fwd", tm_g, comm=_gather_second([wgu2]))
    x2 = _mm(merged, full["w_out"], "nn", "mix_out", res=x1)

    h3 = _rms_fwd(x2, W["ffn2_norm"], "ffn2_rms")
    gu2 = _mm(h3, wgu2, "nn", "ffn2_gu", caps=gu_caps_nn, b_blocks=True)
    a2 = _swiglu_fwd(gu2, "ffn2_swiglu")
    x3 = _mm(a2, wd2, "nn", "ffn2_down", scale=0.5, res=x2, caps=down_caps)
    dy, dy_b, loss_part = _loss_head(x3, tgt2d, "loss")
    loss = lax.psum(loss_part[0, 0], ("x", "y", "c"))

    cx, cy, cc = _position()
    where = jnp.stack([cc, 2 * cx + cy]).astype(jnp.int32)
    G, own, landed = {}, {}, {}

    def row_blocks(g):
        return g.reshape(N_DEV, g.shape[0] // N_DEV, g.shape[1])

    def pair_add(n, partial, from_core):
        own[n], chip_sums = _rs_pair_add(partial, from_core, where, "rs_add_" + n)
        return chip_sums

    da2 = _mm(dy_b, wd2, "nt", "ffn2_da", scale=0.5)
    p_wd2 = row_blocks(_mm(a2, dy_b, "tn", "ffn2_dwd", scale=0.5, caps=dwd_caps))
    dgu2, (l_wd2,) = _swiglu_bwd(gu2, da2, "ffn2_dswiglu", comm=_scatter_first([p_wd2]))
    s_wd2 = pair_add("ffn2_w_down", p_wd2, l_wd2)
    p_wgu2, (landed["ffn2_w_down"],) = _mm(h3, dgu2, "tn", "ffn2_dwgu", caps=gu_caps_tn, out_blocks=True,
                                           comm=_scatter_second([s_wd2]))
    dh3, (l_wgu2,) = _mm(dgu2, wgu2, "nt", "ffn2_dh", caps=gu_caps_nt, b_blocks=True,
                         comm=_scatter_first([p_wgu2]))
    s_wgu2 = pair_add("ffn2_w_gate_up", p_wgu2, l_wgu2)
    dx2, dx2_b, G["ffn2_norm"] = _rms_bwd(x2, W["ffn2_norm"], dh3, dy, "ffn2_drms")

    dmerged = _mm(dx2_b, full["w_out"], "nt", "mix_out_dx")
    P = {"w_out": row_blocks(_mm(merged, dx2_b, "tn", "mix_out_dw"))}

    def gate_bwd_body(gp, gf, gm, yp, yf, ym, dm):
        outs_dl, outs_dy = [], []
        for gl, yv in ((gp, yp), (gf, yf), (gm, ym)):
            s = _sigmoid(gl)
            outs_dl.append((dm * yv) * (s * (1.0 - s)))
            outs_dy.append(dm * s)
        return (jnp.concatenate(outs_dl, axis=1), *outs_dy)

    dz_gate, dy_pool, dy_fox, dy_mem = _rowwise(
        gate_bwd_body, [(z, 0, d), (z, d, d), (z, 2 * d, d), _whole(y_pool), _whole(y_fox), _whole(y_mem), _whole(dmerged)],
        [], [(gate_w, BF16), (d, BF16), (d, BF16), (d, BF16)], [], "gate_bwd", tm_g)

    dypp = _mm(dy_pool, full["w_pool_up"], "nt", "pool_up_dx")
    P["w_pool_up"] = _blocks_from_cols(_mm(ypp, dy_pool, "tn", "pool_up_dw"))

    def pool_bwd_body(dyv, mixed, diff, pw, ps):
        row = lax.broadcasted_iota(jnp.int32, (seq, 1), 0)
        d_scale = jnp.sum(dyv * mixed, axis=0, keepdims=True)
        dmix = (dyv * ps).astype(BF16)
        du, dpw = [], []
        for g in range(POOL_GROUPS):
            sl = slice(g * POOL_GROUP_DIM, (g + 1) * POOL_GROUP_DIM)
            dmg = dmix[:, sl]
            ddiff = lax.dot_general(dmg, pw[g], _DIMS["nt"], preferred_element_type=F32)
            dpw.append(lax.dot_general(diff[:, sl], dmg, _DIMS["tn"], preferred_element_type=F32))
            cnt = jnp.minimum(row + 1, POOL_WINDOWS[g]).astype(F32)
            du.append(_window_sum(ddiff / cnt, g + 1, False) - ddiff)
        return jnp.concatenate(du, axis=1), d_scale, jnp.concatenate(dpw, axis=0)

    dz_u, G["pool_scale"], d_pool_w = _rowwise(
        pool_bwd_body, [_whole(dypp), _whole(pool_mixed), _whole(pool_diff)], [pool_w_b, W["pool_scale"]],
        [(POOL_WIDTH, BF16)], [((1, POOL_WIDTH), F32), ((POOL_WIDTH, POOL_GROUP_DIM), F32)],
        "pool_bwd", seq)
    G["pool_w"] = d_pool_w.reshape(1, POOL_GROUPS, POOL_GROUP_DIM, POOL_GROUP_DIM)

    do_fox = _mm(dy_fox, full["w_fox_o"], "nt", "fox_o_dx")
    P["w_fox_o"] = _blocks_from_cols(_mm(o_fox, dy_fox, "tn", "fox_o_dw"))
    (dkn, dz_v, dqn, dcks, dcq), (landed["ffn2_w_gate_up"],) = _fox_bwd(
        qn, kn, vb, cks, o_fox, lse, do_fox, batch, seq, "fox_bwd", comm=_scatter_second([s_wgu2]))
    dcs = jnp.transpose(dcks.reshape(batch, FOX_HEADS, seq), (0, 2, 1)).reshape(t, FOX_HEADS)
    dcs = jnp.pad(dcs, ((0, 0), (0, F_PAD - FOX_HEADS)))
    dcq = jnp.pad(dcq.reshape(t, FOX_HEADS, FOX_HEAD_DIM)[:, :, 0], ((0, 0), (0, F_PAD - FOX_HEADS)))

    def fox_f_bwd_body(dc, dc_rows, f, bv):
        lane = lax.broadcasted_iota(jnp.int32, (1, F_PAD), 1)
        dlogf = _scan_rows(dc + dc_rows, False)
        df = jnp.where(lane < FOX_HEADS, dlogf * _sigmoid(-(f + bv)), 0.0)
        return df, jnp.sum(df, axis=0, keepdims=True)

    dz_f, db_pad = _rowwise(fox_f_bwd_body, [_whole(dcs), _whole(dcq), (z, z_f, F_PAD)], [b_pad], [(F_PAD, BF16)],
                            [((1, F_PAD), F32)], "fox_f_bwd", seq)
    G["b_forget"] = db_pad[:, :FOX_HEADS]

    def fox_qk_bwd_body(q, k, dq, dk, gqv, gkv):
        dqr, dgq = _head_norm_bwd(q, gqv, dq, FOX_HEAD_DIM)
        dkr, dgk = _head_norm_bwd(k, gkv, dk, FOX_HEAD_DIM)
        return dqr, dkr, dgq, dgk

    dz_q, dz_k, dgq_t, dgk_t = _rowwise(
        fox_qk_bwd_body, [(z, z_q, FOX_WIDTH), (z, z_k, FOX_WIDTH), _whole(dqn), _whole(dkn)], [gq, gk],
        [(FOX_WIDTH, BF16), (FOX_WIDTH, BF16)], [((1, FOX_WIDTH), F32), ((1, FOX_WIDTH), F32)], "fox_qk_bwd", tm_e)
    G["fox_q_norm"] = jnp.sum(dgq_t.reshape(FOX_HEADS, FOX_HEAD_DIM), axis=0, keepdims=True)
    G["fox_k_norm"] = jnp.sum(dgk_t.reshape(FOX_HEADS, FOX_HEAD_DIM), axis=0, keepdims=True)

    do_mem = _mm(dy_mem, full["w_mem_o"], "nt", "mem_o_dx")
    P["w_mem_o"] = _blocks_from_cols(_mm(o_mem, dy_mem, "tn", "mem_o_dw"))
    dqmn, dkmn, dvm = _mem_bwd(qmn, kmn, vmb, do_mem, batch, seq, mlen, "mem_bwd")
    dz_qm, dgqm_t = _rowwise(lambda q, dq, g: _head_norm_bwd(q, g, dq, MEM_HEAD_DIM),
                             [(z, z_qm, MEM_WIDTH), _whole(dqmn)], [gqm], [(MEM_WIDTH, BF16)],
                             [((1, MEM_WIDTH), F32)], "memq_bwd", tm_e)

    def memk_bwd_body(k, dk, dv, g):
        dkr, dg = _head_norm_bwd(k, g, dk, MEM_HEAD_DIM)
        return jnp.concatenate([dkr, dv], axis=1), dg

    dkv, dgkm_t = _rowwise(memk_bwd_body, [(kv, 0, MEM_WIDTH), _whole(dkmn), _whole(dvm)], [gkm],
                           [(2 * MEM_WIDTH, BF16)], [((1, MEM_WIDTH), F32)], "memk_bwd", _tile(batch * mlen, 256, 16))
    G["mem_q_norm"] = jnp.sum(dgqm_t.reshape(MEM_HEADS, MEM_HEAD_DIM), axis=0, keepdims=True)
    G["mem_k_norm"] = jnp.sum(dgkm_t.reshape(MEM_HEADS, MEM_HEAD_DIM), axis=0, keepdims=True)
    P["w_mem_kv"] = row_blocks(_mm(memn, dkv, "tn", "mem_kv_dw"))
    dmemn = _mm(dkv, full["w_mem_kv"], "nt", "mem_kv_dx")
    _, _, G["mem_norm"] = _rms_bwd(mem2d, W["mem_norm"], dmemn, None, "mem_drms")

    dz = jnp.concatenate([dz_gate, dz_q, dz_k, dz_v, dz_u, dz_qm, dz_f,
                          jnp.zeros((t, z_width - z_f - F_PAD), BF16)], axis=1)
    d_w_in_pad, l_small = _mm(h2, dz, "tn", "mix_in_dw", comm=_scatter_first([P[n] for n in mixer_small]))
    s_small = [pair_add(n, P[n], l) for n, l in zip(mixer_small, l_small)]
    p = d_w_in_pad
    p_w_in = _blocks_from_cols(jnp.concatenate(
        [p[:, z_u:z_u + POOL_WIDTH], p[:, z_q:z_q + 3 * FOX_WIDTH], p[:, z_f:z_f + FOX_HEADS],
         p[:, z_qm:z_qm + MEM_WIDTH], p[:, z_gate:z_gate + gate_w]], axis=1))
    dh2, rest = _mm(dz, w_in_pad, "nt", "mix_in_dx", caps=(1024, 1024, 1792),
                    comm=_join(_scatter_second(s_small), _scatter_first([p_w_in])))
    for n, l in zip(mixer_small, rest[:len(mixer_small)]):
        landed[n] = l
    s_w_in = pair_add("w_in", p_w_in, rest[-1])
    dx1, dx1_b, G["mix_norm"] = _rms_bwd(x1, W["mix_norm"], dh2, dx2, "mix_drms")

    da1 = _mm(dx1_b, wd1, "nt", "ffn1_da", scale=0.5)
    dgu1 = _swiglu_bwd(gu1, da1, "ffn1_dswiglu")
    p_wgu1, (landed["w_in"],) = _mm(h1, dgu1, "tn", "ffn1_dwgu", caps=gu_caps_tn, out_blocks=True,
                                    comm=_scatter_second([s_w_in]))
    d_wd1, (l_wgu1,) = _mm(a1, dx1_b, "tn", "ffn1_dwd", scale=0.5, caps=dwd_caps, comm=_scatter_first([p_wgu1]))
    p_wd1 = row_blocks(d_wd1)
    s_wgu1 = pair_add("ffn1_w_gate_up", p_wgu1, l_wgu1)
    dh1, (landed["ffn1_w_gate_up"], l_wd1) = _mm(dgu1, wgu1, "nt", "ffn1_dh", caps=gu_caps_nt, b_blocks=True,
                                                 comm=_join(_scatter_second([s_wgu1]), _scatter_first([p_wd1])))
    s_wd1 = pair_add("ffn1_w_down", p_wd1, l_wd1)
    (dx0, _, G["ffn1_norm"]), (landed["ffn1_w_down"],) = _rms_bwd(x2d, W["ffn1_norm"], dh1, dx1, "ffn1_drms",
                                                                  comm=_scatter_second([s_wd1]))
    grad_x = dx0.reshape(batch, seq, d)

    out_g, out_d, out_m, out_v = {}, {}, {}, {}
    for n in big:
        res = _rs_finish_adam(own[n], landed[n], W[n][0], M[n][0], V[n][0], "adam_" + n)
        out_g[n], out_d[n], out_m[n], out_v[n] = [r[None] for r in res]

    small = [n for n in names if n not in big]
    g_small = _pack_small([G[n].reshape(W[n].shape) for n in small])
    all_small = _all_gather_vmem(g_small, "ag_small_grads")
    res = _allreduce_adam(all_small, _pack_small([W[n] for n in small]), _pack_small([M[n] for n in small]),
                          _pack_small([V[n] for n in small]), "adam_small")
    like = [W[n] for n in small]
    for dst, buf in zip((out_g, out_d, out_m, out_v), res):
        for n, a in zip(small, _unpack_small(buf, like)):
            dst[n] = a

    return (loss, grad_x, *[out_g[n] for n in names], *[out_d[n] for n in names],
            *[out_m[n] for n in names], *[out_v[n] for n in names])
```

```python
import functools

import jax
import jax.numpy as jnp
from jax import lax
from jax.experimental import pallas as pl
from jax.experimental.pallas import tpu as pltpu

F32 = jnp.float32
BF16 = jnp.bfloat16
MESH = pl.DeviceIdType.MESH

N_DEV = 8
EPS = 1e-6
FOX_HEADS = 16
FOX_HEAD_DIM = 64
FOX_WIDTH = FOX_HEADS * FOX_HEAD_DIM
MEM_HEADS = 4
MEM_HEAD_DIM = 128
MEM_WIDTH = MEM_HEADS * MEM_HEAD_DIM
POOL_GROUPS = 4
POOL_GROUP_DIM = 128
POOL_WIDTH = POOL_GROUPS * POOL_GROUP_DIM
POOL_WINDOWS = (2, 4, 8, 16)
LANES = 128
F_PAD = LANES

ADAM_LR = 0.001
ADAM_B1 = 0.9
ADAM_B2 = 0.999
ADAM_EPS = 1e-08
ADAM_WD = 0.01
ADAM_STEP = 10

VMEM_LIMIT = 56 * 1024 * 1024
NEG = -1e30

ANY = pl.BlockSpec(memory_space=pl.ANY)


def _params(sem=None):
    return pltpu.CompilerParams(dimension_semantics=sem, vmem_limit_bytes=VMEM_LIMIT)


MXU_DIM = 256


def _tile(dim, cap, align):
    best = None
    t = align
    while t <= min(dim, cap):
        if dim % t == 0:
            best = t
        t += align
    return dim if best is None else best


def _mxu_tile(dim, cap):
    wide, fine = _tile(dim, cap, MXU_DIM), _tile(dim, cap, LANES)
    whole_widths = wide % MXU_DIM == 0 and wide <= cap
    return wide if whole_widths and fine < 2 * wide else fine


class _Sems:
    def __init__(self, send, recv, local):
        self.send, self.recv, self.local = send, recv, local
        self.n_remote = self.n_local = 0

    def remote(self):
        i = self.n_remote
        self.n_remote += 1
        return self.send.at[i], self.recv.at[i]

    def one_local(self):
        i = self.n_local
        self.n_local += 1
        return self.local.at[i]


class _Comm:
    def __init__(self, srcs, outs, n_remote, n_local, plan, in_place=False):
        self.srcs, self.outs, self.n_remote, self.n_local = list(srcs), list(outs), n_remote, n_local
        self.plan, self.in_place = plan, in_place

    def aliases(self, n_in, n_out):
        return {n_in + i: n_out + i for i in range(len(self.srcs))} if self.in_place else {}

    def scratch(self):
        return [pltpu.SemaphoreType.DMA((self.n_remote,)), pltpu.SemaphoreType.DMA((self.n_remote,)),
                pltpu.SemaphoreType.DMA((max(self.n_local, 1),))]

    def run(self, src_refs, out_refs, sem_refs, first, last):
        def copies():
            return self.plan(list(src_refs), list(out_refs), _Sems(*sem_refs))

        return (lambda: _when(first, lambda: [cp.start() for cp in copies()]),
                lambda: _when(last, lambda: [cp.wait() for cp in copies()]))


def _when(cond, fn):
    @pl.when(cond)
    def _():
        fn()


def _join(*comms):
    comms = [c for c in comms if c is not None]
    assert all(not c.in_place for c in comms)

    def plan(src_refs, out_refs, sems):
        copies, si, oi = [], 0, 0
        for c in comms:
            copies += c.plan(src_refs[si:si + len(c.srcs)], out_refs[oi:oi + len(c.outs)], sems)
            si += len(c.srcs)
            oi += len(c.outs)
        return copies

    return _Comm(sum((c.srcs for c in comms), []), sum((c.outs for c in comms), []),
                 sum(c.n_remote for c in comms), sum(c.n_local for c in comms), plan)


def _split_refs(refs, n_in, n_out, comm):
    if comm is None:
        return refs[:n_in], refs[n_in:n_in + n_out], refs[n_in + n_out:], (), (), ()
    ns, no = len(comm.srcs), len(comm.outs)
    ins = refs[:n_in]
    srcs = refs[n_in:n_in + ns]
    outs = refs[n_in + ns:n_in + ns + n_out]
    couts = refs[n_in + ns + n_out:n_in + ns + n_out + no]
    rest = refs[n_in + ns + n_out + no:]
    return ins, outs, rest[:-3], srcs, couts, rest[-3:]


def _sigmoid(x):
    return 1.0 / (1.0 + jnp.exp(-x))


_DIMS = {"nn": (((1,), (0,)), ((), ())), "nt": (((1,), (1,)), ((), ())), "tn": (((0,), (0,)), ((), ()))}


def _mm(a, b, mode, name, out_dtype=F32, scale=1.0, res=None, caps=None, b_blocks=False, out_blocks=False,
        comm=None):
    nblk = N_DEV
    if b_blocks:
        _, r0, c0 = b.shape
        b_shape = (r0, nblk * c0)
    else:
        b_shape = b.shape
    if mode == "nn":
        (m, k), (k2, n) = a.shape, b_shape
    elif mode == "nt":
        (m, k), (n, k2) = a.shape, b_shape
    else:
        (k, m), (k2, n) = a.shape, b_shape
    assert k == k2, (name, a.shape, b.shape)
    cm, cn, ck = caps or ((1024, 512, 2048) if k <= 2048 else (1024, 1024, 2048))
    n_unit = n // nblk if (out_blocks or (b_blocks and mode == "nn")) else n
    k_unit = k // nblk if (b_blocks and mode == "nt") else k
    tm, tn, tk = _mxu_tile(m, cm), _mxu_tile(n_unit, cn), _mxu_tile(k_unit, ck)
    nk = k // tk
    npb, kpb = n_unit // tn, k_unit // tk
    if mode == "nn":
        a_spec = pl.BlockSpec((tm, tk), lambda i, j, kk: (i, kk))
        b_spec = pl.BlockSpec((tk, tn), lambda i, j, kk: (kk, j))
        if b_blocks:
            b_spec = pl.BlockSpec((None, tk, tn), lambda i, j, kk: (j // npb, kk, j % npb))
    elif mode == "nt":
        a_spec = pl.BlockSpec((tm, tk), lambda i, j, kk: (i, kk))
        b_spec = pl.BlockSpec((tn, tk), lambda i, j, kk: (j, kk))
        if b_blocks:
            b_spec = pl.BlockSpec((None, tn, tk), lambda i, j, kk: (kk // kpb, j, kk % kpb))
    else:
        assert not b_blocks
        a_spec = pl.BlockSpec((tk, tm), lambda i, j, kk: (kk, i))
        b_spec = pl.BlockSpec((tk, tn), lambda i, j, kk: (kk, j))
    if out_blocks:
        assert res is None
        o_spec = pl.BlockSpec((None, tm, tn), lambda i, j, kk: (j // npb, i, j % npb))
        o_shape = jax.ShapeDtypeStruct((nblk, m, n // nblk), out_dtype)
    else:
        o_spec = pl.BlockSpec((tm, tn), lambda i, j, kk: (i, j))
        o_shape = jax.ShapeDtypeStruct((m, n), out_dtype)
    in_specs = [a_spec, b_spec] + ([o_spec] if res is not None else [])
    n_in = len(in_specs)
    dims = _DIMS[mode]
    gm, gn = m // tm, n // tn

    def kern(*refs):
        ins, outs, scratch, c_src, c_out, c_sem = _split_refs(refs, n_in, 1, comm)
        a_ref, b_ref = ins[0], ins[1]
        res_ref = ins[2] if res is not None else None
        o_ref = outs[0]
        if comm is not None:
            i, j, kq = pl.program_id(0), pl.program_id(1), pl.program_id(2)
            first = jnp.logical_and(jnp.logical_and(i == 0, j == 0), kq == 0)
            last = jnp.logical_and(jnp.logical_and(i == gm - 1, j == gn - 1), kq == nk - 1)
            start_comm, wait_comm = comm.run(c_src, c_out, c_sem, first, last)
            start_comm()
        a_tile = a_ref[...].astype(BF16)
        if scale != 1.0:
            a_tile = a_tile * scale

        def product():
            return lax.dot_general(a_tile, b_ref[...].astype(BF16), dims, preferred_element_type=F32)

        if nk == 1:
            r = product()
            if res_ref is not None:
                r = res_ref[...] + r
            o_ref[...] = r.astype(out_dtype)
        else:
            acc_ref = scratch[0] if scratch else o_ref
            kk = pl.program_id(2)

            @pl.when(kk == 0)
            def _():
                acc_ref[...] = jnp.zeros_like(acc_ref) if res_ref is None else res_ref[...]

            acc_ref[...] += product()
            if scratch:
                @pl.when(kk == nk - 1)
                def _():
                    o_ref[...] = acc_ref[...].astype(out_dtype)
        if comm is not None:
            wait_comm()

    assert scale in (1.0, 0.5)
    args = (a, b) + ((res,) if res is not None else ())
    scratch_shapes = [pltpu.VMEM((tm, tn), F32)] if (nk > 1 and out_dtype != F32) else []
    if comm is None:
        return pl.pallas_call(
            kern, name=name, grid=(gm, gn, nk), in_specs=in_specs, out_specs=o_spec, out_shape=o_shape,
            scratch_shapes=scratch_shapes, compiler_params=_params(("parallel", "parallel", "arbitrary")),
        )(*args)
    res_all = pl.pallas_call(
        kern, name=name, grid=(gm, gn, nk), in_specs=in_specs + [ANY] * len(comm.srcs),
        out_specs=[o_spec] + [ANY] * len(comm.outs), out_shape=[o_shape] + comm.outs,
        scratch_shapes=scratch_shapes + comm.scratch(), input_output_aliases=comm.aliases(n_in, 1),
        compiler_params=_params(("arbitrary", "arbitrary", "arbitrary")),
    )(*args, *comm.srcs)
    return res_all[0], list(res_all[1:])


def _rowwise(body, ins, params, outs, accs, name, tm, comm=None):
    rows = ins[0][0].shape[0]
    assert rows % tm == 0, (name, rows, tm)
    in_specs = []
    for arr, off, width in ins:
        assert off % width == 0 and arr.shape[0] == rows, (name, arr.shape, off, width)
        in_specs.append(pl.BlockSpec((tm, width), functools.partial(lambda i, c: (i, c), c=off // width)))
    for p in params:
        in_specs.append(pl.BlockSpec(p.shape, functools.partial(lambda i, nd: (0,) * nd, nd=p.ndim)))
    out_specs = [pl.BlockSpec((tm, w), lambda i: (i, 0)) for w, _ in outs]
    out_specs += [pl.BlockSpec(s, functools.partial(lambda i, nd: (0,) * nd, nd=len(s))) for s, _ in accs]
    out_shape = [jax.ShapeDtypeStruct((rows, w), d) for w, d in outs]
    out_shape += [jax.ShapeDtypeStruct(s, d) for s, d in accs]
    n_in, n_par, n_out = len(ins), len(params), len(outs)

    steps = rows // tm

    def kern(*refs):
        in_refs, out_refs, _, c_src, c_out, c_sem = _split_refs(refs, n_in + n_par, n_out + len(accs), comm)
        first = pl.program_id(0) == 0
        if comm is not None:
            start_comm, wait_comm = comm.run(c_src, c_out, c_sem, first, pl.program_id(0) == steps - 1)
            start_comm()
        res = body(*[r[...] for r in in_refs])
        for r, v in zip(out_refs[:n_out], res[:n_out]):
            r[...] = v.astype(r.dtype)
        for r, v in zip(out_refs[n_out:], res[n_out:]):
            @pl.when(first)
            def _(r=r, v=v):
                r[...] = v.astype(r.dtype)

            @pl.when(jnp.logical_not(first))
            def _(r=r, v=v):
                r[...] += v.astype(r.dtype)
        if comm is not None:
            wait_comm()

    args = [a for a, _, _ in ins] + list(params)
    if comm is None:
        return pl.pallas_call(
            kern, name=name, grid=(steps,), in_specs=in_specs, out_specs=out_specs, out_shape=out_shape,
            compiler_params=_params(("arbitrary",) if accs else ("parallel",)),
        )(*args)
    res_all = pl.pallas_call(
        kern, name=name, grid=(steps,), in_specs=in_specs + [ANY] * len(comm.srcs),
        out_specs=out_specs + [ANY] * len(comm.outs), out_shape=out_shape + comm.outs,
        scratch_shapes=comm.scratch(), input_output_aliases=comm.aliases(len(args), len(out_shape)),
        compiler_params=_params(("arbitrary",)),
    )(*args, *comm.srcs)
    return list(res_all[:len(out_shape)]), list(res_all[len(out_shape):])


def _whole(x):
    return (x, 0, x.shape[1])


def _first(res, comm):
    return res[0] if comm is None else (res[0][0], res[1])


def _rms_fwd(x, gain, name, comm=None):
    def body(xv, g):
        r = lax.rsqrt(jnp.mean(xv * xv, axis=-1, keepdims=True) + EPS)
        return ((xv * r) * g,)

    return _first(_rowwise(body, [_whole(x)], [gain], [(x.shape[1], BF16)], [], name, _tile(x.shape[0], 256, 16),
                           comm=comm), comm)


def _rms_bwd(x, gain, dh, dres, name, comm=None):
    d = x.shape[1]

    def body(*vals):
        if dres is None:
            xv, dhv, g = vals
        else:
            xv, dhv, drv, g = vals
        r = lax.rsqrt(jnp.mean(xv * xv, axis=-1, keepdims=True) + EPS)
        xh = xv * r
        w = dhv * g
        dx = r * (w - xh * jnp.mean(w * xh, axis=-1, keepdims=True))
        if dres is not None:
            dx = drv + dx
        return dx, dx, jnp.sum(dhv * xh, axis=0, keepdims=True)

    ins = [_whole(x), _whole(dh)] + ([_whole(dres)] if dres is not None else [])
    return _rowwise(body, ins, [gain], [(d, F32), (d, BF16)], [((1, d), F32)], name, _tile(x.shape[0], 256, 16),
                    comm=comm)


def _swiglu_fwd(gu, name, comm=None):
    f = gu.shape[1] // 2

    def body(g, u):
        return ((g * _sigmoid(g)) * u,)

    return _first(_rowwise(body, [(gu, 0, f), (gu, f, f)], [], [(f, BF16)], [], name, _tile(gu.shape[0], 128, 16),
                           comm=comm), comm)


def _swiglu_bwd(gu, da, name, comm=None):
    f = gu.shape[1] // 2

    def body(g, u, dav):
        s = _sigmoid(g)
        dg = dav * u * (s * (1.0 + g * (1.0 - s)))
        du = dav * (g * s)
        return (jnp.concatenate([dg, du], axis=1),)

    return _first(_rowwise(body, [(gu, 0, f), (gu, f, f), _whole(da)], [], [(2 * f, BF16)], [], name,
                           _tile(gu.shape[0], 128, 16), comm=comm), comm)


def _loss_head(y, target, name):
    d = y.shape[1]

    def body(yv, tv):
        e = yv - tv
        part = jnp.sum(jnp.sum(e * e, axis=1, keepdims=True), axis=0, keepdims=True)
        return e / d, e / d, jnp.broadcast_to((0.5 / d) * part, (1, LANES))

    return _rowwise(body, [_whole(y), _whole(target)], [], [(d, F32), (d, BF16)], [((1, LANES), F32)], name,
                    _tile(y.shape[0], 256, 16))


def _head_mean(v, head_dim):
    cols = []
    lane = lax.broadcasted_iota(jnp.int32, (1, LANES), 1)
    for j in range(v.shape[1] // LANES):
        blk = v[:, j * LANES:(j + 1) * LANES]
        if head_dim == LANES:
            m = jnp.sum(blk, axis=-1, keepdims=True)
            cols.append(jnp.broadcast_to(m, blk.shape))
        else:
            lo = jnp.sum(jnp.where(lane < head_dim, blk, 0.0), axis=-1, keepdims=True)
            hi = jnp.sum(jnp.where(lane >= head_dim, blk, 0.0), axis=-1, keepdims=True)
            cols.append(jnp.where(lane < head_dim, lo, hi))
    return jnp.concatenate(cols, axis=1) / head_dim


def _head_norm(xv, g, head_dim):
    r = lax.rsqrt(_head_mean(xv * xv, head_dim) + EPS)
    return (xv * r) * g


def _head_norm_bwd(xv, g, dy, head_dim):
    r = lax.rsqrt(_head_mean(xv * xv, head_dim) + EPS)
    xh = xv * r
    w = dy * g
    dx = r * (w - xh * _head_mean(w * xh, head_dim))
    return dx, jnp.sum(dy * xh, axis=0, keepdims=True)


def _log_sigmoid(x):
    return jnp.minimum(x, 0.0) - jnp.log(1.0 + jnp.exp(-jnp.abs(x)))


def _shift_rows(v, sh, down):
    n = v.shape[0]
    row = lax.broadcasted_iota(jnp.int32, (n, 1), 0)
    if down:
        return jnp.where(row >= sh, pltpu.roll(v, sh, 0), 0.0)
    return jnp.where(row < n - sh, pltpu.roll(v, n - sh, 0), 0.0)


def _scan_rows(v, down):
    sh = 1
    while sh < v.shape[0]:
        v = v + _shift_rows(v, sh, down)
        sh *= 2
    return v


def _window_sum(v, steps, down):
    for s in range(steps):
        v = v + _shift_rows(v, 2 ** s, down)
    return v


def _fox_fwd(qn, kn, vb, cks, batch, seq, name, comm=None):
    t, width = qn.shape
    pairs = width // LANES
    tq = _tile(seq, 512, LANES)
    nq = seq // tq

    def kern(*refs):
        (q_ref, k_ref, v_ref, c_ref), (o_ref, lse_ref), _, c_src, c_out, c_sem = _split_refs(refs, 4, 2, comm)
        qi = pl.program_id(2)
        if comm is not None:
            b_id, p_id = pl.program_id(0), pl.program_id(1)
            first = jnp.logical_and(jnp.logical_and(b_id == 0, p_id == 0), qi == 0)
            last = jnp.logical_and(jnp.logical_and(b_id == batch - 1, p_id == pairs - 1), qi == nq - 1)
            start_comm, wait_comm = comm.run(c_src, c_out, c_sem, first, last)
            start_comm()
        lane = lax.broadcasted_iota(jnp.int32, (1, LANES), 1)
        rowi = lax.broadcasted_iota(jnp.int32, (tq, tq), 0)
        coli = lax.broadcasted_iota(jnp.int32, (tq, tq), 1)
        q_all = q_ref[...]
        o_heads, lse_heads = [], []
        for h in range(2):
            hm = (lane < FOX_HEAD_DIM) if h == 0 else (lane >= FOX_HEAD_DIM)
            q = jnp.where(hm, q_all, jnp.zeros_like(q_all))

            def step(j, carry, h=h, q=q, diagonal=False):
                m, l, acc = carry
                start = pl.multiple_of(j * tq, tq)
                k = k_ref[pl.ds(start, tq), :]
                v = v_ref[pl.ds(start, tq), :]
                s = lax.dot_general(q, k, _DIMS["nt"], preferred_element_type=F32)
                s = s - c_ref[0, h:h + 1, pl.ds(start, tq)]
                if diagonal:
                    s = jnp.where(rowi >= coli, s, NEG)
                m_new = jnp.maximum(m, jnp.max(s, axis=-1, keepdims=True))
                alpha = jnp.exp(m - m_new)
                p = jnp.exp(s - m_new)
                l = alpha * l + jnp.sum(p, axis=-1, keepdims=True)
                acc = alpha * acc + lax.dot_general(p.astype(BF16), v, _DIMS["nn"], preferred_element_type=F32)
                return m_new, l, acc

            init = (jnp.full((tq, 1), NEG, F32), jnp.zeros((tq, 1), F32), jnp.zeros((tq, LANES), F32))
            m, l, acc = step(qi, lax.fori_loop(0, qi, step, init), diagonal=True)
            o_heads.append(acc / l)
            lse_heads.append(jnp.broadcast_to(m + jnp.log(l), (tq, LANES)))
        o_ref[...] = jnp.where(lane < FOX_HEAD_DIM, o_heads[0], o_heads[1])
        lse_ref[...] = jnp.where(lane < FOX_HEAD_DIM, lse_heads[0], lse_heads[1])
        if comm is not None:
            wait_comm()

    q_spec = pl.BlockSpec((tq, LANES), lambda b, hp, qi: (b * nq + qi, hp))
    kv_spec = pl.BlockSpec((seq, LANES), lambda b, hp, qi: (b, hp))
    c_spec = pl.BlockSpec((1, 2, seq), lambda b, hp, qi: (b * pairs + hp, 0, 0))
    in_specs = [q_spec, kv_spec, kv_spec, c_spec]
    out_shape = [jax.ShapeDtypeStruct((t, width), F32), jax.ShapeDtypeStruct((t, width), F32)]
    if comm is None:
        return pl.pallas_call(
            kern, name=name, grid=(batch, pairs, nq), in_specs=in_specs, out_specs=[q_spec, q_spec],
            out_shape=out_shape, compiler_params=_params(("parallel", "parallel", "arbitrary")),
        )(qn, kn, vb, cks)
    res_all = pl.pallas_call(
        kern, name=name, grid=(batch, pairs, nq), in_specs=in_specs + [ANY] * len(comm.srcs),
        out_specs=[q_spec, q_spec] + [ANY] * len(comm.outs), out_shape=out_shape + comm.outs,
        scratch_shapes=comm.scratch(), compiler_params=_params(("arbitrary", "arbitrary", "arbitrary")),
    )(qn, kn, vb, cks, *comm.srcs)
    return list(res_all[:2]), list(res_all[2:])


def _fox_bwd(qn, kn, vb, cks, o, lse, do, batch, seq, name, comm=None):
    t, width = qn.shape
    pairs = width // LANES
    tk = _tile(seq, 512, LANES)
    nk = seq // tk
    scale = FOX_HEAD_DIM ** -0.5

    def kern(*refs):
        ins, outs, _, c_src, c_out, c_sem = _split_refs(refs, 7, 5, comm)
        k_ref, v_ref, q_ref, c_ref, o_ref, lse_ref, do_ref = ins
        dk_ref, dv_ref, dq_ref, dc_ref, dcq_ref = outs
        kj = pl.program_id(2)
        if comm is not None:
            b_id, p_id = pl.program_id(0), pl.program_id(1)
            first = jnp.logical_and(jnp.logical_and(b_id == 0, p_id == 0), kj == 0)
            last = jnp.logical_and(jnp.logical_and(b_id == batch - 1, p_id == pairs - 1), kj == nk - 1)
            start_comm, wait_comm = comm.run(c_src, c_out, c_sem, first, last)
            start_comm()
        lane = lax.broadcasted_iota(jnp.int32, (1, LANES), 1)
        rowi = lax.broadcasted_iota(jnp.int32, (tk, tk), 0)
        coli = lax.broadcasted_iota(jnp.int32, (tk, tk), 1)

        @pl.when(kj == 0)
        def _():
            dq_ref[...] = jnp.zeros_like(dq_ref)
            dcq_ref[...] = jnp.zeros_like(dcq_ref)

        k_all = k_ref[...]
        v_all = v_ref[...]
        kstart = pl.multiple_of(kj * tk, tk)
        dk_heads, dv_heads = [], []
        for h in range(2):
            hm = (lane < FOX_HEAD_DIM) if h == 0 else (lane >= FOX_HEAD_DIM)
            kh = jnp.where(hm, k_all, jnp.zeros_like(k_all))
            vh = jnp.where(hm, v_all, jnp.zeros_like(v_all))
            c_row = c_ref[0, h:h + 1, pl.ds(kstart, tk)]

            def step(qi, carry, h=h, hm=hm, kh=kh, vh=vh, c_row=c_row, diagonal=False):
                dk_acc, dv_acc, dc_acc = carry
                start = pl.multiple_of(qi * tk, tk)
                q = q_ref[pl.ds(start, tk), :]
                dov = do_ref[pl.ds(start, tk), :]
                ov = o_ref[pl.ds(start, tk), :]
                lse_col = jnp.max(jnp.where(hm, lse_ref[pl.ds(start, tk), :], NEG), axis=-1, keepdims=True)
                dob = jnp.where(hm, dov, 0.0).astype(BF16)
                dcol = jnp.sum(dob.astype(F32) * ov, axis=-1, keepdims=True)
                s = lax.dot_general(q, kh, _DIMS["nt"], preferred_element_type=F32) - c_row
                p = jnp.exp(s - lse_col)
                if diagonal:
                    p = jnp.where(rowi >= coli, p, 0.0)
                dp = lax.dot_general(dob, vh, _DIMS["nt"], preferred_element_type=F32)
                ds = p * (dp - dcol)
                ds_b = ds.astype(BF16)
                dv_acc = dv_acc + lax.dot_general(p.astype(BF16), dob, _DIMS["tn"], preferred_element_type=F32)
                dk_acc = dk_acc + lax.dot_general(ds_b, q, _DIMS["tn"], preferred_element_type=F32)
                dq_part = lax.dot_general(ds_b * scale, kh, _DIMS["nn"], preferred_element_type=F32)
                dq_ref[pl.ds(start, tk), :] += dq_part
                dcq_ref[pl.ds(start, tk), :] += jnp.where(hm, jnp.sum(ds, axis=-1, keepdims=True), 0.0)
                dc_acc = dc_acc - jnp.sum(ds, axis=0, keepdims=True)
                return dk_acc, dv_acc, dc_acc

            init = (jnp.zeros((tk, LANES), F32), jnp.zeros((tk, LANES), F32), jnp.zeros((1, tk), F32))
            dk_acc, dv_acc, dc_acc = lax.fori_loop(kj + 1, nk, step, step(kj, init, diagonal=True))
            dk_heads.append(dk_acc)
            dv_heads.append(dv_acc)
            dc_ref[0, h:h + 1, pl.ds(kstart, tk)] = dc_acc
        dk_ref[...] = jnp.where(lane < FOX_HEAD_DIM, dk_heads[0], dk_heads[1])
        dv_ref[...] = jnp.where(lane < FOX_HEAD_DIM, dv_heads[0], dv_heads[1]).astype(BF16)
        if comm is not None:
            wait_comm()

    kv_spec = pl.BlockSpec((tk, LANES), lambda b, hp, kj: (b * nk + kj, hp))
    full_spec = pl.BlockSpec((seq, LANES), lambda b, hp, kj: (b, hp))
    c_spec = pl.BlockSpec((1, 2, seq), lambda b, hp, kj: (b * pairs + hp, 0, 0))
    in_specs = [kv_spec, kv_spec, full_spec, c_spec, full_spec, full_spec, full_spec]
    out_specs = [kv_spec, kv_spec, full_spec, c_spec, full_spec]
    out_shape = [jax.ShapeDtypeStruct((t, width), F32), jax.ShapeDtypeStruct((t, width), BF16),
                 jax.ShapeDtypeStruct((t, width), F32), jax.ShapeDtypeStruct(cks.shape, F32),
                 jax.ShapeDtypeStruct((t, width), F32)]
    if comm is None:
        return pl.pallas_call(
            kern, name=name, grid=(batch, pairs, nk), in_specs=in_specs, out_specs=out_specs, out_shape=out_shape,
            compiler_params=_params(("parallel", "parallel", "arbitrary")),
        )(kn, vb, qn, cks, o, lse, do)
    res_all = pl.pallas_call(
        kern, name=name, grid=(batch, pairs, nk), in_specs=in_specs + [ANY] * len(comm.srcs),
        out_specs=out_specs + [ANY] * len(comm.outs), out_shape=out_shape + comm.outs,
        scratch_shapes=comm.scratch(), compiler_params=_params(("arbitrary", "arbitrary", "arbitrary")),
    )(kn, vb, qn, cks, o, lse, do, *comm.srcs)
    return list(res_all[:5]), list(res_all[5:])


def _mem_fwd(qn, kn, vb, batch, seq, mlen, name):
    t, width = qn.shape
    heads = width // LANES
    tq = _tile(seq, 512, LANES)
    nq = seq // tq
    scale = MEM_HEAD_DIM ** -0.5

    def kern(q_ref, k_ref, v_ref, o_ref):
        s = lax.dot_general(q_ref[...], k_ref[...], _DIMS["nt"], preferred_element_type=F32) * scale
        e = jnp.exp(s - jnp.max(s, axis=-1, keepdims=True))
        p = e / jnp.sum(e, axis=-1, keepdims=True)
        o_ref[...] = lax.dot_general(p.astype(BF16), v_ref[...], _DIMS["nn"], preferred_element_type=F32)

    q_spec = pl.BlockSpec((tq, LANES), lambda b, h, qi: (b * nq + qi, h))
    kv_spec = pl.BlockSpec((mlen, LANES), lambda b, h, qi: (b, h))
    return pl.pallas_call(
        kern, name=name, grid=(batch, heads, nq), in_specs=[q_spec, kv_spec, kv_spec], out_specs=q_spec,
        out_shape=jax.ShapeDtypeStruct((t, width), F32),
        compiler_params=_params(("parallel", "parallel", "parallel")),
    )(qn, kn, vb)


def _mem_bwd(qn, kn, vb, do, batch, seq, mlen, name):
    t, width = qn.shape
    heads = width // LANES
    tq = _tile(seq, 512, LANES)
    nq = seq // tq
    scale = MEM_HEAD_DIM ** -0.5

    def kern(q_ref, k_ref, v_ref, do_ref, dq_ref, dk_ref, dv_ref):
        qi = pl.program_id(2)
        q, k, v = q_ref[...], k_ref[...], v_ref[...]
        dob = do_ref[...].astype(BF16)
        s = lax.dot_general(q, k, _DIMS["nt"], preferred_element_type=F32) * scale
        e = jnp.exp(s - jnp.max(s, axis=-1, keepdims=True))
        p = e / jnp.sum(e, axis=-1, keepdims=True)
        dp = lax.dot_general(dob, v, _DIMS["nt"], preferred_element_type=F32)
        ds = p * (dp - jnp.sum(p * dp, axis=-1, keepdims=True))
        dsb = (ds * scale).astype(BF16)
        dq_ref[...] = lax.dot_general(dsb, k, _DIMS["nn"], preferred_element_type=F32)
        dk = lax.dot_general(dsb, q, _DIMS["tn"], preferred_element_type=F32)
        dv = lax.dot_general(p.astype(BF16), dob, _DIMS["tn"], preferred_element_type=F32)

        @pl.when(qi == 0)
        def _():
            dk_ref[...] = dk
            dv_ref[...] = dv

        @pl.when(qi > 0)
        def _():
            dk_ref[...] += dk
            dv_ref[...] += dv

    q_spec = pl.BlockSpec((tq, LANES), lambda b, h, qi: (b * nq + qi, h))
    kv_spec = pl.BlockSpec((mlen, LANES), lambda b, h, qi: (b, h))
    return pl.pallas_call(
        kern, name=name, grid=(batch, heads, nq), in_specs=[q_spec, kv_spec, kv_spec, q_spec],
        out_specs=[q_spec, kv_spec, kv_spec],
        out_shape=[jax.ShapeDtypeStruct((t, width), F32), jax.ShapeDtypeStruct(kn.shape, F32),
                   jax.ShapeDtypeStruct(kn.shape, F32)],
        compiler_params=_params(("parallel", "parallel", "arbitrary")),
    )(qn, kn, vb, do)


def _position():
    return lax.axis_index("x"), lax.axis_index("y"), lax.axis_index("c")


def _other_chips(x, y):
    return [(1 - x, y), (x, 1 - y), (1 - x, 1 - y)]


def _remote(src, dst, sems, to):
    send, recv = sems.remote()
    return pltpu.make_async_remote_copy(src_ref=src, dst_ref=dst, send_sem=send, recv_sem=recv,
                                        device_id=to, device_id_type=MESH)


def _gather_first(shards):
    n = len(shards)

    def plan(src, out, sems):
        x, y, c = _position()
        me = 4 * x + 2 * y + c
        peers = [(x, y, 1 - c)] + [(px, py, c) for px, py in _other_chips(x, y)]
        copies = []
        for i in range(n):
            copies.append(pltpu.make_async_copy(src[i], out[i].at[me], sems.one_local()))
            copies += [_remote(src[i], out[i].at[me], sems, to) for to in peers]
        return copies

    outs = [jax.ShapeDtypeStruct((N_DEV,) + s.shape, s.dtype) for s in shards]
    return _Comm(shards, outs, 4 * n, n, plan)


def _gather_second(bufs):
    n = len(bufs)

    def plan(src, out, sems):
        x, y, c = _position()
        copies = []
        for i in range(n):
            for px, py in _other_chips(x, y):
                blk = out[i].at[4 * px + 2 * py + c]
                copies.append(_remote(blk, blk, sems, (x, y, 1 - c)))
        return copies

    outs = [jax.ShapeDtypeStruct(b.shape, b.dtype) for b in bufs]
    return _Comm(bufs, outs, 3 * n, 0, plan, in_place=True)


def _scatter_first(partials):
    n = len(partials)

    def plan(src, out, sems):
        x, y, c = _position()
        return [_remote(src[i].at[2 * q + (1 - c)], out[i].at[q], sems, (x, y, 1 - c))
                for i in range(n) for q in range(4)]

    outs = [jax.ShapeDtypeStruct((4,) + g.shape[1:], g.dtype) for g in partials]
    return _Comm(partials, outs, 4 * n, 0, plan)


def _scatter_second(halves):
    n = len(halves)

    def plan(src, out, sems):
        x, y, c = _position()
        return [_remote(src[i].at[2 * px + py], out[i].at[r], sems, (px, py, c))
                for i in range(n) for r, (px, py) in enumerate(_other_chips(x, y))]

    outs = [jax.ShapeDtypeStruct((3,) + h.shape[1:], h.dtype) for h in halves]
    return _Comm(halves, outs, 3 * n, 0, plan)


def _comm_call(comm, name):
    ns, no = len(comm.srcs), len(comm.outs)

    def body(*refs):
        copies = comm.plan(list(refs[:ns]), list(refs[ns:ns + no]), _Sems(*refs[ns + no:]))
        for cp in copies:
            cp.start()
        for cp in copies:
            cp.wait()

    return pl.pallas_call(
        body, name=name, in_specs=[ANY] * ns, out_specs=[ANY] * no, out_shape=comm.outs,
        scratch_shapes=comm.scratch(), input_output_aliases=comm.aliases(0, 0),
    )(*comm.srcs)


def _all_gather_vmem(shard, name):
    first = _gather_first([shard])

    def body(x_ref, out_ref, send, recv, local):
        sems = _Sems(send, recv, local)
        copies = first.plan([x_ref], [out_ref], sems)
        for cp in copies:
            cp.start()
        x, y, c = _position()
        passed = []
        for j, (px, py) in enumerate(_other_chips(x, y)):
            copies[2 + j].wait_recv()
            blk = out_ref.at[4 * px + 2 * py + c]
            fwd = _remote(blk, blk, sems, (x, y, 1 - c))
            fwd.start()
            passed.append(fwd)
        copies[0].wait()
        copies[1].wait()
        for cp in copies[2:]:
            cp.wait_send()
        for cp in passed:
            cp.wait()

    vm = pl.BlockSpec(memory_space=pltpu.VMEM)
    return pl.pallas_call(
        body, name=name, in_specs=[vm], out_specs=vm,
        out_shape=jax.ShapeDtypeStruct((N_DEV,) + shard.shape, shard.dtype),
        scratch_shapes=[pltpu.SemaphoreType.DMA((7,)), pltpu.SemaphoreType.DMA((7,)), pltpu.SemaphoreType.DMA((1,))],
    )(shard)


def _rs_pair_add(partial, landed, where, name):
    _, r, c = partial.shape
    tm = _tile(r, 256, 16)

    def kern(where_ref, g_ref, l_ref, own_ref, hb_ref):
        s = g_ref[...] + l_ref[...]
        hb_ref[...] = s.astype(BF16)

        @pl.when(pl.program_id(1) == where_ref[1])
        def _():
            own_ref[...] = s

    grid_spec = pltpu.PrefetchScalarGridSpec(
        num_scalar_prefetch=1, grid=(r // tm, 4),
        in_specs=[pl.BlockSpec((None, tm, c), lambda i, q, wr: (2 * q + wr[0], i, 0)),
                  pl.BlockSpec((None, tm, c), lambda i, q, wr: (q, i, 0))],
        out_specs=[pl.BlockSpec((tm, c), lambda i, q, wr: (i, 0)),
                   pl.BlockSpec((None, tm, c), lambda i, q, wr: (q, i, 0))])
    return pl.pallas_call(
        kern, name=name, grid_spec=grid_spec,
        out_shape=[jax.ShapeDtypeStruct((r, c), F32), jax.ShapeDtypeStruct((4, r, c), BF16)],
        compiler_params=_params(("parallel", "arbitrary")),
    )(where, partial, landed)


def _adam_math(g, w, m, v):
    m = ADAM_B1 * m + (1.0 - ADAM_B1) * g
    v = ADAM_B2 * v + (1.0 - ADAM_B2) * (g * g)
    m_hat = m / (1.0 - ADAM_B1 ** ADAM_STEP)
    v_hat = v / (1.0 - ADAM_B2 ** ADAM_STEP)
    delta = -ADAM_LR * (m_hat / (jnp.sqrt(v_hat) + ADAM_EPS) + ADAM_WD * w)
    return delta, m, v


def _rs_finish_adam(own, landed, w, m, v, name):
    r, c = own.shape
    tm = _tile(r, 128, 16)

    def kern(h_ref, l_ref, w_ref, m_ref, v_ref, *outs):
        g = ((h_ref[...] + l_ref[0].astype(F32)) + l_ref[1].astype(F32)) + l_ref[2].astype(F32)
        delta, m_new, v_new = _adam_math(g, w_ref[...], m_ref[...], v_ref[...])
        for ref, val in zip(outs, (g, delta, m_new, v_new)):
            ref[...] = val

    flat = pl.BlockSpec((tm, c), lambda i: (i, 0))
    lead = pl.BlockSpec((None, tm, c), lambda i: (0, i, 0))
    return pl.pallas_call(
        kern, name=name, grid=(r // tm,), in_specs=[flat, pl.BlockSpec((3, tm, c), lambda i: (0, i, 0)), lead, lead, lead],
        out_specs=[lead] * 4, out_shape=[jax.ShapeDtypeStruct((1, r, c), F32)] * 4,
        compiler_params=_params(("parallel",)))(own, landed, w, m, v)


def _allreduce_adam(gathered, w, m, v, name):
    _, r, c = gathered.shape

    def kern(a_ref, w_ref, m_ref, v_ref, g_out, d_out, m_out, v_out):
        g = a_ref[0]
        for j in range(1, N_DEV):
            g = g + a_ref[j]
        delta, m_new, v_new = _adam_math(g, w_ref[...], m_ref[...], v_ref[...])
        g_out[...] = g
        d_out[...] = delta
        m_out[...] = m_new
        v_out[...] = v_new

    return pl.pallas_call(
        kern, name=name, out_shape=[jax.ShapeDtypeStruct((r, c), F32)] * 4,
        compiler_params=_params(),
    )(gathered, w, m, v)


def _cols_from_blocks(g):
    n, k, nb = g.shape
    return jnp.transpose(g, (1, 0, 2)).reshape(k, n * nb)


def _blocks_from_cols(w):
    k, n = w.shape
    return jnp.transpose(w.reshape(k, N_DEV, n // N_DEV), (1, 0, 2))


def _pack_small(parts):
    flat = []
    for p in parts:
        v = p.reshape(-1)
        flat.append(jnp.pad(v, (0, (-v.shape[0]) % (8 * LANES))))
    return jnp.concatenate(flat).reshape(-1, LANES)


def _unpack_small(buf, like):
    out, pos = [], 0
    flat = buf.reshape(-1)
    for p in like:
        size = p.size
        out.append(flat[pos:pos + size].reshape(p.shape))
        pos += size + (-size) % (8 * LANES)
    return out


def kernel(x, mem, ffn1_norm, ffn1_w_gate_up, ffn1_w_down, mix_norm, mem_norm, w_in, b_forget, pool_w, pool_scale, w_pool_up, fox_q_norm, fox_k_norm, w_fox_o, w_mem_kv, mem_q_norm, mem_k_norm, w_mem_o, w_out, ffn2_norm, ffn2_w_gate_up, ffn2_w_down, loss_target, m_ffn1_norm, m_ffn1_w_gate_up, m_ffn1_w_down, m_mix_norm, m_mem_norm, m_w_in, m_b_forget, m_pool_w, m_pool_scale, m_w_pool_up, m_fox_q_norm, m_fox_k_norm, m_w_fox_o, m_w_mem_kv, m_mem_q_norm, m_mem_k_norm, m_w_mem_o, m_w_out, m_ffn2_norm, m_ffn2_w_gate_up, m_ffn2_w_down, v_ffn1_norm, v_ffn1_w_gate_up, v_ffn1_w_down, v_mix_norm, v_mem_norm, v_w_in, v_b_forget, v_pool_w, v_pool_scale, v_w_pool_up, v_fox_q_norm, v_fox_k_norm, v_w_fox_o, v_w_mem_kv, v_mem_q_norm, v_mem_k_norm, v_w_mem_o, v_w_out, v_ffn2_norm, v_ffn2_w_gate_up, v_ffn2_w_down):
    names = ["ffn1_norm", "ffn1_w_gate_up", "ffn1_w_down", "mix_norm", "mem_norm", "w_in", "b_forget", "pool_w",
             "pool_scale", "w_pool_up", "fox_q_norm", "fox_k_norm", "w_fox_o", "w_mem_kv", "mem_q_norm",
             "mem_k_norm", "w_mem_o", "w_out", "ffn2_norm", "ffn2_w_gate_up", "ffn2_w_down"]
    w_args = [ffn1_norm, ffn1_w_gate_up, ffn1_w_down, mix_norm, mem_norm, w_in, b_forget, pool_w, pool_scale,
              w_pool_up, fox_q_norm, fox_k_norm, w_fox_o, w_mem_kv, mem_q_norm, mem_k_norm, w_mem_o, w_out,
              ffn2_norm, ffn2_w_gate_up, ffn2_w_down]
    m_args = [m_ffn1_norm, m_ffn1_w_gate_up, m_ffn1_w_down, m_mix_norm, m_mem_norm, m_w_in, m_b_forget, m_pool_w,
              m_pool_scale, m_w_pool_up, m_fox_q_norm, m_fox_k_norm, m_w_fox_o, m_w_mem_kv, m_mem_q_norm,
              m_mem_k_norm, m_w_mem_o, m_w_out, m_ffn2_norm, m_ffn2_w_gate_up, m_ffn2_w_down]
    v_args = [v_ffn1_norm, v_ffn1_w_gate_up, v_ffn1_w_down, v_mix_norm, v_mem_norm, v_w_in, v_b_forget, v_pool_w,
              v_pool_scale, v_w_pool_up, v_fox_q_norm, v_fox_k_norm, v_w_fox_o, v_w_mem_kv, v_mem_q_norm,
              v_mem_k_norm, v_w_mem_o, v_w_out, v_ffn2_norm, v_ffn2_w_gate_up, v_ffn2_w_down]
    W = dict(zip(names, w_args))
    M = dict(zip(names, m_args))
    V = dict(zip(names, v_args))

    batch, seq, d = x.shape
    mlen = mem.shape[1]
    t = batch * seq
    gate_w = 3 * d
    z_gate, z_q = 0, gate_w
    z_k, z_v = z_q + FOX_WIDTH, z_q + 2 * FOX_WIDTH
    z_u = z_q + 3 * FOX_WIDTH
    z_qm = z_u + POOL_WIDTH
    z_f = z_qm + MEM_WIDTH
    z_width = -(-(z_f + F_PAD) // 512) * 512

    x2d = x.reshape(t, d)
    mem2d = mem.reshape(batch * mlen, d)
    tgt2d = loss_target.reshape(t, d)

    big = ["ffn1_w_gate_up", "ffn1_w_down", "w_in", "w_pool_up", "w_fox_o", "w_mem_kv", "w_mem_o", "w_out",
           "ffn2_w_gate_up", "ffn2_w_down"]
    col_sharded = {"ffn1_w_gate_up", "ffn2_w_gate_up", "w_in", "w_pool_up", "w_fox_o", "w_mem_o"}
    mixer_small = ["w_pool_up", "w_fox_o", "w_mem_kv", "w_mem_o", "w_out"]
    shard = {n: W[n][0].astype(BF16) for n in big}

    def rows_of(g):
        return g.reshape(-1, g.shape[2])

    gu_caps_nn, gu_caps_nt, gu_caps_tn = (1024, 1408, 2048), (1024, 1024, 1408), (1024, 1408, 2048)
    down_caps, dwd_caps = (1024, 1024, 2816), (1408, 1024, 2048)

    (wgu1,) = _comm_call(_gather_first([shard["ffn1_w_gate_up"]]), "ag_first")
    h1, (wgu1,) = _rms_fwd(x2d, W["ffn1_norm"], "ffn1_rms", comm=_gather_second([wgu1]))
    w_in_top, w_in_bot = shard["w_in"][:d // 2], shard["w_in"][d // 2:]
    gu1, bufs = _mm(h1, wgu1, "nn", "ffn1_gu", caps=gu_caps_nn, b_blocks=True,
                    comm=_gather_first([shard["ffn1_w_down"], w_in_top]))
    a1, (wd1_g, w_in_top_g) = _swiglu_fwd(gu1, "ffn1_swiglu", comm=_gather_second(bufs))
    wd1 = rows_of(wd1_g)
    x1, (w_in_bot_g,) = _mm(a1, wd1, "nn", "ffn1_down", scale=0.5, res=x2d, caps=down_caps,
                            comm=_gather_first([w_in_bot]))
    h2, (w_in_bot_g,) = _rms_fwd(x1, W["mix_norm"], "mix_rms", comm=_gather_second([w_in_bot_g]))

    nb = w_in.shape[2]
    ref_order = [(z_u, POOL_WIDTH), (z_q, 3 * FOX_WIDTH), (z_f, FOX_HEADS), (z_qm, MEM_WIDTH), (z_gate, gate_w)]
    runs, ref_col = [], 0
    for mine, width in ref_order:
        done = 0
        while done < width:
            j, c = divmod(ref_col + done, nb)
            length = min(width - done, nb - c)
            runs.append((j, c, mine + done, length))
            done += length
        ref_col += width
    w_in_pad = jnp.zeros((d, z_width), BF16)
    for j, c, mine, length in runs:
        for row0, half in ((0, w_in_top_g), (d // 2, w_in_bot_g)):
            w_in_pad = lax.dynamic_update_slice(w_in_pad, half[j, :, c:c + length], (row0, mine))
    group_b = mixer_small + ["ffn2_w_down"]
    z, bufs = _mm(h2, w_in_pad, "nn", "mix_in", comm=_gather_first([shard[n] for n in group_b]))

    pool_w_b = W["pool_w"][0].astype(BF16)

    def pool_fwd_body(u, pw, ps):
        row = lax.broadcasted_iota(jnp.int32, (seq, 1), 0)
        diffs, mixed = [], []
        for g in range(POOL_GROUPS):
            ug = u[:, g * POOL_GROUP_DIM:(g + 1) * POOL_GROUP_DIM]
            cnt = jnp.minimum(row + 1, POOL_WINDOWS[g]).astype(F32)
            diff = _window_sum(ug, g + 1, True) / cnt - ug
            diffs.append(diff)
            mixed.append(lax.dot_general(diff.astype(BF16), pw[g], _DIMS["nn"], preferred_element_type=F32))
        diffs = jnp.concatenate(diffs, axis=1)
        mixed = jnp.concatenate(mixed, axis=1)
        return mixed * ps, diffs, mixed

    ypp, pool_diff, pool_mixed = _rowwise(
        pool_fwd_body, [(z, z_u, POOL_WIDTH)], [pool_w_b, W["pool_scale"]],
        [(POOL_WIDTH, BF16), (POOL_WIDTH, BF16), (POOL_WIDTH, F32)], [], "pool_fwd", seq)

    gq = jnp.tile(W["fox_q_norm"], (1, FOX_HEADS))
    gk = jnp.tile(W["fox_k_norm"], (1, FOX_HEADS))
    b_pad = jnp.pad(W["b_forget"], ((0, 0), (0, F_PAD - FOX_HEADS)))

    def fox_prep_body(q, k, v, f, gqv, gkv, bv):
        qn_scaled = _head_norm(q, gqv, FOX_HEAD_DIM).astype(BF16) * FOX_HEAD_DIM ** -0.5
        return qn_scaled, _head_norm(k, gkv, FOX_HEAD_DIM), v, _log_sigmoid(f + bv)

    tm_e = _tile(t, 256, 16)
    (qn, kn, vb, logf), bufs = _rowwise(
        fox_prep_body, [(z, z_q, FOX_WIDTH), (z, z_k, FOX_WIDTH), (z, z_v, FOX_WIDTH), (z, z_f, F_PAD)],
        [gq, gk, b_pad], [(FOX_WIDTH, BF16), (FOX_WIDTH, BF16), (FOX_WIDTH, BF16), (F_PAD, F32)], [], "fox_prep", tm_e,
        comm=_gather_second(bufs))
    full = {n: (_cols_from_blocks(g) if n in col_sharded else rows_of(g)) for n, g in zip(group_b, bufs)}
    wd2 = full["ffn2_w_down"]
    y_pool = _mm(ypp, full["w_pool_up"], "nn", "pool_up")
    csum = _rowwise(lambda v: (_scan_rows(v, True),), [_whole(logf)], [], [(F_PAD, F32)], [], "fox_cumsum", seq)[0]
    cks = jnp.transpose(csum.reshape(batch, seq, F_PAD)[:, :, :FOX_HEADS], (0, 2, 1)).reshape(
        batch * FOX_HEADS // 2, 2, seq)
    (o_fox, lse), (wgu2,) = _fox_fwd(qn, kn, vb, cks, batch, seq, "fox_fwd",
                                     comm=_gather_first([shard["ffn2_w_gate_up"]]))
    y_fox = _mm(o_fox, full["w_fox_o"], "nn", "fox_o")

    memn = _rms_fwd(mem2d, W["mem_norm"], "mem_rms")
    kv = _mm(memn, full["w_mem_kv"], "nn", "mem_kv")
    gqm = jnp.tile(W["mem_q_norm"], (1, MEM_HEADS))
    gkm = jnp.tile(W["mem_k_norm"], (1, MEM_HEADS))
    qmn = _rowwise(lambda q, g: (_head_norm(q, g, MEM_HEAD_DIM),), [(z, z_qm, MEM_WIDTH)], [gqm],
                   [(MEM_WIDTH, BF16)], [], "memq_prep", tm_e)[0]
    kmn, vmb = _rowwise(lambda k, v, g: (_head_norm(k, g, MEM_HEAD_DIM), v),
                        [(kv, 0, MEM_WIDTH), (kv, MEM_WIDTH, MEM_WIDTH)], [gkm],
                        [(MEM_WIDTH, BF16), (MEM_WIDTH, BF16)], [], "memk_prep", _tile(batch * mlen, 256, 16))
    o_mem = _mem_fwd(qmn, kmn, vmb, batch, seq, mlen, "mem_fwd")
    y_mem = _mm(o_mem, full["w_mem_o"], "nn", "mem_o")

    def gate_fwd_body(gp, gf, gm, yp, yf, ym):
        return ((_sigmoid(gp) * yp + _sigmoid(gf) * yf) + _sigmoid(gm) * ym,)

    tm_g = _tile(t, 128, 16)
    (merged,), (wgu2,) = _rowwise(
        gate_fwd_body, [(z, 0, d), (z, d, d), (z, 2 * d, d), _whole(y_pool), _whole(y_fox), _whole(y_mem)],
        [], [(d, BF16)], [], "gate_fwd", tm_g, comm=_gather_second([wgu2]))
    x2 = _mm(merged, full["w_out"], "nn", "mix_out", res=x1)

    h3 = _rms_fwd(x2, W["ffn2_norm"], "ffn2_rms")
    gu2 = _mm(h3, wgu2, "nn", "ffn2_gu", caps=gu_caps_nn, b_blocks=True)
    a2 = _swiglu_fwd(gu2, "ffn2_swiglu")
    x3 = _mm(a2, wd2, "nn", "ffn2_down", scale=0.5, res=x2, caps=down_caps)
    dy, dy_b, loss_part = _loss_head(x3, tgt2d, "loss")
    loss = lax.psum(loss_part[0, 0], ("x", "y", "c"))

    cx, cy, cc = _position()
    where = jnp.stack([cc, 2 * cx + cy]).astype(jnp.int32)
    G, own, landed = {}, {}, {}

    def row_blocks(g):
        return g.reshape(N_DEV, g.shape[0] // N_DEV, g.shape[1])

    def pair_add(n, partial, from_core):
        own[n], chip_sums = _rs_pair_add(partial, from_core, where, "rs_add_" + n)
        return chip_sums

    da2 = _mm(dy_b, wd2, "nt", "ffn2_da", scale=0.5)
    p_wd2 = row_blocks(_mm(a2, dy_b, "tn", "ffn2_dwd", scale=0.5, caps=dwd_caps))
    dgu2, (l_wd2,) = _swiglu_bwd(gu2, da2, "ffn2_dswiglu", comm=_scatter_first([p_wd2]))
    s_wd2 = pair_add("ffn2_w_down", p_wd2, l_wd2)
    p_wgu2, (landed["ffn2_w_down"],) = _mm(h3, dgu2, "tn", "ffn2_dwgu", caps=gu_caps_tn, out_blocks=True,
                                           comm=_scatter_second([s_wd2]))
    dh3, (l_wgu2,) = _mm(dgu2, wgu2, "nt", "ffn2_dh", caps=gu_caps_nt, b_blocks=True,
                         comm=_scatter_first([p_wgu2]))
    s_wgu2 = pair_add("ffn2_w_gate_up", p_wgu2, l_wgu2)
    dx2, dx2_b, G["ffn2_norm"] = _rms_bwd(x2, W["ffn2_norm"], dh3, dy, "ffn2_drms")

    dmerged = _mm(dx2_b, full["w_out"], "nt", "mix_out_dx")
    P = {"w_out": row_blocks(_mm(merged, dx2_b, "tn", "mix_out_dw"))}

    def gate_bwd_body(gp, gf, gm, yp, yf, ym, dm):
        outs_dl, outs_dy = [], []
        for gl, yv in ((gp, yp), (gf, yf), (gm, ym)):
            s = _sigmoid(gl)
            outs_dl.append((dm * yv) * (s * (1.0 - s)))
            outs_dy.append(dm * s)
        return (jnp.concatenate(outs_dl, axis=1), *outs_dy)

    dz_gate, dy_pool, dy_fox, dy_mem = _rowwise(
        gate_bwd_body, [(z, 0, d), (z, d, d), (z, 2 * d, d), _whole(y_pool), _whole(y_fox), _whole(y_mem), _whole(dmerged)],
        [], [(gate_w, BF16), (d, BF16), (d, BF16), (d, BF16)], [], "gate_bwd", tm_g)

    dypp = _mm(dy_pool, full["w_pool_up"], "nt", "pool_up_dx")
    P["w_pool_up"] = _blocks_from_cols(_mm(ypp, dy_pool, "tn", "pool_up_dw"))

    def pool_bwd_body(dyv, mixed, diff, pw, ps):
        row = lax.broadcasted_iota(jnp.int32, (seq, 1), 0)
        d_scale = jnp.sum(dyv * mixed, axis=0, keepdims=True)
        dmix = (dyv * ps).astype(BF16)
        du, dpw = [], []
        for g in range(POOL_GROUPS):
            sl = slice(g * POOL_GROUP_DIM, (g + 1) * POOL_GROUP_DIM)
            dmg = dmix[:, sl]
            ddiff = lax.dot_general(dmg, pw[g], _DIMS["nt"], preferred_element_type=F32)
            dpw.append(lax.dot_general(diff[:, sl], dmg, _DIMS["tn"], preferred_element_type=F32))
            cnt = jnp.minimum(row + 1, POOL_WINDOWS[g]).astype(F32)
            du.append(_window_sum(ddiff / cnt, g + 1, False) - ddiff)
        return jnp.concatenate(du, axis=1), d_scale, jnp.concatenate(dpw, axis=0)

    dz_u, G["pool_scale"], d_pool_w = _rowwise(
        pool_bwd_body, [_whole(dypp), _whole(pool_mixed), _whole(pool_diff)], [pool_w_b, W["pool_scale"]],
        [(POOL_WIDTH, BF16)], [((1, POOL_WIDTH), F32), ((POOL_WIDTH, POOL_GROUP_DIM), F32)],
        "pool_bwd", seq)
    G["pool_w"] = d_pool_w.reshape(1, POOL_GROUPS, POOL_GROUP_DIM, POOL_GROUP_DIM)

    do_fox = _mm(dy_fox, full["w_fox_o"], "nt", "fox_o_dx")
    P["w_fox_o"] = _blocks_from_cols(_mm(o_fox, dy_fox, "tn", "fox_o_dw"))
    (dkn, dz_v, dqn, dcks, dcq), (landed["ffn2_w_gate_up"],) = _fox_bwd(
        qn, kn, vb, cks, o_fox, lse, do_fox, batch, seq, "fox_bwd", comm=_scatter_second([s_wgu2]))
    dcs = jnp.transpose(dcks.reshape(batch, FOX_HEADS, seq), (0, 2, 1)).reshape(t, FOX_HEADS)
    dcs = jnp.pad(dcs, ((0, 0), (0, F_PAD - FOX_HEADS)))
    dcq = jnp.pad(dcq.reshape(t, FOX_HEADS, FOX_HEAD_DIM)[:, :, 0], ((0, 0), (0, F_PAD - FOX_HEADS)))

    def fox_f_bwd_body(dc, dc_rows, f, bv):
        lane = lax.broadcasted_iota(jnp.int32, (1, F_PAD), 1)
        dlogf = _scan_rows(dc + dc_rows, False)
        df = jnp.where(lane < FOX_HEADS, dlogf * _sigmoid(-(f + bv)), 0.0)
        return df, jnp.sum(df, axis=0, keepdims=True)

    dz_f, db_pad = _rowwise(fox_f_bwd_body, [_whole(dcs), _whole(dcq), (z, z_f, F_PAD)], [b_pad], [(F_PAD, BF16)],
                            [((1, F_PAD), F32)], "fox_f_bwd", seq)
    G["b_forget"] = db_pad[:, :FOX_HEADS]

    def fox_qk_bwd_body(q, k, dq, dk, gqv, gkv):
        dqr, dgq = _head_norm_bwd(q, gqv, dq, FOX_HEAD_DIM)
        dkr, dgk = _head_norm_bwd(k, gkv, dk, FOX_HEAD_DIM)
        return dqr, dkr, dgq, dgk

    dz_q, dz_k, dgq_t, dgk_t = _rowwise(
        fox_qk_bwd_body, [(z, z_q, FOX_WIDTH), (z, z_k, FOX_WIDTH), _whole(dqn), _whole(dkn)], [gq, gk],
        [(FOX_WIDTH, BF16), (FOX_WIDTH, BF16)], [((1, FOX_WIDTH), F32), ((1, FOX_WIDTH), F32)], "fox_qk_bwd", tm_e)
    G["fox_q_norm"] = jnp.sum(dgq_t.reshape(FOX_HEADS, FOX_HEAD_DIM), axis=0, keepdims=True)
    G["fox_k_norm"] = jnp.sum(dgk_t.reshape(FOX_HEADS, FOX_HEAD_DIM), axis=0, keepdims=True)

    do_mem = _mm(dy_mem, full["w_mem_o"], "nt", "mem_o_dx")
    P["w_mem_o"] = _blocks_from_cols(_mm(o_mem, dy_mem, "tn", "mem_o_dw"))
    dqmn, dkmn, dvm = _mem_bwd(qmn, kmn, vmb, do_mem, batch, seq, mlen, "mem_bwd")
    dz_qm, dgqm_t = _rowwise(lambda q, dq, g: _head_norm_bwd(q, g, dq, MEM_HEAD_DIM),
                             [(z, z_qm, MEM_WIDTH), _whole(dqmn)], [gqm], [(MEM_WIDTH, BF16)],
                             [((1, MEM_WIDTH), F32)], "memq_bwd", tm_e)

    def memk_bwd_body(k, dk, dv, g):
        dkr, dg = _head_norm_bwd(k, g, dk, MEM_HEAD_DIM)
        return jnp.concatenate([dkr, dv], axis=1), dg

    dkv, dgkm_t = _rowwise(memk_bwd_body, [(kv, 0, MEM_WIDTH), _whole(dkmn), _whole(dvm)], [gkm],
                           [(2 * MEM_WIDTH, BF16)], [((1, MEM_WIDTH), F32)], "memk_bwd", _tile(batch * mlen, 256, 16))
    G["mem_q_norm"] = jnp.sum(dgqm_t.reshape(MEM_HEADS, MEM_HEAD_DIM), axis=0, keepdims=True)
    G["mem_k_norm"] = jnp.sum(dgkm_t.reshape(MEM_HEADS, MEM_HEAD_DIM), axis=0, keepdims=True)
    P["w_mem_kv"] = row_blocks(_mm(memn, dkv, "tn", "mem_kv_dw"))
    dmemn = _mm(dkv, full["w_mem_kv"], "nt", "mem_kv_dx")
    _, _, G["mem_norm"] = _rms_bwd(mem2d, W["mem_norm"], dmemn, None, "mem_drms")

    dz = jnp.concatenate([dz_gate, dz_q, dz_k, dz_v, dz_u, dz_qm, dz_f,
                          jnp.zeros((t, z_width - z_f - F_PAD), BF16)], axis=1)
    d_w_in_pad, l_small = _mm(h2, dz, "tn", "mix_in_dw", comm=_scatter_first([P[n] for n in mixer_small]))
    s_small = [pair_add(n, P[n], l) for n, l in zip(mixer_small, l_small)]
    p_w_in = jnp.zeros((N_DEV, d, nb), F32)
    for j, c, mine, length in runs:
        p_w_in = lax.dynamic_update_slice(p_w_in, d_w_in_pad[None, :, mine:mine + length], (j, 0, c))
    dh2, rest = _mm(dz, w_in_pad, "nt", "mix_in_dx", caps=(1024, 1024, 1792),
                    comm=_join(_scatter_second(s_small), _scatter_first([p_w_in])))
    for n, l in zip(mixer_small, rest[:len(mixer_small)]):
        landed[n] = l
    s_w_in = pair_add("w_in", p_w_in, rest[-1])
    dx1, dx1_b, G["mix_norm"] = _rms_bwd(x1, W["mix_norm"], dh2, dx2, "mix_drms")

    da1 = _mm(dx1_b, wd1, "nt", "ffn1_da", scale=0.5)
    dgu1 = _swiglu_bwd(gu1, da1, "ffn1_dswiglu")
    p_wgu1, (landed["w_in"],) = _mm(h1, dgu1, "tn", "ffn1_dwgu", caps=gu_caps_tn, out_blocks=True,
                                    comm=_scatter_second([s_w_in]))
    d_wd1, (l_wgu1,) = _mm(a1, dx1_b, "tn", "ffn1_dwd", scale=0.5, caps=dwd_caps, comm=_scatter_first([p_wgu1]))
    p_wd1 = row_blocks(d_wd1)
    s_wgu1 = pair_add("ffn1_w_gate_up", p_wgu1, l_wgu1)
    dh1, (landed["ffn1_w_gate_up"], l_wd1) = _mm(dgu1, wgu1, "nt", "ffn1_dh", caps=gu_caps_nt, b_blocks=True,
                                                 comm=_join(_scatter_second([s_wgu1]), _scatter_first([p_wd1])))
    s_wd1 = pair_add("ffn1_w_down", p_wd1, l_wd1)
    (dx0, _, G["ffn1_norm"]), (landed["ffn1_w_down"],) = _rms_bwd(x2d, W["ffn1_norm"], dh1, dx1, "ffn1_drms",
                                                                  comm=_scatter_second([s_wd1]))
    grad_x = dx0.reshape(batch, seq, d)

    out_g, out_d, out_m, out_v = {}, {}, {}, {}
    for n in big:
        out_g[n], out_d[n], out_m[n], out_v[n] = _rs_finish_adam(own[n], landed[n], W[n], M[n], V[n], "adam_" + n)

    small = [n for n in names if n not in big]
    g_small = _pack_small([G[n].reshape(W[n].shape) for n in small])
    all_small = _all_gather_vmem(g_small, "ag_small_grads")
    res = _allreduce_adam(all_small, _pack_small([W[n] for n in small]), _pack_small([M[n] for n in small]),
                          _pack_small([V[n] for n in small]), "adam_small")
    like = [W[n] for n in small]
    for dst, buf in zip((out_g, out_d, out_m, out_v), res):
        for n, a in zip(small, _unpack_small(buf, like)):
            dst[n] = a

    return (loss, grad_x, *[out_g[n] for n in names], *[out_d[n] for n in names],
            *[out_m[n] for n in names], *[out_v[n] for n in names])
```

```python
import functools

import jax
import jax.numpy as jnp
from jax import lax
from jax.experimental import pallas as pl
from jax.experimental.pallas import tpu as pltpu

F32 = jnp.float32
BF16 = jnp.bfloat16
MESH = pl.DeviceIdType.MESH

N_DEV = 8
EPS = 1e-6
FOX_HEADS = 16
FOX_HEAD_DIM = 64
FOX_WIDTH = FOX_HEADS * FOX_HEAD_DIM
MEM_HEADS = 4
MEM_HEAD_DIM = 128
MEM_WIDTH = MEM_HEADS * MEM_HEAD_DIM
POOL_GROUPS = 4
POOL_GROUP_DIM = 128
POOL_WIDTH = POOL_GROUPS * POOL_GROUP_DIM
POOL_WINDOWS = (2, 4, 8, 16)
LANES = 128
F_PAD = LANES

ADAM_LR = 0.001
ADAM_B1 = 0.9
ADAM_B2 = 0.999
ADAM_EPS = 1e-08
ADAM_WD = 0.01
ADAM_STEP = 10

VMEM_LIMIT = 56 * 1024 * 1024
NEG = -1e30

ANY = pl.BlockSpec(memory_space=pl.ANY)


def _params(sem=None):
    return pltpu.CompilerParams(dimension_semantics=sem, vmem_limit_bytes=VMEM_LIMIT)


MXU_DIM = 256


def _tile(dim, cap, align):
    best = None
    t = align
    while t <= min(dim, cap):
        if dim % t == 0:
            best = t
        t += align
    return dim if best is None else best


def _mxu_tile(dim, cap):
    wide, fine = _tile(dim, cap, MXU_DIM), _tile(dim, cap, LANES)
    whole_widths = wide % MXU_DIM == 0 and wide <= cap
    return wide if whole_widths and fine < 2 * wide else fine


class _Sems:
    def __init__(self, send, recv, local):
        self.send, self.recv, self.local = send, recv, local
        self.n_remote = self.n_local = 0

    def remote(self):
        i = self.n_remote
        self.n_remote += 1
        return self.send.at[i], self.recv.at[i]

    def one_local(self):
        i = self.n_local
        self.n_local += 1
        return self.local.at[i]


class _Comm:
    def __init__(self, srcs, outs, n_remote, n_local, plan, in_place=False):
        self.srcs, self.outs, self.n_remote, self.n_local = list(srcs), list(outs), n_remote, n_local
        self.plan, self.in_place = plan, in_place

    def aliases(self, n_in, n_out):
        return {n_in + i: n_out + i for i in range(len(self.srcs))} if self.in_place else {}

    def scratch(self):
        return [pltpu.SemaphoreType.DMA((self.n_remote,)), pltpu.SemaphoreType.DMA((self.n_remote,)),
                pltpu.SemaphoreType.DMA((max(self.n_local, 1),))]

    def run(self, src_refs, out_refs, sem_refs, first, last):
        def copies():
            return self.plan(list(src_refs), list(out_refs), _Sems(*sem_refs))

        return (lambda: _when(first, lambda: [cp.start() for cp in copies()]),
                lambda: _when(last, lambda: [cp.wait() for cp in copies()]))


def _when(cond, fn):
    @pl.when(cond)
    def _():
        fn()


def _join(*comms):
    comms = [c for c in comms if c is not None]
    assert all(not c.in_place for c in comms)

    def plan(src_refs, out_refs, sems):
        copies, si, oi = [], 0, 0
        for c in comms:
            copies += c.plan(src_refs[si:si + len(c.srcs)], out_refs[oi:oi + len(c.outs)], sems)
            si += len(c.srcs)
            oi += len(c.outs)
        return copies

    return _Comm(sum((c.srcs for c in comms), []), sum((c.outs for c in comms), []),
                 sum(c.n_remote for c in comms), sum(c.n_local for c in comms), plan)


def _split_refs(refs, n_in, n_out, comm):
    if comm is None:
        return refs[:n_in], refs[n_in:n_in + n_out], refs[n_in + n_out:], (), (), ()
    ns, no = len(comm.srcs), len(comm.outs)
    ins = refs[:n_in]
    srcs = refs[n_in:n_in + ns]
    outs = refs[n_in + ns:n_in + ns + n_out]
    couts = refs[n_in + ns + n_out:n_in + ns + n_out + no]
    rest = refs[n_in + ns + n_out + no:]
    return ins, outs, rest[:-3], srcs, couts, rest[-3:]


def _sigmoid(x):
    return 1.0 / (1.0 + jnp.exp(-x))


_DIMS = {"nn": (((1,), (0,)), ((), ())), "nt": (((1,), (1,)), ((), ())), "tn": (((0,), (0,)), ((), ()))}


def _mm(a, b, mode, name, out_dtype=F32, scale=1.0, res=None, caps=None, b_blocks=False, out_blocks=False,
        halves=None, comm=None):
    nblk = N_DEV
    a_shape, b_shape = a.shape, b.shape
    if b_blocks:
        _, r0, c0 = b.shape
        b_shape = (r0, nblk * c0)
    if halves == "a":
        assert mode == "nt"
        a_shape = (a.shape[1], 2 * a.shape[2])
    elif halves == "b":
        assert mode == "tn"
        b_shape = (b.shape[1], 2 * b.shape[2])
    if mode == "nn":
        (m, k), (k2, n) = a_shape, b_shape
    elif mode == "nt":
        (m, k), (n, k2) = a_shape, b_shape
    else:
        (k, m), (k2, n) = a_shape, b_shape
    assert k == k2, (name, a.shape, b.shape)
    cm, cn, ck = caps or ((1024, 512, 2048) if k <= 2048 else (1024, 1024, 2048))
    n_unit = n // nblk if (out_blocks or (b_blocks and mode == "nn")) else n
    k_unit = k // nblk if (b_blocks and mode == "nt") else k
    tm, tn, tk = _mxu_tile(m, cm), _mxu_tile(n_unit, cn), _mxu_tile(k_unit, ck)
    nk = k // tk
    npb, kpb = n_unit // tn, k_unit // tk
    n_half, k_half = (n // 2) // tn, (k // 2) // tk
    if mode == "nn":
        a_spec = pl.BlockSpec((tm, tk), lambda i, j, kk: (i, kk))
        b_spec = pl.BlockSpec((tk, tn), lambda i, j, kk: (kk, j))
        if b_blocks:
            b_spec = pl.BlockSpec((None, tk, tn), lambda i, j, kk: (j // npb, kk, j % npb))
    elif mode == "nt":
        a_spec = pl.BlockSpec((tm, tk), lambda i, j, kk: (i, kk))
        b_spec = pl.BlockSpec((tn, tk), lambda i, j, kk: (j, kk))
        if b_blocks:
            b_spec = pl.BlockSpec((None, tn, tk), lambda i, j, kk: (kk // kpb, j, kk % kpb))
        if halves == "a":
            assert (k // 2) % tk == 0
            a_spec = pl.BlockSpec((None, tm, tk), lambda i, j, kk: (kk // k_half, i, kk % k_half))
    else:
        assert not b_blocks
        a_spec = pl.BlockSpec((tk, tm), lambda i, j, kk: (kk, i))
        b_spec = pl.BlockSpec((tk, tn), lambda i, j, kk: (kk, j))
        if halves == "b":
            assert (n // 2) % tn == 0
            b_spec = pl.BlockSpec((None, tk, tn), lambda i, j, kk: (j // n_half, kk, j % n_half))
    if out_blocks:
        assert res is None
        o_spec = pl.BlockSpec((None, tm, tn), lambda i, j, kk: (j // npb, i, j % npb))
        o_shape = jax.ShapeDtypeStruct((nblk, m, n // nblk), out_dtype)
    else:
        o_spec = pl.BlockSpec((tm, tn), lambda i, j, kk: (i, j))
        o_shape = jax.ShapeDtypeStruct((m, n), out_dtype)
    in_specs = [a_spec, b_spec] + ([o_spec] if res is not None else [])
    n_in = len(in_specs)
    dims = _DIMS[mode]
    gm, gn = m // tm, n // tn

    def kern(*refs):
        ins, outs, scratch, c_src, c_out, c_sem = _split_refs(refs, n_in, 1, comm)
        a_ref, b_ref = ins[0], ins[1]
        res_ref = ins[2] if res is not None else None
        o_ref = outs[0]
        if comm is not None:
            i, j, kq = pl.program_id(0), pl.program_id(1), pl.program_id(2)
            first = jnp.logical_and(jnp.logical_and(i == 0, j == 0), kq == 0)
            last = jnp.logical_and(jnp.logical_and(i == gm - 1, j == gn - 1), kq == nk - 1)
            start_comm, wait_comm = comm.run(c_src, c_out, c_sem, first, last)
            start_comm()
        a_tile = a_ref[...].astype(BF16)
        if scale != 1.0:
            a_tile = a_tile * scale

        def product():
            return lax.dot_general(a_tile, b_ref[...].astype(BF16), dims, preferred_element_type=F32)

        if nk == 1:
            r = product()
            if res_ref is not None:
                r = res_ref[...] + r
            o_ref[...] = r.astype(out_dtype)
        else:
            acc_ref = scratch[0] if scratch else o_ref
            kk = pl.program_id(2)

            @pl.when(kk == 0)
            def _():
                acc_ref[...] = jnp.zeros_like(acc_ref) if res_ref is None else res_ref[...]

            acc_ref[...] += product()
            if scratch:
                @pl.when(kk == nk - 1)
                def _():
                    o_ref[...] = acc_ref[...].astype(out_dtype)
        if comm is not None:
            wait_comm()

    assert scale in (1.0, 0.5)
    args = (a, b) + ((res,) if res is not None else ())
    scratch_shapes = [pltpu.VMEM((tm, tn), F32)] if (nk > 1 and out_dtype != F32) else []
    if comm is None:
        return pl.pallas_call(
            kern, name=name, grid=(gm, gn, nk), in_specs=in_specs, out_specs=o_spec, out_shape=o_shape,
            scratch_shapes=scratch_shapes, compiler_params=_params(("parallel", "parallel", "arbitrary")),
        )(*args)
    res_all = pl.pallas_call(
        kern, name=name, grid=(gm, gn, nk), in_specs=in_specs + [ANY] * len(comm.srcs),
        out_specs=[o_spec] + [ANY] * len(comm.outs), out_shape=[o_shape] + comm.outs,
        scratch_shapes=scratch_shapes + comm.scratch(), input_output_aliases=comm.aliases(n_in, 1),
        compiler_params=_params(("arbitrary", "arbitrary", "arbitrary")),
    )(*args, *comm.srcs)
    return res_all[0], list(res_all[1:])


def _gate_up_swiglu(h, w_blocks, name, comm=None):
    t, d = h.shape
    nblk, _, nb = w_blocks.shape
    half = nblk // 2
    tm = _mxu_tile(t, 512)
    steps = t // tm

    def kern(*refs):
        (h_ref, wg_ref, wu_ref), (gu_ref, a_ref), _, c_src, c_out, c_sem = _split_refs(refs, 3, 2, comm)
        if comm is not None:
            j, i = pl.program_id(0), pl.program_id(1)
            start_comm, wait_comm = comm.run(c_src, c_out, c_sem, jnp.logical_and(j == 0, i == 0),
                                             jnp.logical_and(j == half - 1, i == steps - 1))
            start_comm()
        hv = h_ref[...]
        g = lax.dot_general(hv, wg_ref[...], _DIMS["nn"], preferred_element_type=F32)
        u = lax.dot_general(hv, wu_ref[...], _DIMS["nn"], preferred_element_type=F32)
        gu_ref[0] = g
        gu_ref[1] = u
        a_ref[...] = ((g * _sigmoid(g)) * u).astype(BF16)
        if comm is not None:
            wait_comm()

    in_specs = [pl.BlockSpec((tm, d), lambda j, i: (i, 0)), pl.BlockSpec((None, d, nb), lambda j, i: (j, 0, 0)),
                pl.BlockSpec((None, d, nb), lambda j, i: (j + half, 0, 0))]
    out_specs = [pl.BlockSpec((2, tm, nb), lambda j, i: (0, i, j)), pl.BlockSpec((tm, nb), lambda j, i: (i, j))]
    out_shape = [jax.ShapeDtypeStruct((2, t, half * nb), F32), jax.ShapeDtypeStruct((t, half * nb), BF16)]
    if comm is None:
        return pl.pallas_call(kern, name=name, grid=(half, steps), in_specs=in_specs, out_specs=out_specs,
                              out_shape=out_shape, compiler_params=_params(("parallel", "parallel")))(h, w_blocks, w_blocks)
    res_all = pl.pallas_call(
        kern, name=name, grid=(half, steps), in_specs=in_specs + [ANY] * len(comm.srcs),
        out_specs=out_specs + [ANY] * len(comm.outs), out_shape=out_shape + comm.outs,
        scratch_shapes=comm.scratch(), compiler_params=_params(("arbitrary", "arbitrary")),
    )(h, w_blocks, w_blocks, *comm.srcs)
    return list(res_all[:2]), list(res_all[2:])


def _swiglu_bwd_from_out(dy, wd, gu, name, comm=None):
    t, d = dy.shape
    f = wd.shape[0]
    tm, tn = _mxu_tile(t, 1024), _mxu_tile(f, 512)
    gi, gj = t // tm, f // tn

    def kern(*refs):
        (dy_ref, wd_ref, gu_ref), (out_ref,), _, c_src, c_out, c_sem = _split_refs(refs, 3, 1, comm)
        if comm is not None:
            i, j = pl.program_id(0), pl.program_id(1)
            start_comm, wait_comm = comm.run(c_src, c_out, c_sem, jnp.logical_and(i == 0, j == 0),
                                             jnp.logical_and(i == gi - 1, j == gj - 1))
            start_comm()
        da = lax.dot_general(dy_ref[...] * 0.5, wd_ref[...], _DIMS["nt"], preferred_element_type=F32)
        g, u = gu_ref[0], gu_ref[1]
        s = _sigmoid(g)
        out_ref[0] = (da * u * (s * (1.0 + g * (1.0 - s)))).astype(BF16)
        out_ref[1] = (da * (g * s)).astype(BF16)
        if comm is not None:
            wait_comm()

    pair = pl.BlockSpec((2, tm, tn), lambda i, j: (0, i, j))
    in_specs = [pl.BlockSpec((tm, d), lambda i, j: (i, 0)), pl.BlockSpec((tn, d), lambda i, j: (j, 0)), pair]
    out_shape = jax.ShapeDtypeStruct((2, t, f), BF16)
    if comm is None:
        return pl.pallas_call(kern, name=name, grid=(gi, gj), in_specs=in_specs, out_specs=pair, out_shape=out_shape,
                              compiler_params=_params(("parallel", "parallel")))(dy, wd, gu)
    res_all = pl.pallas_call(
        kern, name=name, grid=(gi, gj), in_specs=in_specs + [ANY] * len(comm.srcs),
        out_specs=[pair] + [ANY] * len(comm.outs), out_shape=[out_shape] + comm.outs,
        scratch_shapes=comm.scratch(), compiler_params=_params(("arbitrary", "arbitrary")),
    )(dy, wd, gu, *comm.srcs)
    return res_all[0], list(res_all[1:])


def _rowwise(body, ins, params, outs, accs, name, tm, comm=None):
    rows = ins[0][0].shape[0]
    assert rows % tm == 0, (name, rows, tm)
    in_specs = []
    for arr, off, width in ins:
        assert off % width == 0 and arr.shape[0] == rows, (name, arr.shape, off, width)
        in_specs.append(pl.BlockSpec((tm, width), functools.partial(lambda i, c: (i, c), c=off // width)))
    for p in params:
        in_specs.append(pl.BlockSpec(p.shape, functools.partial(lambda i, nd: (0,) * nd, nd=p.ndim)))
    out_specs = [pl.BlockSpec((tm, w), lambda i: (i, 0)) for w, _ in outs]
    out_specs += [pl.BlockSpec(s, functools.partial(lambda i, nd: (0,) * nd, nd=len(s))) for s, _ in accs]
    out_shape = [jax.ShapeDtypeStruct((rows, w), d) for w, d in outs]
    out_shape += [jax.ShapeDtypeStruct(s, d) for s, d in accs]
    n_in, n_par, n_out = len(ins), len(params), len(outs)

    steps = rows // tm

    def kern(*refs):
        in_refs, out_refs, _, c_src, c_out, c_sem = _split_refs(refs, n_in + n_par, n_out + len(accs), comm)
        first = pl.program_id(0) == 0
        if comm is not None:
            start_comm, wait_comm = comm.run(c_src, c_out, c_sem, first, pl.program_id(0) == steps - 1)
            start_comm()
        res = body(*[r[...] for r in in_refs])
        for r, v in zip(out_refs[:n_out], res[:n_out]):
            r[...] = v.astype(r.dtype)
        for r, v in zip(out_refs[n_out:], res[n_out:]):
            @pl.when(first)
            def _(r=r, v=v):
                r[...] = v.astype(r.dtype)

            @pl.when(jnp.logical_not(first))
            def _(r=r, v=v):
                r[...] += v.astype(r.dtype)
        if comm is not None:
            wait_comm()

    args = [a for a, _, _ in ins] + list(params)
    if comm is None:
        return pl.pallas_call(
            kern, name=name, grid=(steps,), in_specs=in_specs, out_specs=out_specs, out_shape=out_shape,
            compiler_params=_params(("arbitrary",) if accs else ("parallel",)),
        )(*args)
    res_all = pl.pallas_call(
        kern, name=name, grid=(steps,), in_specs=in_specs + [ANY] * len(comm.srcs),
        out_specs=out_specs + [ANY] * len(comm.outs), out_shape=out_shape + comm.outs,
        scratch_shapes=comm.scratch(), input_output_aliases=comm.aliases(len(args), len(out_shape)),
        compiler_params=_params(("arbitrary",)),
    )(*args, *comm.srcs)
    return list(res_all[:len(out_shape)]), list(res_all[len(out_shape):])


def _whole(x):
    return (x, 0, x.shape[1])


def _first(res, comm):
    return res[0] if comm is None else (res[0][0], res[1])


def _rms_fwd(x, gain, name, comm=None):
    def body(xv, g):
        r = lax.rsqrt(jnp.mean(xv * xv, axis=-1, keepdims=True) + EPS)
        return ((xv * r) * g,)

    return _first(_rowwise(body, [_whole(x)], [gain], [(x.shape[1], BF16)], [], name, _tile(x.shape[0], 256, 16),
                           comm=comm), comm)


def _rms_bwd(x, gain, dh, dres, name, comm=None):
    d = x.shape[1]

    def body(*vals):
        if dres is None:
            xv, dhv, g = vals
        else:
            xv, dhv, drv, g = vals
        r = lax.rsqrt(jnp.mean(xv * xv, axis=-1, keepdims=True) + EPS)
        xh = xv * r
        w = dhv * g
        dx = r * (w - xh * jnp.mean(w * xh, axis=-1, keepdims=True))
        if dres is not None:
            dx = drv + dx
        return dx, dx, jnp.sum(dhv * xh, axis=0, keepdims=True)

    ins = [_whole(x), _whole(dh)] + ([_whole(dres)] if dres is not None else [])
    return _rowwise(body, ins, [gain], [(d, F32), (d, BF16)], [((1, d), F32)], name, _tile(x.shape[0], 256, 16),
                    comm=comm)


def _loss_head(y, target, name):
    d = y.shape[1]

    def body(yv, tv):
        e = yv - tv
        part = jnp.sum(jnp.sum(e * e, axis=1, keepdims=True), axis=0, keepdims=True)
        return e / d, e / d, jnp.broadcast_to((0.5 / d) * part, (1, LANES))

    return _rowwise(body, [_whole(y), _whole(target)], [], [(d, F32), (d, BF16)], [((1, LANES), F32)], name,
                    _tile(y.shape[0], 256, 16))


def _head_mean(v, head_dim):
    cols = []
    lane = lax.broadcasted_iota(jnp.int32, (1, LANES), 1)
    for j in range(v.shape[1] // LANES):
        blk = v[:, j * LANES:(j + 1) * LANES]
        if head_dim == LANES:
            m = jnp.sum(blk, axis=-1, keepdims=True)
            cols.append(jnp.broadcast_to(m, blk.shape))
        else:
            lo = jnp.sum(jnp.where(lane < head_dim, blk, 0.0), axis=-1, keepdims=True)
            hi = jnp.sum(jnp.where(lane >= head_dim, blk, 0.0), axis=-1, keepdims=True)
            cols.append(jnp.where(lane < head_dim, lo, hi))
    return jnp.concatenate(cols, axis=1) / head_dim


def _head_norm(xv, g, head_dim):
    r = lax.rsqrt(_head_mean(xv * xv, head_dim) + EPS)
    return (xv * r) * g


def _head_norm_bwd(xv, g, dy, head_dim):
    r = lax.rsqrt(_head_mean(xv * xv, head_dim) + EPS)
    xh = xv * r
    w = dy * g
    dx = r * (w - xh * _head_mean(w * xh, head_dim))
    return dx, jnp.sum(dy * xh, axis=0, keepdims=True)


def _log_sigmoid(x):
    return jnp.minimum(x, 0.0) - jnp.log(1.0 + jnp.exp(-jnp.abs(x)))


def _shift_rows(v, sh, down):
    n = v.shape[0]
    row = lax.broadcasted_iota(jnp.int32, (n, 1), 0)
    if down:
        return jnp.where(row >= sh, pltpu.roll(v, sh, 0), 0.0)
    return jnp.where(row < n - sh, pltpu.roll(v, n - sh, 0), 0.0)


def _scan_rows(v, down):
    sh = 1
    while sh < v.shape[0]:
        v = v + _shift_rows(v, sh, down)
        sh *= 2
    return v


def _window_sum(v, steps, down):
    for s in range(steps):
        v = v + _shift_rows(v, 2 ** s, down)
    return v


def _fox_fwd(qn, kn, vb, cks, batch, seq, name, comm=None):
    t, width = qn.shape
    pairs = width // LANES
    tq = _tile(seq, 512, LANES)
    nq = seq // tq

    def kern(*refs):
        (q_ref, k_ref, v_ref, c_ref), (o_ref, lse_ref), _, c_src, c_out, c_sem = _split_refs(refs, 4, 2, comm)
        qi = pl.program_id(2)
        if comm is not None:
            b_id, p_id = pl.program_id(0), pl.program_id(1)
            first = jnp.logical_and(jnp.logical_and(b_id == 0, p_id == 0), qi == 0)
            last = jnp.logical_and(jnp.logical_and(b_id == batch - 1, p_id == pairs - 1), qi == nq - 1)
            start_comm, wait_comm = comm.run(c_src, c_out, c_sem, first, last)
            start_comm()
        lane = lax.broadcasted_iota(jnp.int32, (1, LANES), 1)
        rowi = lax.broadcasted_iota(jnp.int32, (tq, tq), 0)
        coli = lax.broadcasted_iota(jnp.int32, (tq, tq), 1)
        q_all = q_ref[...]
        o_heads, lse_heads = [], []
        for h in range(2):
            hm = (lane < FOX_HEAD_DIM) if h == 0 else (lane >= FOX_HEAD_DIM)
            q = jnp.where(hm, q_all, jnp.zeros_like(q_all))

            def step(j, carry, h=h, q=q, diagonal=False):
                m, l, acc = carry
                start = pl.multiple_of(j * tq, tq)
                k = k_ref[pl.ds(start, tq), :]
                v = v_ref[pl.ds(start, tq), :]
                s = lax.dot_general(q, k, _DIMS["nt"], preferred_element_type=F32)
                s = s - c_ref[0, h:h + 1, pl.ds(start, tq)]
                if diagonal:
                    s = jnp.where(rowi >= coli, s, NEG)
                m_new = jnp.maximum(m, jnp.max(s, axis=-1, keepdims=True))
                alpha = jnp.exp(m - m_new)
                p = jnp.exp(s - m_new)
                l = alpha * l + jnp.sum(p, axis=-1, keepdims=True)
                acc = alpha * acc + lax.dot_general(p.astype(BF16), v, _DIMS["nn"], preferred_element_type=F32)
                return m_new, l, acc

            init = (jnp.full((tq, 1), NEG, F32), jnp.zeros((tq, 1), F32), jnp.zeros((tq, LANES), F32))
            m, l, acc = step(qi, lax.fori_loop(0, qi, step, init), diagonal=True)
            o_heads.append(acc / l)
            lse_heads.append(jnp.broadcast_to(m + jnp.log(l), (tq, LANES)))
        o_ref[...] = jnp.where(lane < FOX_HEAD_DIM, o_heads[0], o_heads[1])
        lse_ref[...] = jnp.where(lane < FOX_HEAD_DIM, lse_heads[0], lse_heads[1])
        if comm is not None:
            wait_comm()

    q_spec = pl.BlockSpec((tq, LANES), lambda b, hp, qi: (b * nq + qi, hp))
    kv_spec = pl.BlockSpec((seq, LANES), lambda b, hp, qi: (b, hp))
    c_spec = pl.BlockSpec((1, 2, seq), lambda b, hp, qi: (b * pairs + hp, 0, 0))
    in_specs = [q_spec, kv_spec, kv_spec, c_spec]
    out_shape = [jax.ShapeDtypeStruct((t, width), F32), jax.ShapeDtypeStruct((t, width), F32)]
    if comm is None:
        return pl.pallas_call(
            kern, name=name, grid=(batch, pairs, nq), in_specs=in_specs, out_specs=[q_spec, q_spec],
            out_shape=out_shape, compiler_params=_params(("parallel", "parallel", "arbitrary")),
        )(qn, kn, vb, cks)
    res_all = pl.pallas_call(
        kern, name=name, grid=(batch, pairs, nq), in_specs=in_specs + [ANY] * len(comm.srcs),
        out_specs=[q_spec, q_spec] + [ANY] * len(comm.outs), out_shape=out_shape + comm.outs,
        scratch_shapes=comm.scratch(), compiler_params=_params(("arbitrary", "arbitrary", "arbitrary")),
    )(qn, kn, vb, cks, *comm.srcs)
    return list(res_all[:2]), list(res_all[2:])


def _fox_bwd(qn, kn, vb, cks, o, lse, do, batch, seq, name, comm=None):
    t, width = qn.shape
    pairs = width // LANES
    tk = _tile(seq, 512, LANES)
    nk = seq // tk
    scale = FOX_HEAD_DIM ** -0.5

    def kern(*refs):
        ins, outs, _, c_src, c_out, c_sem = _split_refs(refs, 7, 5, comm)
        k_ref, v_ref, q_ref, c_ref, o_ref, lse_ref, do_ref = ins
        dk_ref, dv_ref, dq_ref, dc_ref, dcq_ref = outs
        kj = pl.program_id(2)
        if comm is not None:
            b_id, p_id = pl.program_id(0), pl.program_id(1)
            first = jnp.logical_and(jnp.logical_and(b_id == 0, p_id == 0), kj == 0)
            last = jnp.logical_and(jnp.logical_and(b_id == batch - 1, p_id == pairs - 1), kj == nk - 1)
            start_comm, wait_comm = comm.run(c_src, c_out, c_sem, first, last)
            start_comm()
        lane = lax.broadcasted_iota(jnp.int32, (1, LANES), 1)
        rowi = lax.broadcasted_iota(jnp.int32, (tk, tk), 0)
        coli = lax.broadcasted_iota(jnp.int32, (tk, tk), 1)

        @pl.when(kj == 0)
        def _():
            dq_ref[...] = jnp.zeros_like(dq_ref)
            dcq_ref[...] = jnp.zeros_like(dcq_ref)

        k_all = k_ref[...]
        v_all = v_ref[...]
        kstart = pl.multiple_of(kj * tk, tk)
        dk_heads, dv_heads = [], []
        for h in range(2):
            hm = (lane < FOX_HEAD_DIM) if h == 0 else (lane >= FOX_HEAD_DIM)
            kh = jnp.where(hm, k_all, jnp.zeros_like(k_all))
            vh = jnp.where(hm, v_all, jnp.zeros_like(v_all))
            c_row = c_ref[0, h:h + 1, pl.ds(kstart, tk)]

            def step(qi, carry, h=h, hm=hm, kh=kh, vh=vh, c_row=c_row, diagonal=False):
                dk_acc, dv_acc, dc_acc = carry
                start = pl.multiple_of(qi * tk, tk)
                q = q_ref[pl.ds(start, tk), :]
                dov = do_ref[pl.ds(start, tk), :]
                ov = o_ref[pl.ds(start, tk), :]
                lse_col = jnp.max(jnp.where(hm, lse_ref[pl.ds(start, tk), :], NEG), axis=-1, keepdims=True)
                dob = jnp.where(hm, dov, 0.0).astype(BF16)
                dcol = jnp.sum(dob.astype(F32) * ov, axis=-1, keepdims=True)
                s = lax.dot_general(q, kh, _DIMS["nt"], preferred_element_type=F32) - c_row
                p = jnp.exp(s - lse_col)
                if diagonal:
                    p = jnp.where(rowi >= coli, p, 0.0)
                dp = lax.dot_general(dob, vh, _DIMS["nt"], preferred_element_type=F32)
                ds = p * (dp - dcol)
                ds_b = ds.astype(BF16)
                dv_acc = dv_acc + lax.dot_general(p.astype(BF16), dob, _DIMS["tn"], preferred_element_type=F32)
                dk_acc = dk_acc + lax.dot_general(ds_b, q, _DIMS["tn"], preferred_element_type=F32)
                dq_part = lax.dot_general(ds_b * scale, kh, _DIMS["nn"], preferred_element_type=F32)
                dq_ref[pl.ds(start, tk), :] += dq_part
                dcq_ref[pl.ds(start, tk), :] += jnp.where(hm, jnp.sum(ds, axis=-1, keepdims=True), 0.0)
                dc_acc = dc_acc - jnp.sum(ds, axis=0, keepdims=True)
                return dk_acc, dv_acc, dc_acc

            init = (jnp.zeros((tk, LANES), F32), jnp.zeros((tk, LANES), F32), jnp.zeros((1, tk), F32))
            dk_acc, dv_acc, dc_acc = lax.fori_loop(kj + 1, nk, step, step(kj, init, diagonal=True))
            dk_heads.append(dk_acc)
            dv_heads.append(dv_acc)
            dc_ref[0, h:h + 1, pl.ds(kstart, tk)] = dc_acc
        dk_ref[...] = jnp.where(lane < FOX_HEAD_DIM, dk_heads[0], dk_heads[1])
        dv_ref[...] = jnp.where(lane < FOX_HEAD_DIM, dv_heads[0], dv_heads[1]).astype(BF16)
        if comm is not None:
            wait_comm()

    kv_spec = pl.BlockSpec((tk, LANES), lambda b, hp, kj: (b * nk + kj, hp))
    full_spec = pl.BlockSpec((seq, LANES), lambda b, hp, kj: (b, hp))
    c_spec = pl.BlockSpec((1, 2, seq), lambda b, hp, kj: (b * pairs + hp, 0, 0))
    in_specs = [kv_spec, kv_spec, full_spec, c_spec, full_spec, full_spec, full_spec]
    out_specs = [kv_spec, kv_spec, full_spec, c_spec, full_spec]
    out_shape = [jax.ShapeDtypeStruct((t, width), F32), jax.ShapeDtypeStruct((t, width), BF16),
                 jax.ShapeDtypeStruct((t, width), F32), jax.ShapeDtypeStruct(cks.shape, F32),
                 jax.ShapeDtypeStruct((t, width), F32)]
    if comm is None:
        return pl.pallas_call(
            kern, name=name, grid=(batch, pairs, nk), in_specs=in_specs, out_specs=out_specs, out_shape=out_shape,
            compiler_params=_params(("parallel", "parallel", "arbitrary")),
        )(kn, vb, qn, cks, o, lse, do)
    res_all = pl.pallas_call(
        kern, name=name, grid=(batch, pairs, nk), in_specs=in_specs + [ANY] * len(comm.srcs),
        out_specs=out_specs + [ANY] * len(comm.outs), out_shape=out_shape + comm.outs,
        scratch_shapes=comm.scratch(), compiler_params=_params(("arbitrary", "arbitrary", "arbitrary")),
    )(kn, vb, qn, cks, o, lse, do, *comm.srcs)
    return list(res_all[:5]), list(res_all[5:])


def _mem_fwd(qn, kn, vb, batch, seq, mlen, name):
    t, width = qn.shape
    heads = width // LANES
    tq = _tile(seq, 512, LANES)
    nq = seq // tq
    scale = MEM_HEAD_DIM ** -0.5

    def kern(q_ref, k_ref, v_ref, o_ref):
        s = lax.dot_general(q_ref[...], k_ref[...], _DIMS["nt"], preferred_element_type=F32) * scale
        e = jnp.exp(s - jnp.max(s, axis=-1, keepdims=True))
        p = e / jnp.sum(e, axis=-1, keepdims=True)
        o_ref[...] = lax.dot_general(p.astype(BF16), v_ref[...], _DIMS["nn"], preferred_element_type=F32)

    q_spec = pl.BlockSpec((tq, LANES), lambda b, h, qi: (b * nq + qi, h))
    kv_spec = pl.BlockSpec((mlen, LANES), lambda b, h, qi: (b, h))
    return pl.pallas_call(
        kern, name=name, grid=(batch, heads, nq), in_specs=[q_spec, kv_spec, kv_spec], out_specs=q_spec,
        out_shape=jax.ShapeDtypeStruct((t, width), F32),
        compiler_params=_params(("parallel", "parallel", "parallel")),
    )(qn, kn, vb)


def _mem_bwd(qn, kn, vb, do, batch, seq, mlen, name):
    t, width = qn.shape
    heads = width // LANES
    tq = _tile(seq, 512, LANES)
    nq = seq // tq
    scale = MEM_HEAD_DIM ** -0.5

    def kern(q_ref, k_ref, v_ref, do_ref, dq_ref, dk_ref, dv_ref):
        qi = pl.program_id(2)
        q, k, v = q_ref[...], k_ref[...], v_ref[...]
        dob = do_ref[...].astype(BF16)
        s = lax.dot_general(q, k, _DIMS["nt"], preferred_element_type=F32) * scale
        e = jnp.exp(s - jnp.max(s, axis=-1, keepdims=True))
        p = e / jnp.sum(e, axis=-1, keepdims=True)
        dp = lax.dot_general(dob, v, _DIMS["nt"], preferred_element_type=F32)
        ds = p * (dp - jnp.sum(p * dp, axis=-1, keepdims=True))
        dsb = (ds * scale).astype(BF16)
        dq_ref[...] = lax.dot_general(dsb, k, _DIMS["nn"], preferred_element_type=F32)
        dk = lax.dot_general(dsb, q, _DIMS["tn"], preferred_element_type=F32)
        dv = lax.dot_general(p.astype(BF16), dob, _DIMS["tn"], preferred_element_type=F32)

        @pl.when(qi == 0)
        def _():
            dk_ref[...] = dk
            dv_ref[...] = dv

        @pl.when(qi > 0)
        def _():
            dk_ref[...] += dk
            dv_ref[...] += dv

    q_spec = pl.BlockSpec((tq, LANES), lambda b, h, qi: (b * nq + qi, h))
    kv_spec = pl.BlockSpec((mlen, LANES), lambda b, h, qi: (b, h))
    return pl.pallas_call(
        kern, name=name, grid=(batch, heads, nq), in_specs=[q_spec, kv_spec, kv_spec, q_spec],
        out_specs=[q_spec, kv_spec, kv_spec],
        out_shape=[jax.ShapeDtypeStruct((t, width), F32), jax.ShapeDtypeStruct(kn.shape, F32),
                   jax.ShapeDtypeStruct(kn.shape, F32)],
        compiler_params=_params(("parallel", "parallel", "arbitrary")),
    )(qn, kn, vb, do)


def _position():
    return lax.axis_index("x"), lax.axis_index("y"), lax.axis_index("c")


def _other_chips(x, y):
    return [(1 - x, y), (x, 1 - y), (1 - x, 1 - y)]


def _remote(src, dst, sems, to):
    send, recv = sems.remote()
    return pltpu.make_async_remote_copy(src_ref=src, dst_ref=dst, send_sem=send, recv_sem=recv,
                                        device_id=to, device_id_type=MESH)


def _gather_first(shards):
    n = len(shards)

    def plan(src, out, sems):
        x, y, c = _position()
        me = 4 * x + 2 * y + c
        peers = [(x, y, 1 - c)] + [(px, py, c) for px, py in _other_chips(x, y)]
        copies = []
        for i in range(n):
            copies.append(pltpu.make_async_copy(src[i], out[i].at[me], sems.one_local()))
            copies += [_remote(src[i], out[i].at[me], sems, to) for to in peers]
        return copies

    outs = [jax.ShapeDtypeStruct((N_DEV,) + s.shape, s.dtype) for s in shards]
    return _Comm(shards, outs, 4 * n, n, plan)


def _gather_second(bufs):
    n = len(bufs)

    def plan(src, out, sems):
        x, y, c = _position()
        copies = []
        for i in range(n):
            for px, py in _other_chips(x, y):
                blk = out[i].at[4 * px + 2 * py + c]
                copies.append(_remote(blk, blk, sems, (x, y, 1 - c)))
        return copies

    outs = [jax.ShapeDtypeStruct(b.shape, b.dtype) for b in bufs]
    return _Comm(bufs, outs, 3 * n, 0, plan, in_place=True)


def _scatter_first(partials):
    n = len(partials)

    def plan(src, out, sems):
        x, y, c = _position()
        return [_remote(src[i].at[2 * q + (1 - c)], out[i].at[q], sems, (x, y, 1 - c))
                for i in range(n) for q in range(4)]

    outs = [jax.ShapeDtypeStruct((4,) + g.shape[1:], g.dtype) for g in partials]
    return _Comm(partials, outs, 4 * n, 0, plan)


def _scatter_second(halves):
    n = len(halves)

    def plan(src, out, sems):
        x, y, c = _position()
        return [_remote(src[i].at[2 * px + py], out[i].at[r], sems, (px, py, c))
                for i in range(n) for r, (px, py) in enumerate(_other_chips(x, y))]

    outs = [jax.ShapeDtypeStruct((3,) + h.shape[1:], h.dtype) for h in halves]
    return _Comm(halves, outs, 3 * n, 0, plan)


def _comm_call(comm, name):
    ns, no = len(comm.srcs), len(comm.outs)

    def body(*refs):
        copies = comm.plan(list(refs[:ns]), list(refs[ns:ns + no]), _Sems(*refs[ns + no:]))
        for cp in copies:
            cp.start()
        for cp in copies:
            cp.wait()

    return pl.pallas_call(
        body, name=name, in_specs=[ANY] * ns, out_specs=[ANY] * no, out_shape=comm.outs,
        scratch_shapes=comm.scratch(), input_output_aliases=comm.aliases(0, 0),
    )(*comm.srcs)


def _all_gather_vmem(shard, name):
    first = _gather_first([shard])

    def body(x_ref, out_ref, send, recv, local):
        sems = _Sems(send, recv, local)
        copies = first.plan([x_ref], [out_ref], sems)
        for cp in copies:
            cp.start()
        x, y, c = _position()
        passed = []
        for j, (px, py) in enumerate(_other_chips(x, y)):
            copies[2 + j].wait_recv()
            blk = out_ref.at[4 * px + 2 * py + c]
            fwd = _remote(blk, blk, sems, (x, y, 1 - c))
            fwd.start()
            passed.append(fwd)
        copies[0].wait()
        copies[1].wait()
        for cp in copies[2:]:
            cp.wait_send()
        for cp in passed:
            cp.wait()

    vm = pl.BlockSpec(memory_space=pltpu.VMEM)
    return pl.pallas_call(
        body, name=name, in_specs=[vm], out_specs=vm,
        out_shape=jax.ShapeDtypeStruct((N_DEV,) + shard.shape, shard.dtype),
        scratch_shapes=[pltpu.SemaphoreType.DMA((7,)), pltpu.SemaphoreType.DMA((7,)), pltpu.SemaphoreType.DMA((1,))],
    )(shard)


def _rs_pair_add(partial, landed, where, name):
    _, r, c = partial.shape
    tm = _tile(r, 256, 16)

    def kern(where_ref, g_ref, l_ref, own_ref, hb_ref):
        s = g_ref[...] + l_ref[...]
        hb_ref[...] = s.astype(BF16)

        @pl.when(pl.program_id(1) == where_ref[1])
        def _():
            own_ref[...] = s

    grid_spec = pltpu.PrefetchScalarGridSpec(
        num_scalar_prefetch=1, grid=(r // tm, 4),
        in_specs=[pl.BlockSpec((None, tm, c), lambda i, q, wr: (2 * q + wr[0], i, 0)),
                  pl.BlockSpec((None, tm, c), lambda i, q, wr: (q, i, 0))],
        out_specs=[pl.BlockSpec((tm, c), lambda i, q, wr: (i, 0)),
                   pl.BlockSpec((None, tm, c), lambda i, q, wr: (q, i, 0))])
    return pl.pallas_call(
        kern, name=name, grid_spec=grid_spec,
        out_shape=[jax.ShapeDtypeStruct((r, c), F32), jax.ShapeDtypeStruct((4, r, c), BF16)],
        compiler_params=_params(("parallel", "arbitrary")),
    )(where, partial, landed)


def _adam_math(g, w, m, v):
    m = ADAM_B1 * m + (1.0 - ADAM_B1) * g
    v = ADAM_B2 * v + (1.0 - ADAM_B2) * (g * g)
    m_hat = m / (1.0 - ADAM_B1 ** ADAM_STEP)
    v_hat = v / (1.0 - ADAM_B2 ** ADAM_STEP)
    delta = -ADAM_LR * (m_hat / (jnp.sqrt(v_hat) + ADAM_EPS) + ADAM_WD * w)
    return delta, m, v


def _rs_finish_adam(own, landed, w, m, v, name):
    r, c = own.shape
    tm = _tile(r, 128, 16)

    def kern(h_ref, l_ref, w_ref, m_ref, v_ref, *outs):
        g = ((h_ref[...] + l_ref[0].astype(F32)) + l_ref[1].astype(F32)) + l_ref[2].astype(F32)
        delta, m_new, v_new = _adam_math(g, w_ref[...], m_ref[...], v_ref[...])
        for ref, val in zip(outs, (g, delta, m_new, v_new)):
            ref[...] = val

    flat = pl.BlockSpec((tm, c), lambda i: (i, 0))
    return pl.pallas_call(
        kern, name=name, grid=(r // tm,), in_specs=[flat, pl.BlockSpec((3, tm, c), lambda i: (0, i, 0)), flat, flat, flat],
        out_specs=[flat] * 4, out_shape=[jax.ShapeDtypeStruct((r, c), F32)] * 4,
        compiler_params=_params(("parallel",)))(own, landed, w, m, v)


def _allreduce_adam(gathered, w, m, v, name):
    _, r, c = gathered.shape

    def kern(a_ref, w_ref, m_ref, v_ref, g_out, d_out, m_out, v_out):
        g = a_ref[0]
        for j in range(1, N_DEV):
            g = g + a_ref[j]
        delta, m_new, v_new = _adam_math(g, w_ref[...], m_ref[...], v_ref[...])
        g_out[...] = g
        d_out[...] = delta
        m_out[...] = m_new
        v_out[...] = v_new

    return pl.pallas_call(
        kern, name=name, out_shape=[jax.ShapeDtypeStruct((r, c), F32)] * 4,
        compiler_params=_params(),
    )(gathered, w, m, v)


def _cols_from_blocks(g):
    n, k, nb = g.shape
    return jnp.transpose(g, (1, 0, 2)).reshape(k, n * nb)


def _blocks_from_cols(w):
    k, n = w.shape
    return jnp.transpose(w.reshape(k, N_DEV, n // N_DEV), (1, 0, 2))


def _pack_small(parts):
    flat = []
    for p in parts:
        v = p.reshape(-1)
        flat.append(jnp.pad(v, (0, (-v.shape[0]) % (8 * LANES))))
    return jnp.concatenate(flat).reshape(-1, LANES)


def _unpack_small(buf, like):
    out, pos = [], 0
    flat = buf.reshape(-1)
    for p in like:
        size = p.size
        out.append(flat[pos:pos + size].reshape(p.shape))
        pos += size + (-size) % (8 * LANES)
    return out


def kernel(x, mem, ffn1_norm, ffn1_w_gate_up, ffn1_w_down, mix_norm, mem_norm, w_in, b_forget, pool_w, pool_scale, w_pool_up, fox_q_norm, fox_k_norm, w_fox_o, w_mem_kv, mem_q_norm, mem_k_norm, w_mem_o, w_out, ffn2_norm, ffn2_w_gate_up, ffn2_w_down, loss_target, m_ffn1_norm, m_ffn1_w_gate_up, m_ffn1_w_down, m_mix_norm, m_mem_norm, m_w_in, m_b_forget, m_pool_w, m_pool_scale, m_w_pool_up, m_fox_q_norm, m_fox_k_norm, m_w_fox_o, m_w_mem_kv, m_mem_q_norm, m_mem_k_norm, m_w_mem_o, m_w_out, m_ffn2_norm, m_ffn2_w_gate_up, m_ffn2_w_down, v_ffn1_norm, v_ffn1_w_gate_up, v_ffn1_w_down, v_mix_norm, v_mem_norm, v_w_in, v_b_forget, v_pool_w, v_pool_scale, v_w_pool_up, v_fox_q_norm, v_fox_k_norm, v_w_fox_o, v_w_mem_kv, v_mem_q_norm, v_mem_k_norm, v_w_mem_o, v_w_out, v_ffn2_norm, v_ffn2_w_gate_up, v_ffn2_w_down):
    names = ["ffn1_norm", "ffn1_w_gate_up", "ffn1_w_down", "mix_norm", "mem_norm", "w_in", "b_forget", "pool_w",
             "pool_scale", "w_pool_up", "fox_q_norm", "fox_k_norm", "w_fox_o", "w_mem_kv", "mem_q_norm",
             "mem_k_norm", "w_mem_o", "w_out", "ffn2_norm", "ffn2_w_gate_up", "ffn2_w_down"]
    w_args = [ffn1_norm, ffn1_w_gate_up, ffn1_w_down, mix_norm, mem_norm, w_in, b_forget, pool_w, pool_scale,
              w_pool_up, fox_q_norm, fox_k_norm, w_fox_o, w_mem_kv, mem_q_norm, mem_k_norm, w_mem_o, w_out,
              ffn2_norm, ffn2_w_gate_up, ffn2_w_down]
    m_args = [m_ffn1_norm, m_ffn1_w_gate_up, m_ffn1_w_down, m_mix_norm, m_mem_norm, m_w_in, m_b_forget, m_pool_w,
              m_pool_scale, m_w_pool_up, m_fox_q_norm, m_fox_k_norm, m_w_fox_o, m_w_mem_kv, m_mem_q_norm,
              m_mem_k_norm, m_w_mem_o, m_w_out, m_ffn2_norm, m_ffn2_w_gate_up, m_ffn2_w_down]
    v_args = [v_ffn1_norm, v_ffn1_w_gate_up, v_ffn1_w_down, v_mix_norm, v_mem_norm, v_w_in, v_b_forget, v_pool_w,
              v_pool_scale, v_w_pool_up, v_fox_q_norm, v_fox_k_norm, v_w_fox_o, v_w_mem_kv, v_mem_q_norm,
              v_mem_k_norm, v_w_mem_o, v_w_out, v_ffn2_norm, v_ffn2_w_gate_up, v_ffn2_w_down]
    W = dict(zip(names, w_args))
    M = dict(zip(names, m_args))
    V = dict(zip(names, v_args))

    batch, seq, d = x.shape
    mlen = mem.shape[1]
    t = batch * seq
    gate_w = 3 * d
    z_gate, z_q = 0, gate_w
    z_k, z_v = z_q + FOX_WIDTH, z_q + 2 * FOX_WIDTH
    z_u = z_q + 3 * FOX_WIDTH
    z_qm = z_u + POOL_WIDTH
    z_f = z_qm + MEM_WIDTH
    z_width = -(-(z_f + F_PAD) // 512) * 512

    x2d = x.reshape(t, d)
    mem2d = mem.reshape(batch * mlen, d)
    tgt2d = loss_target.reshape(t, d)

    big = ["ffn1_w_gate_up", "ffn1_w_down", "w_in", "w_pool_up", "w_fox_o", "w_mem_kv", "w_mem_o", "w_out",
           "ffn2_w_gate_up", "ffn2_w_down"]
    col_sharded = {"ffn1_w_gate_up", "ffn2_w_gate_up", "w_in", "w_pool_up", "w_fox_o", "w_mem_o"}
    mixer_small = ["w_pool_up", "w_fox_o", "w_mem_kv", "w_mem_o", "w_out"]
    shard = {n: W[n][0].astype(BF16) for n in big}

    def rows_of(g):
        return g.reshape(-1, g.shape[2])

    gu_caps_nt, gu_caps_tn = (1024, 1024, 1408), (1024, 1408, 2048)
    down_caps, dwd_caps = (1024, 1024, 2816), (1408, 1024, 2048)

    (wgu1,) = _comm_call(_gather_first([shard["ffn1_w_gate_up"]]), "ag_first")
    h1, (wgu1,) = _rms_fwd(x2d, W["ffn1_norm"], "ffn1_rms", comm=_gather_second([wgu1]))
    w_in_top, w_in_bot = shard["w_in"][:d // 2], shard["w_in"][d // 2:]
    (gu1, a1), bufs = _gate_up_swiglu(h1, wgu1, "ffn1_gu", comm=_gather_first([shard["ffn1_w_down"], w_in_top]))
    wd1_g, w_in_top_g = _comm_call(_gather_second(bufs), "ag_second")
    wd1 = rows_of(wd1_g)
    x1, (w_in_bot_g,) = _mm(a1, wd1, "nn", "ffn1_down", scale=0.5, res=x2d, caps=down_caps,
                            comm=_gather_first([w_in_bot]))
    h2, (w_in_bot_g,) = _rms_fwd(x1, W["mix_norm"], "mix_rms", comm=_gather_second([w_in_bot_g]))

    o_u, o_q, o_v = 0, POOL_WIDTH, POOL_WIDTH + 2 * FOX_WIDTH
    o_f = o_v + FOX_WIDTH
    o_qm = o_f + FOX_HEADS
    o_g = o_qm + MEM_WIDTH
    wi = jnp.concatenate([_cols_from_blocks(w_in_top_g), _cols_from_blocks(w_in_bot_g)], axis=0)
    w_in_pad = jnp.concatenate(
        [wi[:, o_g:o_g + gate_w], wi[:, o_q:o_q + 3 * FOX_WIDTH], wi[:, o_u:o_u + POOL_WIDTH],
         wi[:, o_qm:o_qm + MEM_WIDTH], wi[:, o_f:o_f + FOX_HEADS],
         jnp.zeros((d, z_width - z_f - FOX_HEADS), BF16)], axis=1)
    group_b = mixer_small + ["ffn2_w_down"]
    z, bufs = _mm(h2, w_in_pad, "nn", "mix_in", comm=_gather_first([shard[n] for n in group_b]))

    pool_w_b = W["pool_w"][0].astype(BF16)

    def pool_fwd_body(u, pw, ps):
        row = lax.broadcasted_iota(jnp.int32, (seq, 1), 0)
        diffs, mixed = [], []
        for g in range(POOL_GROUPS):
            ug = u[:, g * POOL_GROUP_DIM:(g + 1) * POOL_GROUP_DIM]
            cnt = jnp.minimum(row + 1, POOL_WINDOWS[g]).astype(F32)
            diff = _window_sum(ug, g + 1, True) / cnt - ug
            diffs.append(diff)
            mixed.append(lax.dot_general(diff.astype(BF16), pw[g], _DIMS["nn"], preferred_element_type=F32))
        diffs = jnp.concatenate(diffs, axis=1)
        mixed = jnp.concatenate(mixed, axis=1)
        return mixed * ps, diffs, mixed

    ypp, pool_diff, pool_mixed = _rowwise(
        pool_fwd_body, [(z, z_u, POOL_WIDTH)], [pool_w_b, W["pool_scale"]],
        [(POOL_WIDTH, BF16), (POOL_WIDTH, BF16), (POOL_WIDTH, F32)], [], "pool_fwd", seq)

    gq = jnp.tile(W["fox_q_norm"], (1, FOX_HEADS))
    gk = jnp.tile(W["fox_k_norm"], (1, FOX_HEADS))
    b_pad = jnp.pad(W["b_forget"], ((0, 0), (0, F_PAD - FOX_HEADS)))

    def fox_prep_body(q, k, v, f, gqv, gkv, bv):
        qn_scaled = _head_norm(q, gqv, FOX_HEAD_DIM).astype(BF16) * FOX_HEAD_DIM ** -0.5
        return qn_scaled, _head_norm(k, gkv, FOX_HEAD_DIM), v, _log_sigmoid(f + bv)

    tm_e = _tile(t, 256, 16)
    (qn, kn, vb, logf), bufs = _rowwise(
        fox_prep_body, [(z, z_q, FOX_WIDTH), (z, z_k, FOX_WIDTH), (z, z_v, FOX_WIDTH), (z, z_f, F_PAD)],
        [gq, gk, b_pad], [(FOX_WIDTH, BF16), (FOX_WIDTH, BF16), (FOX_WIDTH, BF16), (F_PAD, F32)], [], "fox_prep", tm_e,
        comm=_gather_second(bufs))
    full = {n: (_cols_from_blocks(g) if n in col_sharded else rows_of(g)) for n, g in zip(group_b, bufs)}
    wd2 = full["ffn2_w_down"]
    y_pool = _mm(ypp, full["w_pool_up"], "nn", "pool_up")
    csum = _rowwise(lambda v: (_scan_rows(v, True),), [_whole(logf)], [], [(F_PAD, F32)], [], "fox_cumsum", seq)[0]
    cks = jnp.transpose(csum.reshape(batch, seq, F_PAD)[:, :, :FOX_HEADS], (0, 2, 1)).reshape(
        batch * FOX_HEADS // 2, 2, seq)
    (o_fox, lse), (wgu2,) = _fox_fwd(qn, kn, vb, cks, batch, seq, "fox_fwd",
                                     comm=_gather_first([shard["ffn2_w_gate_up"]]))
    y_fox = _mm(o_fox, full["w_fox_o"], "nn", "fox_o")

    memn = _rms_fwd(mem2d, W["mem_norm"], "mem_rms")
    kv = _mm(memn, full["w_mem_kv"], "nn", "mem_kv")
    gqm = jnp.tile(W["mem_q_norm"], (1, MEM_HEADS))
    gkm = jnp.tile(W["mem_k_norm"], (1, MEM_HEADS))
    qmn = _rowwise(lambda q, g: (_head_norm(q, g, MEM_HEAD_DIM),), [(z, z_qm, MEM_WIDTH)], [gqm],
                   [(MEM_WIDTH, BF16)], [], "memq_prep", tm_e)[0]
    kmn, vmb = _rowwise(lambda k, v, g: (_head_norm(k, g, MEM_HEAD_DIM), v),
                        [(kv, 0, MEM_WIDTH), (kv, MEM_WIDTH, MEM_WIDTH)], [gkm],
                        [(MEM_WIDTH, BF16), (MEM_WIDTH, BF16)], [], "memk_prep", _tile(batch * mlen, 256, 16))
    o_mem = _mem_fwd(qmn, kmn, vmb, batch, seq, mlen, "mem_fwd")
    y_mem = _mm(o_mem, full["w_mem_o"], "nn", "mem_o")

    def gate_fwd_body(gp, gf, gm, yp, yf, ym):
        return ((_sigmoid(gp) * yp + _sigmoid(gf) * yf) + _sigmoid(gm) * ym,)

    tm_g = _tile(t, 128, 16)
    (merged,), (wgu2,) = _rowwise(
        gate_fwd_body, [(z, 0, d), (z, d, d), (z, 2 * d, d), _whole(y_pool), _whole(y_fox), _whole(y_mem)],
        [], [(d, BF16)], [], "gate_fwd", tm_g, comm=_gather_second([wgu2]))
    x2 = _mm(merged, full["w_out"], "nn", "mix_out", res=x1)

    h3 = _rms_fwd(x2, W["ffn2_norm"], "ffn2_rms")
    gu2, a2 = _gate_up_swiglu(h3, wgu2, "ffn2_gu")
    x3 = _mm(a2, wd2, "nn", "ffn2_down", scale=0.5, res=x2, caps=down_caps)
    dy, dy_b, loss_part = _loss_head(x3, tgt2d, "loss")
    loss = lax.psum(loss_part[0, 0], ("x", "y", "c"))

    cx, cy, cc = _position()
    where = jnp.stack([cc, 2 * cx + cy]).astype(jnp.int32)
    G, own, landed = {}, {}, {}

    def row_blocks(g):
        return g.reshape(N_DEV, g.shape[0] // N_DEV, g.shape[1])

    def pair_add(n, partial, from_core):
        own[n], chip_sums = _rs_pair_add(partial, from_core, where, "rs_add_" + n)
        return chip_sums

    p_wd2 = row_blocks(_mm(a2, dy_b, "tn", "ffn2_dwd", scale=0.5, caps=dwd_caps))
    dgu2, (l_wd2,) = _swiglu_bwd_from_out(dy_b, wd2, gu2, "ffn2_dgu", comm=_scatter_first([p_wd2]))
    s_wd2 = pair_add("ffn2_w_down", p_wd2, l_wd2)
    p_wgu2, (landed["ffn2_w_down"],) = _mm(h3, dgu2, "tn", "ffn2_dwgu", caps=gu_caps_tn, out_blocks=True, halves="b",
                                           comm=_scatter_second([s_wd2]))
    dh3, (l_wgu2,) = _mm(dgu2, wgu2, "nt", "ffn2_dh", caps=gu_caps_nt, b_blocks=True, halves="a",
                         comm=_scatter_first([p_wgu2]))
    s_wgu2 = pair_add("ffn2_w_gate_up", p_wgu2, l_wgu2)
    dx2, dx2_b, G["ffn2_norm"] = _rms_bwd(x2, W["ffn2_norm"], dh3, dy, "ffn2_drms")

    dmerged = _mm(dx2_b, full["w_out"], "nt", "mix_out_dx")
    P = {"w_out": row_blocks(_mm(merged, dx2_b, "tn", "mix_out_dw"))}

    def gate_bwd_body(gp, gf, gm, yp, yf, ym, dm):
        outs_dl, outs_dy = [], []
        for gl, yv in ((gp, yp), (gf, yf), (gm, ym)):
            s = _sigmoid(gl)
            outs_dl.append((dm * yv) * (s * (1.0 - s)))
            outs_dy.append(dm * s)
        return (jnp.concatenate(outs_dl, axis=1), *outs_dy)

    dz_gate, dy_pool, dy_fox, dy_mem = _rowwise(
        gate_bwd_body, [(z, 0, d), (z, d, d), (z, 2 * d, d), _whole(y_pool), _whole(y_fox), _whole(y_mem), _whole(dmerged)],
        [], [(gate_w, BF16), (d, BF16), (d, BF16), (d, BF16)], [], "gate_bwd", tm_g)

    dypp = _mm(dy_pool, full["w_pool_up"], "nt", "pool_up_dx")
    P["w_pool_up"] = _blocks_from_cols(_mm(ypp, dy_pool, "tn", "pool_up_dw"))

    def pool_bwd_body(dyv, mixed, diff, pw, ps):
        row = lax.broadcasted_iota(jnp.int32, (seq, 1), 0)
        d_scale = jnp.sum(dyv * mixed, axis=0, keepdims=True)
        dmix = (dyv * ps).astype(BF16)
        du, dpw = [], []
        for g in range(POOL_GROUPS):
            sl = slice(g * POOL_GROUP_DIM, (g + 1) * POOL_GROUP_DIM)
            dmg = dmix[:, sl]
            ddiff = lax.dot_general(dmg, pw[g], _DIMS["nt"], preferred_element_type=F32)
            dpw.append(lax.dot_general(diff[:, sl], dmg, _DIMS["tn"], preferred_element_type=F32))
            cnt = jnp.minimum(row + 1, POOL_WINDOWS[g]).astype(F32)
            du.append(_window_sum(ddiff / cnt, g + 1, False) - ddiff)
        return jnp.concatenate(du, axis=1), d_scale, jnp.concatenate(dpw, axis=0)

    dz_u, G["pool_scale"], d_pool_w = _rowwise(
        pool_bwd_body, [_whole(dypp), _whole(pool_mixed), _whole(pool_diff)], [pool_w_b, W["pool_scale"]],
        [(POOL_WIDTH, BF16)], [((1, POOL_WIDTH), F32), ((POOL_WIDTH, POOL_GROUP_DIM), F32)],
        "pool_bwd", seq)
    G["pool_w"] = d_pool_w.reshape(1, POOL_GROUPS, POOL_GROUP_DIM, POOL_GROUP_DIM)

    do_fox = _mm(dy_fox, full["w_fox_o"], "nt", "fox_o_dx")
    P["w_fox_o"] = _blocks_from_cols(_mm(o_fox, dy_fox, "tn", "fox_o_dw"))
    (dkn, dz_v, dqn, dcks, dcq), (landed["ffn2_w_gate_up"],) = _fox_bwd(
        qn, kn, vb, cks, o_fox, lse, do_fox, batch, seq, "fox_bwd", comm=_scatter_second([s_wgu2]))
    dcs = jnp.transpose(dcks.reshape(batch, FOX_HEADS, seq), (0, 2, 1)).reshape(t, FOX_HEADS)
    dcs = jnp.pad(dcs, ((0, 0), (0, F_PAD - FOX_HEADS)))
    dcq = jnp.pad(dcq.reshape(t, FOX_HEADS, FOX_HEAD_DIM)[:, :, 0], ((0, 0), (0, F_PAD - FOX_HEADS)))

    def fox_f_bwd_body(dc, dc_rows, f, bv):
        lane = lax.broadcasted_iota(jnp.int32, (1, F_PAD), 1)
        dlogf = _scan_rows(dc + dc_rows, False)
        df = jnp.where(lane < FOX_HEADS, dlogf * _sigmoid(-(f + bv)), 0.0)
        return df, jnp.sum(df, axis=0, keepdims=True)

    dz_f, db_pad = _rowwise(fox_f_bwd_body, [_whole(dcs), _whole(dcq), (z, z_f, F_PAD)], [b_pad], [(F_PAD, BF16)],
                            [((1, F_PAD), F32)], "fox_f_bwd", seq)
    G["b_forget"] = db_pad[:, :FOX_HEADS]

    def fox_qk_bwd_body(q, k, dq, dk, gqv, gkv):
        dqr, dgq = _head_norm_bwd(q, gqv, dq, FOX_HEAD_DIM)
        dkr, dgk = _head_norm_bwd(k, gkv, dk, FOX_HEAD_DIM)
        return dqr, dkr, dgq, dgk

    dz_q, dz_k, dgq_t, dgk_t = _rowwise(
        fox_qk_bwd_body, [(z, z_q, FOX_WIDTH), (z, z_k, FOX_WIDTH), _whole(dqn), _whole(dkn)], [gq, gk],
        [(FOX_WIDTH, BF16), (FOX_WIDTH, BF16)], [((1, FOX_WIDTH), F32), ((1, FOX_WIDTH), F32)], "fox_qk_bwd", tm_e)
    G["fox_q_norm"] = jnp.sum(dgq_t.reshape(FOX_HEADS, FOX_HEAD_DIM), axis=0, keepdims=True)
    G["fox_k_norm"] = jnp.sum(dgk_t.reshape(FOX_HEADS, FOX_HEAD_DIM), axis=0, keepdims=True)

    do_mem = _mm(dy_mem, full["w_mem_o"], "nt", "mem_o_dx")
    P["w_mem_o"] = _blocks_from_cols(_mm(o_mem, dy_mem, "tn", "mem_o_dw"))
    dqmn, dkmn, dvm = _mem_bwd(qmn, kmn, vmb, do_mem, batch, seq, mlen, "mem_bwd")
    dz_qm, dgqm_t = _rowwise(lambda q, dq, g: _head_norm_bwd(q, g, dq, MEM_HEAD_DIM),
                             [(z, z_qm, MEM_WIDTH), _whole(dqmn)], [gqm], [(MEM_WIDTH, BF16)],
                             [((1, MEM_WIDTH), F32)], "memq_bwd", tm_e)

    def memk_bwd_body(k, dk, dv, g):
        dkr, dg = _head_norm_bwd(k, g, dk, MEM_HEAD_DIM)
        return jnp.concatenate([dkr, dv], axis=1), dg

    dkv, dgkm_t = _rowwise(memk_bwd_body, [(kv, 0, MEM_WIDTH), _whole(dkmn), _whole(dvm)], [gkm],
                           [(2 * MEM_WIDTH, BF16)], [((1, MEM_WIDTH), F32)], "memk_bwd", _tile(batch * mlen, 256, 16))
    G["mem_q_norm"] = jnp.sum(dgqm_t.reshape(MEM_HEADS, MEM_HEAD_DIM), axis=0, keepdims=True)
    G["mem_k_norm"] = jnp.sum(dgkm_t.reshape(MEM_HEADS, MEM_HEAD_DIM), axis=0, keepdims=True)
    P["w_mem_kv"] = row_blocks(_mm(memn, dkv, "tn", "mem_kv_dw"))
    dmemn = _mm(dkv, full["w_mem_kv"], "nt", "mem_kv_dx")
    _, _, G["mem_norm"] = _rms_bwd(mem2d, W["mem_norm"], dmemn, None, "mem_drms")

    dz = jnp.concatenate([dz_gate, dz_q, dz_k, dz_v, dz_u, dz_qm, dz_f,
                          jnp.zeros((t, z_width - z_f - F_PAD), BF16)], axis=1)
    d_w_in_pad, l_small = _mm(h2, dz, "tn", "mix_in_dw", comm=_scatter_first([P[n] for n in mixer_small]))
    s_small = [pair_add(n, P[n], l) for n, l in zip(mixer_small, l_small)]
    p = d_w_in_pad
    p_w_in = _blocks_from_cols(jnp.concatenate(
        [p[:, z_u:z_u + POOL_WIDTH], p[:, z_q:z_q + 3 * FOX_WIDTH], p[:, z_f:z_f + FOX_HEADS],
         p[:, z_qm:z_qm + MEM_WIDTH], p[:, z_gate:z_gate + gate_w]], axis=1))
    dh2, rest = _mm(dz, w_in_pad, "nt", "mix_in_dx", caps=(1024, 1024, 1792),
                    comm=_join(_scatter_second(s_small), _scatter_first([p_w_in])))
    for n, l in zip(mixer_small, rest[:len(mixer_small)]):
        landed[n] = l
    s_w_in = pair_add("w_in", p_w_in, rest[-1])
    dx1, dx1_b, G["mix_norm"] = _rms_bwd(x1, W["mix_norm"], dh2, dx2, "mix_drms")

    dgu1 = _swiglu_bwd_from_out(dx1_b, wd1, gu1, "ffn1_dgu")
    p_wgu1, (landed["w_in"],) = _mm(h1, dgu1, "tn", "ffn1_dwgu", caps=gu_caps_tn, out_blocks=True, halves="b",
                                    comm=_scatter_second([s_w_in]))
    d_wd1, (l_wgu1,) = _mm(a1, dx1_b, "tn", "ffn1_dwd", scale=0.5, caps=dwd_caps, comm=_scatter_first([p_wgu1]))
    p_wd1 = row_blocks(d_wd1)
    s_wgu1 = pair_add("ffn1_w_gate_up", p_wgu1, l_wgu1)
    dh1, (landed["ffn1_w_gate_up"], l_wd1) = _mm(dgu1, wgu1, "nt", "ffn1_dh", caps=gu_caps_nt, b_blocks=True, halves="a",
                                                 comm=_join(_scatter_second([s_wgu1]), _scatter_first([p_wd1])))
    s_wd1 = pair_add("ffn1_w_down", p_wd1, l_wd1)
    (dx0, _, G["ffn1_norm"]), (landed["ffn1_w_down"],) = _rms_bwd(x2d, W["ffn1_norm"], dh1, dx1, "ffn1_drms",
                                                                  comm=_scatter_second([s_wd1]))
    grad_x = dx0.reshape(batch, seq, d)

    out_g, out_d, out_m, out_v = {}, {}, {}, {}
    for n in big:
        res = _rs_finish_adam(own[n], landed[n], W[n][0], M[n][0], V[n][0], "adam_" + n)
        out_g[n], out_d[n], out_m[n], out_v[n] = [r[None] for r in res]

    small = [n for n in names if n not in big]
    g_small = _pack_small([G[n].reshape(W[n].shape) for n in small])
    all_small = _all_gather_vmem(g_small, "ag_small_grads")
    res = _allreduce_adam(all_small, _pack_small([W[n] for n in small]), _pack_small([M[n] for n in small]),
                          _pack_small([V[n] for n in small]), "adam_small")
    like = [W[n] for n in small]
    for dst, buf in zip((out_g, out_d, out_m, out_v), res):
        for n, a in zip(small, _unpack_small(buf, like)):
            dst[n] = a

    return (loss, grad_x, *[out_g[n] for n in names], *[out_d[n] for n in names],
            *[out_m[n] for n in names], *[out_v[n] for n in names])
```

```python
import functools

import jax
import jax.numpy as jnp
from jax import lax
from jax.experimental import pallas as pl
from jax.experimental.pallas import tpu as pltpu

F32 = jnp.float32
BF16 = jnp.bfloat16
MESH = pl.DeviceIdType.MESH

N_DEV = 8
EPS = 1e-6
FOX_HEADS = 16
FOX_HEAD_DIM = 64
FOX_WIDTH = FOX_HEADS * FOX_HEAD_DIM
MEM_HEADS = 4
MEM_HEAD_DIM = 128
MEM_WIDTH = MEM_HEADS * MEM_HEAD_DIM
POOL_GROUPS = 4
POOL_GROUP_DIM = 128
POOL_WIDTH = POOL_GROUPS * POOL_GROUP_DIM
POOL_WINDOWS = (2, 4, 8, 16)
LANES = 128
F_PAD = LANES

ADAM_LR = 0.001
ADAM_B1 = 0.9
ADAM_B2 = 0.999
ADAM_EPS = 1e-08
ADAM_WD = 0.01
ADAM_STEP = 10

VMEM_LIMIT = 56 * 1024 * 1024
NEG = -1e30

ANY = pl.BlockSpec(memory_space=pl.ANY)


def _params(sem=None):
    return pltpu.CompilerParams(dimension_semantics=sem, vmem_limit_bytes=VMEM_LIMIT)


MXU_DIM = 256


def _tile(dim, cap, align):
    best = None
    t = align
    while t <= min(dim, cap):
        if dim % t == 0:
            best = t
        t += align
    return dim if best is None else best


def _mxu_tile(dim, cap):
    wide, fine = _tile(dim, cap, MXU_DIM), _tile(dim, cap, LANES)
    whole_widths = wide % MXU_DIM == 0 and wide <= cap
    return wide if whole_widths and fine < 2 * wide else fine


class _Sems:
    def __init__(self, send, recv, local):
        self.send, self.recv, self.local = send, recv, local
        self.n_remote = self.n_local = 0

    def remote(self):
        i = self.n_remote
        self.n_remote += 1
        return self.send.at[i], self.recv.at[i]

    def one_local(self):
        i = self.n_local
        self.n_local += 1
        return self.local.at[i]


class _Comm:
    def __init__(self, srcs, outs, n_remote, n_local, plan, in_place=False):
        self.srcs, self.outs, self.n_remote, self.n_local = list(srcs), list(outs), n_remote, n_local
        self.plan, self.in_place = plan, in_place

    def aliases(self, n_in, n_out):
        return {n_in + i: n_out + i for i in range(len(self.srcs))} if self.in_place else {}

    def scratch(self):
        return [pltpu.SemaphoreType.DMA((self.n_remote,)), pltpu.SemaphoreType.DMA((self.n_remote,)),
                pltpu.SemaphoreType.DMA((max(self.n_local, 1),))]

    def run(self, src_refs, out_refs, sem_refs, first, last):
        def copies():
            return self.plan(list(src_refs), list(out_refs), _Sems(*sem_refs))

        return (lambda: _when(first, lambda: [cp.start() for cp in copies()]),
                lambda: _when(last, lambda: [cp.wait() for cp in copies()]))


def _when(cond, fn):
    @pl.when(cond)
    def _():
        fn()


def _join(*comms):
    comms = [c for c in comms if c is not None]
    assert all(not c.in_place for c in comms)

    def plan(src_refs, out_refs, sems):
        copies, si, oi = [], 0, 0
        for c in comms:
            copies += c.plan(src_refs[si:si + len(c.srcs)], out_refs[oi:oi + len(c.outs)], sems)
            si += len(c.srcs)
            oi += len(c.outs)
        return copies

    return _Comm(sum((c.srcs for c in comms), []), sum((c.outs for c in comms), []),
                 sum(c.n_remote for c in comms), sum(c.n_local for c in comms), plan)


def _split_refs(refs, n_in, n_out, comm):
    if comm is None:
        return refs[:n_in], refs[n_in:n_in + n_out], refs[n_in + n_out:], (), (), ()
    ns, no = len(comm.srcs), len(comm.outs)
    ins = refs[:n_in]
    srcs = refs[n_in:n_in + ns]
    outs = refs[n_in + ns:n_in + ns + n_out]
    couts = refs[n_in + ns + n_out:n_in + ns + n_out + no]
    rest = refs[n_in + ns + n_out + no:]
    return ins, outs, rest[:-3], srcs, couts, rest[-3:]


def _sigmoid(x):
    return 1.0 / (1.0 + jnp.exp(-x))


_DIMS = {"nn": (((1,), (0,)), ((), ())), "nt": (((1,), (1,)), ((), ())), "tn": (((0,), (0,)), ((), ()))}


def _mm(a, b, mode, name, out_dtype=F32, scale=1.0, res=None, caps=None, b_blocks=False, out_blocks=False,
        halves=None, comm=None):
    nblk = N_DEV
    a_shape, b_shape = a.shape, b.shape
    if b_blocks:
        _, r0, c0 = b.shape
        b_shape = (r0, nblk * c0)
    if halves == "a":
        assert mode == "nt"
        a_shape = (a.shape[1], 2 * a.shape[2])
    elif halves == "b":
        assert mode == "tn"
        b_shape = (b.shape[1], 2 * b.shape[2])
    if mode == "nn":
        (m, k), (k2, n) = a_shape, b_shape
    elif mode == "nt":
        (m, k), (n, k2) = a_shape, b_shape
    else:
        (k, m), (k2, n) = a_shape, b_shape
    assert k == k2, (name, a.shape, b.shape)
    cm, cn, ck = caps or ((1024, 512, 2048) if k <= 2048 else (1024, 1024, 2048))
    n_unit = n // nblk if (out_blocks or (b_blocks and mode == "nn")) else n
    k_unit = k // nblk if (b_blocks and mode == "nt") else k
    tm, tn, tk = _mxu_tile(m, cm), _mxu_tile(n_unit, cn), _mxu_tile(k_unit, ck)
    nk = k // tk
    npb, kpb = n_unit // tn, k_unit // tk
    n_half, k_half = (n // 2) // tn, (k // 2) // tk
    if mode == "nn":
        a_spec = pl.BlockSpec((tm, tk), lambda i, j, kk: (i, kk))
        b_spec = pl.BlockSpec((tk, tn), lambda i, j, kk: (kk, j))
        if b_blocks:
            b_spec = pl.BlockSpec((None, tk, tn), lambda i, j, kk: (j // npb, kk, j % npb))
    elif mode == "nt":
        a_spec = pl.BlockSpec((tm, tk), lambda i, j, kk: (i, kk))
        b_spec = pl.BlockSpec((tn, tk), lambda i, j, kk: (j, kk))
        if b_blocks:
            b_spec = pl.BlockSpec((None, tn, tk), lambda i, j, kk: (kk // kpb, j, kk % kpb))
        if halves == "a":
            assert (k // 2) % tk == 0
            a_spec = pl.BlockSpec((None, tm, tk), lambda i, j, kk: (kk // k_half, i, kk % k_half))
    else:
        assert not b_blocks
        a_spec = pl.BlockSpec((tk, tm), lambda i, j, kk: (kk, i))
        b_spec = pl.BlockSpec((tk, tn), lambda i, j, kk: (kk, j))
        if halves == "b":
            assert (n // 2) % tn == 0
            b_spec = pl.BlockSpec((None, tk, tn), lambda i, j, kk: (j // n_half, kk, j % n_half))
    if out_blocks:
        assert res is None
        o_spec = pl.BlockSpec((None, tm, tn), lambda i, j, kk: (j // npb, i, j % npb))
        o_shape = jax.ShapeDtypeStruct((nblk, m, n // nblk), out_dtype)
    else:
        o_spec = pl.BlockSpec((tm, tn), lambda i, j, kk: (i, j))
        o_shape = jax.ShapeDtypeStruct((m, n), out_dtype)
    in_specs = [a_spec, b_spec] + ([o_spec] if res is not None else [])
    n_in = len(in_specs)
    dims = _DIMS[mode]
    gm, gn = m // tm, n // tn

    def kern(*refs):
        ins, outs, scratch, c_src, c_out, c_sem = _split_refs(refs, n_in, 1, comm)
        a_ref, b_ref = ins[0], ins[1]
        res_ref = ins[2] if res is not None else None
        o_ref = outs[0]
        if comm is not None:
            i, j, kq = pl.program_id(0), pl.program_id(1), pl.program_id(2)
            first = jnp.logical_and(jnp.logical_and(i == 0, j == 0), kq == 0)
            last = jnp.logical_and(jnp.logical_and(i == gm - 1, j == gn - 1), kq == nk - 1)
            start_comm, wait_comm = comm.run(c_src, c_out, c_sem, first, last)
            start_comm()
        a_tile = a_ref[...].astype(BF16)
        if scale != 1.0:
            a_tile = a_tile * scale

        def product():
            return lax.dot_general(a_tile, b_ref[...].astype(BF16), dims, preferred_element_type=F32)

        if nk == 1:
            r = product()
            if res_ref is not None:
                r = res_ref[...] + r
            o_ref[...] = r.astype(out_dtype)
        else:
            acc_ref = scratch[0] if scratch else o_ref
            kk = pl.program_id(2)

            @pl.when(kk == 0)
            def _():
                acc_ref[...] = jnp.zeros_like(acc_ref) if res_ref is None else res_ref[...]

            acc_ref[...] += product()
            if scratch:
                @pl.when(kk == nk - 1)
                def _():
                    o_ref[...] = acc_ref[...].astype(out_dtype)
        if comm is not None:
            wait_comm()

    assert scale in (1.0, 0.5)
    args = (a, b) + ((res,) if res is not None else ())
    scratch_shapes = [pltpu.VMEM((tm, tn), F32)] if (nk > 1 and out_dtype != F32) else []
    if comm is None:
        return pl.pallas_call(
            kern, name=name, grid=(gm, gn, nk), in_specs=in_specs, out_specs=o_spec, out_shape=o_shape,
            scratch_shapes=scratch_shapes, compiler_params=_params(("parallel", "parallel", "arbitrary")),
        )(*args)
    res_all = pl.pallas_call(
        kern, name=name, grid=(gm, gn, nk), in_specs=in_specs + [ANY] * len(comm.srcs),
        out_specs=[o_spec] + [ANY] * len(comm.outs), out_shape=[o_shape] + comm.outs,
        scratch_shapes=scratch_shapes + comm.scratch(), input_output_aliases=comm.aliases(n_in, 1),
        compiler_params=_params(("arbitrary", "arbitrary", "arbitrary")),
    )(*args, *comm.srcs)
    return res_all[0], list(res_all[1:])


def _gate_up_swiglu(h, w_blocks, name, comm=None):
    t, d = h.shape
    nblk, _, nb = w_blocks.shape
    half = nblk // 2
    tm = _mxu_tile(t, 512)
    steps = t // tm

    def kern(*refs):
        (h_ref, wg_ref, wu_ref), (gu_ref, a_ref), _, c_src, c_out, c_sem = _split_refs(refs, 3, 2, comm)
        if comm is not None:
            j, i = pl.program_id(0), pl.program_id(1)
            start_comm, wait_comm = comm.run(c_src, c_out, c_sem, jnp.logical_and(j == 0, i == 0),
                                             jnp.logical_and(j == half - 1, i == steps - 1))
            start_comm()
        hv = h_ref[...]
        g = lax.dot_general(hv, wg_ref[...], _DIMS["nn"], preferred_element_type=F32)
        u = lax.dot_general(hv, wu_ref[...], _DIMS["nn"], preferred_element_type=F32)
        gu_ref[0] = g
        gu_ref[1] = u
        a_ref[...] = ((g * _sigmoid(g)) * u).astype(BF16)
        if comm is not None:
            wait_comm()

    in_specs = [pl.BlockSpec((tm, d), lambda j, i: (i, 0)), pl.BlockSpec((None, d, nb), lambda j, i: (j, 0, 0)),
                pl.BlockSpec((None, d, nb), lambda j, i: (j + half, 0, 0))]
    out_specs = [pl.BlockSpec((2, tm, nb), lambda j, i: (0, i, j)), pl.BlockSpec((tm, nb), lambda j, i: (i, j))]
    out_shape = [jax.ShapeDtypeStruct((2, t, half * nb), F32), jax.ShapeDtypeStruct((t, half * nb), BF16)]
    if comm is None:
        return pl.pallas_call(kern, name=name, grid=(half, steps), in_specs=in_specs, out_specs=out_specs,
                              out_shape=out_shape, compiler_params=_params(("parallel", "parallel")))(h, w_blocks, w_blocks)
    res_all = pl.pallas_call(
        kern, name=name, grid=(half, steps), in_specs=in_specs + [ANY] * len(comm.srcs),
        out_specs=out_specs + [ANY] * len(comm.outs), out_shape=out_shape + comm.outs,
        scratch_shapes=comm.scratch(), compiler_params=_params(("arbitrary", "arbitrary")),
    )(h, w_blocks, w_blocks, *comm.srcs)
    return list(res_all[:2]), list(res_all[2:])


def _swiglu_bwd_from_out(dy, wd, gu, name, comm=None):
    t, d = dy.shape
    f = wd.shape[0]
    tm, tn = _mxu_tile(t, 1024), _mxu_tile(f, 512)
    gi, gj = t // tm, f // tn

    def kern(*refs):
        (dy_ref, wd_ref, gu_ref), (out_ref,), _, c_src, c_out, c_sem = _split_refs(refs, 3, 1, comm)
        if comm is not None:
            i, j = pl.program_id(0), pl.program_id(1)
            start_comm, wait_comm = comm.run(c_src, c_out, c_sem, jnp.logical_and(i == 0, j == 0),
                                             jnp.logical_and(i == gi - 1, j == gj - 1))
            start_comm()
        da = lax.dot_general(dy_ref[...] * 0.5, wd_ref[...], _DIMS["nt"], preferred_element_type=F32)
        g, u = gu_ref[0], gu_ref[1]
        s = _sigmoid(g)
        out_ref[0] = (da * u * (s * (1.0 + g * (1.0 - s)))).astype(BF16)
        out_ref[1] = (da * (g * s)).astype(BF16)
        if comm is not None:
            wait_comm()

    pair = pl.BlockSpec((2, tm, tn), lambda i, j: (0, i, j))
    in_specs = [pl.BlockSpec((tm, d), lambda i, j: (i, 0)), pl.BlockSpec((tn, d), lambda i, j: (j, 0)), pair]
    out_shape = jax.ShapeDtypeStruct((2, t, f), BF16)
    if comm is None:
        return pl.pallas_call(kern, name=name, grid=(gi, gj), in_specs=in_specs, out_specs=pair, out_shape=out_shape,
                              compiler_params=_params(("parallel", "parallel")))(dy, wd, gu)
    res_all = pl.pallas_call(
        kern, name=name, grid=(gi, gj), in_specs=in_specs + [ANY] * len(comm.srcs),
        out_specs=[pair] + [ANY] * len(comm.outs), out_shape=[out_shape] + comm.outs,
        scratch_shapes=comm.scratch(), compiler_params=_params(("arbitrary", "arbitrary")),
    )(dy, wd, gu, *comm.srcs)
    return res_all[0], list(res_all[1:])


def _rowwise(body, ins, params, outs, accs, name, tm, comm=None, into=None):
    rows = ins[0][0].shape[0]
    assert rows % tm == 0, (name, rows, tm)
    in_specs = []
    for arr, off, width in ins:
        assert off % width == 0 and arr.shape[0] == rows, (name, arr.shape, off, width)
        in_specs.append(pl.BlockSpec((tm, width), functools.partial(lambda i, c: (i, c), c=off // width)))
    for p in params:
        in_specs.append(pl.BlockSpec(p.shape, functools.partial(lambda i, nd: (0,) * nd, nd=p.ndim)))
    out_specs = [pl.BlockSpec((tm, w), lambda i: (i, 0)) for w, _ in outs]
    out_specs += [pl.BlockSpec(s, functools.partial(lambda i, nd: (0,) * nd, nd=len(s))) for s, _ in accs]
    out_shape = [jax.ShapeDtypeStruct((rows, w), d) for w, d in outs]
    out_shape += [jax.ShapeDtypeStruct(s, d) for s, d in accs]
    n_in, n_par, n_out = len(ins), len(params), len(outs)
    carried = []
    if into is not None:
        buf, total, col = into
        assert comm is None and col % outs[0][0] == 0, (name, col, outs[0])
        out_specs[0] = pl.BlockSpec((tm, outs[0][0]), functools.partial(lambda i, c: (i, c), c=col // outs[0][0]))
        out_shape[0] = jax.ShapeDtypeStruct((rows, total), outs[0][1])
        carried = [] if buf is None else [buf]

    steps = rows // tm

    def kern(*refs):
        in_refs, out_refs, _, c_src, c_out, c_sem = _split_refs(refs, n_in + n_par + len(carried),
                                                                 n_out + len(accs), comm)
        in_refs = in_refs[:n_in + n_par]
        first = pl.program_id(0) == 0
        if comm is not None:
            start_comm, wait_comm = comm.run(c_src, c_out, c_sem, first, pl.program_id(0) == steps - 1)
            start_comm()
        res = body(*[r[...] for r in in_refs])
        for r, v in zip(out_refs[:n_out], res[:n_out]):
            r[...] = v.astype(r.dtype)
        for r, v in zip(out_refs[n_out:], res[n_out:]):
            @pl.when(first)
            def _(r=r, v=v):
                r[...] = v.astype(r.dtype)

            @pl.when(jnp.logical_not(first))
            def _(r=r, v=v):
                r[...] += v.astype(r.dtype)
        if comm is not None:
            wait_comm()

    args = [a for a, _, _ in ins] + list(params)
    if comm is None:
        return pl.pallas_call(
            kern, name=name, grid=(steps,), in_specs=in_specs + [ANY] * len(carried), out_specs=out_specs,
            out_shape=out_shape, input_output_aliases={len(args): 0} if carried else {},
            compiler_params=_params(("arbitrary",) if accs else ("parallel",)),
        )(*args, *carried)
    res_all = pl.pallas_call(
        kern, name=name, grid=(steps,), in_specs=in_specs + [ANY] * len(comm.srcs),
        out_specs=out_specs + [ANY] * len(comm.outs), out_shape=out_shape + comm.outs,
        scratch_shapes=comm.scratch(), input_output_aliases=comm.aliases(len(args), len(out_shape)),
        compiler_params=_params(("arbitrary",)),
    )(*args, *comm.srcs)
    return list(res_all[:len(out_shape)]), list(res_all[len(out_shape):])


def _whole(x):
    return (x, 0, x.shape[1])


def _first(res, comm):
    return res[0] if comm is None else (res[0][0], res[1])


def _rms_fwd(x, gain, name, comm=None):
    def body(xv, g):
        r = lax.rsqrt(jnp.mean(xv * xv, axis=-1, keepdims=True) + EPS)
        return ((xv * r) * g,)

    return _first(_rowwise(body, [_whole(x)], [gain], [(x.shape[1], BF16)], [], name, _tile(x.shape[0], 256, 16),
                           comm=comm), comm)


def _rms_bwd(x, gain, dh, dres, name, comm=None):
    d = x.shape[1]

    def body(*vals):
        if dres is None:
            xv, dhv, g = vals
        else:
            xv, dhv, drv, g = vals
        r = lax.rsqrt(jnp.mean(xv * xv, axis=-1, keepdims=True) + EPS)
        xh = xv * r
        w = dhv * g
        dx = r * (w - xh * jnp.mean(w * xh, axis=-1, keepdims=True))
        if dres is not None:
            dx = drv + dx
        return dx, dx, jnp.sum(dhv * xh, axis=0, keepdims=True)

    ins = [_whole(x), _whole(dh)] + ([_whole(dres)] if dres is not None else [])
    return _rowwise(body, ins, [gain], [(d, F32), (d, BF16)], [((1, d), F32)], name, _tile(x.shape[0], 256, 16),
                    comm=comm)


def _loss_head(y, target, name):
    d = y.shape[1]

    def body(yv, tv):
        e = yv - tv
        part = jnp.sum(jnp.sum(e * e, axis=1, keepdims=True), axis=0, keepdims=True)
        return e / d, e / d, jnp.broadcast_to((0.5 / d) * part, (1, LANES))

    return _rowwise(body, [_whole(y), _whole(target)], [], [(d, F32), (d, BF16)], [((1, LANES), F32)], name,
                    _tile(y.shape[0], 256, 16))


def _head_mean(v, head_dim):
    cols = []
    lane = lax.broadcasted_iota(jnp.int32, (1, LANES), 1)
    for j in range(v.shape[1] // LANES):
        blk = v[:, j * LANES:(j + 1) * LANES]
        if head_dim == LANES:
            m = jnp.sum(blk, axis=-1, keepdims=True)
            cols.append(jnp.broadcast_to(m, blk.shape))
        else:
            lo = jnp.sum(jnp.where(lane < head_dim, blk, 0.0), axis=-1, keepdims=True)
            hi = jnp.sum(jnp.where(lane >= head_dim, blk, 0.0), axis=-1, keepdims=True)
            cols.append(jnp.where(lane < head_dim, lo, hi))
    return jnp.concatenate(cols, axis=1) / head_dim


def _head_norm(xv, g, head_dim):
    r = lax.rsqrt(_head_mean(xv * xv, head_dim) + EPS)
    return (xv * r) * g


def _head_norm_bwd(xv, g, dy, head_dim):
    r = lax.rsqrt(_head_mean(xv * xv, head_dim) + EPS)
    xh = xv * r
    w = dy * g
    dx = r * (w - xh * _head_mean(w * xh, head_dim))
    return dx, jnp.sum(dy * xh, axis=0, keepdims=True)


def _log_sigmoid(x):
    return jnp.minimum(x, 0.0) - jnp.log(1.0 + jnp.exp(-jnp.abs(x)))


def _shift_rows(v, sh, down):
    n = v.shape[0]
    row = lax.broadcasted_iota(jnp.int32, (n, 1), 0)
    if down:
        return jnp.where(row >= sh, pltpu.roll(v, sh, 0), 0.0)
    return jnp.where(row < n - sh, pltpu.roll(v, n - sh, 0), 0.0)


def _scan_rows(v, down):
    sh = 1
    while sh < v.shape[0]:
        v = v + _shift_rows(v, sh, down)
        sh *= 2
    return v


def _window_sum(v, steps, down):
    for s in range(steps):
        v = v + _shift_rows(v, 2 ** s, down)
    return v


def _fox_fwd(qn, kn, vb, cks, batch, seq, name, comm=None):
    t, width = qn.shape
    pairs = width // LANES
    tq = _tile(seq, 512, LANES)
    nq = seq // tq

    def kern(*refs):
        (q_ref, k_ref, v_ref, c_ref), (o_ref, lse_ref), _, c_src, c_out, c_sem = _split_refs(refs, 4, 2, comm)
        qi = pl.program_id(2)
        if comm is not None:
            b_id, p_id = pl.program_id(0), pl.program_id(1)
            first = jnp.logical_and(jnp.logical_and(b_id == 0, p_id == 0), qi == 0)
            last = jnp.logical_and(jnp.logical_and(b_id == batch - 1, p_id == pairs - 1), qi == nq - 1)
            start_comm, wait_comm = comm.run(c_src, c_out, c_sem, first, last)
            start_comm()
        lane = lax.broadcasted_iota(jnp.int32, (1, LANES), 1)
        rowi = lax.broadcasted_iota(jnp.int32, (tq, tq), 0)
        coli = lax.broadcasted_iota(jnp.int32, (tq, tq), 1)
        q_all = q_ref[...]
        o_heads, lse_heads = [], []
        for h in range(2):
            hm = (lane < FOX_HEAD_DIM) if h == 0 else (lane >= FOX_HEAD_DIM)
            q = jnp.where(hm, q_all, jnp.zeros_like(q_all))

            def step(j, carry, h=h, q=q, diagonal=False):
                m, l, acc = carry
                start = pl.multiple_of(j * tq, tq)
                k = k_ref[pl.ds(start, tq), :]
                v = v_ref[pl.ds(start, tq), :]
                s = lax.dot_general(q, k, _DIMS["nt"], preferred_element_type=F32)
                s = s - c_ref[0, h:h + 1, pl.ds(start, tq)]
                if diagonal:
                    s = jnp.where(rowi >= coli, s, NEG)
                m_new = jnp.maximum(m, jnp.max(s, axis=-1, keepdims=True))
                alpha = jnp.exp(m - m_new)
                p = jnp.exp(s - m_new)
                l = alpha * l + jnp.sum(p, axis=-1, keepdims=True)
                acc = alpha * acc + lax.dot_general(p.astype(BF16), v, _DIMS["nn"], preferred_element_type=F32)
                return m_new, l, acc

            init = (jnp.full((tq, 1), NEG, F32), jnp.zeros((tq, 1), F32), jnp.zeros((tq, LANES), F32))
            m, l, acc = step(qi, lax.fori_loop(0, qi, step, init), diagonal=True)
            o_heads.append(acc / l)
            lse_heads.append(jnp.broadcast_to(m + jnp.log(l), (tq, LANES)))
        o_ref[...] = jnp.where(lane < FOX_HEAD_DIM, o_heads[0], o_heads[1])
        lse_ref[...] = jnp.where(lane < FOX_HEAD_DIM, lse_heads[0], lse_heads[1])
        if comm is not None:
            wait_comm()

    q_spec = pl.BlockSpec((tq, LANES), lambda b, hp, qi: (b * nq + qi, hp))
    kv_spec = pl.BlockSpec((seq, LANES), lambda b, hp, qi: (b, hp))
    c_spec = pl.BlockSpec((1, 2, seq), lambda b, hp, qi: (b * pairs + hp, 0, 0))
    in_specs = [q_spec, kv_spec, kv_spec, c_spec]
    out_shape = [jax.ShapeDtypeStruct((t, width), F32), jax.ShapeDtypeStruct((t, width), F32)]
    if comm is None:
        return pl.pallas_call(
            kern, name=name, grid=(batch, pairs, nq), in_specs=in_specs, out_specs=[q_spec, q_spec],
            out_shape=out_shape, compiler_params=_params(("parallel", "parallel", "arbitrary")),
        )(qn, kn, vb, cks)
    res_all = pl.pallas_call(
        kern, name=name, grid=(batch, pairs, nq), in_specs=in_specs + [ANY] * len(comm.srcs),
        out_specs=[q_spec, q_spec] + [ANY] * len(comm.outs), out_shape=out_shape + comm.outs,
        scratch_shapes=comm.scratch(), compiler_params=_params(("arbitrary", "arbitrary", "arbitrary")),
    )(qn, kn, vb, cks, *comm.srcs)
    return list(res_all[:2]), list(res_all[2:])


def _fox_bwd(qn, kn, vb, cks, o, lse, do, dz, dv_col, batch, seq, name, comm):
    t, width = qn.shape
    pairs = width // LANES
    tk = _tile(seq, 512, LANES)
    nk = seq // tk
    scale = FOX_HEAD_DIM ** -0.5

    def kern(*refs):
        ins, outs, _, c_src, c_out, c_sem = _split_refs(refs, 8, 5, comm)
        k_ref, v_ref, q_ref, c_ref, o_ref, lse_ref, do_ref, _ = ins
        dk_ref, dv_ref, dq_ref, dc_ref, dcq_ref = outs
        kj = pl.program_id(2)
        if comm is not None:
            b_id, p_id = pl.program_id(0), pl.program_id(1)
            first = jnp.logical_and(jnp.logical_and(b_id == 0, p_id == 0), kj == 0)
            last = jnp.logical_and(jnp.logical_and(b_id == batch - 1, p_id == pairs - 1), kj == nk - 1)
            start_comm, wait_comm = comm.run(c_src, c_out, c_sem, first, last)
            start_comm()
        lane = lax.broadcasted_iota(jnp.int32, (1, LANES), 1)
        rowi = lax.broadcasted_iota(jnp.int32, (tk, tk), 0)
        coli = lax.broadcasted_iota(jnp.int32, (tk, tk), 1)

        @pl.when(kj == 0)
        def _():
            dq_ref[...] = jnp.zeros_like(dq_ref)
            dcq_ref[...] = jnp.zeros_like(dcq_ref)

        k_all = k_ref[...]
        v_all = v_ref[...]
        kstart = pl.multiple_of(kj * tk, tk)
        dk_heads, dv_heads = [], []
        for h in range(2):
            hm = (lane < FOX_HEAD_DIM) if h == 0 else (lane >= FOX_HEAD_DIM)
            kh = jnp.where(hm, k_all, jnp.zeros_like(k_all))
            vh = jnp.where(hm, v_all, jnp.zeros_like(v_all))
            c_row = c_ref[0, h:h + 1, pl.ds(kstart, tk)]

            def step(qi, carry, h=h, hm=hm, kh=kh, vh=vh, c_row=c_row, diagonal=False):
                dk_acc, dv_acc, dc_acc = carry
                start = pl.multiple_of(qi * tk, tk)
                q = q_ref[pl.ds(start, tk), :]
                dov = do_ref[pl.ds(start, tk), :]
                ov = o_ref[pl.ds(start, tk), :]
                lse_col = jnp.max(jnp.where(hm, lse_ref[pl.ds(start, tk), :], NEG), axis=-1, keepdims=True)
                dob = jnp.where(hm, dov, 0.0).astype(BF16)
                dcol = jnp.sum(dob.astype(F32) * ov, axis=-1, keepdims=True)
                s = lax.dot_general(q, kh, _DIMS["nt"], preferred_element_type=F32) - c_row
                p = jnp.exp(s - lse_col)
                if diagonal:
                    p = jnp.where(rowi >= coli, p, 0.0)
                dp = lax.dot_general(dob, vh, _DIMS["nt"], preferred_element_type=F32)
                ds = p * (dp - dcol)
                ds_b = ds.astype(BF16)
                dv_acc = dv_acc + lax.dot_general(p.astype(BF16), dob, _DIMS["tn"], preferred_element_type=F32)
                dk_acc = dk_acc + lax.dot_general(ds_b, q, _DIMS["tn"], preferred_element_type=F32)
                dq_part = lax.dot_general(ds_b * scale, kh, _DIMS["nn"], preferred_element_type=F32)
                dq_ref[pl.ds(start, tk), :] += dq_part
                dcq_ref[pl.ds(start, tk), :] += jnp.where(hm, jnp.sum(ds, axis=-1, keepdims=True), 0.0)
                dc_acc = dc_acc - jnp.sum(ds, axis=0, keepdims=True)
                return dk_acc, dv_acc, dc_acc

            init = (jnp.zeros((tk, LANES), F32), jnp.zeros((tk, LANES), F32), jnp.zeros((1, tk), F32))
            dk_acc, dv_acc, dc_acc = lax.fori_loop(kj + 1, nk, step, step(kj, init, diagonal=True))
            dk_heads.append(dk_acc)
            dv_heads.append(dv_acc)
            dc_ref[0, h:h + 1, pl.ds(kstart, tk)] = dc_acc
        dk_ref[...] = jnp.where(lane < FOX_HEAD_DIM, dk_heads[0], dk_heads[1])
        dv_ref[...] = jnp.where(lane < FOX_HEAD_DIM, dv_heads[0], dv_heads[1]).astype(BF16)
        if comm is not None:
            wait_comm()

    kv_spec = pl.BlockSpec((tk, LANES), lambda b, hp, kj: (b * nk + kj, hp))
    full_spec = pl.BlockSpec((seq, LANES), lambda b, hp, kj: (b, hp))
    c_spec = pl.BlockSpec((1, 2, seq), lambda b, hp, kj: (b * pairs + hp, 0, 0))
    dv_spec = pl.BlockSpec((tk, LANES), lambda b, hp, kj: (b * nk + kj, hp + dv_col // LANES))
    in_specs = [kv_spec, kv_spec, full_spec, c_spec, full_spec, full_spec, full_spec, ANY]
    out_specs = [kv_spec, dv_spec, full_spec, c_spec, full_spec]
    out_shape = [jax.ShapeDtypeStruct((t, width), F32), jax.ShapeDtypeStruct(dz.shape, BF16),
                 jax.ShapeDtypeStruct((t, width), F32), jax.ShapeDtypeStruct(cks.shape, F32),
                 jax.ShapeDtypeStruct((t, width), F32)]
    res_all = pl.pallas_call(
        kern, name=name, grid=(batch, pairs, nk), in_specs=in_specs + [ANY] * len(comm.srcs),
        out_specs=out_specs + [ANY] * len(comm.outs), out_shape=out_shape + comm.outs,
        scratch_shapes=comm.scratch(), input_output_aliases={7: 1},
        compiler_params=_params(("arbitrary", "arbitrary", "arbitrary")),
    )(kn, vb, qn, cks, o, lse, do, dz, *comm.srcs)
    return list(res_all[:5]), list(res_all[5:])


def _mem_fwd(qn, kn, vb, batch, seq, mlen, name):
    t, width = qn.shape
    heads = width // LANES
    tq = _tile(seq, 512, LANES)
    nq = seq // tq
    scale = MEM_HEAD_DIM ** -0.5

    def kern(q_ref, k_ref, v_ref, o_ref):
        s = lax.dot_general(q_ref[...], k_ref[...], _DIMS["nt"], preferred_element_type=F32) * scale
        e = jnp.exp(s - jnp.max(s, axis=-1, keepdims=True))
        p = e / jnp.sum(e, axis=-1, keepdims=True)
        o_ref[...] = lax.dot_general(p.astype(BF16), v_ref[...], _DIMS["nn"], preferred_element_type=F32)

    q_spec = pl.BlockSpec((tq, LANES), lambda b, h, qi: (b * nq + qi, h))
    kv_spec = pl.BlockSpec((mlen, LANES), lambda b, h, qi: (b, h))
    return pl.pallas_call(
        kern, name=name, grid=(batch, heads, nq), in_specs=[q_spec, kv_spec, kv_spec], out_specs=q_spec,
        out_shape=jax.ShapeDtypeStruct((t, width), F32),
        compiler_params=_params(("parallel", "parallel", "parallel")),
    )(qn, kn, vb)


def _mem_bwd(qn, kn, vb, do, batch, seq, mlen, name):
    t, width = qn.shape
    heads = width // LANES
    tq = _tile(seq, 512, LANES)
    nq = seq // tq
    scale = MEM_HEAD_DIM ** -0.5

    def kern(q_ref, k_ref, v_ref, do_ref, dq_ref, dk_ref, dv_ref):
        qi = pl.program_id(2)
        q, k, v = q_ref[...], k_ref[...], v_ref[...]
        dob = do_ref[...].astype(BF16)
        s = lax.dot_general(q, k, _DIMS["nt"], preferred_element_type=F32) * scale
        e = jnp.exp(s - jnp.max(s, axis=-1, keepdims=True))
        p = e / jnp.sum(e, axis=-1, keepdims=True)
        dp = lax.dot_general(dob, v, _DIMS["nt"], preferred_element_type=F32)
        ds = p * (dp - jnp.sum(p * dp, axis=-1, keepdims=True))
        dsb = (ds * scale).astype(BF16)
        dq_ref[...] = lax.dot_general(dsb, k, _DIMS["nn"], preferred_element_type=F32)
        dk = lax.dot_general(dsb, q, _DIMS["tn"], preferred_element_type=F32)
        dv = lax.dot_general(p.astype(BF16), dob, _DIMS["tn"], preferred_element_type=F32)

        @pl.when(qi == 0)
        def _():
            dk_ref[...] = dk
            dv_ref[...] = dv

        @pl.when(qi > 0)
        def _():
            dk_ref[...] += dk
            dv_ref[...] += dv

    q_spec = pl.BlockSpec((tq, LANES), lambda b, h, qi: (b * nq + qi, h))
    kv_spec = pl.BlockSpec((mlen, LANES), lambda b, h, qi: (b, h))
    return pl.pallas_call(
        kern, name=name, grid=(batch, heads, nq), in_specs=[q_spec, kv_spec, kv_spec, q_spec],
        out_specs=[q_spec, kv_spec, kv_spec],
        out_shape=[jax.ShapeDtypeStruct((t, width), F32), jax.ShapeDtypeStruct(kn.shape, F32),
                   jax.ShapeDtypeStruct(kn.shape, F32)],
        compiler_params=_params(("parallel", "parallel", "arbitrary")),
    )(qn, kn, vb, do)


def _position():
    return lax.axis_index("x"), lax.axis_index("y"), lax.axis_index("c")


def _other_chips(x, y):
    return [(1 - x, y), (x, 1 - y), (1 - x, 1 - y)]


def _remote(src, dst, sems, to):
    send, recv = sems.remote()
    return pltpu.make_async_remote_copy(src_ref=src, dst_ref=dst, send_sem=send, recv_sem=recv,
                                        device_id=to, device_id_type=MESH)


def _gather_first(shards):
    n = len(shards)

    def plan(src, out, sems):
        x, y, c = _position()
        me = 4 * x + 2 * y + c
        peers = [(x, y, 1 - c)] + [(px, py, c) for px, py in _other_chips(x, y)]
        copies = []
        for i in range(n):
            copies.append(pltpu.make_async_copy(src[i], out[i].at[me], sems.one_local()))
            copies += [_remote(src[i], out[i].at[me], sems, to) for to in peers]
        return copies

    outs = [jax.ShapeDtypeStruct((N_DEV,) + s.shape, s.dtype) for s in shards]
    return _Comm(shards, outs, 4 * n, n, plan)


def _gather_second(bufs):
    n = len(bufs)

    def plan(src, out, sems):
        x, y, c = _position()
        copies = []
        for i in range(n):
            for px, py in _other_chips(x, y):
                blk = out[i].at[4 * px + 2 * py + c]
                copies.append(_remote(blk, blk, sems, (x, y, 1 - c)))
        return copies

    outs = [jax.ShapeDtypeStruct(b.shape, b.dtype) for b in bufs]
    return _Comm(bufs, outs, 3 * n, 0, plan, in_place=True)


def _scatter_first(partials):
    n = len(partials)

    def plan(src, out, sems):
        x, y, c = _position()
        return [_remote(src[i].at[2 * q + (1 - c)], out[i].at[q], sems, (x, y, 1 - c))
                for i in range(n) for q in range(4)]

    outs = [jax.ShapeDtypeStruct((4,) + g.shape[1:], g.dtype) for g in partials]
    return _Comm(partials, outs, 4 * n, 0, plan)


def _scatter_second(halves):
    n = len(halves)

    def plan(src, out, sems):
        x, y, c = _position()
        return [_remote(src[i].at[2 * px + py], out[i].at[r], sems, (px, py, c))
                for i in range(n) for r, (px, py) in enumerate(_other_chips(x, y))]

    outs = [jax.ShapeDtypeStruct((3,) + h.shape[1:], h.dtype) for h in halves]
    return _Comm(halves, outs, 3 * n, 0, plan)


def _comm_call(comm, name):
    ns, no = len(comm.srcs), len(comm.outs)

    def body(*refs):
        copies = comm.plan(list(refs[:ns]), list(refs[ns:ns + no]), _Sems(*refs[ns + no:]))
        for cp in copies:
            cp.start()
        for cp in copies:
            cp.wait()

    return pl.pallas_call(
        body, name=name, in_specs=[ANY] * ns, out_specs=[ANY] * no, out_shape=comm.outs,
        scratch_shapes=comm.scratch(), input_output_aliases=comm.aliases(0, 0),
    )(*comm.srcs)


def _gather_first_relayed(shard, name):
    def body(x_ref, out_ref, send, recv, local):
        sems = _Sems(send, recv, local)
        x, y, c = _position()

        def slot(px, py):
            return out_ref.at[4 * px + 2 * py + c]

        mine = pltpu.make_async_copy(x_ref, slot(x, y), sems.one_local())
        to_sibling = _remote(x_ref, slot(x, y), sems, (x, y, 1 - c))
        to_x = _remote(x_ref, slot(x, y), sems, (1 - x, y, c))
        to_y = _remote(x_ref, slot(x, y), sems, (x, 1 - y, c))
        for cp in (mine, to_sibling, to_x, to_y):
            cp.start()
        relay_send, relay_recv = sems.remote()

        def relay(arrived, unused, origin, to):
            arrived.wait_recv()
            blk = slot(*origin)
            fwd = pltpu.make_async_remote_copy(src_ref=blk, dst_ref=blk, send_sem=relay_send, recv_sem=relay_recv,
                                               device_id=(*to, c), device_id_type=MESH)
            fwd.start()
            fwd.wait()
            arrived.wait_send()
            unused.wait()

        _when(c == 0, lambda: relay(to_y, to_x, (x, 1 - y), (1 - x, y)))
        _when(c == 1, lambda: relay(to_x, to_y, (1 - x, y), (x, 1 - y)))
        mine.wait()
        to_sibling.wait()

    return pl.pallas_call(
        body, name=name, in_specs=[ANY], out_specs=ANY,
        out_shape=jax.ShapeDtypeStruct((N_DEV,) + shard.shape, shard.dtype),
        scratch_shapes=[pltpu.SemaphoreType.DMA((4,)), pltpu.SemaphoreType.DMA((4,)), pltpu.SemaphoreType.DMA((1,))],
    )(shard)


def _all_gather_vmem(shard, name):
    first = _gather_first([shard])

    def body(x_ref, out_ref, send, recv, local):
        sems = _Sems(send, recv, local)
        copies = first.plan([x_ref], [out_ref], sems)
        for cp in copies:
            cp.start()
        x, y, c = _position()
        passed = []
        for j, (px, py) in enumerate(_other_chips(x, y)):
            copies[2 + j].wait_recv()
            blk = out_ref.at[4 * px + 2 * py + c]
            fwd = _remote(blk, blk, sems, (x, y, 1 - c))
            fwd.start()
            passed.append(fwd)
        copies[0].wait()
        copies[1].wait()
        for cp in copies[2:]:
            cp.wait_send()
        for cp in passed:
            cp.wait()

    vm = pl.BlockSpec(memory_space=pltpu.VMEM)
    return pl.pallas_call(
        body, name=name, in_specs=[vm], out_specs=vm,
        out_shape=jax.ShapeDtypeStruct((N_DEV,) + shard.shape, shard.dtype),
        scratch_shapes=[pltpu.SemaphoreType.DMA((7,)), pltpu.SemaphoreType.DMA((7,)), pltpu.SemaphoreType.DMA((1,))],
    )(shard)


def _rs_pair_add(partial, landed, where, name):
    _, r, c = partial.shape
    tm = _tile(r, 256, 16)

    def kern(where_ref, g_ref, l_ref, own_ref, hb_ref):
        s = g_ref[...] + l_ref[...]
        hb_ref[...] = s.astype(BF16)

        @pl.when(pl.program_id(1) == where_ref[1])
        def _():
            own_ref[...] = s

    grid_spec = pltpu.PrefetchScalarGridSpec(
        num_scalar_prefetch=1, grid=(r // tm, 4),
        in_specs=[pl.BlockSpec((None, tm, c), lambda i, q, wr: (2 * q + wr[0], i, 0)),
                  pl.BlockSpec((None, tm, c), lambda i, q, wr: (q, i, 0))],
        out_specs=[pl.BlockSpec((tm, c), lambda i, q, wr: (i, 0)),
                   pl.BlockSpec((None, tm, c), lambda i, q, wr: (q, i, 0))])
    return pl.pallas_call(
        kern, name=name, grid_spec=grid_spec,
        out_shape=[jax.ShapeDtypeStruct((r, c), F32), jax.ShapeDtypeStruct((4, r, c), BF16)],
        compiler_params=_params(("parallel", "arbitrary")),
    )(where, partial, landed)


def _adam_math(g, w, m, v):
    m = ADAM_B1 * m + (1.0 - ADAM_B1) * g
    v = ADAM_B2 * v + (1.0 - ADAM_B2) * (g * g)
    m_hat = m / (1.0 - ADAM_B1 ** ADAM_STEP)
    v_hat = v / (1.0 - ADAM_B2 ** ADAM_STEP)
    delta = -ADAM_LR * (m_hat / (jnp.sqrt(v_hat) + ADAM_EPS) + ADAM_WD * w)
    return delta, m, v


def _rs_finish_adam(own, landed, w, m, v, name):
    r, c = own.shape
    tm = _tile(r, 128, 16)

    def kern(h_ref, l_ref, w_ref, m_ref, v_ref, *outs):
        g = ((h_ref[...] + l_ref[0].astype(F32)) + l_ref[1].astype(F32)) + l_ref[2].astype(F32)
        delta, m_new, v_new = _adam_math(g, w_ref[...], m_ref[...], v_ref[...])
        for ref, val in zip(outs, (g, delta, m_new, v_new)):
            ref[...] = val

    flat = pl.BlockSpec((tm, c), lambda i: (i, 0))
    return pl.pallas_call(
        kern, name=name, grid=(r // tm,), in_specs=[flat, pl.BlockSpec((3, tm, c), lambda i: (0, i, 0)), flat, flat, flat],
        out_specs=[flat] * 4, out_shape=[jax.ShapeDtypeStruct((r, c), F32)] * 4,
        compiler_params=_params(("parallel",)))(own, landed, w, m, v)


def _allreduce_adam(gathered, w, m, v, name):
    _, r, c = gathered.shape

    def kern(a_ref, w_ref, m_ref, v_ref, g_out, d_out, m_out, v_out):
        g = a_ref[0]
        for j in range(1, N_DEV):
            g = g + a_ref[j]
        delta, m_new, v_new = _adam_math(g, w_ref[...], m_ref[...], v_ref[...])
        g_out[...] = g
        d_out[...] = delta
        m_out[...] = m_new
        v_out[...] = v_new

    return pl.pallas_call(
        kern, name=name, out_shape=[jax.ShapeDtypeStruct((r, c), F32)] * 4,
        compiler_params=_params(),
    )(gathered, w, m, v)


def _cols_from_blocks(g):
    n, k, nb = g.shape
    return jnp.transpose(g, (1, 0, 2)).reshape(k, n * nb)


def _blocks_from_cols(w):
    k, n = w.shape
    return jnp.transpose(w.reshape(k, N_DEV, n // N_DEV), (1, 0, 2))


def _pack_small(parts):
    flat = []
    for p in parts:
        v = p.reshape(-1)
        flat.append(jnp.pad(v, (0, (-v.shape[0]) % (8 * LANES))))
    return jnp.concatenate(flat).reshape(-1, LANES)


def _unpack_small(buf, like):
    out, pos = [], 0
    flat = buf.reshape(-1)
    for p in like:
        size = p.size
        out.append(flat[pos:pos + size].reshape(p.shape))
        pos += size + (-size) % (8 * LANES)
    return out


def kernel(x, mem, ffn1_norm, ffn1_w_gate_up, ffn1_w_down, mix_norm, mem_norm, w_in, b_forget, pool_w, pool_scale, w_pool_up, fox_q_norm, fox_k_norm, w_fox_o, w_mem_kv, mem_q_norm, mem_k_norm, w_mem_o, w_out, ffn2_norm, ffn2_w_gate_up, ffn2_w_down, loss_target, m_ffn1_norm, m_ffn1_w_gate_up, m_ffn1_w_down, m_mix_norm, m_mem_norm, m_w_in, m_b_forget, m_pool_w, m_pool_scale, m_w_pool_up, m_fox_q_norm, m_fox_k_norm, m_w_fox_o, m_w_mem_kv, m_mem_q_norm, m_mem_k_norm, m_w_mem_o, m_w_out, m_ffn2_norm, m_ffn2_w_gate_up, m_ffn2_w_down, v_ffn1_norm, v_ffn1_w_gate_up, v_ffn1_w_down, v_mix_norm, v_mem_norm, v_w_in, v_b_forget, v_pool_w, v_pool_scale, v_w_pool_up, v_fox_q_norm, v_fox_k_norm, v_w_fox_o, v_w_mem_kv, v_mem_q_norm, v_mem_k_norm, v_w_mem_o, v_w_out, v_ffn2_norm, v_ffn2_w_gate_up, v_ffn2_w_down):
    names = ["ffn1_norm", "ffn1_w_gate_up", "ffn1_w_down", "mix_norm", "mem_norm", "w_in", "b_forget", "pool_w",
             "pool_scale", "w_pool_up", "fox_q_norm", "fox_k_norm", "w_fox_o", "w_mem_kv", "mem_q_norm",
             "mem_k_norm", "w_mem_o", "w_out", "ffn2_norm", "ffn2_w_gate_up", "ffn2_w_down"]
    w_args = [ffn1_norm, ffn1_w_gate_up, ffn1_w_down, mix_norm, mem_norm, w_in, b_forget, pool_w, pool_scale,
              w_pool_up, fox_q_norm, fox_k_norm, w_fox_o, w_mem_kv, mem_q_norm, mem_k_norm, w_mem_o, w_out,
              ffn2_norm, ffn2_w_gate_up, ffn2_w_down]
    m_args = [m_ffn1_norm, m_ffn1_w_gate_up, m_ffn1_w_down, m_mix_norm, m_mem_norm, m_w_in, m_b_forget, m_pool_w,
              m_pool_scale, m_w_pool_up, m_fox_q_norm, m_fox_k_norm, m_w_fox_o, m_w_mem_kv, m_mem_q_norm,
              m_mem_k_norm, m_w_mem_o, m_w_out, m_ffn2_norm, m_ffn2_w_gate_up, m_ffn2_w_down]
    v_args = [v_ffn1_norm, v_ffn1_w_gate_up, v_ffn1_w_down, v_mix_norm, v_mem_norm, v_w_in, v_b_forget, v_pool_w,
              v_pool_scale, v_w_pool_up, v_fox_q_norm, v_fox_k_norm, v_w_fox_o, v_w_mem_kv, v_mem_q_norm,
              v_mem_k_norm, v_w_mem_o, v_w_out, v_ffn2_norm, v_ffn2_w_gate_up, v_ffn2_w_down]
    W = dict(zip(names, w_args))
    M = dict(zip(names, m_args))
    V = dict(zip(names, v_args))

    batch, seq, d = x.shape
    mlen = mem.shape[1]
    t = batch * seq
    gate_w = 3 * d
    z_gate, z_q = 0, gate_w
    z_k, z_v = z_q + FOX_WIDTH, z_q + 2 * FOX_WIDTH
    z_u = z_q + 3 * FOX_WIDTH
    z_qm = z_u + POOL_WIDTH
    z_f = z_qm + MEM_WIDTH
    z_width = -(-(z_f + F_PAD) // 512) * 512

    x2d = x.reshape(t, d)
    mem2d = mem.reshape(batch * mlen, d)
    tgt2d = loss_target.reshape(t, d)

    big = ["ffn1_w_gate_up", "ffn1_w_down", "w_in", "w_pool_up", "w_fox_o", "w_mem_kv", "w_mem_o", "w_out",
           "ffn2_w_gate_up", "ffn2_w_down"]
    col_sharded = {"ffn1_w_gate_up", "ffn2_w_gate_up", "w_in", "w_pool_up", "w_fox_o", "w_mem_o"}
    mixer_small = ["w_pool_up", "w_fox_o", "w_mem_kv", "w_mem_o", "w_out"]
    shard = {n: W[n][0].astype(BF16) for n in big}

    def rows_of(g):
        return g.reshape(-1, g.shape[2])

    gu_caps_nt, gu_caps_tn = (1024, 1024, 1408), (1024, 1408, 2048)
    down_caps, dwd_caps = (1024, 1024, 2816), (1408, 1024, 2048)

    wgu1 = _gather_first_relayed(shard["ffn1_w_gate_up"], "ag_first")
    h1, (wgu1,) = _rms_fwd(x2d, W["ffn1_norm"], "ffn1_rms", comm=_gather_second([wgu1]))
    w_in_top, w_in_bot = shard["w_in"][:d // 2], shard["w_in"][d // 2:]
    (gu1, a1), bufs = _gate_up_swiglu(h1, wgu1, "ffn1_gu", comm=_gather_first([shard["ffn1_w_down"], w_in_top]))
    wd1_g, w_in_top_g = _comm_call(_gather_second(bufs), "ag_second")
    wd1 = rows_of(wd1_g)
    x1, (w_in_bot_g,) = _mm(a1, wd1, "nn", "ffn1_down", scale=0.5, res=x2d, caps=down_caps,
                            comm=_gather_first([w_in_bot]))
    h2, (w_in_bot_g,) = _rms_fwd(x1, W["mix_norm"], "mix_rms", comm=_gather_second([w_in_bot_g]))

    o_u, o_q, o_v = 0, POOL_WIDTH, POOL_WIDTH + 2 * FOX_WIDTH
    o_f = o_v + FOX_WIDTH
    o_qm = o_f + FOX_HEADS
    o_g = o_qm + MEM_WIDTH
    wi = jnp.concatenate([_cols_from_blocks(w_in_top_g), _cols_from_blocks(w_in_bot_g)], axis=0)
    w_in_pad = jnp.concatenate(
        [wi[:, o_g:o_g + gate_w], wi[:, o_q:o_q + 3 * FOX_WIDTH], wi[:, o_u:o_u + POOL_WIDTH],
         wi[:, o_qm:o_qm + MEM_WIDTH], wi[:, o_f:o_f + FOX_HEADS],
         jnp.zeros((d, z_width - z_f - FOX_HEADS), BF16)], axis=1)
    group_b = mixer_small + ["ffn2_w_down"]
    z, bufs = _mm(h2, w_in_pad, "nn", "mix_in", comm=_gather_first([shard[n] for n in group_b]))

    pool_w_b = W["pool_w"][0].astype(BF16)

    def pool_fwd_body(u, pw, ps):
        row = lax.broadcasted_iota(jnp.int32, (seq, 1), 0)
        diffs, mixed = [], []
        for g in range(POOL_GROUPS):
            ug = u[:, g * POOL_GROUP_DIM:(g + 1) * POOL_GROUP_DIM]
            cnt = jnp.minimum(row + 1, POOL_WINDOWS[g]).astype(F32)
            diff = _window_sum(ug, g + 1, True) / cnt - ug
            diffs.append(diff)
            mixed.append(lax.dot_general(diff.astype(BF16), pw[g], _DIMS["nn"], preferred_element_type=F32))
        diffs = jnp.concatenate(diffs, axis=1)
        mixed = jnp.concatenate(mixed, axis=1)
        return mixed * ps, diffs, mixed

    ypp, pool_diff, pool_mixed = _rowwise(
        pool_fwd_body, [(z, z_u, POOL_WIDTH)], [pool_w_b, W["pool_scale"]],
        [(POOL_WIDTH, BF16), (POOL_WIDTH, BF16), (POOL_WIDTH, F32)], [], "pool_fwd", seq)

    gq = jnp.tile(W["fox_q_norm"], (1, FOX_HEADS))
    gk = jnp.tile(W["fox_k_norm"], (1, FOX_HEADS))
    b_pad = jnp.pad(W["b_forget"], ((0, 0), (0, F_PAD - FOX_HEADS)))

    def fox_prep_body(q, k, v, f, gqv, gkv, bv):
        qn_scaled = _head_norm(q, gqv, FOX_HEAD_DIM).astype(BF16) * FOX_HEAD_DIM ** -0.5
        return qn_scaled, _head_norm(k, gkv, FOX_HEAD_DIM), v, _log_sigmoid(f + bv)

    tm_e = _tile(t, 256, 16)
    (qn, kn, vb, logf), bufs = _rowwise(
        fox_prep_body, [(z, z_q, FOX_WIDTH), (z, z_k, FOX_WIDTH), (z, z_v, FOX_WIDTH), (z, z_f, F_PAD)],
        [gq, gk, b_pad], [(FOX_WIDTH, BF16), (FOX_WIDTH, BF16), (FOX_WIDTH, BF16), (F_PAD, F32)], [], "fox_prep", tm_e,
        comm=_gather_second(bufs))
    full = {n: (_cols_from_blocks(g) if n in col_sharded else rows_of(g)) for n, g in zip(group_b, bufs)}
    wd2 = full["ffn2_w_down"]
    y_pool = _mm(ypp, full["w_pool_up"], "nn", "pool_up")
    csum = _rowwise(lambda v: (_scan_rows(v, True),), [_whole(logf)], [], [(F_PAD, F32)], [], "fox_cumsum", seq)[0]
    cks = jnp.transpose(csum.reshape(batch, seq, F_PAD)[:, :, :FOX_HEADS], (0, 2, 1)).reshape(
        batch * FOX_HEADS // 2, 2, seq)
    (o_fox, lse), (wgu2,) = _fox_fwd(qn, kn, vb, cks, batch, seq, "fox_fwd",
                                     comm=_gather_first([shard["ffn2_w_gate_up"]]))
    y_fox = _mm(o_fox, full["w_fox_o"], "nn", "fox_o")

    memn = _rms_fwd(mem2d, W["mem_norm"], "mem_rms")
    kv = _mm(memn, full["w_mem_kv"], "nn", "mem_kv")
    gqm = jnp.tile(W["mem_q_norm"], (1, MEM_HEADS))
    gkm = jnp.tile(W["mem_k_norm"], (1, MEM_HEADS))
    qmn = _rowwise(lambda q, g: (_head_norm(q, g, MEM_HEAD_DIM),), [(z, z_qm, MEM_WIDTH)], [gqm],
                   [(MEM_WIDTH, BF16)], [], "memq_prep", tm_e)[0]
    kmn, vmb = _rowwise(lambda k, v, g: (_head_norm(k, g, MEM_HEAD_DIM), v),
                        [(kv, 0, MEM_WIDTH), (kv, MEM_WIDTH, MEM_WIDTH)], [gkm],
                        [(MEM_WIDTH, BF16), (MEM_WIDTH, BF16)], [], "memk_prep", _tile(batch * mlen, 256, 16))
    o_mem = _mem_fwd(qmn, kmn, vmb, batch, seq, mlen, "mem_fwd")
    y_mem = _mm(o_mem, full["w_mem_o"], "nn", "mem_o")

    def gate_fwd_body(gp, gf, gm, yp, yf, ym):
        return ((_sigmoid(gp) * yp + _sigmoid(gf) * yf) + _sigmoid(gm) * ym,)

    tm_g = _tile(t, 128, 16)
    (merged,), (wgu2,) = _rowwise(
        gate_fwd_body, [(z, 0, d), (z, d, d), (z, 2 * d, d), _whole(y_pool), _whole(y_fox), _whole(y_mem)],
        [], [(d, BF16)], [], "gate_fwd", tm_g, comm=_gather_second([wgu2]))
    x2 = _mm(merged, full["w_out"], "nn", "mix_out", res=x1)

    h3 = _rms_fwd(x2, W["ffn2_norm"], "ffn2_rms")
    gu2, a2 = _gate_up_swiglu(h3, wgu2, "ffn2_gu")
    x3 = _mm(a2, wd2, "nn", "ffn2_down", scale=0.5, res=x2, caps=down_caps)
    dy, dy_b, loss_part = _loss_head(x3, tgt2d, "loss")
    loss = lax.psum(loss_part[0, 0], ("x", "y", "c"))

    cx, cy, cc = _position()
    where = jnp.stack([cc, 2 * cx + cy]).astype(jnp.int32)
    G, own, landed = {}, {}, {}

    def row_blocks(g):
        return g.reshape(N_DEV, g.shape[0] // N_DEV, g.shape[1])

    def pair_add(n, partial, from_core):
        own[n], chip_sums = _rs_pair_add(partial, from_core, where, "rs_add_" + n)
        return chip_sums

    p_wd2 = row_blocks(_mm(a2, dy_b, "tn", "ffn2_dwd", scale=0.5, caps=dwd_caps))
    dgu2, (l_wd2,) = _swiglu_bwd_from_out(dy_b, wd2, gu2, "ffn2_dgu", comm=_scatter_first([p_wd2]))
    s_wd2 = pair_add("ffn2_w_down", p_wd2, l_wd2)
    p_wgu2, (landed["ffn2_w_down"],) = _mm(h3, dgu2, "tn", "ffn2_dwgu", caps=gu_caps_tn, out_blocks=True, halves="b",
                                           comm=_scatter_second([s_wd2]))
    dh3, (l_wgu2,) = _mm(dgu2, wgu2, "nt", "ffn2_dh", caps=gu_caps_nt, b_blocks=True, halves="a",
                         comm=_scatter_first([p_wgu2]))
    s_wgu2 = pair_add("ffn2_w_gate_up", p_wgu2, l_wgu2)
    dx2, dx2_b, G["ffn2_norm"] = _rms_bwd(x2, W["ffn2_norm"], dh3, dy, "ffn2_drms")

    dmerged = _mm(dx2_b, full["w_out"], "nt", "mix_out_dx")
    P = {"w_out": row_blocks(_mm(merged, dx2_b, "tn", "mix_out_dw"))}

    def gate_bwd_body(gp, gf, gm, yp, yf, ym, dm):
        outs_dl, outs_dy = [], []
        for gl, yv in ((gp, yp), (gf, yf), (gm, ym)):
            s = _sigmoid(gl)
            outs_dl.append((dm * yv) * (s * (1.0 - s)))
            outs_dy.append(dm * s)
        return (jnp.concatenate(outs_dl, axis=1), *outs_dy)

    dz, dy_pool, dy_fox, dy_mem = _rowwise(
        gate_bwd_body, [(z, 0, d), (z, d, d), (z, 2 * d, d), _whole(y_pool), _whole(y_fox), _whole(y_mem), _whole(dmerged)],
        [], [(gate_w, BF16), (d, BF16), (d, BF16), (d, BF16)], [], "gate_bwd", tm_g, into=(None, z_width, z_gate))

    dypp = _mm(dy_pool, full["w_pool_up"], "nt", "pool_up_dx")
    P["w_pool_up"] = _blocks_from_cols(_mm(ypp, dy_pool, "tn", "pool_up_dw"))

    def pool_bwd_body(dyv, mixed, diff, pw, ps):
        row = lax.broadcasted_iota(jnp.int32, (seq, 1), 0)
        d_scale = jnp.sum(dyv * mixed, axis=0, keepdims=True)
        dmix = (dyv * ps).astype(BF16)
        du, dpw = [], []
        for g in range(POOL_GROUPS):
            sl = slice(g * POOL_GROUP_DIM, (g + 1) * POOL_GROUP_DIM)
            dmg = dmix[:, sl]
            ddiff = lax.dot_general(dmg, pw[g], _DIMS["nt"], preferred_element_type=F32)
            dpw.append(lax.dot_general(diff[:, sl], dmg, _DIMS["tn"], preferred_element_type=F32))
            cnt = jnp.minimum(row + 1, POOL_WINDOWS[g]).astype(F32)
            du.append(_window_sum(ddiff / cnt, g + 1, False) - ddiff)
        return jnp.concatenate(du, axis=1), d_scale, jnp.concatenate(dpw, axis=0)

    dz, G["pool_scale"], d_pool_w = _rowwise(
        pool_bwd_body, [_whole(dypp), _whole(pool_mixed), _whole(pool_diff)], [pool_w_b, W["pool_scale"]],
        [(POOL_WIDTH, BF16)], [((1, POOL_WIDTH), F32), ((POOL_WIDTH, POOL_GROUP_DIM), F32)],
        "pool_bwd", seq, into=(dz, z_width, z_u))
    G["pool_w"] = d_pool_w.reshape(1, POOL_GROUPS, POOL_GROUP_DIM, POOL_GROUP_DIM)

    do_fox = _mm(dy_fox, full["w_fox_o"], "nt", "fox_o_dx")
    P["w_fox_o"] = _blocks_from_cols(_mm(o_fox, dy_fox, "tn", "fox_o_dw"))
    (dkn, dz, dqn, dcks, dcq), (landed["ffn2_w_gate_up"],) = _fox_bwd(
        qn, kn, vb, cks, o_fox, lse, do_fox, dz, z_v, batch, seq, "fox_bwd", comm=_scatter_second([s_wgu2]))
    dcs = jnp.transpose(dcks.reshape(batch, FOX_HEADS, seq), (0, 2, 1)).reshape(t, FOX_HEADS)
    dcs = jnp.pad(dcs, ((0, 0), (0, F_PAD - FOX_HEADS)))
    dcq = jnp.pad(dcq.reshape(t, FOX_HEADS, FOX_HEAD_DIM)[:, :, 0], ((0, 0), (0, F_PAD - FOX_HEADS)))

    def fox_f_bwd_body(dc, dc_rows, f, bv):
        lane = lax.broadcasted_iota(jnp.int32, (1, F_PAD), 1)
        dlogf = _scan_rows(dc + dc_rows, False)
        df = jnp.where(lane < FOX_HEADS, dlogf * _sigmoid(-(f + bv)), 0.0)
        behind = jnp.zeros((seq, z_width - z_f - F_PAD), F32)
        return jnp.concatenate([df, behind], axis=1), jnp.sum(df, axis=0, keepdims=True)

    dz, db_pad = _rowwise(fox_f_bwd_body, [_whole(dcs), _whole(dcq), (z, z_f, F_PAD)], [b_pad],
                          [(z_width - z_f, BF16)], [((1, F_PAD), F32)], "fox_f_bwd", seq, into=(dz, z_width, z_f))
    G["b_forget"] = db_pad[:, :FOX_HEADS]

    def fox_head_bwd(dz_in, col, dyn, gain, name):
        return _rowwise(lambda v, dyv, g: _head_norm_bwd(v, g, dyv, FOX_HEAD_DIM), [(z, col, FOX_WIDTH), _whole(dyn)],
                        [gain], [(FOX_WIDTH, BF16)], [((1, FOX_WIDTH), F32)], name, tm_e, into=(dz_in, z_width, col))

    dz, dgq_t = fox_head_bwd(dz, z_q, dqn, gq, "fox_q_bwd")
    dz, dgk_t = fox_head_bwd(dz, z_k, dkn, gk, "fox_k_bwd")
    G["fox_q_norm"] = jnp.sum(dgq_t.reshape(FOX_HEADS, FOX_HEAD_DIM), axis=0, keepdims=True)
    G["fox_k_norm"] = jnp.sum(dgk_t.reshape(FOX_HEADS, FOX_HEAD_DIM), axis=0, keepdims=True)

    do_mem = _mm(dy_mem, full["w_mem_o"], "nt", "mem_o_dx")
    P["w_mem_o"] = _blocks_from_cols(_mm(o_mem, dy_mem, "tn", "mem_o_dw"))
    dqmn, dkmn, dvm = _mem_bwd(qmn, kmn, vmb, do_mem, batch, seq, mlen, "mem_bwd")
    dz, dgqm_t = _rowwise(lambda q, dq, g: _head_norm_bwd(q, g, dq, MEM_HEAD_DIM),
                          [(z, z_qm, MEM_WIDTH), _whole(dqmn)], [gqm], [(MEM_WIDTH, BF16)],
                          [((1, MEM_WIDTH), F32)], "memq_bwd", tm_e, into=(dz, z_width, z_qm))

    def memk_bwd_body(k, dk, dv, g):
        dkr, dg = _head_norm_bwd(k, g, dk, MEM_HEAD_DIM)
        return jnp.concatenate([dkr, dv], axis=1), dg

    dkv, dgkm_t = _rowwise(memk_bwd_body, [(kv, 0, MEM_WIDTH), _whole(dkmn), _whole(dvm)], [gkm],
                           [(2 * MEM_WIDTH, BF16)], [((1, MEM_WIDTH), F32)], "memk_bwd", _tile(batch * mlen, 256, 16))
    G["mem_q_norm"] = jnp.sum(dgqm_t.reshape(MEM_HEADS, MEM_HEAD_DIM), axis=0, keepdims=True)
    G["mem_k_norm"] = jnp.sum(dgkm_t.reshape(MEM_HEADS, MEM_HEAD_DIM), axis=0, keepdims=True)
    P["w_mem_kv"] = row_blocks(_mm(memn, dkv, "tn", "mem_kv_dw"))
    dmemn = _mm(dkv, full["w_mem_kv"], "nt", "mem_kv_dx")
    _, _, G["mem_norm"] = _rms_bwd(mem2d, W["mem_norm"], dmemn, None, "mem_drms")

    d_w_in_pad, l_small = _mm(h2, dz, "tn", "mix_in_dw", comm=_scatter_first([P[n] for n in mixer_small]))
    s_small = [pair_add(n, P[n], l) for n, l in zip(mixer_small, l_small)]
    p = d_w_in_pad
    p_w_in = _blocks_from_cols(jnp.concatenate(
        [p[:, z_u:z_u + POOL_WIDTH], p[:, z_q:z_q + 3 * FOX_WIDTH], p[:, z_f:z_f + FOX_HEADS],
         p[:, z_qm:z_qm + MEM_WIDTH], p[:, z_gate:z_gate + gate_w]], axis=1))
    dh2, rest = _mm(dz, w_in_pad, "nt", "mix_in_dx", caps=(1024, 1024, 1792),
                    comm=_join(_scatter_second(s_small), _scatter_first([p_w_in])))
    for n, l in zip(mixer_small, rest[:len(mixer_small)]):
        landed[n] = l
    s_w_in = pair_add("w_in", p_w_in, rest[-1])
    dx1, dx1_b, G["mix_norm"] = _rms_bwd(x1, W["mix_norm"], dh2, dx2, "mix_drms")

    dgu1 = _swiglu_bwd_from_out(dx1_b, wd1, gu1, "ffn1_dgu")
    p_wgu1, (landed["w_in"],) = _mm(h1, dgu1, "tn", "ffn1_dwgu", caps=gu_caps_tn, out_blocks=True, halves="b",
                                    comm=_scatter_second([s_w_in]))
    d_wd1, (l_wgu1,) = _mm(a1, dx1_b, "tn", "ffn1_dwd", scale=0.5, caps=dwd_caps, comm=_scatter_first([p_wgu1]))
    p_wd1 = row_blocks(d_wd1)
    s_wgu1 = pair_add("ffn1_w_gate_up", p_wgu1, l_wgu1)
    dh1, (landed["ffn1_w_gate_up"], l_wd1) = _mm(dgu1, wgu1, "nt", "ffn1_dh", caps=gu_caps_nt, b_blocks=True, halves="a",
                                                 comm=_join(_scatter_second([s_wgu1]), _scatter_first([p_wd1])))
    s_wd1 = pair_add("ffn1_w_down", p_wd1, l_wd1)
    (dx0, _, G["ffn1_norm"]), (landed["ffn1_w_down"],) = _rms_bwd(x2d, W["ffn1_norm"], dh1, dx1, "ffn1_drms",
                                                                  comm=_scatter_second([s_wd1]))
    grad_x = dx0.reshape(batch, seq, d)

    out_g, out_d, out_m, out_v = {}, {}, {}, {}
    for n in big:
        res = _rs_finish_adam(own[n], landed[n], W[n][0], M[n][0], V[n][0], "adam_" + n)
        out_g[n], out_d[n], out_m[n], out_v[n] = [r[None] for r in res]

    small = [n for n in names if n not in big]
    g_small = _pack_small([G[n].reshape(W[n].shape) for n in small])
    all_small = _all_gather_vmem(g_small, "ag_small_grads")
    res = _allreduce_adam(all_small, _pack_small([W[n] for n in small]), _pack_small([M[n] for n in small]),
                          _pack_small([V[n] for n in small]), "adam_small")
    like = [W[n] for n in small]
    for dst, buf in zip((out_g, out_d, out_m, out_v), res):
        for n, a in zip(small, _unpack_small(buf, like)):
            dst[n] = a

    return (loss, grad_x, *[out_g[n] for n in names], *[out_d[n] for n in names],
            *[out_m[n] for n in names], *[out_v[n] for n in names])
```

```python
import functools

import jax
import jax.numpy as jnp
from jax import lax
from jax.experimental import pallas as pl
from jax.experimental.pallas import tpu as pltpu

F32 = jnp.float32
BF16 = jnp.bfloat16
MESH = pl.DeviceIdType.MESH

N_DEV = 8
EPS = 1e-6
FOX_HEADS = 16
FOX_HEAD_DIM = 64
FOX_WIDTH = FOX_HEADS * FOX_HEAD_DIM
MEM_HEADS = 4
MEM_HEAD_DIM = 128
MEM_WIDTH = MEM_HEADS * MEM_HEAD_DIM
POOL_GROUPS = 4
POOL_GROUP_DIM = 128
POOL_WIDTH = POOL_GROUPS * POOL_GROUP_DIM
POOL_WINDOWS = (2, 4, 8, 16)
LANES = 128
F_PAD = LANES

ADAM_LR = 0.001
ADAM_B1 = 0.9
ADAM_B2 = 0.999
ADAM_EPS = 1e-08
ADAM_WD = 0.01
ADAM_STEP = 10

VMEM_LIMIT = 56 * 1024 * 1024
NEG = -1e30

ANY = pl.BlockSpec(memory_space=pl.ANY)


def _params(sem=None):
    return pltpu.CompilerParams(dimension_semantics=sem, vmem_limit_bytes=VMEM_LIMIT)


MXU_DIM = 256


def _tile(dim, cap, align):
    best = None
    t = align
    while t <= min(dim, cap):
        if dim % t == 0:
            best = t
        t += align
    return dim if best is None else best


def _mxu_tile(dim, cap):
    wide, fine = _tile(dim, cap, MXU_DIM), _tile(dim, cap, LANES)
    whole_widths = wide % MXU_DIM == 0 and wide <= cap
    return wide if whole_widths and fine < 2 * wide else fine


class _Sems:
    def __init__(self, send, recv, local):
        self.send, self.recv, self.local = send, recv, local
        self.n_remote = self.n_local = 0

    def remote(self):
        i = self.n_remote
        self.n_remote += 1
        return self.send.at[i], self.recv.at[i]

    def one_local(self):
        i = self.n_local
        self.n_local += 1
        return self.local.at[i]


class _Comm:
    def __init__(self, srcs, outs, n_remote, n_local, plan, in_place=False):
        self.srcs, self.outs, self.n_remote, self.n_local = list(srcs), list(outs), n_remote, n_local
        self.plan, self.in_place = plan, in_place

    def aliases(self, n_in, n_out):
        return {n_in + i: n_out + i for i in range(len(self.srcs))} if self.in_place else {}

    def scratch(self):
        return [pltpu.SemaphoreType.DMA((self.n_remote,)), pltpu.SemaphoreType.DMA((self.n_remote,)),
                pltpu.SemaphoreType.DMA((max(self.n_local, 1),))]

    def run(self, src_refs, out_refs, sem_refs, first, last):
        def copies():
            return self.plan(list(src_refs), list(out_refs), _Sems(*sem_refs))

        return (lambda: _when(first, lambda: [cp.start() for cp in copies()]),
                lambda: _when(last, lambda: [cp.wait() for cp in copies()]))


def _when(cond, fn):
    @pl.when(cond)
    def _():
        fn()


def _join(*comms):
    comms = [c for c in comms if c is not None]
    assert all(not c.in_place for c in comms)

    def plan(src_refs, out_refs, sems):
        copies, si, oi = [], 0, 0
        for c in comms:
            copies += c.plan(src_refs[si:si + len(c.srcs)], out_refs[oi:oi + len(c.outs)], sems)
            si += len(c.srcs)
            oi += len(c.outs)
        return copies

    return _Comm(sum((c.srcs for c in comms), []), sum((c.outs for c in comms), []),
                 sum(c.n_remote for c in comms), sum(c.n_local for c in comms), plan)


def _split_refs(refs, n_in, n_out, comm):
    if comm is None:
        return refs[:n_in], refs[n_in:n_in + n_out], refs[n_in + n_out:], (), (), ()
    ns, no = len(comm.srcs), len(comm.outs)
    ins = refs[:n_in]
    srcs = refs[n_in:n_in + ns]
    outs = refs[n_in + ns:n_in + ns + n_out]
    couts = refs[n_in + ns + n_out:n_in + ns + n_out + no]
    rest = refs[n_in + ns + n_out + no:]
    return ins, outs, rest[:-3], srcs, couts, rest[-3:]


def _sigmoid(x):
    return 1.0 / (1.0 + jnp.exp(-x))


_DIMS = {"nn": (((1,), (0,)), ((), ())), "nt": (((1,), (1,)), ((), ())), "tn": (((0,), (0,)), ((), ()))}


def _mm(a, b, mode, name, out_dtype=F32, scale=1.0, res=None, caps=None, b_blocks=False, out_blocks=False,
        halves=None, comm=None):
    nblk = N_DEV
    a_shape, b_shape = a.shape, b.shape
    if b_blocks:
        _, r0, c0 = b.shape
        b_shape = (r0, nblk * c0)
    if halves == "a":
        assert mode == "nt"
        a_shape = (a.shape[1], 2 * a.shape[2])
    elif halves == "b":
        assert mode == "tn"
        b_shape = (b.shape[1], 2 * b.shape[2])
    if mode == "nn":
        (m, k), (k2, n) = a_shape, b_shape
    elif mode == "nt":
        (m, k), (n, k2) = a_shape, b_shape
    else:
        (k, m), (k2, n) = a_shape, b_shape
    assert k == k2, (name, a.shape, b.shape)
    cm, cn, ck = caps or ((1024, 512, 2048) if k <= 2048 else (1024, 1024, 2048))
    n_unit = n // nblk if (out_blocks or (b_blocks and mode == "nn")) else n
    k_unit = k // nblk if (b_blocks and mode == "nt") else k
    tm, tn, tk = _mxu_tile(m, cm), _mxu_tile(n_unit, cn), _mxu_tile(k_unit, ck)
    nk = k // tk
    npb, kpb = n_unit // tn, k_unit // tk
    n_half, k_half = (n // 2) // tn, (k // 2) // tk
    if mode == "nn":
        a_spec = pl.BlockSpec((tm, tk), lambda i, j, kk: (i, kk))
        b_spec = pl.BlockSpec((tk, tn), lambda i, j, kk: (kk, j))
        if b_blocks:
            b_spec = pl.BlockSpec((None, tk, tn), lambda i, j, kk: (j // npb, kk, j % npb))
    elif mode == "nt":
        a_spec = pl.BlockSpec((tm, tk), lambda i, j, kk: (i, kk))
        b_spec = pl.BlockSpec((tn, tk), lambda i, j, kk: (j, kk))
        if b_blocks:
            b_spec = pl.BlockSpec((None, tn, tk), lambda i, j, kk: (kk // kpb, j, kk % kpb))
        if halves == "a":
            assert (k // 2) % tk == 0
            a_spec = pl.BlockSpec((None, tm, tk), lambda i, j, kk: (kk // k_half, i, kk % k_half))
    else:
        assert not b_blocks
        a_spec = pl.BlockSpec((tk, tm), lambda i, j, kk: (kk, i))
        b_spec = pl.BlockSpec((tk, tn), lambda i, j, kk: (kk, j))
        if halves == "b":
            assert (n // 2) % tn == 0
            b_spec = pl.BlockSpec((None, tk, tn), lambda i, j, kk: (j // n_half, kk, j % n_half))
    if out_blocks:
        assert res is None
        o_spec = pl.BlockSpec((None, tm, tn), lambda i, j, kk: (j // npb, i, j % npb))
        o_shape = jax.ShapeDtypeStruct((nblk, m, n // nblk), out_dtype)
    else:
        o_spec = pl.BlockSpec((tm, tn), lambda i, j, kk: (i, j))
        o_shape = jax.ShapeDtypeStruct((m, n), out_dtype)
    in_specs = [a_spec, b_spec] + ([o_spec] if res is not None else [])
    n_in = len(in_specs)
    dims = _DIMS[mode]
    gm, gn = m // tm, n // tn

    def kern(*refs):
        ins, outs, scratch, c_src, c_out, c_sem = _split_refs(refs, n_in, 1, comm)
        a_ref, b_ref = ins[0], ins[1]
        res_ref = ins[2] if res is not None else None
        o_ref = outs[0]
        if comm is not None:
            i, j, kq = pl.program_id(0), pl.program_id(1), pl.program_id(2)
            first = jnp.logical_and(jnp.logical_and(i == 0, j == 0), kq == 0)
            last = jnp.logical_and(jnp.logical_and(i == gm - 1, j == gn - 1), kq == nk - 1)
            start_comm, wait_comm = comm.run(c_src, c_out, c_sem, first, last)
            start_comm()
        a_tile = a_ref[...].astype(BF16)
        if scale != 1.0:
            a_tile = a_tile * scale

        def product():
            return lax.dot_general(a_tile, b_ref[...].astype(BF16), dims, preferred_element_type=F32)

        if nk == 1:
            r = product()
            if res_ref is not None:
                r = res_ref[...] + r
            o_ref[...] = r.astype(out_dtype)
        else:
            acc_ref = scratch[0] if scratch else o_ref
            kk = pl.program_id(2)

            @pl.when(kk == 0)
            def _():
                acc_ref[...] = jnp.zeros_like(acc_ref) if res_ref is None else res_ref[...]

            acc_ref[...] += product()
            if scratch:
                @pl.when(kk == nk - 1)
                def _():
                    o_ref[...] = acc_ref[...].astype(out_dtype)
        if comm is not None:
            wait_comm()

    assert scale in (1.0, 0.5)
    args = (a, b) + ((res,) if res is not None else ())
    scratch_shapes = [pltpu.VMEM((tm, tn), F32)] if (nk > 1 and out_dtype != F32) else []
    if comm is None:
        return pl.pallas_call(
            kern, name=name, grid=(gm, gn, nk), in_specs=in_specs, out_specs=o_spec, out_shape=o_shape,
            scratch_shapes=scratch_shapes, compiler_params=_params(("parallel", "parallel", "arbitrary")),
        )(*args)
    res_all = pl.pallas_call(
        kern, name=name, grid=(gm, gn, nk), in_specs=in_specs + [ANY] * len(comm.srcs),
        out_specs=[o_spec] + [ANY] * len(comm.outs), out_shape=[o_shape] + comm.outs,
        scratch_shapes=scratch_shapes + comm.scratch(), input_output_aliases=comm.aliases(n_in, 1),
        compiler_params=_params(("arbitrary", "arbitrary", "arbitrary")),
    )(*args, *comm.srcs)
    return res_all[0], list(res_all[1:])


def _gate_up_swiglu(h, w_blocks, name, comm=None):
    t, d = h.shape
    nblk, _, nb = w_blocks.shape
    half = nblk // 2
    tm = _mxu_tile(t, 512)
    steps = t // tm

    def kern(*refs):
        (h_ref, wg_ref, wu_ref), (gu_ref, a_ref), _, c_src, c_out, c_sem = _split_refs(refs, 3, 2, comm)
        if comm is not None:
            j, i = pl.program_id(0), pl.program_id(1)
            start_comm, wait_comm = comm.run(c_src, c_out, c_sem, jnp.logical_and(j == 0, i == 0),
                                             jnp.logical_and(j == half - 1, i == steps - 1))
            start_comm()
        hv = h_ref[...]
        g = lax.dot_general(hv, wg_ref[...], _DIMS["nn"], preferred_element_type=F32)
        u = lax.dot_general(hv, wu_ref[...], _DIMS["nn"], preferred_element_type=F32)
        gu_ref[0] = g
        gu_ref[1] = u
        a_ref[...] = ((g * _sigmoid(g)) * u).astype(BF16)
        if comm is not None:
            wait_comm()

    in_specs = [pl.BlockSpec((tm, d), lambda j, i: (i, 0)), pl.BlockSpec((None, d, nb), lambda j, i: (j, 0, 0)),
                pl.BlockSpec((None, d, nb), lambda j, i: (j + half, 0, 0))]
    out_specs = [pl.BlockSpec((2, tm, nb), lambda j, i: (0, i, j)), pl.BlockSpec((tm, nb), lambda j, i: (i, j))]
    out_shape = [jax.ShapeDtypeStruct((2, t, half * nb), F32), jax.ShapeDtypeStruct((t, half * nb), BF16)]
    if comm is None:
        return pl.pallas_call(kern, name=name, grid=(half, steps), in_specs=in_specs, out_specs=out_specs,
                              out_shape=out_shape, compiler_params=_params(("parallel", "parallel")))(h, w_blocks, w_blocks)
    res_all = pl.pallas_call(
        kern, name=name, grid=(half, steps), in_specs=in_specs + [ANY] * len(comm.srcs),
        out_specs=out_specs + [ANY] * len(comm.outs), out_shape=out_shape + comm.outs,
        scratch_shapes=comm.scratch(), compiler_params=_params(("arbitrary", "arbitrary")),
    )(h, w_blocks, w_blocks, *comm.srcs)
    return list(res_all[:2]), list(res_all[2:])


def _swiglu_bwd_from_out(dy, wd, gu, name, comm=None):
    t, d = dy.shape
    f = wd.shape[0]
    tm, tn = _mxu_tile(t, 1024), _mxu_tile(f, 512)
    gi, gj = t // tm, f // tn

    def kern(*refs):
        (dy_ref, wd_ref, gu_ref), (out_ref,), _, c_src, c_out, c_sem = _split_refs(refs, 3, 1, comm)
        if comm is not None:
            i, j = pl.program_id(0), pl.program_id(1)
            start_comm, wait_comm = comm.run(c_src, c_out, c_sem, jnp.logical_and(i == 0, j == 0),
                                             jnp.logical_and(i == gi - 1, j == gj - 1))
            start_comm()
        dy_half = dy_ref[...] * 0.5
        width = min(tn, MXU_DIM)
        for c0 in range(0, tn, width):
            cols = slice(c0, c0 + width)
            da = lax.dot_general(dy_half, wd_ref[cols, :], _DIMS["nt"], preferred_element_type=F32)
            g, u = gu_ref[0, :, cols], gu_ref[1, :, cols]
            s = _sigmoid(g)
            out_ref[0, :, cols] = (da * u * (s * (1.0 + g * (1.0 - s)))).astype(BF16)
            out_ref[1, :, cols] = (da * (g * s)).astype(BF16)
        if comm is not None:
            wait_comm()

    pair = pl.BlockSpec((2, tm, tn), lambda i, j: (0, i, j))
    in_specs = [pl.BlockSpec((tm, d), lambda i, j: (i, 0)), pl.BlockSpec((tn, d), lambda i, j: (j, 0)), pair]
    out_shape = jax.ShapeDtypeStruct((2, t, f), BF16)
    if comm is None:
        return pl.pallas_call(kern, name=name, grid=(gi, gj), in_specs=in_specs, out_specs=pair, out_shape=out_shape,
                              compiler_params=_params(("parallel", "parallel")))(dy, wd, gu)
    res_all = pl.pallas_call(
        kern, name=name, grid=(gi, gj), in_specs=in_specs + [ANY] * len(comm.srcs),
        out_specs=[pair] + [ANY] * len(comm.outs), out_shape=[out_shape] + comm.outs,
        scratch_shapes=comm.scratch(), compiler_params=_params(("arbitrary", "arbitrary")),
    )(dy, wd, gu, *comm.srcs)
    return res_all[0], list(res_all[1:])


def _rowwise(body, ins, params, outs, accs, name, tm, comm=None, into=None):
    rows = ins[0][0].shape[0]
    assert rows % tm == 0, (name, rows, tm)
    in_specs = []
    for arr, off, width in ins:
        assert off % width == 0 and arr.shape[0] == rows, (name, arr.shape, off, width)
        in_specs.append(pl.BlockSpec((tm, width), functools.partial(lambda i, c: (i, c), c=off // width)))
    for p in params:
        in_specs.append(pl.BlockSpec(p.shape, functools.partial(lambda i, nd: (0,) * nd, nd=p.ndim)))
    out_specs = [pl.BlockSpec((tm, w), lambda i: (i, 0)) for w, _ in outs]
    out_specs += [pl.BlockSpec(s, functools.partial(lambda i, nd: (0,) * nd, nd=len(s))) for s, _ in accs]
    out_shape = [jax.ShapeDtypeStruct((rows, w), d) for w, d in outs]
    out_shape += [jax.ShapeDtypeStruct(s, d) for s, d in accs]
    n_in, n_par, n_out = len(ins), len(params), len(outs)
    carried = []
    if into is not None:
        buf, total, col = into
        assert comm is None and col % outs[0][0] == 0, (name, col, outs[0])
        out_specs[0] = pl.BlockSpec((tm, outs[0][0]), functools.partial(lambda i, c: (i, c), c=col // outs[0][0]))
        out_shape[0] = jax.ShapeDtypeStruct((rows, total), outs[0][1])
        carried = [] if buf is None else [buf]

    steps = rows // tm

    def kern(*refs):
        in_refs, out_refs, _, c_src, c_out, c_sem = _split_refs(refs, n_in + n_par + len(carried),
                                                                 n_out + len(accs), comm)
        in_refs = in_refs[:n_in + n_par]
        first = pl.program_id(0) == 0
        if comm is not None:
            start_comm, wait_comm = comm.run(c_src, c_out, c_sem, first, pl.program_id(0) == steps - 1)
            start_comm()
        res = body(*[r[...] for r in in_refs])
        for r, v in zip(out_refs[:n_out], res[:n_out]):
            r[...] = v.astype(r.dtype)
        for r, v in zip(out_refs[n_out:], res[n_out:]):
            @pl.when(first)
            def _(r=r, v=v):
                r[...] = v.astype(r.dtype)

            @pl.when(jnp.logical_not(first))
            def _(r=r, v=v):
                r[...] += v.astype(r.dtype)
        if comm is not None:
            wait_comm()

    args = [a for a, _, _ in ins] + list(params)
    if comm is None:
        return pl.pallas_call(
            kern, name=name, grid=(steps,), in_specs=in_specs + [ANY] * len(carried), out_specs=out_specs,
            out_shape=out_shape, input_output_aliases={len(args): 0} if carried else {},
            compiler_params=_params(("arbitrary",) if accs else ("parallel",)),
        )(*args, *carried)
    res_all = pl.pallas_call(
        kern, name=name, grid=(steps,), in_specs=in_specs + [ANY] * len(comm.srcs),
        out_specs=out_specs + [ANY] * len(comm.outs), out_shape=out_shape + comm.outs,
        scratch_shapes=comm.scratch(), input_output_aliases=comm.aliases(len(args), len(out_shape)),
        compiler_params=_params(("arbitrary",)),
    )(*args, *comm.srcs)
    return list(res_all[:len(out_shape)]), list(res_all[len(out_shape):])


def _whole(x):
    return (x, 0, x.shape[1])


def _first(res, comm):
    return res[0] if comm is None else (res[0][0], res[1])


def _rms_fwd(x, gain, name, comm=None):
    def body(xv, g):
        r = lax.rsqrt(jnp.mean(xv * xv, axis=-1, keepdims=True) + EPS)
        return ((xv * r) * g,)

    return _first(_rowwise(body, [_whole(x)], [gain], [(x.shape[1], BF16)], [], name, _tile(x.shape[0], 256, 16),
                           comm=comm), comm)


def _rms_bwd(x, gain, dh, dres, name, comm=None):
    d = x.shape[1]

    def body(*vals):
        if dres is None:
            xv, dhv, g = vals
        else:
            xv, dhv, drv, g = vals
        r = lax.rsqrt(jnp.mean(xv * xv, axis=-1, keepdims=True) + EPS)
        xh = xv * r
        w = dhv * g
        dx = r * (w - xh * jnp.mean(w * xh, axis=-1, keepdims=True))
        if dres is not None:
            dx = drv + dx
        return dx, dx, jnp.sum(dhv * xh, axis=0, keepdims=True)

    ins = [_whole(x), _whole(dh)] + ([_whole(dres)] if dres is not None else [])
    return _rowwise(body, ins, [gain], [(d, F32), (d, BF16)], [((1, d), F32)], name, _tile(x.shape[0], 256, 16),
                    comm=comm)


def _loss_head(y, target, name):
    d = y.shape[1]

    def body(yv, tv):
        e = yv - tv
        part = jnp.sum(jnp.sum(e * e, axis=1, keepdims=True), axis=0, keepdims=True)
        return e / d, e / d, jnp.broadcast_to((0.5 / d) * part, (1, LANES))

    return _rowwise(body, [_whole(y), _whole(target)], [], [(d, F32), (d, BF16)], [((1, LANES), F32)], name,
                    _tile(y.shape[0], 256, 16))


def _head_mean(v, head_dim):
    cols = []
    lane = lax.broadcasted_iota(jnp.int32, (1, LANES), 1)
    for j in range(v.shape[1] // LANES):
        blk = v[:, j * LANES:(j + 1) * LANES]
        if head_dim == LANES:
            m = jnp.sum(blk, axis=-1, keepdims=True)
            cols.append(jnp.broadcast_to(m, blk.shape))
        else:
            lo = jnp.sum(jnp.where(lane < head_dim, blk, 0.0), axis=-1, keepdims=True)
            hi = jnp.sum(jnp.where(lane >= head_dim, blk, 0.0), axis=-1, keepdims=True)
            cols.append(jnp.where(lane < head_dim, lo, hi))
    return jnp.concatenate(cols, axis=1) / head_dim


def _head_norm(xv, g, head_dim):
    r = lax.rsqrt(_head_mean(xv * xv, head_dim) + EPS)
    return (xv * r) * g


def _head_norm_bwd(xv, g, dy, head_dim):
    r = lax.rsqrt(_head_mean(xv * xv, head_dim) + EPS)
    xh = xv * r
    w = dy * g
    dx = r * (w - xh * _head_mean(w * xh, head_dim))
    return dx, jnp.sum(dy * xh, axis=0, keepdims=True)


def _log_sigmoid(x):
    return jnp.minimum(x, 0.0) - jnp.log(1.0 + jnp.exp(-jnp.abs(x)))


def _shift_rows(v, sh, down):
    n = v.shape[0]
    row = lax.broadcasted_iota(jnp.int32, (n, 1), 0)
    if down:
        return jnp.where(row >= sh, pltpu.roll(v, sh, 0), 0.0)
    return jnp.where(row < n - sh, pltpu.roll(v, n - sh, 0), 0.0)


def _scan_rows(v, down):
    sh = 1
    while sh < v.shape[0]:
        v = v + _shift_rows(v, sh, down)
        sh *= 2
    return v


def _window_sum(v, steps, down):
    for s in range(steps):
        v = v + _shift_rows(v, 2 ** s, down)
    return v


def _fox_fwd(qn, kn, vb, cks, batch, seq, name, comm=None):
    t, width = qn.shape
    pairs = width // LANES
    tq = _tile(seq, 512, LANES)
    nq = seq // tq

    def kern(*refs):
        (q_ref, k_ref, v_ref, c_ref), (o_ref, lse_ref), _, c_src, c_out, c_sem = _split_refs(refs, 4, 2, comm)
        qi = pl.program_id(2)
        if comm is not None:
            b_id, p_id = pl.program_id(0), pl.program_id(1)
            first = jnp.logical_and(jnp.logical_and(b_id == 0, p_id == 0), qi == 0)
            last = jnp.logical_and(jnp.logical_and(b_id == batch - 1, p_id == pairs - 1), qi == nq - 1)
            start_comm, wait_comm = comm.run(c_src, c_out, c_sem, first, last)
            start_comm()
        lane = lax.broadcasted_iota(jnp.int32, (1, LANES), 1)
        rowi = lax.broadcasted_iota(jnp.int32, (tq, tq), 0)
        coli = lax.broadcasted_iota(jnp.int32, (tq, tq), 1)
        q_all = q_ref[...]
        o_heads, lse_heads = [], []
        for h in range(2):
            hm = (lane < FOX_HEAD_DIM) if h == 0 else (lane >= FOX_HEAD_DIM)
            q = jnp.where(hm, q_all, jnp.zeros_like(q_all))

            def step(j, carry, h=h, q=q, diagonal=False):
                m, l, acc = carry
                start = pl.multiple_of(j * tq, tq)
                k = k_ref[pl.ds(start, tq), :]
                v = v_ref[pl.ds(start, tq), :]
                s = lax.dot_general(q, k, _DIMS["nt"], preferred_element_type=F32)
                s = s - c_ref[0, h:h + 1, pl.ds(start, tq)]
                if diagonal:
                    s = jnp.where(rowi >= coli, s, NEG)
                m_new = jnp.maximum(m, jnp.max(s, axis=-1, keepdims=True))
                alpha = jnp.exp(m - m_new)
                p = jnp.exp(s - m_new)
                l = alpha * l + jnp.sum(p, axis=-1, keepdims=True)
                acc = alpha * acc + lax.dot_general(p.astype(BF16), v, _DIMS["nn"], preferred_element_type=F32)
                return m_new, l, acc

            init = (jnp.full((tq, 1), NEG, F32), jnp.zeros((tq, 1), F32), jnp.zeros((tq, LANES), F32))
            m, l, acc = step(qi, lax.fori_loop(0, qi, step, init), diagonal=True)
            o_heads.append(acc / l)
            lse_heads.append(jnp.broadcast_to(m + jnp.log(l), (tq, LANES)))
        o_ref[...] = jnp.where(lane < FOX_HEAD_DIM, o_heads[0], o_heads[1])
        lse_ref[...] = jnp.where(lane < FOX_HEAD_DIM, lse_heads[0], lse_heads[1])
        if comm is not None:
            wait_comm()

    q_spec = pl.BlockSpec((tq, LANES), lambda b, hp, qi: (b * nq + qi, hp))
    kv_spec = pl.BlockSpec((seq, LANES), lambda b, hp, qi: (b, hp))
    c_spec = pl.BlockSpec((1, 2, seq), lambda b, hp, qi: (b * pairs + hp, 0, 0))
    in_specs = [q_spec, kv_spec, kv_spec, c_spec]
    out_shape = [jax.ShapeDtypeStruct((t, width), F32), jax.ShapeDtypeStruct((t, width), F32)]
    if comm is None:
        return pl.pallas_call(
            kern, name=name, grid=(batch, pairs, nq), in_specs=in_specs, out_specs=[q_spec, q_spec],
            out_shape=out_shape, compiler_params=_params(("parallel", "parallel", "arbitrary")),
        )(qn, kn, vb, cks)
    res_all = pl.pallas_call(
        kern, name=name, grid=(batch, pairs, nq), in_specs=in_specs + [ANY] * len(comm.srcs),
        out_specs=[q_spec, q_spec] + [ANY] * len(comm.outs), out_shape=out_shape + comm.outs,
        scratch_shapes=comm.scratch(), compiler_params=_params(("arbitrary", "arbitrary", "arbitrary")),
    )(qn, kn, vb, cks, *comm.srcs)
    return list(res_all[:2]), list(res_all[2:])


def _fox_bwd(qn, kn, vb, cks, o, lse, do, dz, dv_col, batch, seq, name, comm):
    t, width = qn.shape
    pairs = width // LANES
    tk = _tile(seq, 512, LANES)
    nk = seq // tk
    scale = FOX_HEAD_DIM ** -0.5

    def kern(*refs):
        ins, outs, _, c_src, c_out, c_sem = _split_refs(refs, 8, 5, comm)
        k_ref, v_ref, q_ref, c_ref, o_ref, lse_ref, do_ref, _ = ins
        dk_ref, dv_ref, dq_ref, dc_ref, dcq_ref = outs
        kj = pl.program_id(2)
        if comm is not None:
            b_id, p_id = pl.program_id(0), pl.program_id(1)
            first = jnp.logical_and(jnp.logical_and(b_id == 0, p_id == 0), kj == 0)
            last = jnp.logical_and(jnp.logical_and(b_id == batch - 1, p_id == pairs - 1), kj == nk - 1)
            start_comm, wait_comm = comm.run(c_src, c_out, c_sem, first, last)
            start_comm()
        lane = lax.broadcasted_iota(jnp.int32, (1, LANES), 1)
        rowi = lax.broadcasted_iota(jnp.int32, (tk, tk), 0)
        coli = lax.broadcasted_iota(jnp.int32, (tk, tk), 1)

        @pl.when(kj == 0)
        def _():
            dq_ref[...] = jnp.zeros_like(dq_ref)
            dcq_ref[...] = jnp.zeros_like(dcq_ref)

        k_all = k_ref[...]
        v_all = v_ref[...]
        kstart = pl.multiple_of(kj * tk, tk)
        dk_heads, dv_heads = [], []
        for h in range(2):
            hm = (lane < FOX_HEAD_DIM) if h == 0 else (lane >= FOX_HEAD_DIM)
            kh = jnp.where(hm, k_all, jnp.zeros_like(k_all))
            vh = jnp.where(hm, v_all, jnp.zeros_like(v_all))
            c_row = c_ref[0, h:h + 1, pl.ds(kstart, tk)]

            def step(qi, carry, h=h, hm=hm, kh=kh, vh=vh, c_row=c_row, diagonal=False):
                dk_acc, dv_acc, dc_acc = carry
                start = pl.multiple_of(qi * tk, tk)
                q = q_ref[pl.ds(start, tk), :]
                dov = do_ref[pl.ds(start, tk), :]
                ov = o_ref[pl.ds(start, tk), :]
                lse_col = jnp.max(jnp.where(hm, lse_ref[pl.ds(start, tk), :], NEG), axis=-1, keepdims=True)
                dob = jnp.where(hm, dov, 0.0).astype(BF16)
                dcol = jnp.sum(dob.astype(F32) * ov, axis=-1, keepdims=True)
                s = lax.dot_general(q, kh, _DIMS["nt"], preferred_element_type=F32) - c_row
                p = jnp.exp(s - lse_col)
                if diagonal:
                    p = jnp.where(rowi >= coli, p, 0.0)
                dp = lax.dot_general(dob, vh, _DIMS["nt"], preferred_element_type=F32)
                ds = p * (dp - dcol)
                ds_b = ds.astype(BF16)
                dv_acc = dv_acc + lax.dot_general(p.astype(BF16), dob, _DIMS["tn"], preferred_element_type=F32)
                dk_acc = dk_acc + lax.dot_general(ds_b, q, _DIMS["tn"], preferred_element_type=F32)
                dq_part = lax.dot_general(ds_b * scale, kh, _DIMS["nn"], preferred_element_type=F32)
                dq_ref[pl.ds(start, tk), :] += dq_part
                dcq_ref[pl.ds(start, tk), :] += jnp.where(hm, jnp.sum(ds, axis=-1, keepdims=True), 0.0)
                dc_acc = dc_acc - jnp.sum(ds, axis=0, keepdims=True)
                return dk_acc, dv_acc, dc_acc

            init = (jnp.zeros((tk, LANES), F32), jnp.zeros((tk, LANES), F32), jnp.zeros((1, tk), F32))
            dk_acc, dv_acc, dc_acc = lax.fori_loop(kj + 1, nk, step, step(kj, init, diagonal=True))
            dk_heads.append(dk_acc)
            dv_heads.append(dv_acc)
            dc_ref[0, h:h + 1, pl.ds(kstart, tk)] = dc_acc
        dk_ref[...] = jnp.where(lane < FOX_HEAD_DIM, dk_heads[0], dk_heads[1])
        dv_ref[...] = jnp.where(lane < FOX_HEAD_DIM, dv_heads[0], dv_heads[1]).astype(BF16)
        if comm is not None:
            wait_comm()

    kv_spec = pl.BlockSpec((tk, LANES), lambda b, hp, kj: (b * nk + kj, hp))
    full_spec = pl.BlockSpec((seq, LANES), lambda b, hp, kj: (b, hp))
    c_spec = pl.BlockSpec((1, 2, seq), lambda b, hp, kj: (b * pairs + hp, 0, 0))
    dv_spec = pl.BlockSpec((tk, LANES), lambda b, hp, kj: (b * nk + kj, hp + dv_col // LANES))
    in_specs = [kv_spec, kv_spec, full_spec, c_spec, full_spec, full_spec, full_spec, ANY]
    out_specs = [kv_spec, dv_spec, full_spec, c_spec, full_spec]
    out_shape = [jax.ShapeDtypeStruct((t, width), F32), jax.ShapeDtypeStruct(dz.shape, BF16),
                 jax.ShapeDtypeStruct((t, width), F32), jax.ShapeDtypeStruct(cks.shape, F32),
                 jax.ShapeDtypeStruct((t, width), F32)]
    res_all = pl.pallas_call(
        kern, name=name, grid=(batch, pairs, nk), in_specs=in_specs + [ANY] * len(comm.srcs),
        out_specs=out_specs + [ANY] * len(comm.outs), out_shape=out_shape + comm.outs,
        scratch_shapes=comm.scratch(), input_output_aliases={7: 1},
        compiler_params=_params(("arbitrary", "arbitrary", "arbitrary")),
    )(kn, vb, qn, cks, o, lse, do, dz, *comm.srcs)
    return list(res_all[:5]), list(res_all[5:])


def _mem_fwd(qn, kn, vb, batch, seq, mlen, name):
    t, width = qn.shape
    heads = width // LANES
    tq = _tile(seq, 512, LANES)
    nq = seq // tq
    scale = MEM_HEAD_DIM ** -0.5

    def kern(q_ref, k_ref, v_ref, o_ref):
        s = lax.dot_general(q_ref[...], k_ref[...], _DIMS["nt"], preferred_element_type=F32) * scale
        e = jnp.exp(s - jnp.max(s, axis=-1, keepdims=True))
        p = e / jnp.sum(e, axis=-1, keepdims=True)
        o_ref[...] = lax.dot_general(p.astype(BF16), v_ref[...], _DIMS["nn"], preferred_element_type=F32)

    q_spec = pl.BlockSpec((tq, LANES), lambda b, h, qi: (b * nq + qi, h))
    kv_spec = pl.BlockSpec((mlen, LANES), lambda b, h, qi: (b, h))
    return pl.pallas_call(
        kern, name=name, grid=(batch, heads, nq), in_specs=[q_spec, kv_spec, kv_spec], out_specs=q_spec,
        out_shape=jax.ShapeDtypeStruct((t, width), F32),
        compiler_params=_params(("parallel", "parallel", "parallel")),
    )(qn, kn, vb)


def _mem_bwd(qn, kn, vb, do, batch, seq, mlen, name):
    t, width = qn.shape
    heads = width // LANES
    tq = _tile(seq, 512, LANES)
    nq = seq // tq
    scale = MEM_HEAD_DIM ** -0.5

    def kern(q_ref, k_ref, v_ref, do_ref, dq_ref, dk_ref, dv_ref):
        qi = pl.program_id(2)
        q, k, v = q_ref[...], k_ref[...], v_ref[...]
        dob = do_ref[...].astype(BF16)
        s = lax.dot_general(q, k, _DIMS["nt"], preferred_element_type=F32) * scale
        e = jnp.exp(s - jnp.max(s, axis=-1, keepdims=True))
        p = e / jnp.sum(e, axis=-1, keepdims=True)
        dp = lax.dot_general(dob, v, _DIMS["nt"], preferred_element_type=F32)
        ds = p * (dp - jnp.sum(p * dp, axis=-1, keepdims=True))
        dsb = (ds * scale).astype(BF16)
        dq_ref[...] = lax.dot_general(dsb, k, _DIMS["nn"], preferred_element_type=F32)
        dk = lax.dot_general(dsb, q, _DIMS["tn"], preferred_element_type=F32)
        dv = lax.dot_general(p.astype(BF16), dob, _DIMS["tn"], preferred_element_type=F32)

        @pl.when(qi == 0)
        def _():
            dk_ref[...] = dk
            dv_ref[...] = dv

        @pl.when(qi > 0)
        def _():
            dk_ref[...] += dk
            dv_ref[...] += dv

    q_spec = pl.BlockSpec((tq, LANES), lambda b, h, qi: (b * nq + qi, h))
    kv_spec = pl.BlockSpec((mlen, LANES), lambda b, h, qi: (b, h))
    return pl.pallas_call(
        kern, name=name, grid=(batch, heads, nq), in_specs=[q_spec, kv_spec, kv_spec, q_spec],
        out_specs=[q_spec, kv_spec, kv_spec],
        out_shape=[jax.ShapeDtypeStruct((t, width), F32), jax.ShapeDtypeStruct(kn.shape, F32),
                   jax.ShapeDtypeStruct(kn.shape, F32)],
        compiler_params=_params(("parallel", "parallel", "arbitrary")),
    )(qn, kn, vb, do)


def _position():
    return lax.axis_index("x"), lax.axis_index("y"), lax.axis_index("c")


def _other_chips(x, y):
    return [(1 - x, y), (x, 1 - y), (1 - x, 1 - y)]


def _remote(src, dst, sems, to):
    send, recv = sems.remote()
    return pltpu.make_async_remote_copy(src_ref=src, dst_ref=dst, send_sem=send, recv_sem=recv,
                                        device_id=to, device_id_type=MESH)


def _gather_first(shards):
    n = len(shards)

    def plan(src, out, sems):
        x, y, c = _position()
        me = 4 * x + 2 * y + c
        peers = [(x, y, 1 - c)] + [(px, py, c) for px, py in _other_chips(x, y)]
        copies = []
        for i in range(n):
            copies.append(pltpu.make_async_copy(src[i], out[i].at[me], sems.one_local()))
            copies += [_remote(src[i], out[i].at[me], sems, to) for to in peers]
        return copies

    outs = [jax.ShapeDtypeStruct((N_DEV,) + s.shape, s.dtype) for s in shards]
    return _Comm(shards, outs, 4 * n, n, plan)


def _gather_second(bufs):
    n = len(bufs)

    def plan(src, out, sems):
        x, y, c = _position()
        copies = []
        for i in range(n):
            for px, py in _other_chips(x, y):
                blk = out[i].at[4 * px + 2 * py + c]
                copies.append(_remote(blk, blk, sems, (x, y, 1 - c)))
        return copies

    outs = [jax.ShapeDtypeStruct(b.shape, b.dtype) for b in bufs]
    return _Comm(bufs, outs, 3 * n, 0, plan, in_place=True)


def _scatter_first(partials):
    n = len(partials)

    def plan(src, out, sems):
        x, y, c = _position()
        return [_remote(src[i].at[2 * q + (1 - c)], out[i].at[q], sems, (x, y, 1 - c))
                for i in range(n) for q in range(4)]

    outs = [jax.ShapeDtypeStruct((4,) + g.shape[1:], g.dtype) for g in partials]
    return _Comm(partials, outs, 4 * n, 0, plan)


def _scatter_second(halves):
    n = len(halves)

    def plan(src, out, sems):
        x, y, c = _position()
        return [_remote(src[i].at[2 * px + py], out[i].at[r], sems, (px, py, c))
                for i in range(n) for r, (px, py) in enumerate(_other_chips(x, y))]

    outs = [jax.ShapeDtypeStruct((3,) + h.shape[1:], h.dtype) for h in halves]
    return _Comm(halves, outs, 3 * n, 0, plan)


def _comm_call(comm, name):
    ns, no = len(comm.srcs), len(comm.outs)

    def body(*refs):
        copies = comm.plan(list(refs[:ns]), list(refs[ns:ns + no]), _Sems(*refs[ns + no:]))
        for cp in copies:
            cp.start()
        for cp in copies:
            cp.wait()

    return pl.pallas_call(
        body, name=name, in_specs=[ANY] * ns, out_specs=[ANY] * no, out_shape=comm.outs,
        scratch_shapes=comm.scratch(), input_output_aliases=comm.aliases(0, 0),
    )(*comm.srcs)


def _gather_first_relayed(shard, name):
    def body(x_ref, out_ref, send, recv, local):
        sems = _Sems(send, recv, local)
        x, y, c = _position()

        def slot(px, py):
            return out_ref.at[4 * px + 2 * py + c]

        mine = pltpu.make_async_copy(x_ref, slot(x, y), sems.one_local())
        to_sibling = _remote(x_ref, slot(x, y), sems, (x, y, 1 - c))
        to_x = _remote(x_ref, slot(x, y), sems, (1 - x, y, c))
        to_y = _remote(x_ref, slot(x, y), sems, (x, 1 - y, c))
        for cp in (mine, to_sibling, to_x, to_y):
            cp.start()
        relay_send, relay_recv = sems.remote()

        def relay(arrived, unused, origin, to):
            arrived.wait_recv()
            blk = slot(*origin)
            fwd = pltpu.make_async_remote_copy(src_ref=blk, dst_ref=blk, send_sem=relay_send, recv_sem=relay_recv,
                                               device_id=(*to, c), device_id_type=MESH)
            fwd.start()
            fwd.wait()
            arrived.wait_send()
            unused.wait()

        _when(c == 0, lambda: relay(to_y, to_x, (x, 1 - y), (1 - x, y)))
        _when(c == 1, lambda: relay(to_x, to_y, (1 - x, y), (x, 1 - y)))
        mine.wait()
        to_sibling.wait()

    return pl.pallas_call(
        body, name=name, in_specs=[ANY], out_specs=ANY,
        out_shape=jax.ShapeDtypeStruct((N_DEV,) + shard.shape, shard.dtype),
        scratch_shapes=[pltpu.SemaphoreType.DMA((4,)), pltpu.SemaphoreType.DMA((4,)), pltpu.SemaphoreType.DMA((1,))],
    )(shard)


def _all_gather_vmem(shard, name):
    first = _gather_first([shard])

    def body(x_ref, out_ref, send, recv, local):
        sems = _Sems(send, recv, local)
        copies = first.plan([x_ref], [out_ref], sems)
        for cp in copies:
            cp.start()
        x, y, c = _position()
        passed = []
        for j, (px, py) in enumerate(_other_chips(x, y)):
            copies[2 + j].wait_recv()
            blk = out_ref.at[4 * px + 2 * py + c]
            fwd = _remote(blk, blk, sems, (x, y, 1 - c))
            fwd.start()
            passed.append(fwd)
        copies[0].wait()
        copies[1].wait()
        for cp in copies[2:]:
            cp.wait_send()
        for cp in passed:
            cp.wait()

    vm = pl.BlockSpec(memory_space=pltpu.VMEM)
    return pl.pallas_call(
        body, name=name, in_specs=[vm], out_specs=vm,
        out_shape=jax.ShapeDtypeStruct((N_DEV,) + shard.shape, shard.dtype),
        scratch_shapes=[pltpu.SemaphoreType.DMA((7,)), pltpu.SemaphoreType.DMA((7,)), pltpu.SemaphoreType.DMA((1,))],
    )(shard)


def _rs_pair_add(partial, landed, where, name):
    _, r, c = partial.shape
    tm = _tile(r, 256, 16)

    def kern(where_ref, g_ref, l_ref, own_ref, hb_ref):
        s = g_ref[...] + l_ref[...]
        hb_ref[...] = s.astype(BF16)

        @pl.when(pl.program_id(1) == where_ref[1])
        def _():
            own_ref[...] = s

    grid_spec = pltpu.PrefetchScalarGridSpec(
        num_scalar_prefetch=1, grid=(r // tm, 4),
        in_specs=[pl.BlockSpec((None, tm, c), lambda i, q, wr: (2 * q + wr[0], i, 0)),
                  pl.BlockSpec((None, tm, c), lambda i, q, wr: (q, i, 0))],
        out_specs=[pl.BlockSpec((tm, c), lambda i, q, wr: (i, 0)),
                   pl.BlockSpec((None, tm, c), lambda i, q, wr: (q, i, 0))])
    return pl.pallas_call(
        kern, name=name, grid_spec=grid_spec,
        out_shape=[jax.ShapeDtypeStruct((r, c), F32), jax.ShapeDtypeStruct((4, r, c), BF16)],
        compiler_params=_params(("parallel", "arbitrary")),
    )(where, partial, landed)


def _adam_math(g, w, m, v):
    m = ADAM_B1 * m + (1.0 - ADAM_B1) * g
    v = ADAM_B2 * v + (1.0 - ADAM_B2) * (g * g)
    m_hat = m / (1.0 - ADAM_B1 ** ADAM_STEP)
    v_hat = v / (1.0 - ADAM_B2 ** ADAM_STEP)
    delta = -ADAM_LR * (m_hat / (jnp.sqrt(v_hat) + ADAM_EPS) + ADAM_WD * w)
    return delta, m, v


def _rs_finish_adam(own, landed, w, m, v, name):
    r, c = own.shape
    tm = _tile(r, 128, 16)

    def kern(h_ref, l_ref, w_ref, m_ref, v_ref, *outs):
        g = ((h_ref[...] + l_ref[0].astype(F32)) + l_ref[1].astype(F32)) + l_ref[2].astype(F32)
        delta, m_new, v_new = _adam_math(g, w_ref[...], m_ref[...], v_ref[...])
        for ref, val in zip(outs, (g, delta, m_new, v_new)):
            ref[...] = val

    flat = pl.BlockSpec((tm, c), lambda i: (i, 0))
    return pl.pallas_call(
        kern, name=name, grid=(r // tm,), in_specs=[flat, pl.BlockSpec((3, tm, c), lambda i: (0, i, 0)), flat, flat, flat],
        out_specs=[flat] * 4, out_shape=[jax.ShapeDtypeStruct((r, c), F32)] * 4,
        compiler_params=_params(("parallel",)))(own, landed, w, m, v)


def _allreduce_adam(gathered, w, m, v, name):
    _, r, c = gathered.shape

    def kern(a_ref, w_ref, m_ref, v_ref, g_out, d_out, m_out, v_out):
        g = a_ref[0]
        for j in range(1, N_DEV):
            g = g + a_ref[j]
        delta, m_new, v_new = _adam_math(g, w_ref[...], m_ref[...], v_ref[...])
        g_out[...] = g
        d_out[...] = delta
        m_out[...] = m_new
        v_out[...] = v_new

    return pl.pallas_call(
        kern, name=name, out_shape=[jax.ShapeDtypeStruct((r, c), F32)] * 4,
        compiler_params=_params(),
    )(gathered, w, m, v)


def _cols_from_blocks(g):
    n, k, nb = g.shape
    return jnp.transpose(g, (1, 0, 2)).reshape(k, n * nb)


def _blocks_from_cols(w):
    k, n = w.shape
    return jnp.transpose(w.reshape(k, N_DEV, n // N_DEV), (1, 0, 2))


def _pack_small(parts):
    flat = []
    for p in parts:
        v = p.reshape(-1)
        flat.append(jnp.pad(v, (0, (-v.shape[0]) % (8 * LANES))))
    return jnp.concatenate(flat).reshape(-1, LANES)


def _unpack_small(buf, like):
    out, pos = [], 0
    flat = buf.reshape(-1)
    for p in like:
        size = p.size
        out.append(flat[pos:pos + size].reshape(p.shape))
        pos += size + (-size) % (8 * LANES)
    return out


def kernel(x, mem, ffn1_norm, ffn1_w_gate_up, ffn1_w_down, mix_norm, mem_norm, w_in, b_forget, pool_w, pool_scale, w_pool_up, fox_q_norm, fox_k_norm, w_fox_o, w_mem_kv, mem_q_norm, mem_k_norm, w_mem_o, w_out, ffn2_norm, ffn2_w_gate_up, ffn2_w_down, loss_target, m_ffn1_norm, m_ffn1_w_gate_up, m_ffn1_w_down, m_mix_norm, m_mem_norm, m_w_in, m_b_forget, m_pool_w, m_pool_scale, m_w_pool_up, m_fox_q_norm, m_fox_k_norm, m_w_fox_o, m_w_mem_kv, m_mem_q_norm, m_mem_k_norm, m_w_mem_o, m_w_out, m_ffn2_norm, m_ffn2_w_gate_up, m_ffn2_w_down, v_ffn1_norm, v_ffn1_w_gate_up, v_ffn1_w_down, v_mix_norm, v_mem_norm, v_w_in, v_b_forget, v_pool_w, v_pool_scale, v_w_pool_up, v_fox_q_norm, v_fox_k_norm, v_w_fox_o, v_w_mem_kv, v_mem_q_norm, v_mem_k_norm, v_w_mem_o, v_w_out, v_ffn2_norm, v_ffn2_w_gate_up, v_ffn2_w_down):
    names = ["ffn1_norm", "ffn1_w_gate_up", "ffn1_w_down", "mix_norm", "mem_norm", "w_in", "b_forget", "pool_w",
             "pool_scale", "w_pool_up", "fox_q_norm", "fox_k_norm", "w_fox_o", "w_mem_kv", "mem_q_norm",
             "mem_k_norm", "w_mem_o", "w_out", "ffn2_norm", "ffn2_w_gate_up", "ffn2_w_down"]
    w_args = [ffn1_norm, ffn1_w_gate_up, ffn1_w_down, mix_norm, mem_norm, w_in, b_forget, pool_w, pool_scale,
              w_pool_up, fox_q_norm, fox_k_norm, w_fox_o, w_mem_kv, mem_q_norm, mem_k_norm, w_mem_o, w_out,
              ffn2_norm, ffn2_w_gate_up, ffn2_w_down]
    m_args = [m_ffn1_norm, m_ffn1_w_gate_up, m_ffn1_w_down, m_mix_norm, m_mem_norm, m_w_in, m_b_forget, m_pool_w,
              m_pool_scale, m_w_pool_up, m_fox_q_norm, m_fox_k_norm, m_w_fox_o, m_w_mem_kv, m_mem_q_norm,
              m_mem_k_norm, m_w_mem_o, m_w_out, m_ffn2_norm, m_ffn2_w_gate_up, m_ffn2_w_down]
    v_args = [v_ffn1_norm, v_ffn1_w_gate_up, v_ffn1_w_down, v_mix_norm, v_mem_norm, v_w_in, v_b_forget, v_pool_w,
              v_pool_scale, v_w_pool_up, v_fox_q_norm, v_fox_k_norm, v_w_fox_o, v_w_mem_kv, v_mem_q_norm,
              v_mem_k_norm, v_w_mem_o, v_w_out, v_ffn2_norm, v_ffn2_w_gate_up, v_ffn2_w_down]
    W = dict(zip(names, w_args))
    M = dict(zip(names, m_args))
    V = dict(zip(names, v_args))

    batch, seq, d = x.shape
    mlen = mem.shape[1]
    t = batch * seq
    gate_w = 3 * d
    z_gate, z_q = 0, gate_w
    z_k, z_v = z_q + FOX_WIDTH, z_q + 2 * FOX_WIDTH
    z_u = z_q + 3 * FOX_WIDTH
    z_qm = z_u + POOL_WIDTH
    z_f = z_qm + MEM_WIDTH
    z_width = -(-(z_f + F_PAD) // 512) * 512

    x2d = x.reshape(t, d)
    mem2d = mem.reshape(batch * mlen, d)
    tgt2d = loss_target.reshape(t, d)

    big = ["ffn1_w_gate_up", "ffn1_w_down", "w_in", "w_pool_up", "w_fox_o", "w_mem_kv", "w_mem_o", "w_out",
           "ffn2_w_gate_up", "ffn2_w_down"]
    col_sharded = {"ffn1_w_gate_up", "ffn2_w_gate_up", "w_in", "w_pool_up", "w_fox_o", "w_mem_o"}
    mixer_small = ["w_pool_up", "w_fox_o", "w_mem_kv", "w_mem_o", "w_out"]
    shard = {n: W[n][0].astype(BF16) for n in big}

    def rows_of(g):
        return g.reshape(-1, g.shape[2])

    gu_caps_nt, gu_caps_tn = (1024, 1024, 1408), (1024, 1408, 2048)
    down_caps, dwd_caps = (1024, 1024, 2816), (1408, 1024, 2048)

    wgu1 = _gather_first_relayed(shard["ffn1_w_gate_up"], "ag_first")
    h1, (wgu1,) = _rms_fwd(x2d, W["ffn1_norm"], "ffn1_rms", comm=_gather_second([wgu1]))
    w_in_top, w_in_bot = shard["w_in"][:d // 2], shard["w_in"][d // 2:]
    (gu1, a1), bufs = _gate_up_swiglu(h1, wgu1, "ffn1_gu", comm=_gather_first([shard["ffn1_w_down"], w_in_top]))
    wd1_g, w_in_top_g = _comm_call(_gather_second(bufs), "ag_second")
    wd1 = rows_of(wd1_g)
    x1, (w_in_bot_g,) = _mm(a1, wd1, "nn", "ffn1_down", scale=0.5, res=x2d, caps=down_caps,
                            comm=_gather_first([w_in_bot]))
    h2, (w_in_bot_g,) = _rms_fwd(x1, W["mix_norm"], "mix_rms", comm=_gather_second([w_in_bot_g]))

    o_u, o_q, o_v = 0, POOL_WIDTH, POOL_WIDTH + 2 * FOX_WIDTH
    o_f = o_v + FOX_WIDTH
    o_qm = o_f + FOX_HEADS
    o_g = o_qm + MEM_WIDTH
    wi = jnp.concatenate([_cols_from_blocks(w_in_top_g), _cols_from_blocks(w_in_bot_g)], axis=0)
    w_in_pad = jnp.concatenate(
        [wi[:, o_g:o_g + gate_w], wi[:, o_q:o_q + 3 * FOX_WIDTH], wi[:, o_u:o_u + POOL_WIDTH],
         wi[:, o_qm:o_qm + MEM_WIDTH], wi[:, o_f:o_f + FOX_HEADS],
         jnp.zeros((d, z_width - z_f - FOX_HEADS), BF16)], axis=1)
    group_b = mixer_small + ["ffn2_w_down"]
    z, bufs = _mm(h2, w_in_pad, "nn", "mix_in", caps=(1024, 1536, 2048), comm=_gather_first([shard[n] for n in group_b]))

    pool_w_b = W["pool_w"][0].astype(BF16)

    def pool_fwd_body(u, pw, ps):
        row = lax.broadcasted_iota(jnp.int32, (seq, 1), 0)
        diffs, mixed = [], []
        for g in range(POOL_GROUPS):
            ug = u[:, g * POOL_GROUP_DIM:(g + 1) * POOL_GROUP_DIM]
            cnt = jnp.minimum(row + 1, POOL_WINDOWS[g]).astype(F32)
            diff = _window_sum(ug, g + 1, True) / cnt - ug
            diffs.append(diff)
            mixed.append(lax.dot_general(diff.astype(BF16), pw[g], _DIMS["nn"], preferred_element_type=F32))
        diffs = jnp.concatenate(diffs, axis=1)
        mixed = jnp.concatenate(mixed, axis=1)
        return mixed * ps, diffs, mixed

    ypp, pool_diff, pool_mixed = _rowwise(
        pool_fwd_body, [(z, z_u, POOL_WIDTH)], [pool_w_b, W["pool_scale"]],
        [(POOL_WIDTH, BF16), (POOL_WIDTH, BF16), (POOL_WIDTH, F32)], [], "pool_fwd", seq)

    gq = jnp.tile(W["fox_q_norm"], (1, FOX_HEADS))
    gk = jnp.tile(W["fox_k_norm"], (1, FOX_HEADS))
    b_pad = jnp.pad(W["b_forget"], ((0, 0), (0, F_PAD - FOX_HEADS)))

    def fox_prep_body(q, k, v, f, gqv, gkv, bv):
        qn_scaled = _head_norm(q, gqv, FOX_HEAD_DIM).astype(BF16) * FOX_HEAD_DIM ** -0.5
        return qn_scaled, _head_norm(k, gkv, FOX_HEAD_DIM), v, _log_sigmoid(f + bv)

    tm_e = _tile(t, 256, 16)
    (qn, kn, vb, logf), bufs = _rowwise(
        fox_prep_body, [(z, z_q, FOX_WIDTH), (z, z_k, FOX_WIDTH), (z, z_v, FOX_WIDTH), (z, z_f, F_PAD)],
        [gq, gk, b_pad], [(FOX_WIDTH, BF16), (FOX_WIDTH, BF16), (FOX_WIDTH, BF16), (F_PAD, F32)], [], "fox_prep", tm_e,
        comm=_gather_second(bufs))
    full = {n: (_cols_from_blocks(g) if n in col_sharded else rows_of(g)) for n, g in zip(group_b, bufs)}
    wd2 = full["ffn2_w_down"]
    y_pool = _mm(ypp, full["w_pool_up"], "nn", "pool_up")
    csum = _rowwise(lambda v: (_scan_rows(v, True),), [_whole(logf)], [], [(F_PAD, F32)], [], "fox_cumsum", seq)[0]
    cks = jnp.transpose(csum.reshape(batch, seq, F_PAD)[:, :, :FOX_HEADS], (0, 2, 1)).reshape(
        batch * FOX_HEADS // 2, 2, seq)
    (o_fox, lse), (wgu2,) = _fox_fwd(qn, kn, vb, cks, batch, seq, "fox_fwd",
                                     comm=_gather_first([shard["ffn2_w_gate_up"]]))
    y_fox = _mm(o_fox, full["w_fox_o"], "nn", "fox_o")

    memn = _rms_fwd(mem2d, W["mem_norm"], "mem_rms")
    kv = _mm(memn, full["w_mem_kv"], "nn", "mem_kv")
    gqm = jnp.tile(W["mem_q_norm"], (1, MEM_HEADS))
    gkm = jnp.tile(W["mem_k_norm"], (1, MEM_HEADS))
    qmn = _rowwise(lambda q, g: (_head_norm(q, g, MEM_HEAD_DIM),), [(z, z_qm, MEM_WIDTH)], [gqm],
                   [(MEM_WIDTH, BF16)], [], "memq_prep", tm_e)[0]
    kmn, vmb = _rowwise(lambda k, v, g: (_head_norm(k, g, MEM_HEAD_DIM), v),
                        [(kv, 0, MEM_WIDTH), (kv, MEM_WIDTH, MEM_WIDTH)], [gkm],
                        [(MEM_WIDTH, BF16), (MEM_WIDTH, BF16)], [], "memk_prep", _tile(batch * mlen, 256, 16))
    o_mem = _mem_fwd(qmn, kmn, vmb, batch, seq, mlen, "mem_fwd")
    y_mem = _mm(o_mem, full["w_mem_o"], "nn", "mem_o")

    def gate_fwd_body(gp, gf, gm, yp, yf, ym):
        return ((_sigmoid(gp) * yp + _sigmoid(gf) * yf) + _sigmoid(gm) * ym,)

    tm_g = _tile(t, 128, 16)
    (merged,), (wgu2,) = _rowwise(
        gate_fwd_body, [(z, 0, d), (z, d, d), (z, 2 * d, d), _whole(y_pool), _whole(y_fox), _whole(y_mem)],
        [], [(d, BF16)], [], "gate_fwd", tm_g, comm=_gather_second([wgu2]))
    x2 = _mm(merged, full["w_out"], "nn", "mix_out", res=x1)

    h3 = _rms_fwd(x2, W["ffn2_norm"], "ffn2_rms")
    gu2, a2 = _gate_up_swiglu(h3, wgu2, "ffn2_gu")
    x3 = _mm(a2, wd2, "nn", "ffn2_down", scale=0.5, res=x2, caps=down_caps)
    dy, dy_b, loss_part = _loss_head(x3, tgt2d, "loss")
    loss = lax.psum(loss_part[0, 0], ("x", "y", "c"))

    cx, cy, cc = _position()
    where = jnp.stack([cc, 2 * cx + cy]).astype(jnp.int32)
    G, own, landed = {}, {}, {}

    def row_blocks(g):
        return g.reshape(N_DEV, g.shape[0] // N_DEV, g.shape[1])

    def pair_add(n, partial, from_core):
        own[n], chip_sums = _rs_pair_add(partial, from_core, where, "rs_add_" + n)
        return chip_sums

    p_wd2 = row_blocks(_mm(a2, dy_b, "tn", "ffn2_dwd", scale=0.5, caps=dwd_caps))
    dgu2, (l_wd2,) = _swiglu_bwd_from_out(dy_b, wd2, gu2, "ffn2_dgu", comm=_scatter_first([p_wd2]))
    s_wd2 = pair_add("ffn2_w_down", p_wd2, l_wd2)
    p_wgu2, (landed["ffn2_w_down"],) = _mm(h3, dgu2, "tn", "ffn2_dwgu", caps=gu_caps_tn, out_blocks=True, halves="b",
                                           comm=_scatter_second([s_wd2]))
    dh3, (l_wgu2,) = _mm(dgu2, wgu2, "nt", "ffn2_dh", caps=gu_caps_nt, b_blocks=True, halves="a",
                         comm=_scatter_first([p_wgu2]))
    s_wgu2 = pair_add("ffn2_w_gate_up", p_wgu2, l_wgu2)
    dx2, dx2_b, G["ffn2_norm"] = _rms_bwd(x2, W["ffn2_norm"], dh3, dy, "ffn2_drms")

    dmerged = _mm(dx2_b, full["w_out"], "nt", "mix_out_dx")
    P = {"w_out": row_blocks(_mm(merged, dx2_b, "tn", "mix_out_dw"))}

    def gate_bwd_body(gp, gf, gm, yp, yf, ym, dm):
        outs_dl, outs_dy = [], []
        for gl, yv in ((gp, yp), (gf, yf), (gm, ym)):
            s = _sigmoid(gl)
            outs_dl.append((dm * yv) * (s * (1.0 - s)))
            outs_dy.append(dm * s)
        return (jnp.concatenate(outs_dl, axis=1), *outs_dy)

    dz, dy_pool, dy_fox, dy_mem = _rowwise(
        gate_bwd_body, [(z, 0, d), (z, d, d), (z, 2 * d, d), _whole(y_pool), _whole(y_fox), _whole(y_mem), _whole(dmerged)],
        [], [(gate_w, BF16), (d, BF16), (d, BF16), (d, BF16)], [], "gate_bwd", tm_g, into=(None, z_width, z_gate))

    dypp = _mm(dy_pool, full["w_pool_up"], "nt", "pool_up_dx")
    P["w_pool_up"] = _blocks_from_cols(_mm(ypp, dy_pool, "tn", "pool_up_dw"))

    def pool_bwd_body(dyv, mixed, diff, pw, ps):
        row = lax.broadcasted_iota(jnp.int32, (seq, 1), 0)
        d_scale = jnp.sum(dyv * mixed, axis=0, keepdims=True)
        dmix = (dyv * ps).astype(BF16)
        du, dpw = [], []
        for g in range(POOL_GROUPS):
            sl = slice(g * POOL_GROUP_DIM, (g + 1) * POOL_GROUP_DIM)
            dmg = dmix[:, sl]
            ddiff = lax.dot_general(dmg, pw[g], _DIMS["nt"], preferred_element_type=F32)
            dpw.append(lax.dot_general(diff[:, sl], dmg, _DIMS["tn"], preferred_element_type=F32))
            cnt = jnp.minimum(row + 1, POOL_WINDOWS[g]).astype(F32)
            du.append(_window_sum(ddiff / cnt, g + 1, False) - ddiff)
        return jnp.concatenate(du, axis=1), d_scale, jnp.concatenate(dpw, axis=0)

    dz, G["pool_scale"], d_pool_w = _rowwise(
        pool_bwd_body, [_whole(dypp), _whole(pool_mixed), _whole(pool_diff)], [pool_w_b, W["pool_scale"]],
        [(POOL_WIDTH, BF16)], [((1, POOL_WIDTH), F32), ((POOL_WIDTH, POOL_GROUP_DIM), F32)],
        "pool_bwd", seq, into=(dz, z_width, z_u))
    G["pool_w"] = d_pool_w.reshape(1, POOL_GROUPS, POOL_GROUP_DIM, POOL_GROUP_DIM)

    do_fox = _mm(dy_fox, full["w_fox_o"], "nt", "fox_o_dx")
    P["w_fox_o"] = _blocks_from_cols(_mm(o_fox, dy_fox, "tn", "fox_o_dw"))
    (dkn, dz, dqn, dcks, dcq), (landed["ffn2_w_gate_up"],) = _fox_bwd(
        qn, kn, vb, cks, o_fox, lse, do_fox, dz, z_v, batch, seq, "fox_bwd", comm=_scatter_second([s_wgu2]))
    dcs = jnp.transpose(dcks.reshape(batch, FOX_HEADS, seq), (0, 2, 1)).reshape(t, FOX_HEADS)
    dcs = jnp.pad(dcs, ((0, 0), (0, F_PAD - FOX_HEADS)))
    dcq = jnp.pad(dcq.reshape(t, FOX_HEADS, FOX_HEAD_DIM)[:, :, 0], ((0, 0), (0, F_PAD - FOX_HEADS)))

    def fox_f_bwd_body(dc, dc_rows, f, bv):
        lane = lax.broadcasted_iota(jnp.int32, (1, F_PAD), 1)
        dlogf = _scan_rows(dc + dc_rows, False)
        df = jnp.where(lane < FOX_HEADS, dlogf * _sigmoid(-(f + bv)), 0.0)
        behind = jnp.zeros((seq, z_width - z_f - F_PAD), F32)
        return jnp.concatenate([df, behind], axis=1), jnp.sum(df, axis=0, keepdims=True)

    dz, db_pad = _rowwise(fox_f_bwd_body, [_whole(dcs), _whole(dcq), (z, z_f, F_PAD)], [b_pad],
                          [(z_width - z_f, BF16)], [((1, F_PAD), F32)], "fox_f_bwd", seq, into=(dz, z_width, z_f))
    G["b_forget"] = db_pad[:, :FOX_HEADS]

    def fox_head_bwd(dz_in, col, dyn, gain, name):
        return _rowwise(lambda v, dyv, g: _head_norm_bwd(v, g, dyv, FOX_HEAD_DIM), [(z, col, FOX_WIDTH), _whole(dyn)],
                        [gain], [(FOX_WIDTH, BF16)], [((1, FOX_WIDTH), F32)], name, tm_e, into=(dz_in, z_width, col))

    dz, dgq_t = fox_head_bwd(dz, z_q, dqn, gq, "fox_q_bwd")
    dz, dgk_t = fox_head_bwd(dz, z_k, dkn, gk, "fox_k_bwd")
    G["fox_q_norm"] = jnp.sum(dgq_t.reshape(FOX_HEADS, FOX_HEAD_DIM), axis=0, keepdims=True)
    G["fox_k_norm"] = jnp.sum(dgk_t.reshape(FOX_HEADS, FOX_HEAD_DIM), axis=0, keepdims=True)

    do_mem = _mm(dy_mem, full["w_mem_o"], "nt", "mem_o_dx")
    P["w_mem_o"] = _blocks_from_cols(_mm(o_mem, dy_mem, "tn", "mem_o_dw"))
    dqmn, dkmn, dvm = _mem_bwd(qmn, kmn, vmb, do_mem, batch, seq, mlen, "mem_bwd")
    dz, dgqm_t = _rowwise(lambda q, dq, g: _head_norm_bwd(q, g, dq, MEM_HEAD_DIM),
                          [(z, z_qm, MEM_WIDTH), _whole(dqmn)], [gqm], [(MEM_WIDTH, BF16)],
                          [((1, MEM_WIDTH), F32)], "memq_bwd", tm_e, into=(dz, z_width, z_qm))

    def memk_bwd_body(k, dk, dv, g):
        dkr, dg = _head_norm_bwd(k, g, dk, MEM_HEAD_DIM)
        return jnp.concatenate([dkr, dv], axis=1), dg

    dkv, dgkm_t = _rowwise(memk_bwd_body, [(kv, 0, MEM_WIDTH), _whole(dkmn), _whole(dvm)], [gkm],
                           [(2 * MEM_WIDTH, BF16)], [((1, MEM_WIDTH), F32)], "memk_bwd", _tile(batch * mlen, 256, 16))
    G["mem_q_norm"] = jnp.sum(dgqm_t.reshape(MEM_HEADS, MEM_HEAD_DIM), axis=0, keepdims=True)
    G["mem_k_norm"] = jnp.sum(dgkm_t.reshape(MEM_HEADS, MEM_HEAD_DIM), axis=0, keepdims=True)
    P["w_mem_kv"] = row_blocks(_mm(memn, dkv, "tn", "mem_kv_dw"))
    dmemn = _mm(dkv, full["w_mem_kv"], "nt", "mem_kv_dx")
    _, _, G["mem_norm"] = _rms_bwd(mem2d, W["mem_norm"], dmemn, None, "mem_drms")

    d_w_in_pad, l_small = _mm(h2, dz, "tn", "mix_in_dw", caps=(1024, 1536, 2048), comm=_scatter_first([P[n] for n in mixer_small]))
    s_small = [pair_add(n, P[n], l) for n, l in zip(mixer_small, l_small)]
    p = d_w_in_pad
    p_w_in = _blocks_from_cols(jnp.concatenate(
        [p[:, z_u:z_u + POOL_WIDTH], p[:, z_q:z_q + 3 * FOX_WIDTH], p[:, z_f:z_f + FOX_HEADS],
         p[:, z_qm:z_qm + MEM_WIDTH], p[:, z_gate:z_gate + gate_w]], axis=1))
    dh2, rest = _mm(dz, w_in_pad, "nt", "mix_in_dx", caps=(1024, 1024, 1792),
                    comm=_join(_scatter_second(s_small), _scatter_first([p_w_in])))
    for n, l in zip(mixer_small, rest[:len(mixer_small)]):
        landed[n] = l
    s_w_in = pair_add("w_in", p_w_in, rest[-1])
    dx1, dx1_b, G["mix_norm"] = _rms_bwd(x1, W["mix_norm"], dh2, dx2, "mix_drms")

    dgu1 = _swiglu_bwd_from_out(dx1_b, wd1, gu1, "ffn1_dgu")
    p_wgu1, (landed["w_in"],) = _mm(h1, dgu1, "tn", "ffn1_dwgu", caps=gu_caps_tn, out_blocks=True, halves="b",
                                    comm=_scatter_second([s_w_in]))
    d_wd1, (l_wgu1,) = _mm(a1, dx1_b, "tn", "ffn1_dwd", scale=0.5, caps=dwd_caps, comm=_scatter_first([p_wgu1]))
    p_wd1 = row_blocks(d_wd1)
    s_wgu1 = pair_add("ffn1_w_gate_up", p_wgu1, l_wgu1)
    dh1, (landed["ffn1_w_gate_up"], l_wd1) = _mm(dgu1, wgu1, "nt", "ffn1_dh", caps=gu_caps_nt, b_blocks=True, halves="a",
                                                 comm=_join(_scatter_second([s_wgu1]), _scatter_first([p_wd1])))
    s_wd1 = pair_add("ffn1_w_down", p_wd1, l_wd1)
    (dx0, _, G["ffn1_norm"]), (landed["ffn1_w_down"],) = _rms_bwd(x2d, W["ffn1_norm"], dh1, dx1, "ffn1_drms",
                                                                  comm=_scatter_second([s_wd1]))
    grad_x = dx0.reshape(batch, seq, d)

    out_g, out_d, out_m, out_v = {}, {}, {}, {}
    for n in big:
        res = _rs_finish_adam(own[n], landed[n], W[n][0], M[n][0], V[n][0], "adam_" + n)
        out_g[n], out_d[n], out_m[n], out_v[n] = [r[None] for r in res]

    small = [n for n in names if n not in big]
    g_small = _pack_small([G[n].reshape(W[n].shape) for n in small])
    all_small = _all_gather_vmem(g_small, "ag_small_grads")
    res = _allreduce_adam(all_small, _pack_small([W[n] for n in small]), _pack_small([M[n] for n in small]),
                          _pack_small([V[n] for n in small]), "adam_small")
    like = [W[n] for n in small]
    for dst, buf in zip((out_g, out_d, out_m, out_v), res):
        for n, a in zip(small, _unpack_small(buf, like)):
            dst[n] = a

    return (loss, grad_x, *[out_g[n] for n in names], *[out_d[n] for n in names],
            *[out_m[n] for n in names], *[out_v[n] for n in names])
```

```python
import functools

import jax
import jax.numpy as jnp
from jax import lax
from jax.experimental import pallas as pl
from jax.experimental.pallas import tpu as pltpu

F32 = jnp.float32
BF16 = jnp.bfloat16
MESH = pl.DeviceIdType.MESH

N_DEV = 8
EPS = 1e-6
FOX_HEADS = 16
FOX_HEAD_DIM = 64
FOX_WIDTH = FOX_HEADS * FOX_HEAD_DIM
MEM_HEADS = 4
MEM_HEAD_DIM = 128
MEM_WIDTH = MEM_HEADS * MEM_HEAD_DIM
POOL_GROUPS = 4
POOL_GROUP_DIM = 128
POOL_WIDTH = POOL_GROUPS * POOL_GROUP_DIM
POOL_WINDOWS = (2, 4, 8, 16)
LANES = 128
F_PAD = LANES

ADAM_LR = 0.001
ADAM_B1 = 0.9
ADAM_B2 = 0.999
ADAM_EPS = 1e-08
ADAM_WD = 0.01
ADAM_STEP = 10

VMEM_LIMIT = 56 * 1024 * 1024
NEG = -1e30

ANY = pl.BlockSpec(memory_space=pl.ANY)


def _params(sem=None):
    return pltpu.CompilerParams(dimension_semantics=sem, vmem_limit_bytes=VMEM_LIMIT)


MXU_DIM = 256


def _tile(dim, cap, align):
    best = None
    t = align
    while t <= min(dim, cap):
        if dim % t == 0:
            best = t
        t += align
    return dim if best is None else best


def _mxu_tile(dim, cap):
    wide, fine = _tile(dim, cap, MXU_DIM), _tile(dim, cap, LANES)
    whole_widths = wide % MXU_DIM == 0 and wide <= cap
    return wide if whole_widths and fine < 2 * wide else fine


class _Sems:
    def __init__(self, send, recv, local):
        self.send, self.recv, self.local = send, recv, local
        self.n_remote = self.n_local = 0

    def remote(self):
        i = self.n_remote
        self.n_remote += 1
        return self.send.at[i], self.recv.at[i]

    def one_local(self):
        i = self.n_local
        self.n_local += 1
        return self.local.at[i]


class _Comm:
    def __init__(self, srcs, outs, n_remote, n_local, plan, in_place=False):
        self.srcs, self.outs, self.n_remote, self.n_local = list(srcs), list(outs), n_remote, n_local
        self.plan, self.in_place = plan, in_place

    def aliases(self, n_in, n_out):
        return {n_in + i: n_out + i for i in range(len(self.srcs))} if self.in_place else {}

    def scratch(self):
        return [pltpu.SemaphoreType.DMA((self.n_remote,)), pltpu.SemaphoreType.DMA((self.n_remote,)),
                pltpu.SemaphoreType.DMA((max(self.n_local, 1),))]

    def run(self, src_refs, out_refs, sem_refs, first, last):
        def copies():
            return self.plan(list(src_refs), list(out_refs), _Sems(*sem_refs))

        return (lambda: _when(first, lambda: [cp.start() for cp in copies()]),
                lambda: _when(last, lambda: [cp.wait() for cp in copies()]))


def _when(cond, fn):
    @pl.when(cond)
    def _():
        fn()


def _join(*comms):
    comms = [c for c in comms if c is not None]
    assert all(not c.in_place for c in comms)

    def plan(src_refs, out_refs, sems):
        copies, si, oi = [], 0, 0
        for c in comms:
            copies += c.plan(src_refs[si:si + len(c.srcs)], out_refs[oi:oi + len(c.outs)], sems)
            si += len(c.srcs)
            oi += len(c.outs)
        return copies

    return _Comm(sum((c.srcs for c in comms), []), sum((c.outs for c in comms), []),
                 sum(c.n_remote for c in comms), sum(c.n_local for c in comms), plan)


def _split_refs(refs, n_in, n_out, comm):
    if comm is None:
        return refs[:n_in], refs[n_in:n_in + n_out], refs[n_in + n_out:], (), (), ()
    ns, no = len(comm.srcs), len(comm.outs)
    ins = refs[:n_in]
    srcs = refs[n_in:n_in + ns]
    outs = refs[n_in + ns:n_in + ns + n_out]
    couts = refs[n_in + ns + n_out:n_in + ns + n_out + no]
    rest = refs[n_in + ns + n_out + no:]
    return ins, outs, rest[:-3], srcs, couts, rest[-3:]


def _sigmoid(x):
    return 1.0 / (1.0 + jnp.exp(-x))


_DIMS = {"nn": (((1,), (0,)), ((), ())), "nt": (((1,), (1,)), ((), ())), "tn": (((0,), (0,)), ((), ()))}


def _mm(a, b, mode, name, out_dtype=F32, scale=1.0, res=None, caps=None, b_blocks=False, out_blocks=False,
        halves=None, comm=None):
    nblk = N_DEV
    a_shape, b_shape = a.shape, b.shape
    if b_blocks:
        _, r0, c0 = b.shape
        b_shape = (r0, nblk * c0)
    if halves == "a":
        assert mode == "nt"
        a_shape = (a.shape[1], 2 * a.shape[2])
    elif halves == "b":
        assert mode == "tn"
        b_shape = (b.shape[1], 2 * b.shape[2])
    if mode == "nn":
        (m, k), (k2, n) = a_shape, b_shape
    elif mode == "nt":
        (m, k), (n, k2) = a_shape, b_shape
    else:
        (k, m), (k2, n) = a_shape, b_shape
    assert k == k2, (name, a.shape, b.shape)
    cm, cn, ck = caps or ((1024, 512, 2048) if k <= 2048 else (1024, 1024, 2048))
    n_unit = n // nblk if (out_blocks or (b_blocks and mode == "nn")) else n
    k_unit = k // nblk if (b_blocks and mode == "nt") else k
    tm, tn, tk = _mxu_tile(m, cm), _mxu_tile(n_unit, cn), _mxu_tile(k_unit, ck)
    nk = k // tk
    npb, kpb = n_unit // tn, k_unit // tk
    n_half, k_half = (n // 2) // tn, (k // 2) // tk
    if mode == "nn":
        a_spec = pl.BlockSpec((tm, tk), lambda i, j, kk: (i, kk))
        b_spec = pl.BlockSpec((tk, tn), lambda i, j, kk: (kk, j))
        if b_blocks:
            b_spec = pl.BlockSpec((None, tk, tn), lambda i, j, kk: (j // npb, kk, j % npb))
    elif mode == "nt":
        a_spec = pl.BlockSpec((tm, tk), lambda i, j, kk: (i, kk))
        b_spec = pl.BlockSpec((tn, tk), lambda i, j, kk: (j, kk))
        if b_blocks:
            b_spec = pl.BlockSpec((None, tn, tk), lambda i, j, kk: (kk // kpb, j, kk % kpb))
        if halves == "a":
            assert (k // 2) % tk == 0
            a_spec = pl.BlockSpec((None, tm, tk), lambda i, j, kk: (kk // k_half, i, kk % k_half))
    else:
        assert not b_blocks
        a_spec = pl.BlockSpec((tk, tm), lambda i, j, kk: (kk, i))
        b_spec = pl.BlockSpec((tk, tn), lambda i, j, kk: (kk, j))
        if halves == "b":
            assert (n // 2) % tn == 0
            b_spec = pl.BlockSpec((None, tk, tn), lambda i, j, kk: (j // n_half, kk, j % n_half))
    if out_blocks:
        assert res is None
        o_spec = pl.BlockSpec((None, tm, tn), lambda i, j, kk: (j // npb, i, j % npb))
        o_shape = jax.ShapeDtypeStruct((nblk, m, n // nblk), out_dtype)
    else:
        o_spec = pl.BlockSpec((tm, tn), lambda i, j, kk: (i, j))
        o_shape = jax.ShapeDtypeStruct((m, n), out_dtype)
    in_specs = [a_spec, b_spec] + ([o_spec] if res is not None else [])
    n_in = len(in_specs)
    dims = _DIMS[mode]
    gm, gn = m // tm, n // tn

    def kern(*refs):
        ins, outs, scratch, c_src, c_out, c_sem = _split_refs(refs, n_in, 1, comm)
        a_ref, b_ref = ins[0], ins[1]
        res_ref = ins[2] if res is not None else None
        o_ref = outs[0]
        if comm is not None:
            i, j, kq = pl.program_id(0), pl.program_id(1), pl.program_id(2)
            first = jnp.logical_and(jnp.logical_and(i == 0, j == 0), kq == 0)
            last = jnp.logical_and(jnp.logical_and(i == gm - 1, j == gn - 1), kq == nk - 1)
            start_comm, wait_comm = comm.run(c_src, c_out, c_sem, first, last)
            start_comm()
        a_tile = a_ref[...].astype(BF16)
        if scale != 1.0:
            a_tile = a_tile * scale

        def product():
            return lax.dot_general(a_tile, b_ref[...].astype(BF16), dims, preferred_element_type=F32)

        if nk == 1:
            r = product()
            if res_ref is not None:
                r = res_ref[...] + r
            o_ref[...] = r.astype(out_dtype)
        else:
            acc_ref = scratch[0] if scratch else o_ref
            kk = pl.program_id(2)

            @pl.when(kk == 0)
            def _():
                acc_ref[...] = jnp.zeros_like(acc_ref) if res_ref is None else res_ref[...]

            acc_ref[...] += product()
            if scratch:
                @pl.when(kk == nk - 1)
                def _():
                    o_ref[...] = acc_ref[...].astype(out_dtype)
        if comm is not None:
            wait_comm()

    assert scale in (1.0, 0.5)
    args = (a, b) + ((res,) if res is not None else ())
    scratch_shapes = [pltpu.VMEM((tm, tn), F32)] if (nk > 1 and out_dtype != F32) else []
    if comm is None:
        return pl.pallas_call(
            kern, name=name, grid=(gm, gn, nk), in_specs=in_specs, out_specs=o_spec, out_shape=o_shape,
            scratch_shapes=scratch_shapes, compiler_params=_params(("parallel", "parallel", "arbitrary")),
        )(*args)
    res_all = pl.pallas_call(
        kern, name=name, grid=(gm, gn, nk), in_specs=in_specs + [ANY] * len(comm.srcs),
        out_specs=[o_spec] + [ANY] * len(comm.outs), out_shape=[o_shape] + comm.outs,
        scratch_shapes=scratch_shapes + comm.scratch(), input_output_aliases=comm.aliases(n_in, 1),
        compiler_params=_params(("arbitrary", "arbitrary", "arbitrary")),
    )(*args, *comm.srcs)
    return res_all[0], list(res_all[1:])


def _gate_up_swiglu(h, w_blocks, name, comm=None):
    t, d = h.shape
    nblk, _, nb = w_blocks.shape
    half = nblk // 2
    tm = _mxu_tile(t, 512)
    steps = t // tm

    def kern(*refs):
        (h_ref, wg_ref, wu_ref), (gu_ref, a_ref), _, c_src, c_out, c_sem = _split_refs(refs, 3, 2, comm)
        if comm is not None:
            j, i = pl.program_id(0), pl.program_id(1)
            start_comm, wait_comm = comm.run(c_src, c_out, c_sem, jnp.logical_and(j == 0, i == 0),
                                             jnp.logical_and(j == half - 1, i == steps - 1))
            start_comm()
        hv = h_ref[...]
        g = lax.dot_general(hv, wg_ref[...], _DIMS["nn"], preferred_element_type=F32)
        u = lax.dot_general(hv, wu_ref[...], _DIMS["nn"], preferred_element_type=F32)
        gu_ref[0] = g
        gu_ref[1] = u
        a_ref[...] = ((g * _sigmoid(g)) * u).astype(BF16)
        if comm is not None:
            wait_comm()

    in_specs = [pl.BlockSpec((tm, d), lambda j, i: (i, 0)), pl.BlockSpec((None, d, nb), lambda j, i: (j, 0, 0)),
                pl.BlockSpec((None, d, nb), lambda j, i: (j + half, 0, 0))]
    out_specs = [pl.BlockSpec((2, tm, nb), lambda j, i: (0, i, j)), pl.BlockSpec((tm, nb), lambda j, i: (i, j))]
    out_shape = [jax.ShapeDtypeStruct((2, t, half * nb), F32), jax.ShapeDtypeStruct((t, half * nb), BF16)]
    if comm is None:
        return pl.pallas_call(kern, name=name, grid=(half, steps), in_specs=in_specs, out_specs=out_specs,
                              out_shape=out_shape, compiler_params=_params(("parallel", "parallel")))(h, w_blocks, w_blocks)
    res_all = pl.pallas_call(
        kern, name=name, grid=(half, steps), in_specs=in_specs + [ANY] * len(comm.srcs),
        out_specs=out_specs + [ANY] * len(comm.outs), out_shape=out_shape + comm.outs,
        scratch_shapes=comm.scratch(), compiler_params=_params(("arbitrary", "arbitrary")),
    )(h, w_blocks, w_blocks, *comm.srcs)
    return list(res_all[:2]), list(res_all[2:])


def _swiglu_bwd_from_out(dy, wd, gu, name, comm=None):
    t, d = dy.shape
    f = wd.shape[0]
    tm, tn = _mxu_tile(t, 1024), _mxu_tile(f, 512)
    gi, gj = t // tm, f // tn

    def kern(*refs):
        (dy_ref, wd_ref, gu_ref), (out_ref,), _, c_src, c_out, c_sem = _split_refs(refs, 3, 1, comm)
        if comm is not None:
            i, j = pl.program_id(0), pl.program_id(1)
            start_comm, wait_comm = comm.run(c_src, c_out, c_sem, jnp.logical_and(i == 0, j == 0),
                                             jnp.logical_and(i == gi - 1, j == gj - 1))
            start_comm()
        dy_half = dy_ref[...] * 0.5
        width = min(tn, MXU_DIM)
        for c0 in range(0, tn, width):
            cols = slice(c0, c0 + width)
            da = lax.dot_general(dy_half, wd_ref[cols, :], _DIMS["nt"], preferred_element_type=F32)
            g, u = gu_ref[0, :, cols], gu_ref[1, :, cols]
            s = _sigmoid(g)
            out_ref[0, :, cols] = (da * u * (s * (1.0 + g * (1.0 - s)))).astype(BF16)
            out_ref[1, :, cols] = (da * (g * s)).astype(BF16)
        if comm is not None:
            wait_comm()

    pair = pl.BlockSpec((2, tm, tn), lambda i, j: (0, i, j))
    in_specs = [pl.BlockSpec((tm, d), lambda i, j: (i, 0)), pl.BlockSpec((tn, d), lambda i, j: (j, 0)), pair]
    out_shape = jax.ShapeDtypeStruct((2, t, f), BF16)
    if comm is None:
        return pl.pallas_call(kern, name=name, grid=(gi, gj), in_specs=in_specs, out_specs=pair, out_shape=out_shape,
                              compiler_params=_params(("parallel", "parallel")))(dy, wd, gu)
    res_all = pl.pallas_call(
        kern, name=name, grid=(gi, gj), in_specs=in_specs + [ANY] * len(comm.srcs),
        out_specs=[pair] + [ANY] * len(comm.outs), out_shape=[out_shape] + comm.outs,
        scratch_shapes=comm.scratch(), compiler_params=_params(("arbitrary", "arbitrary")),
    )(dy, wd, gu, *comm.srcs)
    return res_all[0], list(res_all[1:])


def _rowwise(body, ins, params, outs, accs, name, tm, comm=None, into=None):
    rows = ins[0][0].shape[0]
    assert rows % tm == 0, (name, rows, tm)
    in_specs = []
    for arr, off, width in ins:
        assert off % width == 0 and arr.shape[0] == rows, (name, arr.shape, off, width)
        in_specs.append(pl.BlockSpec((tm, width), functools.partial(lambda i, c: (i, c), c=off // width)))
    for p in params:
        in_specs.append(pl.BlockSpec(p.shape, functools.partial(lambda i, nd: (0,) * nd, nd=p.ndim)))
    out_specs = [pl.BlockSpec((tm, w), lambda i: (i, 0)) for w, _ in outs]
    out_specs += [pl.BlockSpec(s, functools.partial(lambda i, nd: (0,) * nd, nd=len(s))) for s, _ in accs]
    out_shape = [jax.ShapeDtypeStruct((rows, w), d) for w, d in outs]
    out_shape += [jax.ShapeDtypeStruct(s, d) for s, d in accs]
    n_in, n_par, n_out = len(ins), len(params), len(outs)
    carried = []
    if into is not None:
        buf, total, col = into
        assert comm is None and col % outs[0][0] == 0, (name, col, outs[0])
        out_specs[0] = pl.BlockSpec((tm, outs[0][0]), functools.partial(lambda i, c: (i, c), c=col // outs[0][0]))
        out_shape[0] = jax.ShapeDtypeStruct((rows, total), outs[0][1])
        carried = [] if buf is None else [buf]

    steps = rows // tm

    def kern(*refs):
        in_refs, out_refs, _, c_src, c_out, c_sem = _split_refs(refs, n_in + n_par + len(carried),
                                                                 n_out + len(accs), comm)
        in_refs = in_refs[:n_in + n_par]
        first = pl.program_id(0) == 0
        if comm is not None:
            start_comm, wait_comm = comm.run(c_src, c_out, c_sem, first, pl.program_id(0) == steps - 1)
            start_comm()
        res = body(*[r[...] for r in in_refs])
        for r, v in zip(out_refs[:n_out], res[:n_out]):
            r[...] = v.astype(r.dtype)
        for r, v in zip(out_refs[n_out:], res[n_out:]):
            @pl.when(first)
            def _(r=r, v=v):
                r[...] = v.astype(r.dtype)

            @pl.when(jnp.logical_not(first))
            def _(r=r, v=v):
                r[...] += v.astype(r.dtype)
        if comm is not None:
            wait_comm()

    args = [a for a, _, _ in ins] + list(params)
    if comm is None:
        return pl.pallas_call(
            kern, name=name, grid=(steps,), in_specs=in_specs + [ANY] * len(carried), out_specs=out_specs,
            out_shape=out_shape, input_output_aliases={len(args): 0} if carried else {},
            compiler_params=_params(("arbitrary",) if accs else ("parallel",)),
        )(*args, *carried)
    res_all = pl.pallas_call(
        kern, name=name, grid=(steps,), in_specs=in_specs + [ANY] * len(comm.srcs),
        out_specs=out_specs + [ANY] * len(comm.outs), out_shape=out_shape + comm.outs,
        scratch_shapes=comm.scratch(), input_output_aliases=comm.aliases(len(args), len(out_shape)),
        compiler_params=_params(("arbitrary",)),
    )(*args, *comm.srcs)
    return list(res_all[:len(out_shape)]), list(res_all[len(out_shape):])


def _whole(x):
    return (x, 0, x.shape[1])


def _first(res, comm):
    return res[0] if comm is None else (res[0][0], res[1])


def _rms_fwd(x, gain, name, comm=None):
    def body(xv, g):
        r = lax.rsqrt(jnp.mean(xv * xv, axis=-1, keepdims=True) + EPS)
        return ((xv * r) * g,)

    return _first(_rowwise(body, [_whole(x)], [gain], [(x.shape[1], BF16)], [], name, _tile(x.shape[0], 256, 16),
                           comm=comm), comm)


def _rms_bwd(x, gain, dh, dres, name, comm=None):
    d = x.shape[1]

    def body(*vals):
        if dres is None:
            xv, dhv, g = vals
        else:
            xv, dhv, drv, g = vals
        r = lax.rsqrt(jnp.mean(xv * xv, axis=-1, keepdims=True) + EPS)
        xh = xv * r
        w = dhv * g
        dx = r * (w - xh * jnp.mean(w * xh, axis=-1, keepdims=True))
        if dres is not None:
            dx = drv + dx
        return dx, dx, jnp.sum(dhv * xh, axis=0, keepdims=True)

    ins = [_whole(x), _whole(dh)] + ([_whole(dres)] if dres is not None else [])
    return _rowwise(body, ins, [gain], [(d, F32), (d, BF16)], [((1, d), F32)], name, _tile(x.shape[0], 256, 16),
                    comm=comm)


def _loss_head(y, target, name):
    d = y.shape[1]

    def body(yv, tv):
        e = yv - tv
        part = jnp.sum(jnp.sum(e * e, axis=1, keepdims=True), axis=0, keepdims=True)
        return e / d, e / d, jnp.broadcast_to((0.5 / d) * part, (1, LANES))

    return _rowwise(body, [_whole(y), _whole(target)], [], [(d, F32), (d, BF16)], [((1, LANES), F32)], name,
                    _tile(y.shape[0], 256, 16))


def _head_mean(v, head_dim):
    cols = []
    lane = lax.broadcasted_iota(jnp.int32, (1, LANES), 1)
    for j in range(v.shape[1] // LANES):
        blk = v[:, j * LANES:(j + 1) * LANES]
        if head_dim == LANES:
            m = jnp.sum(blk, axis=-1, keepdims=True)
            cols.append(jnp.broadcast_to(m, blk.shape))
        else:
            lo = jnp.sum(jnp.where(lane < head_dim, blk, 0.0), axis=-1, keepdims=True)
            hi = jnp.sum(jnp.where(lane >= head_dim, blk, 0.0), axis=-1, keepdims=True)
            cols.append(jnp.where(lane < head_dim, lo, hi))
    return jnp.concatenate(cols, axis=1) / head_dim


def _head_norm(xv, g, head_dim):
    r = lax.rsqrt(_head_mean(xv * xv, head_dim) + EPS)
    return (xv * r) * g


def _head_norm_bwd(xv, g, dy, head_dim):
    r = lax.rsqrt(_head_mean(xv * xv, head_dim) + EPS)
    xh = xv * r
    w = dy * g
    dx = r * (w - xh * _head_mean(w * xh, head_dim))
    return dx, jnp.sum(dy * xh, axis=0, keepdims=True)


def _log_sigmoid(x):
    return jnp.minimum(x, 0.0) - jnp.log(1.0 + jnp.exp(-jnp.abs(x)))


def _shift_rows(v, sh, down):
    n = v.shape[0]
    row = lax.broadcasted_iota(jnp.int32, (n, 1), 0)
    if down:
        return jnp.where(row >= sh, pltpu.roll(v, sh, 0), 0.0)
    return jnp.where(row < n - sh, pltpu.roll(v, n - sh, 0), 0.0)


def _scan_rows(v, down):
    sh = 1
    while sh < v.shape[0]:
        v = v + _shift_rows(v, sh, down)
        sh *= 2
    return v


def _window_sum(v, steps, down):
    for s in range(steps):
        v = v + _shift_rows(v, 2 ** s, down)
    return v


def _fox_fwd(qn, kn, vb, cks, batch, seq, name, comm=None):
    t, width = qn.shape
    pairs = width // LANES
    tq = _tile(seq, 512, LANES)
    nq = seq // tq

    def kern(*refs):
        (q_ref, k_ref, v_ref, c_ref), (o_ref, lse_ref), _, c_src, c_out, c_sem = _split_refs(refs, 4, 2, comm)
        qi = pl.program_id(2)
        if comm is not None:
            b_id, p_id = pl.program_id(0), pl.program_id(1)
            first = jnp.logical_and(jnp.logical_and(b_id == 0, p_id == 0), qi == 0)
            last = jnp.logical_and(jnp.logical_and(b_id == batch - 1, p_id == pairs - 1), qi == nq - 1)
            start_comm, wait_comm = comm.run(c_src, c_out, c_sem, first, last)
            start_comm()
        lane = lax.broadcasted_iota(jnp.int32, (1, LANES), 1)
        rowi = lax.broadcasted_iota(jnp.int32, (tq, tq), 0)
        coli = lax.broadcasted_iota(jnp.int32, (tq, tq), 1)
        q_all = q_ref[...]
        o_heads, lse_heads = [], []
        for h in range(2):
            hm = (lane < FOX_HEAD_DIM) if h == 0 else (lane >= FOX_HEAD_DIM)
            q = jnp.where(hm, q_all, jnp.zeros_like(q_all))

            def step(j, carry, h=h, q=q, diagonal=False):
                m, l, acc = carry
                start = pl.multiple_of(j * tq, tq)
                k = k_ref[pl.ds(start, tq), :]
                v = v_ref[pl.ds(start, tq), :]
                s = lax.dot_general(q, k, _DIMS["nt"], preferred_element_type=F32)
                s = s - c_ref[0, h:h + 1, pl.ds(start, tq)]
                if diagonal:
                    s = jnp.where(rowi >= coli, s, NEG)
                m_new = jnp.maximum(m, jnp.max(s, axis=-1, keepdims=True))
                alpha = jnp.exp(m - m_new)
                p = jnp.exp(s - m_new)
                l = alpha * l + jnp.sum(p, axis=-1, keepdims=True)
                acc = alpha * acc + lax.dot_general(p.astype(BF16), v, _DIMS["nn"], preferred_element_type=F32)
                return m_new, l, acc

            init = (jnp.full((tq, 1), NEG, F32), jnp.zeros((tq, 1), F32), jnp.zeros((tq, LANES), F32))
            m, l, acc = step(qi, lax.fori_loop(0, qi, step, init), diagonal=True)
            o_heads.append(acc / l)
            lse_heads.append(jnp.broadcast_to(m + jnp.log(l), (tq, LANES)))
        o_ref[...] = jnp.where(lane < FOX_HEAD_DIM, o_heads[0], o_heads[1])
        lse_ref[...] = jnp.where(lane < FOX_HEAD_DIM, lse_heads[0], lse_heads[1])
        if comm is not None:
            wait_comm()

    q_spec = pl.BlockSpec((tq, LANES), lambda b, hp, qi: (b * nq + qi, hp))
    kv_spec = pl.BlockSpec((seq, LANES), lambda b, hp, qi: (b, hp))
    c_spec = pl.BlockSpec((1, 2, seq), lambda b, hp, qi: (b * pairs + hp, 0, 0))
    in_specs = [q_spec, kv_spec, kv_spec, c_spec]
    out_shape = [jax.ShapeDtypeStruct((t, width), F32), jax.ShapeDtypeStruct((t, width), F32)]
    if comm is None:
        return pl.pallas_call(
            kern, name=name, grid=(batch, pairs, nq), in_specs=in_specs, out_specs=[q_spec, q_spec],
            out_shape=out_shape, compiler_params=_params(("parallel", "parallel", "arbitrary")),
        )(qn, kn, vb, cks)
    res_all = pl.pallas_call(
        kern, name=name, grid=(batch, pairs, nq), in_specs=in_specs + [ANY] * len(comm.srcs),
        out_specs=[q_spec, q_spec] + [ANY] * len(comm.outs), out_shape=out_shape + comm.outs,
        scratch_shapes=comm.scratch(), compiler_params=_params(("arbitrary", "arbitrary", "arbitrary")),
    )(qn, kn, vb, cks, *comm.srcs)
    return list(res_all[:2]), list(res_all[2:])


def _fox_bwd(qn, kn, vb, cks, o, lse, do, dz, dv_col, batch, seq, name, comm):
    t, width = qn.shape
    pairs = width // LANES
    tk = _tile(seq, 512, LANES)
    nk = seq // tk
    scale = FOX_HEAD_DIM ** -0.5

    def kern(*refs):
        ins, outs, _, c_src, c_out, c_sem = _split_refs(refs, 8, 5, comm)
        k_ref, v_ref, q_ref, c_ref, o_ref, lse_ref, do_ref, _ = ins
        dk_ref, dv_ref, dq_ref, dc_ref, dcq_ref = outs
        kj = pl.program_id(2)
        if comm is not None:
            b_id, p_id = pl.program_id(0), pl.program_id(1)
            first = jnp.logical_and(jnp.logical_and(b_id == 0, p_id == 0), kj == 0)
            last = jnp.logical_and(jnp.logical_and(b_id == batch - 1, p_id == pairs - 1), kj == nk - 1)
            start_comm, wait_comm = comm.run(c_src, c_out, c_sem, first, last)
            start_comm()
        lane = lax.broadcasted_iota(jnp.int32, (1, LANES), 1)
        rowi = lax.broadcasted_iota(jnp.int32, (tk, tk), 0)
        coli = lax.broadcasted_iota(jnp.int32, (tk, tk), 1)

        @pl.when(kj == 0)
        def _():
            dq_ref[...] = jnp.zeros_like(dq_ref)
            dcq_ref[...] = jnp.zeros_like(dcq_ref)

        k_all = k_ref[...]
        v_all = v_ref[...]
        kstart = pl.multiple_of(kj * tk, tk)
        dk_heads, dv_heads = [], []
        for h in range(2):
            hm = (lane < FOX_HEAD_DIM) if h == 0 else (lane >= FOX_HEAD_DIM)
            kh = jnp.where(hm, k_all, jnp.zeros_like(k_all))
            vh = jnp.where(hm, v_all, jnp.zeros_like(v_all))
            c_row = c_ref[0, h:h + 1, pl.ds(kstart, tk)]

            def step(qi, carry, h=h, hm=hm, kh=kh, vh=vh, c_row=c_row, diagonal=False):
                dk_acc, dv_acc, dc_acc = carry
                start = pl.multiple_of(qi * tk, tk)
                q = q_ref[pl.ds(start, tk), :]
                dov = do_ref[pl.ds(start, tk), :]
                ov = o_ref[pl.ds(start, tk), :]
                lse_col = jnp.max(jnp.where(hm, lse_ref[pl.ds(start, tk), :], NEG), axis=-1, keepdims=True)
                dob = jnp.where(hm, dov, 0.0).astype(BF16)
                dcol = jnp.sum(dob.astype(F32) * ov, axis=-1, keepdims=True)
                s = lax.dot_general(q, kh, _DIMS["nt"], preferred_element_type=F32) - c_row
                p = jnp.exp(s - lse_col)
                if diagonal:
                    p = jnp.where(rowi >= coli, p, 0.0)
                dp = lax.dot_general(dob, vh, _DIMS["nt"], preferred_element_type=F32)
                ds = p * (dp - dcol)
                ds_b = ds.astype(BF16)
                dv_acc = dv_acc + lax.dot_general(p.astype(BF16), dob, _DIMS["tn"], preferred_element_type=F32)
                dk_acc = dk_acc + lax.dot_general(ds_b, q, _DIMS["tn"], preferred_element_type=F32)
                dq_part = lax.dot_general(ds_b * scale, kh, _DIMS["nn"], preferred_element_type=F32)
                dq_ref[pl.ds(start, tk), :] += dq_part
                dcq_ref[pl.ds(start, tk), :] += jnp.where(hm, jnp.sum(ds, axis=-1, keepdims=True), 0.0)
                dc_acc = dc_acc - jnp.sum(ds, axis=0, keepdims=True)
                return dk_acc, dv_acc, dc_acc

            init = (jnp.zeros((tk, LANES), F32), jnp.zeros((tk, LANES), F32), jnp.zeros((1, tk), F32))
            dk_acc, dv_acc, dc_acc = lax.fori_loop(kj + 1, nk, step, step(kj, init, diagonal=True))
            dk_heads.append(dk_acc)
            dv_heads.append(dv_acc)
            dc_ref[0, h:h + 1, pl.ds(kstart, tk)] = dc_acc
        dk_ref[...] = jnp.where(lane < FOX_HEAD_DIM, dk_heads[0], dk_heads[1])
        dv_ref[...] = jnp.where(lane < FOX_HEAD_DIM, dv_heads[0], dv_heads[1]).astype(BF16)
        if comm is not None:
            wait_comm()

    kv_spec = pl.BlockSpec((tk, LANES), lambda b, hp, kj: (b * nk + kj, hp))
    full_spec = pl.BlockSpec((seq, LANES), lambda b, hp, kj: (b, hp))
    c_spec = pl.BlockSpec((1, 2, seq), lambda b, hp, kj: (b * pairs + hp, 0, 0))
    dv_spec = pl.BlockSpec((tk, LANES), lambda b, hp, kj: (b * nk + kj, hp + dv_col // LANES))
    in_specs = [kv_spec, kv_spec, full_spec, c_spec, full_spec, full_spec, full_spec, ANY]
    out_specs = [kv_spec, dv_spec, full_spec, c_spec, full_spec]
    out_shape = [jax.ShapeDtypeStruct((t, width), F32), jax.ShapeDtypeStruct(dz.shape, BF16),
                 jax.ShapeDtypeStruct((t, width), F32), jax.ShapeDtypeStruct(cks.shape, F32),
                 jax.ShapeDtypeStruct((t, width), F32)]
    res_all = pl.pallas_call(
        kern, name=name, grid=(batch, pairs, nk), in_specs=in_specs + [ANY] * len(comm.srcs),
        out_specs=out_specs + [ANY] * len(comm.outs), out_shape=out_shape + comm.outs,
        scratch_shapes=comm.scratch(), input_output_aliases={7: 1},
        compiler_params=_params(("arbitrary", "arbitrary", "arbitrary")),
    )(kn, vb, qn, cks, o, lse, do, dz, *comm.srcs)
    return list(res_all[:5]), list(res_all[5:])


def _mem_fwd(qn, kn, vb, batch, seq, mlen, name):
    t, width = qn.shape
    heads = width // LANES
    tq = _tile(seq, 512, LANES)
    nq = seq // tq
    scale = MEM_HEAD_DIM ** -0.5

    def kern(q_ref, k_ref, v_ref, o_ref):
        s = lax.dot_general(q_ref[...], k_ref[...], _DIMS["nt"], preferred_element_type=F32) * scale
        e = jnp.exp(s - jnp.max(s, axis=-1, keepdims=True))
        p = e / jnp.sum(e, axis=-1, keepdims=True)
        o_ref[...] = lax.dot_general(p.astype(BF16), v_ref[...], _DIMS["nn"], preferred_element_type=F32)

    q_spec = pl.BlockSpec((tq, LANES), lambda b, h, qi: (b * nq + qi, h))
    kv_spec = pl.BlockSpec((mlen, LANES), lambda b, h, qi: (b, h))
    return pl.pallas_call(
        kern, name=name, grid=(batch, heads, nq), in_specs=[q_spec, kv_spec, kv_spec], out_specs=q_spec,
        out_shape=jax.ShapeDtypeStruct((t, width), F32),
        compiler_params=_params(("parallel", "parallel", "parallel")),
    )(qn, kn, vb)


def _mem_bwd(qn, kn, vb, do, batch, seq, mlen, name):
    t, width = qn.shape
    heads = width // LANES
    tq = _tile(seq, 512, LANES)
    nq = seq // tq
    scale = MEM_HEAD_DIM ** -0.5

    def kern(q_ref, k_ref, v_ref, do_ref, dq_ref, dk_ref, dv_ref):
        qi = pl.program_id(2)
        q, k, v = q_ref[...], k_ref[...], v_ref[...]
        dob = do_ref[...].astype(BF16)
        s = lax.dot_general(q, k, _DIMS["nt"], preferred_element_type=F32) * scale
        e = jnp.exp(s - jnp.max(s, axis=-1, keepdims=True))
        p = e / jnp.sum(e, axis=-1, keepdims=True)
        dp = lax.dot_general(dob, v, _DIMS["nt"], preferred_element_type=F32)
        ds = p * (dp - jnp.sum(p * dp, axis=-1, keepdims=True))
        dsb = (ds * scale).astype(BF16)
        dq_ref[...] = lax.dot_general(dsb, k, _DIMS["nn"], preferred_element_type=F32)
        dk = lax.dot_general(dsb, q, _DIMS["tn"], preferred_element_type=F32)
        dv = lax.dot_general(p.astype(BF16), dob, _DIMS["tn"], preferred_element_type=F32)

        @pl.when(qi == 0)
        def _():
            dk_ref[...] = dk
            dv_ref[...] = dv

        @pl.when(qi > 0)
        def _():
            dk_ref[...] += dk
            dv_ref[...] += dv

    q_spec = pl.BlockSpec((tq, LANES), lambda b, h, qi: (b * nq + qi, h))
    kv_spec = pl.BlockSpec((mlen, LANES), lambda b, h, qi: (b, h))
    return pl.pallas_call(
        kern, name=name, grid=(batch, heads, nq), in_specs=[q_spec, kv_spec, kv_spec, q_spec],
        out_specs=[q_spec, kv_spec, kv_spec],
        out_shape=[jax.ShapeDtypeStruct((t, width), F32), jax.ShapeDtypeStruct(kn.shape, F32),
                   jax.ShapeDtypeStruct(kn.shape, F32)],
        compiler_params=_params(("parallel", "parallel", "arbitrary")),
    )(qn, kn, vb, do)


def _position():
    return lax.axis_index("x"), lax.axis_index("y"), lax.axis_index("c")


def _other_chips(x, y):
    return [(1 - x, y), (x, 1 - y), (1 - x, 1 - y)]


def _remote(src, dst, sems, to):
    send, recv = sems.remote()
    return pltpu.make_async_remote_copy(src_ref=src, dst_ref=dst, send_sem=send, recv_sem=recv,
                                        device_id=to, device_id_type=MESH)


def _gather_first(shards):
    n = len(shards)

    def plan(src, out, sems):
        x, y, c = _position()
        me = 4 * x + 2 * y + c
        peers = [(x, y, 1 - c)] + [(px, py, c) for px, py in _other_chips(x, y)]
        copies = []
        for i in range(n):
            copies.append(pltpu.make_async_copy(src[i], out[i].at[me], sems.one_local()))
            copies += [_remote(src[i], out[i].at[me], sems, to) for to in peers]
        return copies

    outs = [jax.ShapeDtypeStruct((N_DEV,) + s.shape, s.dtype) for s in shards]
    return _Comm(shards, outs, 4 * n, n, plan)


def _gather_second(bufs):
    n = len(bufs)

    def plan(src, out, sems):
        x, y, c = _position()
        copies = []
        for i in range(n):
            for px, py in _other_chips(x, y):
                blk = out[i].at[4 * px + 2 * py + c]
                copies.append(_remote(blk, blk, sems, (x, y, 1 - c)))
        return copies

    outs = [jax.ShapeDtypeStruct(b.shape, b.dtype) for b in bufs]
    return _Comm(bufs, outs, 3 * n, 0, plan, in_place=True)


def _scatter_first(partials):
    n = len(partials)

    def plan(src, out, sems):
        x, y, c = _position()
        copies = []
        for i in range(n):
            for r, (px, py) in enumerate([(x, y)] + _other_chips(x, y)):
                first = 2 * (2 * px + py)
                copies.append(pltpu.make_async_copy(src[i].at[first + c], out[2 * i].at[r], sems.one_local()))
                copies.append(_remote(src[i].at[first + 1 - c], out[2 * i + 1].at[r], sems, (x, y, 1 - c)))
        return copies

    outs = [jax.ShapeDtypeStruct((4,) + g.shape[1:], g.dtype) for g in partials for _ in range(2)]
    return _Comm(partials, outs, 4 * n, 4 * n, plan)


def _scatter_second(chip_sums):
    n = len(chip_sums)

    def plan(src, out, sems):
        x, y, c = _position()
        return [_remote(src[i].at[1 + r], out[i].at[r], sems, (px, py, c))
                for i in range(n) for r, (px, py) in enumerate(_other_chips(x, y))]

    outs = [jax.ShapeDtypeStruct((3,) + h.shape[1:], h.dtype) for h in chip_sums]
    return _Comm(chip_sums, outs, 3 * n, 0, plan)


def _comm_call(comm, name):
    ns, no = len(comm.srcs), len(comm.outs)

    def body(*refs):
        copies = comm.plan(list(refs[:ns]), list(refs[ns:ns + no]), _Sems(*refs[ns + no:]))
        for cp in copies:
            cp.start()
        for cp in copies:
            cp.wait()

    return pl.pallas_call(
        body, name=name, in_specs=[ANY] * ns, out_specs=[ANY] * no, out_shape=comm.outs,
        scratch_shapes=comm.scratch(), input_output_aliases=comm.aliases(0, 0),
    )(*comm.srcs)


def _gather_first_relayed(shard, name):
    def body(x_ref, out_ref, send, recv, local):
        sems = _Sems(send, recv, local)
        x, y, c = _position()

        def slot(px, py):
            return out_ref.at[4 * px + 2 * py + c]

        mine = pltpu.make_async_copy(x_ref, slot(x, y), sems.one_local())
        to_sibling = _remote(x_ref, slot(x, y), sems, (x, y, 1 - c))
        to_x = _remote(x_ref, slot(x, y), sems, (1 - x, y, c))
        to_y = _remote(x_ref, slot(x, y), sems, (x, 1 - y, c))
        for cp in (mine, to_sibling, to_x, to_y):
            cp.start()
        relay_send, relay_recv = sems.remote()

        def relay(arrived, unused, origin, to):
            arrived.wait_recv()
            blk = slot(*origin)
            fwd = pltpu.make_async_remote_copy(src_ref=blk, dst_ref=blk, send_sem=relay_send, recv_sem=relay_recv,
                                               device_id=(*to, c), device_id_type=MESH)
            fwd.start()
            fwd.wait()
            arrived.wait_send()
            unused.wait()

        _when(c == 0, lambda: relay(to_y, to_x, (x, 1 - y), (1 - x, y)))
        _when(c == 1, lambda: relay(to_x, to_y, (1 - x, y), (x, 1 - y)))
        mine.wait()
        to_sibling.wait()

    return pl.pallas_call(
        body, name=name, in_specs=[ANY], out_specs=ANY,
        out_shape=jax.ShapeDtypeStruct((N_DEV,) + shard.shape, shard.dtype),
        scratch_shapes=[pltpu.SemaphoreType.DMA((4,)), pltpu.SemaphoreType.DMA((4,)), pltpu.SemaphoreType.DMA((1,))],
    )(shard)


def _all_gather_vmem(shard, name):
    first = _gather_first([shard])

    def body(x_ref, out_ref, send, recv, local):
        sems = _Sems(send, recv, local)
        copies = first.plan([x_ref], [out_ref], sems)
        for cp in copies:
            cp.start()
        x, y, c = _position()
        passed = []
        for j, (px, py) in enumerate(_other_chips(x, y)):
            copies[2 + j].wait_recv()
            blk = out_ref.at[4 * px + 2 * py + c]
            fwd = _remote(blk, blk, sems, (x, y, 1 - c))
            fwd.start()
            passed.append(fwd)
        copies[0].wait()
        copies[1].wait()
        for cp in copies[2:]:
            cp.wait_send()
        for cp in passed:
            cp.wait()

    vm = pl.BlockSpec(memory_space=pltpu.VMEM)
    return pl.pallas_call(
        body, name=name, in_specs=[vm], out_specs=vm,
        out_shape=jax.ShapeDtypeStruct((N_DEV,) + shard.shape, shard.dtype),
        scratch_shapes=[pltpu.SemaphoreType.DMA((7,)), pltpu.SemaphoreType.DMA((7,)), pltpu.SemaphoreType.DMA((1,))],
    )(shard)


def _rs_pair_add(mine, theirs, name):
    _, r, c = mine.shape
    tm = _tile(r, 256, 16)

    def kern(g_ref, l_ref, own_ref, hb_ref):
        s = g_ref[...] + l_ref[...]
        hb_ref[...] = s.astype(BF16)

        @pl.when(pl.program_id(1) == 0)
        def _():
            own_ref[...] = s

    block = pl.BlockSpec((None, tm, c), lambda i, q: (q, i, 0))
    return pl.pallas_call(
        kern, name=name, grid=(r // tm, 4), in_specs=[block, block],
        out_specs=[pl.BlockSpec((tm, c), lambda i, q: (i, 0)), block],
        out_shape=[jax.ShapeDtypeStruct((r, c), F32), jax.ShapeDtypeStruct((4, r, c), BF16)],
        compiler_params=_params(("parallel", "arbitrary")),
    )(mine, theirs)


def _adam_math(g, w, m, v):
    m = ADAM_B1 * m + (1.0 - ADAM_B1) * g
    v = ADAM_B2 * v + (1.0 - ADAM_B2) * (g * g)
    m_hat = m / (1.0 - ADAM_B1 ** ADAM_STEP)
    v_hat = v / (1.0 - ADAM_B2 ** ADAM_STEP)
    delta = -ADAM_LR * (m_hat / (jnp.sqrt(v_hat) + ADAM_EPS) + ADAM_WD * w)
    return delta, m, v


def _rs_finish_adam(own, landed, w, m, v, name):
    r, c = own.shape
    tm = _tile(r, 128, 16)

    def kern(h_ref, l_ref, w_ref, m_ref, v_ref, *outs):
        g = ((h_ref[...] + l_ref[0].astype(F32)) + l_ref[1].astype(F32)) + l_ref[2].astype(F32)
        delta, m_new, v_new = _adam_math(g, w_ref[...], m_ref[...], v_ref[...])
        for ref, val in zip(outs, (g, delta, m_new, v_new)):
            ref[...] = val

    flat = pl.BlockSpec((tm, c), lambda i: (i, 0))
    return pl.pallas_call(
        kern, name=name, grid=(r // tm,), in_specs=[flat, pl.BlockSpec((3, tm, c), lambda i: (0, i, 0)), flat, flat, flat],
        out_specs=[flat] * 4, out_shape=[jax.ShapeDtypeStruct((r, c), F32)] * 4,
        compiler_params=_params(("parallel",)))(own, landed, w, m, v)


def _allreduce_adam(gathered, w, m, v, name):
    _, r, c = gathered.shape

    def kern(a_ref, w_ref, m_ref, v_ref, g_out, d_out, m_out, v_out):
        g = a_ref[0]
        for j in range(1, N_DEV):
            g = g + a_ref[j]
        delta, m_new, v_new = _adam_math(g, w_ref[...], m_ref[...], v_ref[...])
        g_out[...] = g
        d_out[...] = delta
        m_out[...] = m_new
        v_out[...] = v_new

    return pl.pallas_call(
        kern, name=name, out_shape=[jax.ShapeDtypeStruct((r, c), F32)] * 4,
        compiler_params=_params(),
    )(gathered, w, m, v)


def _cols_from_blocks(g):
    n, k, nb = g.shape
    return jnp.transpose(g, (1, 0, 2)).reshape(k, n * nb)


def _blocks_from_cols(w):
    k, n = w.shape
    return jnp.transpose(w.reshape(k, N_DEV, n // N_DEV), (1, 0, 2))


def _pack_small(parts):
    flat = []
    for p in parts:
        v = p.reshape(-1)
        flat.append(jnp.pad(v, (0, (-v.shape[0]) % (8 * LANES))))
    return jnp.concatenate(flat).reshape(-1, LANES)


def _unpack_small(buf, like):
    out, pos = [], 0
    flat = buf.reshape(-1)
    for p in like:
        size = p.size
        out.append(flat[pos:pos + size].reshape(p.shape))
        pos += size + (-size) % (8 * LANES)
    return out


def kernel(x, mem, ffn1_norm, ffn1_w_gate_up, ffn1_w_down, mix_norm, mem_norm, w_in, b_forget, pool_w, pool_scale, w_pool_up, fox_q_norm, fox_k_norm, w_fox_o, w_mem_kv, mem_q_norm, mem_k_norm, w_mem_o, w_out, ffn2_norm, ffn2_w_gate_up, ffn2_w_down, loss_target, m_ffn1_norm, m_ffn1_w_gate_up, m_ffn1_w_down, m_mix_norm, m_mem_norm, m_w_in, m_b_forget, m_pool_w, m_pool_scale, m_w_pool_up, m_fox_q_norm, m_fox_k_norm, m_w_fox_o, m_w_mem_kv, m_mem_q_norm, m_mem_k_norm, m_w_mem_o, m_w_out, m_ffn2_norm, m_ffn2_w_gate_up, m_ffn2_w_down, v_ffn1_norm, v_ffn1_w_gate_up, v_ffn1_w_down, v_mix_norm, v_mem_norm, v_w_in, v_b_forget, v_pool_w, v_pool_scale, v_w_pool_up, v_fox_q_norm, v_fox_k_norm, v_w_fox_o, v_w_mem_kv, v_mem_q_norm, v_mem_k_norm, v_w_mem_o, v_w_out, v_ffn2_norm, v_ffn2_w_gate_up, v_ffn2_w_down):
    names = ["ffn1_norm", "ffn1_w_gate_up", "ffn1_w_down", "mix_norm", "mem_norm", "w_in", "b_forget", "pool_w",
             "pool_scale", "w_pool_up", "fox_q_norm", "fox_k_norm", "w_fox_o", "w_mem_kv", "mem_q_norm",
             "mem_k_norm", "w_mem_o", "w_out", "ffn2_norm", "ffn2_w_gate_up", "ffn2_w_down"]
    w_args = [ffn1_norm, ffn1_w_gate_up, ffn1_w_down, mix_norm, mem_norm, w_in, b_forget, pool_w, pool_scale,
              w_pool_up, fox_q_norm, fox_k_norm, w_fox_o, w_mem_kv, mem_q_norm, mem_k_norm, w_mem_o, w_out,
              ffn2_norm, ffn2_w_gate_up, ffn2_w_down]
    m_args = [m_ffn1_norm, m_ffn1_w_gate_up, m_ffn1_w_down, m_mix_norm, m_mem_norm, m_w_in, m_b_forget, m_pool_w,
              m_pool_scale, m_w_pool_up, m_fox_q_norm, m_fox_k_norm, m_w_fox_o, m_w_mem_kv, m_mem_q_norm,
              m_mem_k_norm, m_w_mem_o, m_w_out, m_ffn2_norm, m_ffn2_w_gate_up, m_ffn2_w_down]
    v_args = [v_ffn1_norm, v_ffn1_w_gate_up, v_ffn1_w_down, v_mix_norm, v_mem_norm, v_w_in, v_b_forget, v_pool_w,
              v_pool_scale, v_w_pool_up, v_fox_q_norm, v_fox_k_norm, v_w_fox_o, v_w_mem_kv, v_mem_q_norm,
              v_mem_k_norm, v_w_mem_o, v_w_out, v_ffn2_norm, v_ffn2_w_gate_up, v_ffn2_w_down]
    W = dict(zip(names, w_args))
    M = dict(zip(names, m_args))
    V = dict(zip(names, v_args))

    batch, seq, d = x.shape
    mlen = mem.shape[1]
    t = batch * seq
    gate_w = 3 * d
    z_gate, z_q = 0, gate_w
    z_k, z_v = z_q + FOX_WIDTH, z_q + 2 * FOX_WIDTH
    z_u = z_q + 3 * FOX_WIDTH
    z_qm = z_u + POOL_WIDTH
    z_f = z_qm + MEM_WIDTH
    z_width = -(-(z_f + F_PAD) // 512) * 512

    x2d = x.reshape(t, d)
    mem2d = mem.reshape(batch * mlen, d)
    tgt2d = loss_target.reshape(t, d)

    big = ["ffn1_w_gate_up", "ffn1_w_down", "w_in", "w_pool_up", "w_fox_o", "w_mem_kv", "w_mem_o", "w_out",
           "ffn2_w_gate_up", "ffn2_w_down"]
    col_sharded = {"ffn1_w_gate_up", "ffn2_w_gate_up", "w_in", "w_pool_up", "w_fox_o", "w_mem_o"}
    mixer_small = ["w_pool_up", "w_fox_o", "w_mem_kv", "w_mem_o", "w_out"]
    shard = {n: W[n][0].astype(BF16) for n in big}

    def rows_of(g):
        return g.reshape(-1, g.shape[2])

    gu_caps_nt, gu_caps_tn = (1024, 1024, 1408), (1024, 1408, 2048)
    down_caps, dwd_caps = (1024, 1024, 2816), (1408, 1024, 2048)

    wgu1 = _gather_first_relayed(shard["ffn1_w_gate_up"], "ag_first")
    h1, (wgu1,) = _rms_fwd(x2d, W["ffn1_norm"], "ffn1_rms", comm=_gather_second([wgu1]))
    w_in_top, w_in_bot = shard["w_in"][:d // 2], shard["w_in"][d // 2:]
    (gu1, a1), bufs = _gate_up_swiglu(h1, wgu1, "ffn1_gu", comm=_gather_first([shard["ffn1_w_down"], w_in_top]))
    wd1_g, w_in_top_g = _comm_call(_gather_second(bufs), "ag_second")
    wd1 = rows_of(wd1_g)
    x1, (w_in_bot_g,) = _mm(a1, wd1, "nn", "ffn1_down", scale=0.5, res=x2d, caps=down_caps,
                            comm=_gather_first([w_in_bot]))
    h2, (w_in_bot_g,) = _rms_fwd(x1, W["mix_norm"], "mix_rms", comm=_gather_second([w_in_bot_g]))

    o_u, o_q, o_v = 0, POOL_WIDTH, POOL_WIDTH + 2 * FOX_WIDTH
    o_f = o_v + FOX_WIDTH
    o_qm = o_f + FOX_HEADS
    o_g = o_qm + MEM_WIDTH
    wi = jnp.concatenate([_cols_from_blocks(w_in_top_g), _cols_from_blocks(w_in_bot_g)], axis=0)
    w_in_pad = jnp.concatenate(
        [wi[:, o_g:o_g + gate_w], wi[:, o_q:o_q + 3 * FOX_WIDTH], wi[:, o_u:o_u + POOL_WIDTH],
         wi[:, o_qm:o_qm + MEM_WIDTH], wi[:, o_f:o_f + FOX_HEADS],
         jnp.zeros((d, z_width - z_f - FOX_HEADS), BF16)], axis=1)
    group_b = mixer_small + ["ffn2_w_down"]
    z, bufs = _mm(h2, w_in_pad, "nn", "mix_in", caps=(1024, 1536, 2048), comm=_gather_first([shard[n] for n in group_b]))

    pool_w_b = W["pool_w"][0].astype(BF16)

    def pool_fwd_body(u, pw, ps):
        row = lax.broadcasted_iota(jnp.int32, (seq, 1), 0)
        diffs, mixed = [], []
        for g in range(POOL_GROUPS):
            ug = u[:, g * POOL_GROUP_DIM:(g + 1) * POOL_GROUP_DIM]
            cnt = jnp.minimum(row + 1, POOL_WINDOWS[g]).astype(F32)
            diff = _window_sum(ug, g + 1, True) / cnt - ug
            diffs.append(diff)
            mixed.append(lax.dot_general(diff.astype(BF16), pw[g], _DIMS["nn"], preferred_element_type=F32))
        diffs = jnp.concatenate(diffs, axis=1)
        mixed = jnp.concatenate(mixed, axis=1)
        return mixed * ps, diffs, mixed

    ypp, pool_diff, pool_mixed = _rowwise(
        pool_fwd_body, [(z, z_u, POOL_WIDTH)], [pool_w_b, W["pool_scale"]],
        [(POOL_WIDTH, BF16), (POOL_WIDTH, BF16), (POOL_WIDTH, F32)], [], "pool_fwd", seq)

    gq = jnp.tile(W["fox_q_norm"], (1, FOX_HEADS))
    gk = jnp.tile(W["fox_k_norm"], (1, FOX_HEADS))
    b_pad = jnp.pad(W["b_forget"], ((0, 0), (0, F_PAD - FOX_HEADS)))

    def fox_prep_body(q, k, v, f, gqv, gkv, bv):
        qn_scaled = _head_norm(q, gqv, FOX_HEAD_DIM).astype(BF16) * FOX_HEAD_DIM ** -0.5
        return qn_scaled, _head_norm(k, gkv, FOX_HEAD_DIM), v, _log_sigmoid(f + bv)

    tm_e = _tile(t, 256, 16)
    (qn, kn, vb, logf), bufs = _rowwise(
        fox_prep_body, [(z, z_q, FOX_WIDTH), (z, z_k, FOX_WIDTH), (z, z_v, FOX_WIDTH), (z, z_f, F_PAD)],
        [gq, gk, b_pad], [(FOX_WIDTH, BF16), (FOX_WIDTH, BF16), (FOX_WIDTH, BF16), (F_PAD, F32)], [], "fox_prep", tm_e,
        comm=_gather_second(bufs))
    full = {n: (_cols_from_blocks(g) if n in col_sharded else rows_of(g)) for n, g in zip(group_b, bufs)}
    wd2 = full["ffn2_w_down"]
    y_pool = _mm(ypp, full["w_pool_up"], "nn", "pool_up")
    csum = _rowwise(lambda v: (_scan_rows(v, True),), [_whole(logf)], [], [(F_PAD, F32)], [], "fox_cumsum", seq)[0]
    cks = jnp.transpose(csum.reshape(batch, seq, F_PAD)[:, :, :FOX_HEADS], (0, 2, 1)).reshape(
        batch * FOX_HEADS // 2, 2, seq)
    (o_fox, lse), (wgu2,) = _fox_fwd(qn, kn, vb, cks, batch, seq, "fox_fwd",
                                     comm=_gather_first([shard["ffn2_w_gate_up"]]))
    y_fox = _mm(o_fox, full["w_fox_o"], "nn", "fox_o")

    memn = _rms_fwd(mem2d, W["mem_norm"], "mem_rms")
    kv = _mm(memn, full["w_mem_kv"], "nn", "mem_kv")
    gqm = jnp.tile(W["mem_q_norm"], (1, MEM_HEADS))
    gkm = jnp.tile(W["mem_k_norm"], (1, MEM_HEADS))
    qmn = _rowwise(lambda q, g: (_head_norm(q, g, MEM_HEAD_DIM),), [(z, z_qm, MEM_WIDTH)], [gqm],
                   [(MEM_WIDTH, BF16)], [], "memq_prep", tm_e)[0]
    kmn, vmb = _rowwise(lambda k, v, g: (_head_norm(k, g, MEM_HEAD_DIM), v),
                        [(kv, 0, MEM_WIDTH), (kv, MEM_WIDTH, MEM_WIDTH)], [gkm],
                        [(MEM_WIDTH, BF16), (MEM_WIDTH, BF16)], [], "memk_prep", _tile(batch * mlen, 256, 16))
    o_mem = _mem_fwd(qmn, kmn, vmb, batch, seq, mlen, "mem_fwd")
    y_mem = _mm(o_mem, full["w_mem_o"], "nn", "mem_o")

    def gate_fwd_body(gp, gf, gm, yp, yf, ym):
        return ((_sigmoid(gp) * yp + _sigmoid(gf) * yf) + _sigmoid(gm) * ym,)

    tm_g = _tile(t, 128, 16)
    (merged,), (wgu2,) = _rowwise(
        gate_fwd_body, [(z, 0, d), (z, d, d), (z, 2 * d, d), _whole(y_pool), _whole(y_fox), _whole(y_mem)],
        [], [(d, BF16)], [], "gate_fwd", tm_g, comm=_gather_second([wgu2]))
    x2 = _mm(merged, full["w_out"], "nn", "mix_out", res=x1)

    h3 = _rms_fwd(x2, W["ffn2_norm"], "ffn2_rms")
    gu2, a2 = _gate_up_swiglu(h3, wgu2, "ffn2_gu")
    x3 = _mm(a2, wd2, "nn", "ffn2_down", scale=0.5, res=x2, caps=down_caps)
    dy, dy_b, loss_part = _loss_head(x3, tgt2d, "loss")
    loss = lax.psum(loss_part[0, 0], ("x", "y", "c"))

    G, own, landed = {}, {}, {}

    def row_blocks(g):
        return g.reshape(N_DEV, g.shape[0] // N_DEV, g.shape[1])

    def pair_add(n, mine, theirs):
        own[n], chip_sums = _rs_pair_add(mine, theirs, "rs_add_" + n)
        return chip_sums

    p_wd2 = row_blocks(_mm(a2, dy_b, "tn", "ffn2_dwd", scale=0.5, caps=dwd_caps))
    dgu2, halves = _swiglu_bwd_from_out(dy_b, wd2, gu2, "ffn2_dgu", comm=_scatter_first([p_wd2]))
    s_wd2 = pair_add("ffn2_w_down", *halves)
    p_wgu2, (landed["ffn2_w_down"],) = _mm(h3, dgu2, "tn", "ffn2_dwgu", caps=gu_caps_tn, out_blocks=True, halves="b",
                                           comm=_scatter_second([s_wd2]))
    dh3, halves = _mm(dgu2, wgu2, "nt", "ffn2_dh", caps=gu_caps_nt, b_blocks=True, halves="a",
                         comm=_scatter_first([p_wgu2]))
    s_wgu2 = pair_add("ffn2_w_gate_up", *halves)
    dx2, dx2_b, G["ffn2_norm"] = _rms_bwd(x2, W["ffn2_norm"], dh3, dy, "ffn2_drms")

    dmerged = _mm(dx2_b, full["w_out"], "nt", "mix_out_dx")
    P = {"w_out": row_blocks(_mm(merged, dx2_b, "tn", "mix_out_dw"))}

    def gate_bwd_body(gp, gf, gm, yp, yf, ym, dm):
        outs_dl, outs_dy = [], []
        for gl, yv in ((gp, yp), (gf, yf), (gm, ym)):
            s = _sigmoid(gl)
            outs_dl.append((dm * yv) * (s * (1.0 - s)))
            outs_dy.append(dm * s)
        return (jnp.concatenate(outs_dl, axis=1), *outs_dy)

    dz, dy_pool, dy_fox, dy_mem = _rowwise(
        gate_bwd_body, [(z, 0, d), (z, d, d), (z, 2 * d, d), _whole(y_pool), _whole(y_fox), _whole(y_mem), _whole(dmerged)],
        [], [(gate_w, BF16), (d, BF16), (d, BF16), (d, BF16)], [], "gate_bwd", tm_g, into=(None, z_width, z_gate))

    dypp = _mm(dy_pool, full["w_pool_up"], "nt", "pool_up_dx")
    P["w_pool_up"] = _blocks_from_cols(_mm(ypp, dy_pool, "tn", "pool_up_dw"))

    def pool_bwd_body(dyv, mixed, diff, pw, ps):
        row = lax.broadcasted_iota(jnp.int32, (seq, 1), 0)
        d_scale = jnp.sum(dyv * mixed, axis=0, keepdims=True)
        dmix = (dyv * ps).astype(BF16)
        du, dpw = [], []
        for g in range(POOL_GROUPS):
            sl = slice(g * POOL_GROUP_DIM, (g + 1) * POOL_GROUP_DIM)
            dmg = dmix[:, sl]
            ddiff = lax.dot_general(dmg, pw[g], _DIMS["nt"], preferred_element_type=F32)
            dpw.append(lax.dot_general(diff[:, sl], dmg, _DIMS["tn"], preferred_element_type=F32))
            cnt = jnp.minimum(row + 1, POOL_WINDOWS[g]).astype(F32)
            du.append(_window_sum(ddiff / cnt, g + 1, False) - ddiff)
        return jnp.concatenate(du, axis=1), d_scale, jnp.concatenate(dpw, axis=0)

    dz, G["pool_scale"], d_pool_w = _rowwise(
        pool_bwd_body, [_whole(dypp), _whole(pool_mixed), _whole(pool_diff)], [pool_w_b, W["pool_scale"]],
        [(POOL_WIDTH, BF16)], [((1, POOL_WIDTH), F32), ((POOL_WIDTH, POOL_GROUP_DIM), F32)],
        "pool_bwd", seq, into=(dz, z_width, z_u))
    G["pool_w"] = d_pool_w.reshape(1, POOL_GROUPS, POOL_GROUP_DIM, POOL_GROUP_DIM)

    do_fox = _mm(dy_fox, full["w_fox_o"], "nt", "fox_o_dx")
    P["w_fox_o"] = _blocks_from_cols(_mm(o_fox, dy_fox, "tn", "fox_o_dw"))
    (dkn, dz, dqn, dcks, dcq), (landed["ffn2_w_gate_up"],) = _fox_bwd(
        qn, kn, vb, cks, o_fox, lse, do_fox, dz, z_v, batch, seq, "fox_bwd", comm=_scatter_second([s_wgu2]))
    dcs = jnp.transpose(dcks.reshape(batch, FOX_HEADS, seq), (0, 2, 1)).reshape(t, FOX_HEADS)
    dcs = jnp.pad(dcs, ((0, 0), (0, F_PAD - FOX_HEADS)))
    dcq = jnp.pad(dcq.reshape(t, FOX_HEADS, FOX_HEAD_DIM)[:, :, 0], ((0, 0), (0, F_PAD - FOX_HEADS)))

    def fox_f_bwd_body(dc, dc_rows, f, bv):
        lane = lax.broadcasted_iota(jnp.int32, (1, F_PAD), 1)
        dlogf = _scan_rows(dc + dc_rows, False)
        df = jnp.where(lane < FOX_HEADS, dlogf * _sigmoid(-(f + bv)), 0.0)
        behind = jnp.zeros((seq, z_width - z_f - F_PAD), F32)
        return jnp.concatenate([df, behind], axis=1), jnp.sum(df, axis=0, keepdims=True)

    dz, db_pad = _rowwise(fox_f_bwd_body, [_whole(dcs), _whole(dcq), (z, z_f, F_PAD)], [b_pad],
                          [(z_width - z_f, BF16)], [((1, F_PAD), F32)], "fox_f_bwd", seq, into=(dz, z_width, z_f))
    G["b_forget"] = db_pad[:, :FOX_HEADS]

    def fox_head_bwd(dz_in, col, dyn, gain, name):
        return _rowwise(lambda v, dyv, g: _head_norm_bwd(v, g, dyv, FOX_HEAD_DIM), [(z, col, FOX_WIDTH), _whole(dyn)],
                        [gain], [(FOX_WIDTH, BF16)], [((1, FOX_WIDTH), F32)], name, tm_e, into=(dz_in, z_width, col))

    dz, dgq_t = fox_head_bwd(dz, z_q, dqn, gq, "fox_q_bwd")
    dz, dgk_t = fox_head_bwd(dz, z_k, dkn, gk, "fox_k_bwd")
    G["fox_q_norm"] = jnp.sum(dgq_t.reshape(FOX_HEADS, FOX_HEAD_DIM), axis=0, keepdims=True)
    G["fox_k_norm"] = jnp.sum(dgk_t.reshape(FOX_HEADS, FOX_HEAD_DIM), axis=0, keepdims=True)

    do_mem = _mm(dy_mem, full["w_mem_o"], "nt", "mem_o_dx")
    P["w_mem_o"] = _blocks_from_cols(_mm(o_mem, dy_mem, "tn", "mem_o_dw"))
    dqmn, dkmn, dvm = _mem_bwd(qmn, kmn, vmb, do_mem, batch, seq, mlen, "mem_bwd")
    dz, dgqm_t = _rowwise(lambda q, dq, g: _head_norm_bwd(q, g, dq, MEM_HEAD_DIM),
                          [(z, z_qm, MEM_WIDTH), _whole(dqmn)], [gqm], [(MEM_WIDTH, BF16)],
                          [((1, MEM_WIDTH), F32)], "memq_bwd", tm_e, into=(dz, z_width, z_qm))

    def memk_bwd_body(k, dk, dv, g):
        dkr, dg = _head_norm_bwd(k, g, dk, MEM_HEAD_DIM)
        return jnp.concatenate([dkr, dv], axis=1), dg

    dkv, dgkm_t = _rowwise(memk_bwd_body, [(kv, 0, MEM_WIDTH), _whole(dkmn), _whole(dvm)], [gkm],
                           [(2 * MEM_WIDTH, BF16)], [((1, MEM_WIDTH), F32)], "memk_bwd", _tile(batch * mlen, 256, 16))
    G["mem_q_norm"] = jnp.sum(dgqm_t.reshape(MEM_HEADS, MEM_HEAD_DIM), axis=0, keepdims=True)
    G["mem_k_norm"] = jnp.sum(dgkm_t.reshape(MEM_HEADS, MEM_HEAD_DIM), axis=0, keepdims=True)
    P["w_mem_kv"] = row_blocks(_mm(memn, dkv, "tn", "mem_kv_dw"))
    dmemn = _mm(dkv, full["w_mem_kv"], "nt", "mem_kv_dx")
    _, _, G["mem_norm"] = _rms_bwd(mem2d, W["mem_norm"], dmemn, None, "mem_drms")

    d_w_in_pad, halves = _mm(h2, dz, "tn", "mix_in_dw", caps=(1024, 1536, 2048),
                             comm=_scatter_first([P[n] for n in mixer_small]))
    s_small = [pair_add(n, halves[2 * i], halves[2 * i + 1]) for i, n in enumerate(mixer_small)]
    p = d_w_in_pad
    p_w_in = _blocks_from_cols(jnp.concatenate(
        [p[:, z_u:z_u + POOL_WIDTH], p[:, z_q:z_q + 3 * FOX_WIDTH], p[:, z_f:z_f + FOX_HEADS],
         p[:, z_qm:z_qm + MEM_WIDTH], p[:, z_gate:z_gate + gate_w]], axis=1))
    dh2, rest = _mm(dz, w_in_pad, "nt", "mix_in_dx", caps=(1024, 1024, 1792),
                    comm=_join(_scatter_second(s_small), _scatter_first([p_w_in])))
    for n, l in zip(mixer_small, rest[:len(mixer_small)]):
        landed[n] = l
    s_w_in = pair_add("w_in", rest[-2], rest[-1])
    dx1, dx1_b, G["mix_norm"] = _rms_bwd(x1, W["mix_norm"], dh2, dx2, "mix_drms")

    dgu1 = _swiglu_bwd_from_out(dx1_b, wd1, gu1, "ffn1_dgu")
    p_wgu1, (landed["w_in"],) = _mm(h1, dgu1, "tn", "ffn1_dwgu", caps=gu_caps_tn, out_blocks=True, halves="b",
                                    comm=_scatter_second([s_w_in]))
    d_wd1, halves = _mm(a1, dx1_b, "tn", "ffn1_dwd", scale=0.5, caps=dwd_caps, comm=_scatter_first([p_wgu1]))
    p_wd1 = row_blocks(d_wd1)
    s_wgu1 = pair_add("ffn1_w_gate_up", *halves)
    dh1, (landed["ffn1_w_gate_up"], *halves) = _mm(dgu1, wgu1, "nt", "ffn1_dh", caps=gu_caps_nt, b_blocks=True, halves="a",
                                                 comm=_join(_scatter_second([s_wgu1]), _scatter_first([p_wd1])))
    s_wd1 = pair_add("ffn1_w_down", *halves)
    (dx0, _, G["ffn1_norm"]), (landed["ffn1_w_down"],) = _rms_bwd(x2d, W["ffn1_norm"], dh1, dx1, "ffn1_drms",
                                                                  comm=_scatter_second([s_wd1]))
    grad_x = dx0.reshape(batch, seq, d)

    out_g, out_d, out_m, out_v = {}, {}, {}, {}
    for n in big:
        res = _rs_finish_adam(own[n], landed[n], W[n][0], M[n][0], V[n][0], "adam_" + n)
        out_g[n], out_d[n], out_m[n], out_v[n] = [r[None] for r in res]

    small = [n for n in names if n not in big]
    g_small = _pack_small([G[n].reshape(W[n].shape) for n in small])
    all_small = _all_gather_vmem(g_small, "ag_small_grads")
    res = _allreduce_adam(all_small, _pack_small([W[n] for n in small]), _pack_small([M[n] for n in small]),
                          _pack_small([V[n] for n in small]), "adam_small")
    like = [W[n] for n in small]
    for dst, buf in zip((out_g, out_d, out_m, out_v), res):
        for n, a in zip(small, _unpack_small(buf, like)):
            dst[n] = a

    return (loss, grad_x, *[out_g[n] for n in names], *[out_d[n] for n in names],
            *[out_m[n] for n in names], *[out_v[n] for n in names])
```

```python
import functools

import jax
import jax.numpy as jnp
from jax import lax
from jax.experimental import pallas as pl
from jax.experimental.pallas import tpu as pltpu

F32 = jnp.float32
BF16 = jnp.bfloat16
MESH = pl.DeviceIdType.MESH

N_DEV = 8
EPS = 1e-6
FOX_HEADS = 16
FOX_HEAD_DIM = 64
FOX_WIDTH = FOX_HEADS * FOX_HEAD_DIM
MEM_HEADS = 4
MEM_HEAD_DIM = 128
MEM_WIDTH = MEM_HEADS * MEM_HEAD_DIM
POOL_GROUPS = 4
POOL_GROUP_DIM = 128
POOL_WIDTH = POOL_GROUPS * POOL_GROUP_DIM
POOL_WINDOWS = (2, 4, 8, 16)
LANES = 128
F_PAD = LANES

ADAM_LR = 0.001
ADAM_B1 = 0.9
ADAM_B2 = 0.999
ADAM_EPS = 1e-08
ADAM_WD = 0.01
ADAM_STEP = 10

VMEM_LIMIT = 56 * 1024 * 1024
NEG = -1e30

ANY = pl.BlockSpec(memory_space=pl.ANY)


def _params(sem=None):
    return pltpu.CompilerParams(dimension_semantics=sem, vmem_limit_bytes=VMEM_LIMIT)


MXU_DIM = 256


def _tile(dim, cap, align):
    best = None
    t = align
    while t <= min(dim, cap):
        if dim % t == 0:
            best = t
        t += align
    return dim if best is None else best


def _mxu_tile(dim, cap):
    wide, fine = _tile(dim, cap, MXU_DIM), _tile(dim, cap, LANES)
    whole_widths = wide % MXU_DIM == 0 and wide <= cap
    return wide if whole_widths and fine < 2 * wide else fine


class _Sems:
    def __init__(self, send, recv, local):
        self.send, self.recv, self.local = send, recv, local
        self.n_remote = self.n_local = 0

    def remote(self):
        i = self.n_remote
        self.n_remote += 1
        return self.send.at[i], self.recv.at[i]

    def one_local(self):
        i = self.n_local
        self.n_local += 1
        return self.local.at[i]


class _Comm:
    def __init__(self, srcs, outs, n_remote, n_local, plan, in_place=False):
        self.srcs, self.outs, self.n_remote, self.n_local = list(srcs), list(outs), n_remote, n_local
        self.plan, self.in_place = plan, in_place

    def aliases(self, n_in, n_out):
        return {n_in + i: n_out + i for i in range(len(self.srcs))} if self.in_place else {}

    def scratch(self):
        return [pltpu.SemaphoreType.DMA((self.n_remote,)), pltpu.SemaphoreType.DMA((self.n_remote,)),
                pltpu.SemaphoreType.DMA((max(self.n_local, 1),))]

    def run(self, src_refs, out_refs, sem_refs, first, last):
        def copies():
            return self.plan(list(src_refs), list(out_refs), _Sems(*sem_refs))

        return (lambda: _when(first, lambda: [cp.start() for cp in copies()]),
                lambda: _when(last, lambda: [cp.wait() for cp in copies()]))


def _when(cond, fn):
    @pl.when(cond)
    def _():
        fn()


def _join(*comms):
    comms = [c for c in comms if c is not None]
    assert all(not c.in_place for c in comms)

    def plan(src_refs, out_refs, sems):
        copies, si, oi = [], 0, 0
        for c in comms:
            copies += c.plan(src_refs[si:si + len(c.srcs)], out_refs[oi:oi + len(c.outs)], sems)
            si += len(c.srcs)
            oi += len(c.outs)
        return copies

    return _Comm(sum((c.srcs for c in comms), []), sum((c.outs for c in comms), []),
                 sum(c.n_remote for c in comms), sum(c.n_local for c in comms), plan)


def _split_refs(refs, n_in, n_out, comm):
    if comm is None:
        return refs[:n_in], refs[n_in:n_in + n_out], refs[n_in + n_out:], (), (), ()
    ns, no = len(comm.srcs), len(comm.outs)
    ins = refs[:n_in]
    srcs = refs[n_in:n_in + ns]
    outs = refs[n_in + ns:n_in + ns + n_out]
    couts = refs[n_in + ns + n_out:n_in + ns + n_out + no]
    rest = refs[n_in + ns + n_out + no:]
    return ins, outs, rest[:-3], srcs, couts, rest[-3:]


def _sigmoid(x):
    return 1.0 / (1.0 + jnp.exp(-x))


_DIMS = {"nn": (((1,), (0,)), ((), ())), "nt": (((1,), (1,)), ((), ())), "tn": (((0,), (0,)), ((), ()))}


def _mm(a, b, mode, name, out_dtype=F32, scale=1.0, res=None, caps=None, b_blocks=False, out_blocks=False,
        halves=None, comm=None):
    nblk = N_DEV
    a_shape, b_shape = a.shape, b.shape
    if b_blocks:
        _, r0, c0 = b.shape
        b_shape = (r0, nblk * c0)
    if halves == "a":
        assert mode == "nt"
        a_shape = (a.shape[1], 2 * a.shape[2])
    elif halves == "b":
        assert mode == "tn"
        b_shape = (b.shape[1], 2 * b.shape[2])
    if mode == "nn":
        (m, k), (k2, n) = a_shape, b_shape
    elif mode == "nt":
        (m, k), (n, k2) = a_shape, b_shape
    else:
        (k, m), (k2, n) = a_shape, b_shape
    assert k == k2, (name, a.shape, b.shape)
    cm, cn, ck = caps or ((1024, 512, 2048) if k <= 2048 else (1024, 1024, 2048))
    n_unit = n // nblk if (out_blocks or (b_blocks and mode == "nn")) else n
    k_unit = k // nblk if (b_blocks and mode == "nt") else k
    tm, tn, tk = _mxu_tile(m, cm), _mxu_tile(n_unit, cn), _mxu_tile(k_unit, ck)
    nk = k // tk
    npb, kpb = n_unit // tn, k_unit // tk
    n_half, k_half = (n // 2) // tn, (k // 2) // tk
    if mode == "nn":
        a_spec = pl.BlockSpec((tm, tk), lambda i, j, kk: (i, kk))
        b_spec = pl.BlockSpec((tk, tn), lambda i, j, kk: (kk, j))
        if b_blocks:
            b_spec = pl.BlockSpec((None, tk, tn), lambda i, j, kk: (j // npb, kk, j % npb))
    elif mode == "nt":
        a_spec = pl.BlockSpec((tm, tk), lambda i, j, kk: (i, kk))
        b_spec = pl.BlockSpec((tn, tk), lambda i, j, kk: (j, kk))
        if b_blocks:
            b_spec = pl.BlockSpec((None, tn, tk), lambda i, j, kk: (kk // kpb, j, kk % kpb))
        if halves == "a":
            assert (k // 2) % tk == 0
            a_spec = pl.BlockSpec((None, tm, tk), lambda i, j, kk: (kk // k_half, i, kk % k_half))
    else:
        assert not b_blocks
        a_spec = pl.BlockSpec((tk, tm), lambda i, j, kk: (kk, i))
        b_spec = pl.BlockSpec((tk, tn), lambda i, j, kk: (kk, j))
        if halves == "b":
            assert (n // 2) % tn == 0
            b_spec = pl.BlockSpec((None, tk, tn), lambda i, j, kk: (j // n_half, kk, j % n_half))
    if out_blocks:
        assert res is None
        o_spec = pl.BlockSpec((None, tm, tn), lambda i, j, kk: (j // npb, i, j % npb))
        o_shape = jax.ShapeDtypeStruct((nblk, m, n // nblk), out_dtype)
    else:
        o_spec = pl.BlockSpec((tm, tn), lambda i, j, kk: (i, j))
        o_shape = jax.ShapeDtypeStruct((m, n), out_dtype)
    in_specs = [a_spec, b_spec] + ([o_spec] if res is not None else [])
    n_in = len(in_specs)
    dims = _DIMS[mode]
    gm, gn = m // tm, n // tn

    def kern(*refs):
        ins, outs, scratch, c_src, c_out, c_sem = _split_refs(refs, n_in, 1, comm)
        a_ref, b_ref = ins[0], ins[1]
        res_ref = ins[2] if res is not None else None
        o_ref = outs[0]
        if comm is not None:
            i, j, kq = pl.program_id(0), pl.program_id(1), pl.program_id(2)
            first = jnp.logical_and(jnp.logical_and(i == 0, j == 0), kq == 0)
            last = jnp.logical_and(jnp.logical_and(i == gm - 1, j == gn - 1), kq == nk - 1)
            start_comm, wait_comm = comm.run(c_src, c_out, c_sem, first, last)
            start_comm()
        a_tile = a_ref[...].astype(BF16)
        if scale != 1.0:
            a_tile = a_tile * scale

        def product():
            return lax.dot_general(a_tile, b_ref[...].astype(BF16), dims, preferred_element_type=F32)

        if nk == 1:
            r = product()
            if res_ref is not None:
                r = res_ref[...] + r
            o_ref[...] = r.astype(out_dtype)
        else:
            acc_ref = scratch[0] if scratch else o_ref
            kk = pl.program_id(2)

            @pl.when(kk == 0)
            def _():
                acc_ref[...] = jnp.zeros_like(acc_ref) if res_ref is None else res_ref[...]

            acc_ref[...] += product()
            if scratch:
                @pl.when(kk == nk - 1)
                def _():
                    o_ref[...] = acc_ref[...].astype(out_dtype)
        if comm is not None:
            wait_comm()

    assert scale in (1.0, 0.5)
    args = (a, b) + ((res,) if res is not None else ())
    scratch_shapes = [pltpu.VMEM((tm, tn), F32)] if (nk > 1 and out_dtype != F32) else []
    if comm is None:
        return pl.pallas_call(
            kern, name=name, grid=(gm, gn, nk), in_specs=in_specs, out_specs=o_spec, out_shape=o_shape,
            scratch_shapes=scratch_shapes, compiler_params=_params(("parallel", "parallel", "arbitrary")),
        )(*args)
    res_all = pl.pallas_call(
        kern, name=name, grid=(gm, gn, nk), in_specs=in_specs + [ANY] * len(comm.srcs),
        out_specs=[o_spec] + [ANY] * len(comm.outs), out_shape=[o_shape] + comm.outs,
        scratch_shapes=scratch_shapes + comm.scratch(), input_output_aliases=comm.aliases(n_in, 1),
        compiler_params=_params(("arbitrary", "arbitrary", "arbitrary")),
    )(*args, *comm.srcs)
    return res_all[0], list(res_all[1:])


def _gate_up_swiglu(h, w_blocks, name, comm=None):
    t, d = h.shape
    nblk, _, nb = w_blocks.shape
    half = nblk // 2
    tm = _mxu_tile(t, 512)
    steps = t // tm

    def kern(*refs):
        (h_ref, wg_ref, wu_ref), (gu_ref, a_ref), _, c_src, c_out, c_sem = _split_refs(refs, 3, 2, comm)
        if comm is not None:
            j, i = pl.program_id(0), pl.program_id(1)
            start_comm, wait_comm = comm.run(c_src, c_out, c_sem, jnp.logical_and(j == 0, i == 0),
                                             jnp.logical_and(j == half - 1, i == steps - 1))
            start_comm()
        hv = h_ref[...]
        g = lax.dot_general(hv, wg_ref[...], _DIMS["nn"], preferred_element_type=F32)
        u = lax.dot_general(hv, wu_ref[...], _DIMS["nn"], preferred_element_type=F32)
        gu_ref[0] = g
        gu_ref[1] = u
        a_ref[...] = ((g * _sigmoid(g)) * u).astype(BF16)
        if comm is not None:
            wait_comm()

    in_specs = [pl.BlockSpec((tm, d), lambda j, i: (i, 0)), pl.BlockSpec((None, d, nb), lambda j, i: (j, 0, 0)),
                pl.BlockSpec((None, d, nb), lambda j, i: (j + half, 0, 0))]
    out_specs = [pl.BlockSpec((2, tm, nb), lambda j, i: (0, i, j)), pl.BlockSpec((tm, nb), lambda j, i: (i, j))]
    out_shape = [jax.ShapeDtypeStruct((2, t, half * nb), F32), jax.ShapeDtypeStruct((t, half * nb), BF16)]
    if comm is None:
        return pl.pallas_call(kern, name=name, grid=(half, steps), in_specs=in_specs, out_specs=out_specs,
                              out_shape=out_shape, compiler_params=_params(("parallel", "parallel")))(h, w_blocks, w_blocks)
    res_all = pl.pallas_call(
        kern, name=name, grid=(half, steps), in_specs=in_specs + [ANY] * len(comm.srcs),
        out_specs=out_specs + [ANY] * len(comm.outs), out_shape=out_shape + comm.outs,
        scratch_shapes=comm.scratch(), compiler_params=_params(("arbitrary", "arbitrary")),
    )(h, w_blocks, w_blocks, *comm.srcs)
    return list(res_all[:2]), list(res_all[2:])


def _swiglu_bwd_from_out(dy, wd, gu, name, comm=None):
    t, d = dy.shape
    f = wd.shape[0]
    tm, tn = _mxu_tile(t, 1024), _mxu_tile(f, 512)
    gi, gj = t // tm, f // tn

    def kern(*refs):
        (dy_ref, wd_ref, gu_ref), (out_ref,), _, c_src, c_out, c_sem = _split_refs(refs, 3, 1, comm)
        if comm is not None:
            i, j = pl.program_id(0), pl.program_id(1)
            start_comm, wait_comm = comm.run(c_src, c_out, c_sem, jnp.logical_and(i == 0, j == 0),
                                             jnp.logical_and(i == gi - 1, j == gj - 1))
            start_comm()
        da = lax.dot_general(dy_ref[...] * 0.5, wd_ref[...], _DIMS["nt"], preferred_element_type=F32)
        g, u = gu_ref[0], gu_ref[1]
        s = _sigmoid(g)
        out_ref[0] = (da * u * (s * (1.0 + g * (1.0 - s)))).astype(BF16)
        out_ref[1] = (da * (g * s)).astype(BF16)
        if comm is not None:
            wait_comm()

    pair = pl.BlockSpec((2, tm, tn), lambda i, j: (0, i, j))
    in_specs = [pl.BlockSpec((tm, d), lambda i, j: (i, 0)), pl.BlockSpec((tn, d), lambda i, j: (j, 0)), pair]
    out_shape = jax.ShapeDtypeStruct((2, t, f), BF16)
    if comm is None:
        return pl.pallas_call(kern, name=name, grid=(gi, gj), in_specs=in_specs, out_specs=pair, out_shape=out_shape,
                              compiler_params=_params(("parallel", "parallel")))(dy, wd, gu)
    res_all = pl.pallas_call(
        kern, name=name, grid=(gi, gj), in_specs=in_specs + [ANY] * len(comm.srcs),
        out_specs=[pair] + [ANY] * len(comm.outs), out_shape=[out_shape] + comm.outs,
        scratch_shapes=comm.scratch(), compiler_params=_params(("arbitrary", "arbitrary")),
    )(dy, wd, gu, *comm.srcs)
    return res_all[0], list(res_all[1:])


def _rowwise(body, ins, params, outs, accs, name, tm, comm=None, into=None):
    rows = ins[0][0].shape[0]
    assert rows % tm == 0, (name, rows, tm)
    in_specs = []
    for arr, off, width in ins:
        assert off % width == 0 and arr.shape[0] == rows, (name, arr.shape, off, width)
        in_specs.append(pl.BlockSpec((tm, width), functools.partial(lambda i, c: (i, c), c=off // width)))
    for p in params:
        in_specs.append(pl.BlockSpec(p.shape, functools.partial(lambda i, nd: (0,) * nd, nd=p.ndim)))
    out_specs = [pl.BlockSpec((tm, w), lambda i: (i, 0)) for w, _ in outs]
    out_specs += [pl.BlockSpec(s, functools.partial(lambda i, nd: (0,) * nd, nd=len(s))) for s, _ in accs]
    out_shape = [jax.ShapeDtypeStruct((rows, w), d) for w, d in outs]
    out_shape += [jax.ShapeDtypeStruct(s, d) for s, d in accs]
    n_in, n_par, n_out = len(ins), len(params), len(outs)
    carried = []
    if into is not None:
        buf, total, col = into
        assert comm is None and col % outs[0][0] == 0, (name, col, outs[0])
        out_specs[0] = pl.BlockSpec((tm, outs[0][0]), functools.partial(lambda i, c: (i, c), c=col // outs[0][0]))
        out_shape[0] = jax.ShapeDtypeStruct((rows, total), outs[0][1])
        carried = [] if buf is None else [buf]

    steps = rows // tm

    def kern(*refs):
        in_refs, out_refs, _, c_src, c_out, c_sem = _split_refs(refs, n_in + n_par + len(carried),
                                                                 n_out + len(accs), comm)
        in_refs = in_refs[:n_in + n_par]
        first = pl.program_id(0) == 0
        if comm is not None:
            start_comm, wait_comm = comm.run(c_src, c_out, c_sem, first, pl.program_id(0) == steps - 1)
            start_comm()
        res = body(*[r[...] for r in in_refs])
        for r, v in zip(out_refs[:n_out], res[:n_out]):
            r[...] = v.astype(r.dtype)
        for r, v in zip(out_refs[n_out:], res[n_out:]):
            @pl.when(first)
            def _(r=r, v=v):
                r[...] = v.astype(r.dtype)

            @pl.when(jnp.logical_not(first))
            def _(r=r, v=v):
                r[...] += v.astype(r.dtype)
        if comm is not None:
            wait_comm()

    args = [a for a, _, _ in ins] + list(params)
    if comm is None:
        return pl.pallas_call(
            kern, name=name, grid=(steps,), in_specs=in_specs + [ANY] * len(carried), out_specs=out_specs,
            out_shape=out_shape, input_output_aliases={len(args): 0} if carried else {},
            compiler_params=_params(("arbitrary",) if accs else ("parallel",)),
        )(*args, *carried)
    res_all = pl.pallas_call(
        kern, name=name, grid=(steps,), in_specs=in_specs + [ANY] * len(comm.srcs),
        out_specs=out_specs + [ANY] * len(comm.outs), out_shape=out_shape + comm.outs,
        scratch_shapes=comm.scratch(), input_output_aliases=comm.aliases(len(args), len(out_shape)),
        compiler_params=_params(("arbitrary",)),
    )(*args, *comm.srcs)
    return list(res_all[:len(out_shape)]), list(res_all[len(out_shape):])


def _whole(x):
    return (x, 0, x.shape[1])


def _first(res, comm):
    return res[0] if comm is None else (res[0][0], res[1])


def _rms_fwd(x, gain, name, comm=None):
    def body(xv, g):
        r = lax.rsqrt(jnp.mean(xv * xv, axis=-1, keepdims=True) + EPS)
        return ((xv * r) * g,)

    return _first(_rowwise(body, [_whole(x)], [gain], [(x.shape[1], BF16)], [], name, _tile(x.shape[0], 256, 16),
                           comm=comm), comm)


def _rms_bwd(x, gain, dh, dres, name, comm=None):
    d = x.shape[1]

    def body(*vals):
        if dres is None:
            xv, dhv, g = vals
        else:
            xv, dhv, drv, g = vals
        r = lax.rsqrt(jnp.mean(xv * xv, axis=-1, keepdims=True) + EPS)
        xh = xv * r
        w = dhv * g
        dx = r * (w - xh * jnp.mean(w * xh, axis=-1, keepdims=True))
        if dres is not None:
            dx = drv + dx
        return dx, dx, jnp.sum(dhv * xh, axis=0, keepdims=True)

    ins = [_whole(x), _whole(dh)] + ([_whole(dres)] if dres is not None else [])
    return _rowwise(body, ins, [gain], [(d, F32), (d, BF16)], [((1, d), F32)], name, _tile(x.shape[0], 256, 16),
                    comm=comm)


def _loss_head(y, target, name):
    d = y.shape[1]

    def body(yv, tv):
        e = yv - tv
        part = jnp.sum(jnp.sum(e * e, axis=1, keepdims=True), axis=0, keepdims=True)
        return e / d, e / d, jnp.broadcast_to((0.5 / d) * part, (1, LANES))

    return _rowwise(body, [_whole(y), _whole(target)], [], [(d, F32), (d, BF16)], [((1, LANES), F32)], name,
                    _tile(y.shape[0], 256, 16))


def _head_mean(v, head_dim):
    cols = []
    lane = lax.broadcasted_iota(jnp.int32, (1, LANES), 1)
    for j in range(v.shape[1] // LANES):
        blk = v[:, j * LANES:(j + 1) * LANES]
        if head_dim == LANES:
            m = jnp.sum(blk, axis=-1, keepdims=True)
            cols.append(jnp.broadcast_to(m, blk.shape))
        else:
            lo = jnp.sum(jnp.where(lane < head_dim, blk, 0.0), axis=-1, keepdims=True)
            hi = jnp.sum(jnp.where(lane >= head_dim, blk, 0.0), axis=-1, keepdims=True)
            cols.append(jnp.where(lane < head_dim, lo, hi))
    return jnp.concatenate(cols, axis=1) / head_dim


def _head_norm(xv, g, head_dim):
    r = lax.rsqrt(_head_mean(xv * xv, head_dim) + EPS)
    return (xv * r) * g


def _head_norm_bwd(xv, g, dy, head_dim):
    r = lax.rsqrt(_head_mean(xv * xv, head_dim) + EPS)
    xh = xv * r
    w = dy * g
    dx = r * (w - xh * _head_mean(w * xh, head_dim))
    return dx, jnp.sum(dy * xh, axis=0, keepdims=True)


def _log_sigmoid(x):
    return jnp.minimum(x, 0.0) - jnp.log(1.0 + jnp.exp(-jnp.abs(x)))


def _shift_rows(v, sh, down):
    n = v.shape[0]
    row = lax.broadcasted_iota(jnp.int32, (n, 1), 0)
    if down:
        return jnp.where(row >= sh, pltpu.roll(v, sh, 0), 0.0)
    return jnp.where(row < n - sh, pltpu.roll(v, n - sh, 0), 0.0)


def _scan_rows(v, down):
    sh = 1
    while sh < v.shape[0]:
        v = v + _shift_rows(v, sh, down)
        sh *= 2
    return v


def _window_sum(v, steps, down):
    for s in range(steps):
        v = v + _shift_rows(v, 2 ** s, down)
    return v


def _fox_fwd(qn, kn, vb, cks, batch, seq, name, comm=None):
    t, width = qn.shape
    pairs = width // LANES
    tq = _tile(seq, 512, LANES)
    nq = seq // tq

    def kern(*refs):
        (q_ref, k_ref, v_ref, c_ref), (o_ref, lse_ref), _, c_src, c_out, c_sem = _split_refs(refs, 4, 2, comm)
        qi = pl.program_id(2)
        if comm is not None:
            b_id, p_id = pl.program_id(0), pl.program_id(1)
            first = jnp.logical_and(jnp.logical_and(b_id == 0, p_id == 0), qi == 0)
            last = jnp.logical_and(jnp.logical_and(b_id == batch - 1, p_id == pairs - 1), qi == nq - 1)
            start_comm, wait_comm = comm.run(c_src, c_out, c_sem, first, last)
            start_comm()
        lane = lax.broadcasted_iota(jnp.int32, (1, LANES), 1)
        rowi = lax.broadcasted_iota(jnp.int32, (tq, tq), 0)
        coli = lax.broadcasted_iota(jnp.int32, (tq, tq), 1)
        q_all = q_ref[...]
        o_heads, lse_heads = [], []
        for h in range(2):
            hm = (lane < FOX_HEAD_DIM) if h == 0 else (lane >= FOX_HEAD_DIM)
            q = jnp.where(hm, q_all, jnp.zeros_like(q_all))

            def step(j, carry, h=h, q=q, diagonal=False):
                m, l, acc = carry
                start = pl.multiple_of(j * tq, tq)
                k = k_ref[pl.ds(start, tq), :]
                v = v_ref[pl.ds(start, tq), :]
                s = lax.dot_general(q, k, _DIMS["nt"], preferred_element_type=F32)
                s = s - c_ref[0, h:h + 1, pl.ds(start, tq)]
                if diagonal:
                    s = jnp.where(rowi >= coli, s, NEG)
                m_new = jnp.maximum(m, jnp.max(s, axis=-1, keepdims=True))
                alpha = jnp.exp(m - m_new)
                p = jnp.exp(s - m_new)
                l = alpha * l + jnp.sum(p, axis=-1, keepdims=True)
                acc = alpha * acc + lax.dot_general(p.astype(BF16), v, _DIMS["nn"], preferred_element_type=F32)
                return m_new, l, acc

            init = (jnp.full((tq, 1), NEG, F32), jnp.zeros((tq, 1), F32), jnp.zeros((tq, LANES), F32))
            m, l, acc = step(qi, lax.fori_loop(0, qi, step, init), diagonal=True)
            o_heads.append(acc / l)
            lse_heads.append(jnp.broadcast_to(m + jnp.log(l), (tq, LANES)))
        o_ref[...] = jnp.where(lane < FOX_HEAD_DIM, o_heads[0], o_heads[1])
        lse_ref[...] = jnp.where(lane < FOX_HEAD_DIM, lse_heads[0], lse_heads[1])
        if comm is not None:
            wait_comm()

    q_spec = pl.BlockSpec((tq, LANES), lambda b, hp, qi: (b * nq + qi, hp))
    kv_spec = pl.BlockSpec((seq, LANES), lambda b, hp, qi: (b, hp))
    c_spec = pl.BlockSpec((1, 2, seq), lambda b, hp, qi: (b * pairs + hp, 0, 0))
    in_specs = [q_spec, kv_spec, kv_spec, c_spec]
    out_shape = [jax.ShapeDtypeStruct((t, width), F32), jax.ShapeDtypeStruct((t, width), F32)]
    if comm is None:
        return pl.pallas_call(
            kern, name=name, grid=(batch, pairs, nq), in_specs=in_specs, out_specs=[q_spec, q_spec],
            out_shape=out_shape, compiler_params=_params(("parallel", "parallel", "arbitrary")),
        )(qn, kn, vb, cks)
    res_all = pl.pallas_call(
        kern, name=name, grid=(batch, pairs, nq), in_specs=in_specs + [ANY] * len(comm.srcs),
        out_specs=[q_spec, q_spec] + [ANY] * len(comm.outs), out_shape=out_shape + comm.outs,
        scratch_shapes=comm.scratch(), compiler_params=_params(("arbitrary", "arbitrary", "arbitrary")),
    )(qn, kn, vb, cks, *comm.srcs)
    return list(res_all[:2]), list(res_all[2:])


def _fox_bwd(qn, kn, vb, cks, o, lse, do, dz, dv_col, batch, seq, name, comm):
    t, width = qn.shape
    pairs = width // LANES
    tk = _tile(seq, 512, LANES)
    nk = seq // tk
    scale = FOX_HEAD_DIM ** -0.5

    def kern(*refs):
        ins, outs, _, c_src, c_out, c_sem = _split_refs(refs, 8, 5, comm)
        k_ref, v_ref, q_ref, c_ref, o_ref, lse_ref, do_ref, _ = ins
        dk_ref, dv_ref, dq_ref, dc_ref, dcq_ref = outs
        kj = pl.program_id(2)
        if comm is not None:
            b_id, p_id = pl.program_id(0), pl.program_id(1)
            first = jnp.logical_and(jnp.logical_and(b_id == 0, p_id == 0), kj == 0)
            last = jnp.logical_and(jnp.logical_and(b_id == batch - 1, p_id == pairs - 1), kj == nk - 1)
            start_comm, wait_comm = comm.run(c_src, c_out, c_sem, first, last)
            start_comm()
        lane = lax.broadcasted_iota(jnp.int32, (1, LANES), 1)
        rowi = lax.broadcasted_iota(jnp.int32, (tk, tk), 0)
        coli = lax.broadcasted_iota(jnp.int32, (tk, tk), 1)

        @pl.when(kj == 0)
        def _():
            dq_ref[...] = jnp.zeros_like(dq_ref)
            dcq_ref[...] = jnp.zeros_like(dcq_ref)

        k_all = k_ref[...]
        v_all = v_ref[...]
        kstart = pl.multiple_of(kj * tk, tk)
        dk_heads, dv_heads = [], []
        for h in range(2):
            hm = (lane < FOX_HEAD_DIM) if h == 0 else (lane >= FOX_HEAD_DIM)
            kh = jnp.where(hm, k_all, jnp.zeros_like(k_all))
            vh = jnp.where(hm, v_all, jnp.zeros_like(v_all))
            c_row = c_ref[0, h:h + 1, pl.ds(kstart, tk)]

            def step(qi, carry, h=h, hm=hm, kh=kh, vh=vh, c_row=c_row, diagonal=False):
                dk_acc, dv_acc, dc_acc = carry
                start = pl.multiple_of(qi * tk, tk)
                q = q_ref[pl.ds(start, tk), :]
                dov = do_ref[pl.ds(start, tk), :]
                ov = o_ref[pl.ds(start, tk), :]
                lse_col = jnp.max(jnp.where(hm, lse_ref[pl.ds(start, tk), :], NEG), axis=-1, keepdims=True)
                dob = jnp.where(hm, dov, 0.0).astype(BF16)
                dcol = jnp.sum(dob.astype(F32) * ov, axis=-1, keepdims=True)
                s = lax.dot_general(q, kh, _DIMS["nt"], preferred_element_type=F32) - c_row
                p = jnp.exp(s - lse_col)
                if diagonal:
                    p = jnp.where(rowi >= coli, p, 0.0)
                dp = lax.dot_general(dob, vh, _DIMS["nt"], preferred_element_type=F32)
                ds = p * (dp - dcol)
                ds_b = ds.astype(BF16)
                dv_acc = dv_acc + lax.dot_general(p.astype(BF16), dob, _DIMS["tn"], preferred_element_type=F32)
                dk_acc = dk_acc + lax.dot_general(ds_b, q, _DIMS["tn"], preferred_element_type=F32)
                dq_part = lax.dot_general(ds_b * scale, kh, _DIMS["nn"], preferred_element_type=F32)
                dq_ref[pl.ds(start, tk), :] += dq_part
                dcq_ref[pl.ds(start, tk), :] += jnp.where(hm, jnp.sum(ds, axis=-1, keepdims=True), 0.0)
                dc_acc = dc_acc - jnp.sum(ds, axis=0, keepdims=True)
                return dk_acc, dv_acc, dc_acc

            init = (jnp.zeros((tk, LANES), F32), jnp.zeros((tk, LANES), F32), jnp.zeros((1, tk), F32))
            dk_acc, dv_acc, dc_acc = lax.fori_loop(kj + 1, nk, step, step(kj, init, diagonal=True))
            dk_heads.append(dk_acc)
            dv_heads.append(dv_acc)
            dc_ref[0, h:h + 1, pl.ds(kstart, tk)] = dc_acc
        dk_ref[...] = jnp.where(lane < FOX_HEAD_DIM, dk_heads[0], dk_heads[1])
        dv_ref[...] = jnp.where(lane < FOX_HEAD_DIM, dv_heads[0], dv_heads[1]).astype(BF16)
        if comm is not None:
            wait_comm()

    kv_spec = pl.BlockSpec((tk, LANES), lambda b, hp, kj: (b * nk + kj, hp))
    full_spec = pl.BlockSpec((seq, LANES), lambda b, hp, kj: (b, hp))
    c_spec = pl.BlockSpec((1, 2, seq), lambda b, hp, kj: (b * pairs + hp, 0, 0))
    dv_spec = pl.BlockSpec((tk, LANES), lambda b, hp, kj: (b * nk + kj, hp + dv_col // LANES))
    in_specs = [kv_spec, kv_spec, full_spec, c_spec, full_spec, full_spec, full_spec, ANY]
    out_specs = [kv_spec, dv_spec, full_spec, c_spec, full_spec]
    out_shape = [jax.ShapeDtypeStruct((t, width), F32), jax.ShapeDtypeStruct(dz.shape, BF16),
                 jax.ShapeDtypeStruct((t, width), F32), jax.ShapeDtypeStruct(cks.shape, F32),
                 jax.ShapeDtypeStruct((t, width), F32)]
    res_all = pl.pallas_call(
        kern, name=name, grid=(batch, pairs, nk), in_specs=in_specs + [ANY] * len(comm.srcs),
        out_specs=out_specs + [ANY] * len(comm.outs), out_shape=out_shape + comm.outs,
        scratch_shapes=comm.scratch(), input_output_aliases={7: 1},
        compiler_params=_params(("arbitrary", "arbitrary", "arbitrary")),
    )(kn, vb, qn, cks, o, lse, do, dz, *comm.srcs)
    return list(res_all[:5]), list(res_all[5:])


def _mem_fwd(qn, kn, vb, batch, seq, mlen, name):
    t, width = qn.shape
    heads = width // LANES
    tq = _tile(seq, 512, LANES)
    nq = seq // tq
    scale = MEM_HEAD_DIM ** -0.5

    def kern(q_ref, k_ref, v_ref, o_ref):
        s = lax.dot_general(q_ref[...], k_ref[...], _DIMS["nt"], preferred_element_type=F32) * scale
        e = jnp.exp(s - jnp.max(s, axis=-1, keepdims=True))
        p = e / jnp.sum(e, axis=-1, keepdims=True)
        o_ref[...] = lax.dot_general(p.astype(BF16), v_ref[...], _DIMS["nn"], preferred_element_type=F32)

    q_spec = pl.BlockSpec((tq, LANES), lambda b, h, qi: (b * nq + qi, h))
    kv_spec = pl.BlockSpec((mlen, LANES), lambda b, h, qi: (b, h))
    return pl.pallas_call(
        kern, name=name, grid=(batch, heads, nq), in_specs=[q_spec, kv_spec, kv_spec], out_specs=q_spec,
        out_shape=jax.ShapeDtypeStruct((t, width), F32),
        compiler_params=_params(("parallel", "parallel", "parallel")),
    )(qn, kn, vb)


def _mem_bwd(qn, kn, vb, do, batch, seq, mlen, name):
    t, width = qn.shape
    heads = width // LANES
    tq = _tile(seq, 512, LANES)
    nq = seq // tq
    scale = MEM_HEAD_DIM ** -0.5

    def kern(q_ref, k_ref, v_ref, do_ref, dq_ref, dk_ref, dv_ref):
        qi = pl.program_id(2)
        q, k, v = q_ref[...], k_ref[...], v_ref[...]
        dob = do_ref[...].astype(BF16)
        s = lax.dot_general(q, k, _DIMS["nt"], preferred_element_type=F32) * scale
        e = jnp.exp(s - jnp.max(s, axis=-1, keepdims=True))
        p = e / jnp.sum(e, axis=-1, keepdims=True)
        dp = lax.dot_general(dob, v, _DIMS["nt"], preferred_element_type=F32)
        ds = p * (dp - jnp.sum(p * dp, axis=-1, keepdims=True))
        dsb = (ds * scale).astype(BF16)
        dq_ref[...] = lax.dot_general(dsb, k, _DIMS["nn"], preferred_element_type=F32)
        dk = lax.dot_general(dsb, q, _DIMS["tn"], preferred_element_type=F32)
        dv = lax.dot_general(p.astype(BF16), dob, _DIMS["tn"], preferred_element_type=F32)

        @pl.when(qi == 0)
        def _():
            dk_ref[...] = dk
            dv_ref[...] = dv

        @pl.when(qi > 0)
        def _():
            dk_ref[...] += dk
            dv_ref[...] += dv

    q_spec = pl.BlockSpec((tq, LANES), lambda b, h, qi: (b * nq + qi, h))
    kv_spec = pl.BlockSpec((mlen, LANES), lambda b, h, qi: (b, h))
    return pl.pallas_call(
        kern, name=name, grid=(batch, heads, nq), in_specs=[q_spec, kv_spec, kv_spec, q_spec],
        out_specs=[q_spec, kv_spec, kv_spec],
        out_shape=[jax.ShapeDtypeStruct((t, width), F32), jax.ShapeDtypeStruct(kn.shape, F32),
                   jax.ShapeDtypeStruct(kn.shape, F32)],
        compiler_params=_params(("parallel", "parallel", "arbitrary")),
    )(qn, kn, vb, do)


def _position():
    return lax.axis_index("x"), lax.axis_index("y"), lax.axis_index("c")


def _other_chips(x, y):
    return [(1 - x, y), (x, 1 - y), (1 - x, 1 - y)]


def _remote(src, dst, sems, to):
    send, recv = sems.remote()
    return pltpu.make_async_remote_copy(src_ref=src, dst_ref=dst, send_sem=send, recv_sem=recv,
                                        device_id=to, device_id_type=MESH)


def _gather_first(shards):
    n = len(shards)

    def plan(src, out, sems):
        x, y, c = _position()
        me = 4 * x + 2 * y + c
        peers = [(x, y, 1 - c)] + [(px, py, c) for px, py in _other_chips(x, y)]
        copies = []
        for i in range(n):
            copies.append(pltpu.make_async_copy(src[i], out[i].at[me], sems.one_local()))
            copies += [_remote(src[i], out[i].at[me], sems, to) for to in peers]
        return copies

    outs = [jax.ShapeDtypeStruct((N_DEV,) + s.shape, s.dtype) for s in shards]
    return _Comm(shards, outs, 4 * n, n, plan)


def _gather_second(bufs):
    n = len(bufs)

    def plan(src, out, sems):
        x, y, c = _position()
        copies = []
        for i in range(n):
            for px, py in _other_chips(x, y):
                blk = out[i].at[4 * px + 2 * py + c]
                copies.append(_remote(blk, blk, sems, (x, y, 1 - c)))
        return copies

    outs = [jax.ShapeDtypeStruct(b.shape, b.dtype) for b in bufs]
    return _Comm(bufs, outs, 3 * n, 0, plan, in_place=True)


def _scatter_first(partials):
    n = len(partials)

    def plan(src, out, sems):
        x, y, c = _position()
        return [_remote(src[i].at[2 * q + (1 - c)], out[i].at[q], sems, (x, y, 1 - c))
                for i in range(n) for q in range(4)]

    outs = [jax.ShapeDtypeStruct((4,) + g.shape[1:], g.dtype) for g in partials]
    return _Comm(partials, outs, 4 * n, 0, plan)


def _scatter_second(halves):
    n = len(halves)

    def plan(src, out, sems):
        x, y, c = _position()
        return [_remote(src[i].at[2 * px + py], out[i].at[r], sems, (px, py, c))
                for i in range(n) for r, (px, py) in enumerate(_other_chips(x, y))]

    outs = [jax.ShapeDtypeStruct((3,) + h.shape[1:], h.dtype) for h in halves]
    return _Comm(halves, outs, 3 * n, 0, plan)


def _comm_call(comm, name):
    ns, no = len(comm.srcs), len(comm.outs)

    def body(*refs):
        copies = comm.plan(list(refs[:ns]), list(refs[ns:ns + no]), _Sems(*refs[ns + no:]))
        for cp in copies:
            cp.start()
        for cp in copies:
            cp.wait()

    return pl.pallas_call(
        body, name=name, in_specs=[ANY] * ns, out_specs=[ANY] * no, out_shape=comm.outs,
        scratch_shapes=comm.scratch(), input_output_aliases=comm.aliases(0, 0),
    )(*comm.srcs)


def _gather_first_relayed(shard, name):
    def body(x_ref, out_ref, send, recv, local):
        sems = _Sems(send, recv, local)
        x, y, c = _position()

        def slot(px, py):
            return out_ref.at[4 * px + 2 * py + c]

        mine = pltpu.make_async_copy(x_ref, slot(x, y), sems.one_local())
        to_sibling = _remote(x_ref, slot(x, y), sems, (x, y, 1 - c))
        to_x = _remote(x_ref, slot(x, y), sems, (1 - x, y, c))
        to_y = _remote(x_ref, slot(x, y), sems, (x, 1 - y, c))
        for cp in (mine, to_sibling, to_x, to_y):
            cp.start()
        relay_send, relay_recv = sems.remote()

        def relay(arrived, unused, origin, to):
            arrived.wait_recv()
            blk = slot(*origin)
            fwd = pltpu.make_async_remote_copy(src_ref=blk, dst_ref=blk, send_sem=relay_send, recv_sem=relay_recv,
                                               device_id=(*to, c), device_id_type=MESH)
            fwd.start()
            fwd.wait()
            arrived.wait_send()
            unused.wait()

        _when(c == 0, lambda: relay(to_y, to_x, (x, 1 - y), (1 - x, y)))
        _when(c == 1, lambda: relay(to_x, to_y, (1 - x, y), (x, 1 - y)))
        mine.wait()
        to_sibling.wait()

    return pl.pallas_call(
        body, name=name, in_specs=[ANY], out_specs=ANY,
        out_shape=jax.ShapeDtypeStruct((N_DEV,) + shard.shape, shard.dtype),
        scratch_shapes=[pltpu.SemaphoreType.DMA((4,)), pltpu.SemaphoreType.DMA((4,)), pltpu.SemaphoreType.DMA((1,))],
    )(shard)


def _all_gather_vmem(shard, name):
    first = _gather_first([shard])

    def body(x_ref, out_ref, send, recv, local):
        sems = _Sems(send, recv, local)
        copies = first.plan([x_ref], [out_ref], sems)
        for cp in copies:
            cp.start()
        x, y, c = _position()
        passed = []
        for j, (px, py) in enumerate(_other_chips(x, y)):
            copies[2 + j].wait_recv()
            blk = out_ref.at[4 * px + 2 * py + c]
            fwd = _remote(blk, blk, sems, (x, y, 1 - c))
            fwd.start()
            passed.append(fwd)
        copies[0].wait()
        copies[1].wait()
        for cp in copies[2:]:
            cp.wait_send()
        for cp in passed:
            cp.wait()

    vm = pl.BlockSpec(memory_space=pltpu.VMEM)
    return pl.pallas_call(
        body, name=name, in_specs=[vm], out_specs=vm,
        out_shape=jax.ShapeDtypeStruct((N_DEV,) + shard.shape, shard.dtype),
        scratch_shapes=[pltpu.SemaphoreType.DMA((7,)), pltpu.SemaphoreType.DMA((7,)), pltpu.SemaphoreType.DMA((1,))],
    )(shard)


def _rs_pair_add(partial, landed, where, name):
    _, r, c = partial.shape
    tm = _tile(r, 256, 16)

    def kern(where_ref, g_ref, l_ref, own_ref, hb_ref):
        s = g_ref[...] + l_ref[...]
        hb_ref[...] = s.astype(BF16)

        @pl.when(pl.program_id(1) == where_ref[1])
        def _():
            own_ref[...] = s

    grid_spec = pltpu.PrefetchScalarGridSpec(
        num_scalar_prefetch=1, grid=(r // tm, 4),
        in_specs=[pl.BlockSpec((None, tm, c), lambda i, q, wr: (2 * q + wr[0], i, 0)),
                  pl.BlockSpec((None, tm, c), lambda i, q, wr: (q, i, 0))],
        out_specs=[pl.BlockSpec((tm, c), lambda i, q, wr: (i, 0)),
                   pl.BlockSpec((None, tm, c), lambda i, q, wr: (q, i, 0))])
    return pl.pallas_call(
        kern, name=name, grid_spec=grid_spec,
        out_shape=[jax.ShapeDtypeStruct((r, c), F32), jax.ShapeDtypeStruct((4, r, c), BF16)],
        compiler_params=_params(("parallel", "arbitrary")),
    )(where, partial, landed)


def _adam_math(g, w, m, v):
    m = ADAM_B1 * m + (1.0 - ADAM_B1) * g
    v = ADAM_B2 * v + (1.0 - ADAM_B2) * (g * g)
    m_hat = m / (1.0 - ADAM_B1 ** ADAM_STEP)
    v_hat = v / (1.0 - ADAM_B2 ** ADAM_STEP)
    delta = -ADAM_LR * (m_hat / (jnp.sqrt(v_hat) + ADAM_EPS) + ADAM_WD * w)
    return delta, m, v


def _rs_finish_adam(own, landed, w, m, v, name):
    r, c = own.shape
    tm = _tile(r, 128, 16)

    def kern(h_ref, l_ref, w_ref, m_ref, v_ref, *outs):
        g = ((h_ref[...] + l_ref[0].astype(F32)) + l_ref[1].astype(F32)) + l_ref[2].astype(F32)
        delta, m_new, v_new = _adam_math(g, w_ref[...], m_ref[...], v_ref[...])
        for ref, val in zip(outs, (g, delta, m_new, v_new)):
            ref[...] = val

    flat = pl.BlockSpec((tm, c), lambda i: (i, 0))
    return pl.pallas_call(
        kern, name=name, grid=(r // tm,), in_specs=[flat, pl.BlockSpec((3, tm, c), lambda i: (0, i, 0)), flat, flat, flat],
        out_specs=[flat] * 4, out_shape=[jax.ShapeDtypeStruct((r, c), F32)] * 4,
        compiler_params=_params(("parallel",)))(own, landed, w, m, v)


def _allreduce_adam(gathered, w, m, v, name):
    _, r, c = gathered.shape

    def kern(a_ref, w_ref, m_ref, v_ref, g_out, d_out, m_out, v_out):
        g = a_ref[0]
        for j in range(1, N_DEV):
            g = g + a_ref[j]
        delta, m_new, v_new = _adam_math(g, w_ref[...], m_ref[...], v_ref[...])
        g_out[...] = g
        d_out[...] = delta
        m_out[...] = m_new
        v_out[...] = v_new

    return pl.pallas_call(
        kern, name=name, out_shape=[jax.ShapeDtypeStruct((r, c), F32)] * 4,
        compiler_params=_params(),
    )(gathered, w, m, v)


def _cols_from_blocks(g):
    n, k, nb = g.shape
    return jnp.transpose(g, (1, 0, 2)).reshape(k, n * nb)


def _blocks_from_cols(w):
    k, n = w.shape
    return jnp.transpose(w.reshape(k, N_DEV, n // N_DEV), (1, 0, 2))


def _pack_small(parts):
    flat = []
    for p in parts:
        v = p.reshape(-1)
        flat.append(jnp.pad(v, (0, (-v.shape[0]) % (8 * LANES))))
    return jnp.concatenate(flat).reshape(-1, LANES)


def _unpack_small(buf, like):
    out, pos = [], 0
    flat = buf.reshape(-1)
    for p in like:
        size = p.size
        out.append(flat[pos:pos + size].reshape(p.shape))
        pos += size + (-size) % (8 * LANES)
    return out


def kernel(x, mem, ffn1_norm, ffn1_w_gate_up, ffn1_w_down, mix_norm, mem_norm, w_in, b_forget, pool_w, pool_scale, w_pool_up, fox_q_norm, fox_k_norm, w_fox_o, w_mem_kv, mem_q_norm, mem_k_norm, w_mem_o, w_out, ffn2_norm, ffn2_w_gate_up, ffn2_w_down, loss_target, m_ffn1_norm, m_ffn1_w_gate_up, m_ffn1_w_down, m_mix_norm, m_mem_norm, m_w_in, m_b_forget, m_pool_w, m_pool_scale, m_w_pool_up, m_fox_q_norm, m_fox_k_norm, m_w_fox_o, m_w_mem_kv, m_mem_q_norm, m_mem_k_norm, m_w_mem_o, m_w_out, m_ffn2_norm, m_ffn2_w_gate_up, m_ffn2_w_down, v_ffn1_norm, v_ffn1_w_gate_up, v_ffn1_w_down, v_mix_norm, v_mem_norm, v_w_in, v_b_forget, v_pool_w, v_pool_scale, v_w_pool_up, v_fox_q_norm, v_fox_k_norm, v_w_fox_o, v_w_mem_kv, v_mem_q_norm, v_mem_k_norm, v_w_mem_o, v_w_out, v_ffn2_norm, v_ffn2_w_gate_up, v_ffn2_w_down):
    names = ["ffn1_norm", "ffn1_w_gate_up", "ffn1_w_down", "mix_norm", "mem_norm", "w_in", "b_forget", "pool_w",
             "pool_scale", "w_pool_up", "fox_q_norm", "fox_k_norm", "w_fox_o", "w_mem_kv", "mem_q_norm",
             "mem_k_norm", "w_mem_o", "w_out", "ffn2_norm", "ffn2_w_gate_up", "ffn2_w_down"]
    w_args = [ffn1_norm, ffn1_w_gate_up, ffn1_w_down, mix_norm, mem_norm, w_in, b_forget, pool_w, pool_scale,
              w_pool_up, fox_q_norm, fox_k_norm, w_fox_o, w_mem_kv, mem_q_norm, mem_k_norm, w_mem_o, w_out,
              ffn2_norm, ffn2_w_gate_up, ffn2_w_down]
    m_args = [m_ffn1_norm, m_ffn1_w_gate_up, m_ffn1_w_down, m_mix_norm, m_mem_norm, m_w_in, m_b_forget, m_pool_w,
              m_pool_scale, m_w_pool_up, m_fox_q_norm, m_fox_k_norm, m_w_fox_o, m_w_mem_kv, m_mem_q_norm,
              m_mem_k_norm, m_w_mem_o, m_w_out, m_ffn2_norm, m_ffn2_w_gate_up, m_ffn2_w_down]
    v_args = [v_ffn1_norm, v_ffn1_w_gate_up, v_ffn1_w_down, v_mix_norm, v_mem_norm, v_w_in, v_b_forget, v_pool_w,
              v_pool_scale, v_w_pool_up, v_fox_q_norm, v_fox_k_norm, v_w_fox_o, v_w_mem_kv, v_mem_q_norm,
              v_mem_k_norm, v_w_mem_o, v_w_out, v_ffn2_norm, v_ffn2_w_gate_up, v_ffn2_w_down]
    W = dict(zip(names, w_args))
    M = dict(zip(names, m_args))
    V = dict(zip(names, v_args))

    batch, seq, d = x.shape
    mlen = mem.shape[1]
    t = batch * seq
    gate_w = 3 * d
    z_gate, z_q = 0, gate_w
    z_k, z_v = z_q + FOX_WIDTH, z_q + 2 * FOX_WIDTH
    z_u = z_q + 3 * FOX_WIDTH
    z_qm = z_u + POOL_WIDTH
    z_f = z_qm + MEM_WIDTH
    z_width = -(-(z_f + F_PAD) // 512) * 512

    x2d = x.reshape(t, d)
    mem2d = mem.reshape(batch * mlen, d)
    tgt2d = loss_target.reshape(t, d)

    big = ["ffn1_w_gate_up", "ffn1_w_down", "w_in", "w_pool_up", "w_fox_o", "w_mem_kv", "w_mem_o", "w_out",
           "ffn2_w_gate_up", "ffn2_w_down"]
    col_sharded = {"ffn1_w_gate_up", "ffn2_w_gate_up", "w_in", "w_pool_up", "w_fox_o", "w_mem_o"}
    mixer_small = ["w_pool_up", "w_fox_o", "w_mem_kv", "w_mem_o", "w_out"]
    shard = {n: W[n][0].astype(BF16) for n in big}

    def rows_of(g):
        return g.reshape(-1, g.shape[2])

    gu_caps_nt, gu_caps_tn = (1024, 1024, 1408), (1024, 1408, 2048)
    down_caps, dwd_caps = (1024, 1024, 2816), (1408, 1024, 2048)

    wgu1 = _gather_first_relayed(shard["ffn1_w_gate_up"], "ag_first")
    h1, (wgu1,) = _rms_fwd(x2d, W["ffn1_norm"], "ffn1_rms", comm=_gather_second([wgu1]))
    w_in_top, w_in_bot = shard["w_in"][:d // 2], shard["w_in"][d // 2:]
    (gu1, a1), bufs = _gate_up_swiglu(h1, wgu1, "ffn1_gu", comm=_gather_first([shard["ffn1_w_down"], w_in_top]))
    wd1_g, w_in_top_g = _comm_call(_gather_second(bufs), "ag_second")
    wd1 = rows_of(wd1_g)
    x1, (w_in_bot_g,) = _mm(a1, wd1, "nn", "ffn1_down", scale=0.5, res=x2d, caps=down_caps,
                            comm=_gather_first([w_in_bot]))
    h2, (w_in_bot_g,) = _rms_fwd(x1, W["mix_norm"], "mix_rms", comm=_gather_second([w_in_bot_g]))

    o_u, o_q, o_v = 0, POOL_WIDTH, POOL_WIDTH + 2 * FOX_WIDTH
    o_f = o_v + FOX_WIDTH
    o_qm = o_f + FOX_HEADS
    o_g = o_qm + MEM_WIDTH
    wi = jnp.concatenate([_cols_from_blocks(w_in_top_g), _cols_from_blocks(w_in_bot_g)], axis=0)
    w_in_pad = jnp.concatenate(
        [wi[:, o_g:o_g + gate_w], wi[:, o_q:o_q + 3 * FOX_WIDTH], wi[:, o_u:o_u + POOL_WIDTH],
         wi[:, o_qm:o_qm + MEM_WIDTH], wi[:, o_f:o_f + FOX_HEADS],
         jnp.zeros((d, z_width - z_f - FOX_HEADS), BF16)], axis=1)
    group_b = mixer_small + ["ffn2_w_down"]
    z, bufs = _mm(h2, w_in_pad, "nn", "mix_in", caps=(1024, 1536, 2048), comm=_gather_first([shard[n] for n in group_b]))

    pool_w_b = W["pool_w"][0].astype(BF16)

    def pool_fwd_body(u, pw, ps):
        row = lax.broadcasted_iota(jnp.int32, (seq, 1), 0)
        diffs, mixed = [], []
        for g in range(POOL_GROUPS):
            ug = u[:, g * POOL_GROUP_DIM:(g + 1) * POOL_GROUP_DIM]
            cnt = jnp.minimum(row + 1, POOL_WINDOWS[g]).astype(F32)
            diff = _window_sum(ug, g + 1, True) / cnt - ug
            diffs.append(diff)
            mixed.append(lax.dot_general(diff.astype(BF16), pw[g], _DIMS["nn"], preferred_element_type=F32))
        diffs = jnp.concatenate(diffs, axis=1)
        mixed = jnp.concatenate(mixed, axis=1)
        return mixed * ps, diffs, mixed

    ypp, pool_diff, pool_mixed = _rowwise(
        pool_fwd_body, [(z, z_u, POOL_WIDTH)], [pool_w_b, W["pool_scale"]],
        [(POOL_WIDTH, BF16), (POOL_WIDTH, BF16), (POOL_WIDTH, F32)], [], "pool_fwd", seq)

    gq = jnp.tile(W["fox_q_norm"], (1, FOX_HEADS))
    gk = jnp.tile(W["fox_k_norm"], (1, FOX_HEADS))
    b_pad = jnp.pad(W["b_forget"], ((0, 0), (0, F_PAD - FOX_HEADS)))

    def fox_prep_body(q, k, v, f, gqv, gkv, bv):
        qn_scaled = _head_norm(q, gqv, FOX_HEAD_DIM).astype(BF16) * FOX_HEAD_DIM ** -0.5
        return qn_scaled, _head_norm(k, gkv, FOX_HEAD_DIM), v, _log_sigmoid(f + bv)

    tm_e = _tile(t, 256, 16)
    (qn, kn, vb, logf), bufs = _rowwise(
        fox_prep_body, [(z, z_q, FOX_WIDTH), (z, z_k, FOX_WIDTH), (z, z_v, FOX_WIDTH), (z, z_f, F_PAD)],
        [gq, gk, b_pad], [(FOX_WIDTH, BF16), (FOX_WIDTH, BF16), (FOX_WIDTH, BF16), (F_PAD, F32)], [], "fox_prep", tm_e,
        comm=_gather_second(bufs))
    full = {n: (_cols_from_blocks(g) if n in col_sharded else rows_of(g)) for n, g in zip(group_b, bufs)}
    wd2 = full["ffn2_w_down"]
    y_pool = _mm(ypp, full["w_pool_up"], "nn", "pool_up")
    csum = _rowwise(lambda v: (_scan_rows(v, True),), [_whole(logf)], [], [(F_PAD, F32)], [], "fox_cumsum", seq)[0]
    cks = jnp.transpose(csum.reshape(batch, seq, F_PAD)[:, :, :FOX_HEADS], (0, 2, 1)).reshape(
        batch * FOX_HEADS // 2, 2, seq)
    (o_fox, lse), (wgu2,) = _fox_fwd(qn, kn, vb, cks, batch, seq, "fox_fwd",
                                     comm=_gather_first([shard["ffn2_w_gate_up"]]))
    y_fox = _mm(o_fox, full["w_fox_o"], "nn", "fox_o")

    memn = _rms_fwd(mem2d, W["mem_norm"], "mem_rms")
    kv = _mm(memn, full["w_mem_kv"], "nn", "mem_kv")
    gqm = jnp.tile(W["mem_q_norm"], (1, MEM_HEADS))
    gkm = jnp.tile(W["mem_k_norm"], (1, MEM_HEADS))
    qmn = _rowwise(lambda q, g: (_head_norm(q, g, MEM_HEAD_DIM),), [(z, z_qm, MEM_WIDTH)], [gqm],
                   [(MEM_WIDTH, BF16)], [], "memq_prep", tm_e)[0]
    kmn, vmb = _rowwise(lambda k, v, g: (_head_norm(k, g, MEM_HEAD_DIM), v),
                        [(kv, 0, MEM_WIDTH), (kv, MEM_WIDTH, MEM_WIDTH)], [gkm],
                        [(MEM_WIDTH, BF16), (MEM_WIDTH, BF16)], [], "memk_prep", _tile(batch * mlen, 256, 16))
    o_mem = _mem_fwd(qmn, kmn, vmb, batch, seq, mlen, "mem_fwd")
    y_mem = _mm(o_mem, full["w_mem_o"], "nn", "mem_o")

    def gate_fwd_body(gp, gf, gm, yp, yf, ym):
        return ((_sigmoid(gp) * yp + _sigmoid(gf) * yf) + _sigmoid(gm) * ym,)

    tm_g = _tile(t, 128, 16)
    (merged,), (wgu2,) = _rowwise(
        gate_fwd_body, [(z, 0, d), (z, d, d), (z, 2 * d, d), _whole(y_pool), _whole(y_fox), _whole(y_mem)],
        [], [(d, BF16)], [], "gate_fwd", tm_g, comm=_gather_second([wgu2]))
    x2 = _mm(merged, full["w_out"], "nn", "mix_out", res=x1)

    h3 = _rms_fwd(x2, W["ffn2_norm"], "ffn2_rms")
    gu2, a2 = _gate_up_swiglu(h3, wgu2, "ffn2_gu")
    x3 = _mm(a2, wd2, "nn", "ffn2_down", scale=0.5, res=x2, caps=down_caps)
    dy, dy_b, loss_part = _loss_head(x3, tgt2d, "loss")
    loss = lax.psum(loss_part[0, 0], ("x", "y", "c"))

    cx, cy, cc = _position()
    where = jnp.stack([cc, 2 * cx + cy]).astype(jnp.int32)
    G, own, landed = {}, {}, {}

    def row_blocks(g):
        return g.reshape(N_DEV, g.shape[0] // N_DEV, g.shape[1])

    def pair_add(n, partial, from_core):
        own[n], chip_sums = _rs_pair_add(partial, from_core, where, "rs_add_" + n)
        return chip_sums

    p_wd2 = row_blocks(_mm(a2, dy_b, "tn", "ffn2_dwd", scale=0.5, caps=dwd_caps))
    dgu2, (l_wd2,) = _swiglu_bwd_from_out(dy_b, wd2, gu2, "ffn2_dgu", comm=_scatter_first([p_wd2]))
    s_wd2 = pair_add("ffn2_w_down", p_wd2, l_wd2)
    p_wgu2, (landed["ffn2_w_down"],) = _mm(h3, dgu2, "tn", "ffn2_dwgu", caps=gu_caps_tn, out_blocks=True, halves="b",
                                           comm=_scatter_second([s_wd2]))
    dh3, (l_wgu2,) = _mm(dgu2, wgu2, "nt", "ffn2_dh", caps=gu_caps_nt, b_blocks=True, halves="a",
                         comm=_scatter_first([p_wgu2]))
    s_wgu2 = pair_add("ffn2_w_gate_up", p_wgu2, l_wgu2)
    dx2, dx2_b, G["ffn2_norm"] = _rms_bwd(x2, W["ffn2_norm"], dh3, dy, "ffn2_drms")

    dmerged = _mm(dx2_b, full["w_out"], "nt", "mix_out_dx")
    P = {"w_out": row_blocks(_mm(merged, dx2_b, "tn", "mix_out_dw"))}

    def gate_bwd_body(gp, gf, gm, yp, yf, ym, dm):
        outs_dl, outs_dy = [], []
        for gl, yv in ((gp, yp), (gf, yf), (gm, ym)):
            s = _sigmoid(gl)
            outs_dl.append((dm * yv) * (s * (1.0 - s)))
            outs_dy.append(dm * s)
        return (jnp.concatenate(outs_dl, axis=1), *outs_dy)

    dz, dy_pool, dy_fox, dy_mem = _rowwise(
        gate_bwd_body, [(z, 0, d), (z, d, d), (z, 2 * d, d), _whole(y_pool), _whole(y_fox), _whole(y_mem), _whole(dmerged)],
        [], [(gate_w, BF16), (d, BF16), (d, BF16), (d, BF16)], [], "gate_bwd", tm_g, into=(None, z_width, z_gate))

    dypp = _mm(dy_pool, full["w_pool_up"], "nt", "pool_up_dx")
    P["w_pool_up"] = _blocks_from_cols(_mm(ypp, dy_pool, "tn", "pool_up_dw"))

    def pool_bwd_body(dyv, mixed, diff, pw, ps):
        row = lax.broadcasted_iota(jnp.int32, (seq, 1), 0)
        d_scale = jnp.sum(dyv * mixed, axis=0, keepdims=True)
        dmix = (dyv * ps).astype(BF16)
        du, dpw = [], []
        for g in range(POOL_GROUPS):
            sl = slice(g * POOL_GROUP_DIM, (g + 1) * POOL_GROUP_DIM)
            dmg = dmix[:, sl]
            ddiff = lax.dot_general(dmg, pw[g], _DIMS["nt"], preferred_element_type=F32)
            dpw.append(lax.dot_general(diff[:, sl], dmg, _DIMS["tn"], preferred_element_type=F32))
            cnt = jnp.minimum(row + 1, POOL_WINDOWS[g]).astype(F32)
            du.append(_window_sum(ddiff / cnt, g + 1, False) - ddiff)
        return jnp.concatenate(du, axis=1), d_scale, jnp.concatenate(dpw, axis=0)

    dz, G["pool_scale"], d_pool_w = _rowwise(
        pool_bwd_body, [_whole(dypp), _whole(pool_mixed), _whole(pool_diff)], [pool_w_b, W["pool_scale"]],
        [(POOL_WIDTH, BF16)], [((1, POOL_WIDTH), F32), ((POOL_WIDTH, POOL_GROUP_DIM), F32)],
        "pool_bwd", seq, into=(dz, z_width, z_u))
    G["pool_w"] = d_pool_w.reshape(1, POOL_GROUPS, POOL_GROUP_DIM, POOL_GROUP_DIM)

    do_fox = _mm(dy_fox, full["w_fox_o"], "nt", "fox_o_dx")
    P["w_fox_o"] = _blocks_from_cols(_mm(o_fox, dy_fox, "tn", "fox_o_dw"))
    (dkn, dz, dqn, dcks, dcq), (landed["ffn2_w_gate_up"],) = _fox_bwd(
        qn, kn, vb, cks, o_fox, lse, do_fox, dz, z_v, batch, seq, "fox_bwd", comm=_scatter_second([s_wgu2]))
    dcs = jnp.transpose(dcks.reshape(batch, FOX_HEADS, seq), (0, 2, 1)).reshape(t, FOX_HEADS)
    dcs = jnp.pad(dcs, ((0, 0), (0, F_PAD - FOX_HEADS)))
    dcq = jnp.pad(dcq.reshape(t, FOX_HEADS, FOX_HEAD_DIM)[:, :, 0], ((0, 0), (0, F_PAD - FOX_HEADS)))

    def fox_f_bwd_body(dc, dc_rows, f, bv):
        lane = lax.broadcasted_iota(jnp.int32, (1, F_PAD), 1)
        dlogf = _scan_rows(dc + dc_rows, False)
        df = jnp.where(lane < FOX_HEADS, dlogf * _sigmoid(-(f + bv)), 0.0)
        behind = jnp.zeros((seq, z_width - z_f - F_PAD), F32)
        return jnp.concatenate([df, behind], axis=1), jnp.sum(df, axis=0, keepdims=True)

    dz, db_pad = _rowwise(fox_f_bwd_body, [_whole(dcs), _whole(dcq), (z, z_f, F_PAD)], [b_pad],
                          [(z_width - z_f, BF16)], [((1, F_PAD), F32)], "fox_f_bwd", seq, into=(dz, z_width, z_f))
    G["b_forget"] = db_pad[:, :FOX_HEADS]

    def fox_head_bwd(dz_in, col, dyn, gain, name):
        return _rowwise(lambda v, dyv, g: _head_norm_bwd(v, g, dyv, FOX_HEAD_DIM), [(z, col, FOX_WIDTH), _whole(dyn)],
                        [gain], [(FOX_WIDTH, BF16)], [((1, FOX_WIDTH), F32)], name, tm_e, into=(dz_in, z_width, col))

    dz, dgq_t = fox_head_bwd(dz, z_q, dqn, gq, "fox_q_bwd")
    dz, dgk_t = fox_head_bwd(dz, z_k, dkn, gk, "fox_k_bwd")
    G["fox_q_norm"] = jnp.sum(dgq_t.reshape(FOX_HEADS, FOX_HEAD_DIM), axis=0, keepdims=True)
    G["fox_k_norm"] = jnp.sum(dgk_t.reshape(FOX_HEADS, FOX_HEAD_DIM), axis=0, keepdims=True)

    do_mem = _mm(dy_mem, full["w_mem_o"], "nt", "mem_o_dx")
    P["w_mem_o"] = _blocks_from_cols(_mm(o_mem, dy_mem, "tn", "mem_o_dw"))
    dqmn, dkmn, dvm = _mem_bwd(qmn, kmn, vmb, do_mem, batch, seq, mlen, "mem_bwd")
    dz, dgqm_t = _rowwise(lambda q, dq, g: _head_norm_bwd(q, g, dq, MEM_HEAD_DIM),
                          [(z, z_qm, MEM_WIDTH), _whole(dqmn)], [gqm], [(MEM_WIDTH, BF16)],
                          [((1, MEM_WIDTH), F32)], "memq_bwd", tm_e, into=(dz, z_width, z_qm))

    def memk_bwd_body(k, dk, dv, g):
        dkr, dg = _head_norm_bwd(k, g, dk, MEM_HEAD_DIM)
        return jnp.concatenate([dkr, dv], axis=1), dg

    dkv, dgkm_t = _rowwise(memk_bwd_body, [(kv, 0, MEM_WIDTH), _whole(dkmn), _whole(dvm)], [gkm],
                           [(2 * MEM_WIDTH, BF16)], [((1, MEM_WIDTH), F32)], "memk_bwd", _tile(batch * mlen, 256, 16))
    G["mem_q_norm"] = jnp.sum(dgqm_t.reshape(MEM_HEADS, MEM_HEAD_DIM), axis=0, keepdims=True)
    G["mem_k_norm"] = jnp.sum(dgkm_t.reshape(MEM_HEADS, MEM_HEAD_DIM), axis=0, keepdims=True)
    P["w_mem_kv"] = row_blocks(_mm(memn, dkv, "tn", "mem_kv_dw"))
    dmemn = _mm(dkv, full["w_mem_kv"], "nt", "mem_kv_dx")
    _, _, G["mem_norm"] = _rms_bwd(mem2d, W["mem_norm"], dmemn, None, "mem_drms")

    d_w_in_pad, l_small = _mm(h2, dz, "tn", "mix_in_dw", caps=(1024, 1536, 2048), comm=_scatter_first([P[n] for n in mixer_small]))
    s_small = [pair_add(n, P[n], l) for n, l in zip(mixer_small, l_small)]
    p = d_w_in_pad
    p_w_in = _blocks_from_cols(jnp.concatenate(
        [p[:, z_u:z_u + POOL_WIDTH], p[:, z_q:z_q + 3 * FOX_WIDTH], p[:, z_f:z_f + FOX_HEADS],
         p[:, z_qm:z_qm + MEM_WIDTH], p[:, z_gate:z_gate + gate_w]], axis=1))
    dh2, rest = _mm(dz, w_in_pad, "nt", "mix_in_dx", caps=(1024, 1024, 1792),
                    comm=_join(_scatter_second(s_small), _scatter_first([p_w_in])))
    for n, l in zip(mixer_small, rest[:len(mixer_small)]):
        landed[n] = l
    s_w_in = pair_add("w_in", p_w_in, rest[-1])
    dx1, dx1_b, G["mix_norm"] = _rms_bwd(x1, W["mix_norm"], dh2, dx2, "mix_drms")

    dgu1 = _swiglu_bwd_from_out(dx1_b, wd1, gu1, "ffn1_dgu")
    p_wgu1, (landed["w_in"],) = _mm(h1, dgu1, "tn", "ffn1_dwgu", caps=gu_caps_tn, out_blocks=True, halves="b",
                                    comm=_scatter_second([s_w_in]))
    d_wd1, (l_wgu1,) = _mm(a1, dx1_b, "tn", "ffn1_dwd", scale=0.5, caps=dwd_caps, comm=_scatter_first([p_wgu1]))
    p_wd1 = row_blocks(d_wd1)
    s_wgu1 = pair_add("ffn1_w_gate_up", p_wgu1, l_wgu1)
    dh1, (landed["ffn1_w_gate_up"], l_wd1) = _mm(dgu1, wgu1, "nt", "ffn1_dh", caps=gu_caps_nt, b_blocks=True, halves="a",
                                                 comm=_join(_scatter_second([s_wgu1]), _scatter_first([p_wd1])))
    s_wd1 = pair_add("ffn1_w_down", p_wd1, l_wd1)
    (dx0, _, G["ffn1_norm"]), (landed["ffn1_w_down"],) = _rms_bwd(x2d, W["ffn1_norm"], dh1, dx1, "ffn1_drms",
                                                                  comm=_scatter_second([s_wd1]))
    grad_x = dx0.reshape(batch, seq, d)

    out_g, out_d, out_m, out_v = {}, {}, {}, {}
    for n in big:
        res = _rs_finish_adam(own[n], landed[n], W[n][0], M[n][0], V[n][0], "adam_" + n)
        out_g[n], out_d[n], out_m[n], out_v[n] = [r[None] for r in res]

    small = [n for n in names if n not in big]
    g_small = _pack_small([G[n].reshape(W[n].shape) for n in small])
    all_small = _all_gather_vmem(g_small, "ag_small_grads")
    res = _allreduce_adam(all_small, _pack_small([W[n] for n in small]), _pack_small([M[n] for n in small]),
                          _pack_small([V[n] for n in small]), "adam_small")
    like = [W[n] for n in small]
    for dst, buf in zip((out_g, out_d, out_m, out_v), res):
        for n, a in zip(small, _unpack_small(buf, like)):
            dst[n] = a

    return (loss, grad_x, *[out_g[n] for n in names], *[out_d[n] for n in names],
            *[out_m[n] for n in names], *[out_v[n] for n in names])
```

```python
import functools

import jax
import jax.numpy as jnp
from jax import lax
from jax.experimental import pallas as pl
from jax.experimental.pallas import tpu as pltpu

F32 = jnp.float32
BF16 = jnp.bfloat16
MESH = pl.DeviceIdType.MESH

N_DEV = 8
EPS = 1e-6
FOX_HEADS = 16
FOX_HEAD_DIM = 64
FOX_WIDTH = FOX_HEADS * FOX_HEAD_DIM
MEM_HEADS = 4
MEM_HEAD_DIM = 128
MEM_WIDTH = MEM_HEADS * MEM_HEAD_DIM
POOL_GROUPS = 4
POOL_GROUP_DIM = 128
POOL_WIDTH = POOL_GROUPS * POOL_GROUP_DIM
POOL_WINDOWS = (2, 4, 8, 16)
LANES = 128
F_PAD = LANES

ADAM_LR = 0.001
ADAM_B1 = 0.9
ADAM_B2 = 0.999
ADAM_EPS = 1e-08
ADAM_WD = 0.01
ADAM_STEP = 10

VMEM_LIMIT = 56 * 1024 * 1024
NEG = -1e30

ANY = pl.BlockSpec(memory_space=pl.ANY)


def _params(sem=None):
    return pltpu.CompilerParams(dimension_semantics=sem, vmem_limit_bytes=VMEM_LIMIT)


MXU_DIM = 256


def _tile(dim, cap, align):
    best = None
    t = align
    while t <= min(dim, cap):
        if dim % t == 0:
            best = t
        t += align
    return dim if best is None else best


def _mxu_tile(dim, cap):
    wide, fine = _tile(dim, cap, MXU_DIM), _tile(dim, cap, LANES)
    whole_widths = wide % MXU_DIM == 0 and wide <= cap
    return wide if whole_widths and fine < 2 * wide else fine


class _Sems:
    def __init__(self, send, recv, local):
        self.send, self.recv, self.local = send, recv, local
        self.n_remote = self.n_local = 0

    def remote(self):
        i = self.n_remote
        self.n_remote += 1
        return self.send.at[i], self.recv.at[i]

    def one_local(self):
        i = self.n_local
        self.n_local += 1
        return self.local.at[i]


class _Comm:
    def __init__(self, srcs, outs, n_remote, n_local, plan, in_place=False):
        self.srcs, self.outs, self.n_remote, self.n_local = list(srcs), list(outs), n_remote, n_local
        self.plan, self.in_place = plan, in_place

    def aliases(self, n_in, n_out):
        return {n_in + i: n_out + i for i in range(len(self.srcs))} if self.in_place else {}

    def scratch(self):
        return [pltpu.SemaphoreType.DMA((self.n_remote,)), pltpu.SemaphoreType.DMA((self.n_remote,)),
                pltpu.SemaphoreType.DMA((max(self.n_local, 1),))]

    def run(self, src_refs, out_refs, sem_refs, first, last):
        def copies():
            return self.plan(list(src_refs), list(out_refs), _Sems(*sem_refs))

        return (lambda: _when(first, lambda: [cp.start() for cp in copies()]),
                lambda: _when(last, lambda: [cp.wait() for cp in copies()]))


def _when(cond, fn):
    @pl.when(cond)
    def _():
        fn()


def _join(*comms):
    comms = [c for c in comms if c is not None]
    assert all(not c.in_place for c in comms)

    def plan(src_refs, out_refs, sems):
        copies, si, oi = [], 0, 0
        for c in comms:
            copies += c.plan(src_refs[si:si + len(c.srcs)], out_refs[oi:oi + len(c.outs)], sems)
            si += len(c.srcs)
            oi += len(c.outs)
        return copies

    return _Comm(sum((c.srcs for c in comms), []), sum((c.outs for c in comms), []),
                 sum(c.n_remote for c in comms), sum(c.n_local for c in comms), plan)


def _split_refs(refs, n_in, n_out, comm):
    if comm is None:
        return refs[:n_in], refs[n_in:n_in + n_out], refs[n_in + n_out:], (), (), ()
    ns, no = len(comm.srcs), len(comm.outs)
    ins = refs[:n_in]
    srcs = refs[n_in:n_in + ns]
    outs = refs[n_in + ns:n_in + ns + n_out]
    couts = refs[n_in + ns + n_out:n_in + ns + n_out + no]
    rest = refs[n_in + ns + n_out + no:]
    return ins, outs, rest[:-3], srcs, couts, rest[-3:]


def _sigmoid(x):
    return 1.0 / (1.0 + jnp.exp(-x))


_DIMS = {"nn": (((1,), (0,)), ((), ())), "nt": (((1,), (1,)), ((), ())), "tn": (((0,), (0,)), ((), ()))}


def _mm(a, b, mode, name, out_dtype=F32, scale=1.0, res=None, caps=None, b_blocks=False, out_blocks=False,
        halves=None, comm=None):
    nblk = N_DEV
    a_shape, b_shape = a.shape, b.shape
    if b_blocks:
        _, r0, c0 = b.shape
        b_shape = (r0, nblk * c0)
    if halves == "a":
        assert mode == "nt"
        a_shape = (a.shape[1], 2 * a.shape[2])
    elif halves == "b":
        assert mode == "tn"
        b_shape = (b.shape[1], 2 * b.shape[2])
    if mode == "nn":
        (m, k), (k2, n) = a_shape, b_shape
    elif mode == "nt":
        (m, k), (n, k2) = a_shape, b_shape
    else:
        (k, m), (k2, n) = a_shape, b_shape
    assert k == k2, (name, a.shape, b.shape)
    cm, cn, ck = caps or ((1024, 512, 2048) if k <= 2048 else (1024, 1024, 2048))
    n_unit = n // nblk if (out_blocks or (b_blocks and mode == "nn")) else n
    k_unit = k // nblk if (b_blocks and mode == "nt") else k
    tm, tn, tk = _mxu_tile(m, cm), _mxu_tile(n_unit, cn), _mxu_tile(k_unit, ck)
    nk = k // tk
    npb, kpb = n_unit // tn, k_unit // tk
    n_half, k_half = (n // 2) // tn, (k // 2) // tk
    if mode == "nn":
        a_spec = pl.BlockSpec((tm, tk), lambda i, j, kk: (i, kk))
        b_spec = pl.BlockSpec((tk, tn), lambda i, j, kk: (kk, j))
        if b_blocks:
            b_spec = pl.BlockSpec((None, tk, tn), lambda i, j, kk: (j // npb, kk, j % npb))
    elif mode == "nt":
        a_spec = pl.BlockSpec((tm, tk), lambda i, j, kk: (i, kk))
        b_spec = pl.BlockSpec((tn, tk), lambda i, j, kk: (j, kk))
        if b_blocks:
            b_spec = pl.BlockSpec((None, tn, tk), lambda i, j, kk: (kk // kpb, j, kk % kpb))
        if halves == "a":
            assert (k // 2) % tk == 0
            a_spec = pl.BlockSpec((None, tm, tk), lambda i, j, kk: (kk // k_half, i, kk % k_half))
    else:
        assert not b_blocks
        a_spec = pl.BlockSpec((tk, tm), lambda i, j, kk: (kk, i))
        b_spec = pl.BlockSpec((tk, tn), lambda i, j, kk: (kk, j))
        if halves == "b":
            assert (n // 2) % tn == 0
            b_spec = pl.BlockSpec((None, tk, tn), lambda i, j, kk: (j // n_half, kk, j % n_half))
    if out_blocks:
        assert res is None
        o_spec = pl.BlockSpec((None, tm, tn), lambda i, j, kk: (j // npb, i, j % npb))
        o_shape = jax.ShapeDtypeStruct((nblk, m, n // nblk), out_dtype)
    else:
        o_spec = pl.BlockSpec((tm, tn), lambda i, j, kk: (i, j))
        o_shape = jax.ShapeDtypeStruct((m, n), out_dtype)
    in_specs = [a_spec, b_spec] + ([o_spec] if res is not None else [])
    n_in = len(in_specs)
    dims = _DIMS[mode]
    gm, gn = m // tm, n // tn

    def kern(*refs):
        ins, outs, scratch, c_src, c_out, c_sem = _split_refs(refs, n_in, 1, comm)
        a_ref, b_ref = ins[0], ins[1]
        res_ref = ins[2] if res is not None else None
        o_ref = outs[0]
        if comm is not None:
            i, j, kq = pl.program_id(0), pl.program_id(1), pl.program_id(2)
            first = jnp.logical_and(jnp.logical_and(i == 0, j == 0), kq == 0)
            last = jnp.logical_and(jnp.logical_and(i == gm - 1, j == gn - 1), kq == nk - 1)
            start_comm, wait_comm = comm.run(c_src, c_out, c_sem, first, last)
            start_comm()
        a_tile = a_ref[...].astype(BF16)
        if scale != 1.0:
            a_tile = a_tile * scale

        def product():
            return lax.dot_general(a_tile, b_ref[...].astype(BF16), dims, preferred_element_type=F32)

        if nk == 1:
            r = product()
            if res_ref is not None:
                r = res_ref[...] + r
            o_ref[...] = r.astype(out_dtype)
        else:
            acc_ref = scratch[0] if scratch else o_ref
            kk = pl.program_id(2)

            @pl.when(kk == 0)
            def _():
                acc_ref[...] = jnp.zeros_like(acc_ref) if res_ref is None else res_ref[...]

            acc_ref[...] += product()
            if scratch:
                @pl.when(kk == nk - 1)
                def _():
                    o_ref[...] = acc_ref[...].astype(out_dtype)
        if comm is not None:
            wait_comm()

    assert scale in (1.0, 0.5)
    args = (a, b) + ((res,) if res is not None else ())
    scratch_shapes = [pltpu.VMEM((tm, tn), F32)] if (nk > 1 and out_dtype != F32) else []
    if comm is None:
        return pl.pallas_call(
            kern, name=name, grid=(gm, gn, nk), in_specs=in_specs, out_specs=o_spec, out_shape=o_shape,
            scratch_shapes=scratch_shapes, compiler_params=_params(("parallel", "parallel", "arbitrary")),
        )(*args)
    res_all = pl.pallas_call(
        kern, name=name, grid=(gm, gn, nk), in_specs=in_specs + [ANY] * len(comm.srcs),
        out_specs=[o_spec] + [ANY] * len(comm.outs), out_shape=[o_shape] + comm.outs,
        scratch_shapes=scratch_shapes + comm.scratch(), input_output_aliases=comm.aliases(n_in, 1),
        compiler_params=_params(("arbitrary", "arbitrary", "arbitrary")),
    )(*args, *comm.srcs)
    return res_all[0], list(res_all[1:])


def _gate_up_swiglu(h, w_blocks, name, comm=None):
    t, d = h.shape
    nblk, _, nb = w_blocks.shape
    half = nblk // 2
    tm = _mxu_tile(t, 512)
    steps = t // tm

    def kern(*refs):
        (h_ref, wg_ref, wu_ref), (gu_ref, a_ref), _, c_src, c_out, c_sem = _split_refs(refs, 3, 2, comm)
        if comm is not None:
            j, i = pl.program_id(0), pl.program_id(1)
            start_comm, wait_comm = comm.run(c_src, c_out, c_sem, jnp.logical_and(j == 0, i == 0),
                                             jnp.logical_and(j == half - 1, i == steps - 1))
            start_comm()
        hv = h_ref[...]
        g = lax.dot_general(hv, wg_ref[...], _DIMS["nn"], preferred_element_type=F32)
        u = lax.dot_general(hv, wu_ref[...], _DIMS["nn"], preferred_element_type=F32)
        gu_ref[0] = g
        gu_ref[1] = u
        a_ref[...] = ((g * _sigmoid(g)) * u).astype(BF16)
        if comm is not None:
            wait_comm()

    in_specs = [pl.BlockSpec((tm, d), lambda j, i: (i, 0)), pl.BlockSpec((None, d, nb), lambda j, i: (j, 0, 0)),
                pl.BlockSpec((None, d, nb), lambda j, i: (j + half, 0, 0))]
    out_specs = [pl.BlockSpec((2, tm, nb), lambda j, i: (0, i, j)), pl.BlockSpec((tm, nb), lambda j, i: (i, j))]
    out_shape = [jax.ShapeDtypeStruct((2, t, half * nb), F32), jax.ShapeDtypeStruct((t, half * nb), BF16)]
    if comm is None:
        return pl.pallas_call(kern, name=name, grid=(half, steps), in_specs=in_specs, out_specs=out_specs,
                              out_shape=out_shape, compiler_params=_params(("parallel", "parallel")))(h, w_blocks, w_blocks)
    res_all = pl.pallas_call(
        kern, name=name, grid=(half, steps), in_specs=in_specs + [ANY] * len(comm.srcs),
        out_specs=out_specs + [ANY] * len(comm.outs), out_shape=out_shape + comm.outs,
        scratch_shapes=comm.scratch(), compiler_params=_params(("arbitrary", "arbitrary")),
    )(h, w_blocks, w_blocks, *comm.srcs)
    return list(res_all[:2]), list(res_all[2:])


def _swiglu_bwd_from_out(dy, wd, gu, name, comm=None):
    t, d = dy.shape
    f = wd.shape[0]
    tm, tn = _mxu_tile(t, 1024), _mxu_tile(f, 512)
    gi, gj = t // tm, f // tn

    def kern(*refs):
        (dy_ref, wd_ref, gu_ref), (out_ref,), _, c_src, c_out, c_sem = _split_refs(refs, 3, 1, comm)
        if comm is not None:
            i, j = pl.program_id(0), pl.program_id(1)
            start_comm, wait_comm = comm.run(c_src, c_out, c_sem, jnp.logical_and(i == 0, j == 0),
                                             jnp.logical_and(i == gi - 1, j == gj - 1))
            start_comm()
        da = lax.dot_general(dy_ref[...] * 0.5, wd_ref[...], _DIMS["nt"], preferred_element_type=F32)
        g, u = gu_ref[0], gu_ref[1]
        s = _sigmoid(g)
        out_ref[0] = (da * u * (s * (1.0 + g * (1.0 - s)))).astype(BF16)
        out_ref[1] = (da * (g * s)).astype(BF16)
        if comm is not None:
            wait_comm()

    pair = pl.BlockSpec((2, tm, tn), lambda i, j: (0, i, j))
    in_specs = [pl.BlockSpec((tm, d), lambda i, j: (i, 0)), pl.BlockSpec((tn, d), lambda i, j: (j, 0)), pair]
    out_shape = jax.ShapeDtypeStruct((2, t, f), BF16)
    if comm is None:
        return pl.pallas_call(kern, name=name, grid=(gi, gj), in_specs=in_specs, out_specs=pair, out_shape=out_shape,
                              compiler_params=_params(("parallel", "parallel")))(dy, wd, gu)
    res_all = pl.pallas_call(
        kern, name=name, grid=(gi, gj), in_specs=in_specs + [ANY] * len(comm.srcs),
        out_specs=[pair] + [ANY] * len(comm.outs), out_shape=[out_shape] + comm.outs,
        scratch_shapes=comm.scratch(), compiler_params=_params(("arbitrary", "arbitrary")),
    )(dy, wd, gu, *comm.srcs)
    return res_all[0], list(res_all[1:])


def _rowwise(body, ins, params, outs, accs, name, tm, comm=None, into=None):
    rows = ins[0][0].shape[0]
    assert rows % tm == 0, (name, rows, tm)
    in_specs = []
    for arr, off, width in ins:
        assert off % width == 0 and arr.shape[0] == rows, (name, arr.shape, off, width)
        in_specs.append(pl.BlockSpec((tm, width), functools.partial(lambda i, c: (i, c), c=off // width)))
    for p in params:
        in_specs.append(pl.BlockSpec(p.shape, functools.partial(lambda i, nd: (0,) * nd, nd=p.ndim)))
    out_specs = [pl.BlockSpec((tm, w), lambda i: (i, 0)) for w, _ in outs]
    out_specs += [pl.BlockSpec(s, functools.partial(lambda i, nd: (0,) * nd, nd=len(s))) for s, _ in accs]
    out_shape = [jax.ShapeDtypeStruct((rows, w), d) for w, d in outs]
    out_shape += [jax.ShapeDtypeStruct(s, d) for s, d in accs]
    n_in, n_par, n_out = len(ins), len(params), len(outs)
    carried = []
    if into is not None:
        buf, total, col = into
        assert comm is None and col % outs[0][0] == 0, (name, col, outs[0])
        out_specs[0] = pl.BlockSpec((tm, outs[0][0]), functools.partial(lambda i, c: (i, c), c=col // outs[0][0]))
        out_shape[0] = jax.ShapeDtypeStruct((rows, total), outs[0][1])
        carried = [] if buf is None else [buf]

    steps = rows // tm

    def kern(*refs):
        in_refs, out_refs, _, c_src, c_out, c_sem = _split_refs(refs, n_in + n_par + len(carried),
                                                                 n_out + len(accs), comm)
        in_refs = in_refs[:n_in + n_par]
        first = pl.program_id(0) == 0
        if comm is not None:
            start_comm, wait_comm = comm.run(c_src, c_out, c_sem, first, pl.program_id(0) == steps - 1)
            start_comm()
        res = body(*[r[...] for r in in_refs])
        for r, v in zip(out_refs[:n_out], res[:n_out]):
            r[...] = v.astype(r.dtype)
        for r, v in zip(out_refs[n_out:], res[n_out:]):
            @pl.when(first)
            def _(r=r, v=v):
                r[...] = v.astype(r.dtype)

            @pl.when(jnp.logical_not(first))
            def _(r=r, v=v):
                r[...] += v.astype(r.dtype)
        if comm is not None:
            wait_comm()

    args = [a for a, _, _ in ins] + list(params)
    if comm is None:
        return pl.pallas_call(
            kern, name=name, grid=(steps,), in_specs=in_specs + [ANY] * len(carried), out_specs=out_specs,
            out_shape=out_shape, input_output_aliases={len(args): 0} if carried else {},
            compiler_params=_params(("arbitrary",) if accs else ("parallel",)),
        )(*args, *carried)
    res_all = pl.pallas_call(
        kern, name=name, grid=(steps,), in_specs=in_specs + [ANY] * len(comm.srcs),
        out_specs=out_specs + [ANY] * len(comm.outs), out_shape=out_shape + comm.outs,
        scratch_shapes=comm.scratch(), input_output_aliases=comm.aliases(len(args), len(out_shape)),
        compiler_params=_params(("arbitrary",)),
    )(*args, *comm.srcs)
    return list(res_all[:len(out_shape)]), list(res_all[len(out_shape):])


def _whole(x):
    return (x, 0, x.shape[1])


def _first(res, comm):
    return res[0] if comm is None else (res[0][0], res[1])


def _rms_fwd(x, gain, name, comm=None):
    def body(xv, g):
        r = lax.rsqrt(jnp.mean(xv * xv, axis=-1, keepdims=True) + EPS)
        return ((xv * r) * g,)

    return _first(_rowwise(body, [_whole(x)], [gain], [(x.shape[1], BF16)], [], name, _tile(x.shape[0], 256, 16),
                           comm=comm), comm)


def _rms_bwd(x, gain, dh, dres, name, comm=None):
    d = x.shape[1]

    def body(*vals):
        if dres is None:
            xv, dhv, g = vals
        else:
            xv, dhv, drv, g = vals
        r = lax.rsqrt(jnp.mean(xv * xv, axis=-1, keepdims=True) + EPS)
        xh = xv * r
        w = dhv * g
        dx = r * (w - xh * jnp.mean(w * xh, axis=-1, keepdims=True))
        if dres is not None:
            dx = drv + dx
        return dx, dx, jnp.sum(dhv * xh, axis=0, keepdims=True)

    ins = [_whole(x), _whole(dh)] + ([_whole(dres)] if dres is not None else [])
    return _rowwise(body, ins, [gain], [(d, F32), (d, BF16)], [((1, d), F32)], name, _tile(x.shape[0], 256, 16),
                    comm=comm)


def _loss_head(y, target, name):
    d = y.shape[1]

    def body(yv, tv):
        e = yv - tv
        part = jnp.sum(jnp.sum(e * e, axis=1, keepdims=True), axis=0, keepdims=True)
        return e / d, e / d, jnp.broadcast_to((0.5 / d) * part, (1, LANES))

    return _rowwise(body, [_whole(y), _whole(target)], [], [(d, F32), (d, BF16)], [((1, LANES), F32)], name,
                    _tile(y.shape[0], 256, 16))


def _head_mean(v, head_dim):
    cols = []
    lane = lax.broadcasted_iota(jnp.int32, (1, LANES), 1)
    for j in range(v.shape[1] // LANES):
        blk = v[:, j * LANES:(j + 1) * LANES]
        if head_dim == LANES:
            m = jnp.sum(blk, axis=-1, keepdims=True)
            cols.append(jnp.broadcast_to(m, blk.shape))
        else:
            lo = jnp.sum(jnp.where(lane < head_dim, blk, 0.0), axis=-1, keepdims=True)
            hi = jnp.sum(jnp.where(lane >= head_dim, blk, 0.0), axis=-1, keepdims=True)
            cols.append(jnp.where(lane < head_dim, lo, hi))
    return jnp.concatenate(cols, axis=1) / head_dim


def _head_norm(xv, g, head_dim):
    r = lax.rsqrt(_head_mean(xv * xv, head_dim) + EPS)
    return (xv * r) * g


def _head_norm_bwd(xv, g, dy, head_dim):
    r = lax.rsqrt(_head_mean(xv * xv, head_dim) + EPS)
    xh = xv * r
    w = dy * g
    dx = r * (w - xh * _head_mean(w * xh, head_dim))
    return dx, jnp.sum(dy * xh, axis=0, keepdims=True)


def _log_sigmoid(x):
    return jnp.minimum(x, 0.0) - jnp.log(1.0 + jnp.exp(-jnp.abs(x)))


def _shift_rows(v, sh, down):
    n = v.shape[0]
    row = lax.broadcasted_iota(jnp.int32, (n, 1), 0)
    if down:
        return jnp.where(row >= sh, pltpu.roll(v, sh, 0), 0.0)
    return jnp.where(row < n - sh, pltpu.roll(v, n - sh, 0), 0.0)


def _scan_rows(v, down):
    sh = 1
    while sh < v.shape[0]:
        v = v + _shift_rows(v, sh, down)
        sh *= 2
    return v


def _window_sum(v, steps, down):
    for s in range(steps):
        v = v + _shift_rows(v, 2 ** s, down)
    return v


def _fox_fwd(qn, kn, vb, cks, batch, seq, name, comm=None):
    t, width = qn.shape
    pairs = width // LANES
    tq = _tile(seq, 512, LANES)
    nq = seq // tq

    def kern(*refs):
        (q_ref, k_ref, v_ref, c_ref), (o_ref, lse_ref), _, c_src, c_out, c_sem = _split_refs(refs, 4, 2, comm)
        qi = pl.program_id(2)
        if comm is not None:
            b_id, p_id = pl.program_id(0), pl.program_id(1)
            first = jnp.logical_and(jnp.logical_and(b_id == 0, p_id == 0), qi == 0)
            last = jnp.logical_and(jnp.logical_and(b_id == batch - 1, p_id == pairs - 1), qi == nq - 1)
            start_comm, wait_comm = comm.run(c_src, c_out, c_sem, first, last)
            start_comm()
        lane = lax.broadcasted_iota(jnp.int32, (1, LANES), 1)
        rowi = lax.broadcasted_iota(jnp.int32, (tq, tq), 0)
        coli = lax.broadcasted_iota(jnp.int32, (tq, tq), 1)
        q_all = q_ref[...]
        o_heads, lse_heads = [], []
        for h in range(2):
            hm = (lane < FOX_HEAD_DIM) if h == 0 else (lane >= FOX_HEAD_DIM)
            q = jnp.where(hm, q_all, jnp.zeros_like(q_all))

            def step(j, carry, h=h, q=q, diagonal=False):
                m, l, acc = carry
                start = pl.multiple_of(j * tq, tq)
                k = k_ref[pl.ds(start, tq), :]
                v = v_ref[pl.ds(start, tq), :]
                s = lax.dot_general(q, k, _DIMS["nt"], preferred_element_type=F32)
                s = s - c_ref[0, h:h + 1, pl.ds(start, tq)]
                if diagonal:
                    s = jnp.where(rowi >= coli, s, NEG)
                m_new = jnp.maximum(m, jnp.max(s, axis=-1, keepdims=True))
                alpha = jnp.exp(m - m_new)
                p = jnp.exp(s - m_new)
                l = alpha * l + jnp.sum(p, axis=-1, keepdims=True)
                acc = alpha * acc + lax.dot_general(p.astype(BF16), v, _DIMS["nn"], preferred_element_type=F32)
                return m_new, l, acc

            init = (jnp.full((tq, 1), NEG, F32), jnp.zeros((tq, 1), F32), jnp.zeros((tq, LANES), F32))
            m, l, acc = step(qi, lax.fori_loop(0, qi, step, init), diagonal=True)
            o_heads.append(acc / l)
            lse_heads.append(jnp.broadcast_to(m + jnp.log(l), (tq, LANES)))
        o_ref[...] = jnp.where(lane < FOX_HEAD_DIM, o_heads[0], o_heads[1])
        lse_ref[...] = jnp.where(lane < FOX_HEAD_DIM, lse_heads[0], lse_heads[1])
        if comm is not None:
            wait_comm()

    q_spec = pl.BlockSpec((tq, LANES), lambda b, hp, qi: (b * nq + qi, hp))
    kv_spec = pl.BlockSpec((seq, LANES), lambda b, hp, qi: (b, hp))
    c_spec = pl.BlockSpec((1, 2, seq), lambda b, hp, qi: (b * pairs + hp, 0, 0))
    in_specs = [q_spec, kv_spec, kv_spec, c_spec]
    out_shape = [jax.ShapeDtypeStruct((t, width), F32), jax.ShapeDtypeStruct((t, width), F32)]
    if comm is None:
        return pl.pallas_call(
            kern, name=name, grid=(batch, pairs, nq), in_specs=in_specs, out_specs=[q_spec, q_spec],
            out_shape=out_shape, compiler_params=_params(("parallel", "parallel", "arbitrary")),
        )(qn, kn, vb, cks)
    res_all = pl.pallas_call(
        kern, name=name, grid=(batch, pairs, nq), in_specs=in_specs + [ANY] * len(comm.srcs),
        out_specs=[q_spec, q_spec] + [ANY] * len(comm.outs), out_shape=out_shape + comm.outs,
        scratch_shapes=comm.scratch(), compiler_params=_params(("arbitrary", "arbitrary", "arbitrary")),
    )(qn, kn, vb, cks, *comm.srcs)
    return list(res_all[:2]), list(res_all[2:])


def _fox_bwd(qn, kn, vb, cks, o, lse, do, dz, dv_col, batch, seq, name, comm):
    t, width = qn.shape
    pairs = width // LANES
    tk = _tile(seq, 512, LANES)
    nk = seq // tk
    scale = FOX_HEAD_DIM ** -0.5

    def kern(*refs):
        ins, outs, _, c_src, c_out, c_sem = _split_refs(refs, 8, 5, comm)
        k_ref, v_ref, q_ref, c_ref, o_ref, lse_ref, do_ref, _ = ins
        dk_ref, dv_ref, dq_ref, dc_ref, dcq_ref = outs
        kj = pl.program_id(2)
        if comm is not None:
            b_id, p_id = pl.program_id(0), pl.program_id(1)
            first = jnp.logical_and(jnp.logical_and(b_id == 0, p_id == 0), kj == 0)
            last = jnp.logical_and(jnp.logical_and(b_id == batch - 1, p_id == pairs - 1), kj == nk - 1)
            start_comm, wait_comm = comm.run(c_src, c_out, c_sem, first, last)
            start_comm()
        lane = lax.broadcasted_iota(jnp.int32, (1, LANES), 1)
        rowi = lax.broadcasted_iota(jnp.int32, (tk, tk), 0)
        coli = lax.broadcasted_iota(jnp.int32, (tk, tk), 1)

        @pl.when(kj == 0)
        def _():
            dq_ref[...] = jnp.zeros_like(dq_ref)
            dcq_ref[...] = jnp.zeros_like(dcq_ref)

        k_all = k_ref[...]
        v_all = v_ref[...]
        kstart = pl.multiple_of(kj * tk, tk)
        dk_heads, dv_heads = [], []
        for h in range(2):
            hm = (lane < FOX_HEAD_DIM) if h == 0 else (lane >= FOX_HEAD_DIM)
            kh = jnp.where(hm, k_all, jnp.zeros_like(k_all))
            vh = jnp.where(hm, v_all, jnp.zeros_like(v_all))
            c_row = c_ref[0, h:h + 1, pl.ds(kstart, tk)]

            def step(qi, carry, h=h, hm=hm, kh=kh, vh=vh, c_row=c_row, diagonal=False):
                dk_acc, dv_acc, dc_acc = carry
                start = pl.multiple_of(qi * tk, tk)
                q = q_ref[pl.ds(start, tk), :]
                dov = do_ref[pl.ds(start, tk), :]
                ov = o_ref[pl.ds(start, tk), :]
                lse_col = jnp.max(jnp.where(hm, lse_ref[pl.ds(start, tk), :], NEG), axis=-1, keepdims=True)
                dob = jnp.where(hm, dov, 0.0).astype(BF16)
                dcol = jnp.sum(dob.astype(F32) * ov, axis=-1, keepdims=True)
                s = lax.dot_general(q, kh, _DIMS["nt"], preferred_element_type=F32) - c_row
                p = jnp.exp(s - lse_col)
                if diagonal:
                    p = jnp.where(rowi >= coli, p, 0.0)
                dp = lax.dot_general(dob, vh, _DIMS["nt"], preferred_element_type=F32)
                ds = p * (dp - dcol)
                ds_b = ds.astype(BF16)
                dv_acc = dv_acc + lax.dot_general(p.astype(BF16), dob, _DIMS["tn"], preferred_element_type=F32)
                dk_acc = dk_acc + lax.dot_general(ds_b, q, _DIMS["tn"], preferred_element_type=F32)
                dq_part = lax.dot_general(ds_b * scale, kh, _DIMS["nn"], preferred_element_type=F32)
                dq_ref[pl.ds(start, tk), :] += dq_part
                dcq_ref[pl.ds(start, tk), :] += jnp.where(hm, jnp.sum(ds, axis=-1, keepdims=True), 0.0)
                dc_acc = dc_acc - jnp.sum(ds, axis=0, keepdims=True)
                return dk_acc, dv_acc, dc_acc

            init = (jnp.zeros((tk, LANES), F32), jnp.zeros((tk, LANES), F32), jnp.zeros((1, tk), F32))
            dk_acc, dv_acc, dc_acc = lax.fori_loop(kj + 1, nk, step, step(kj, init, diagonal=True))
            dk_heads.append(dk_acc)
            dv_heads.append(dv_acc)
            dc_ref[0, h:h + 1, pl.ds(kstart, tk)] = dc_acc
        dk_ref[...] = jnp.where(lane < FOX_HEAD_DIM, dk_heads[0], dk_heads[1])
        dv_ref[...] = jnp.where(lane < FOX_HEAD_DIM, dv_heads[0], dv_heads[1]).astype(BF16)
        if comm is not None:
            wait_comm()

    kv_spec = pl.BlockSpec((tk, LANES), lambda b, hp, kj: (b * nk + kj, hp))
    full_spec = pl.BlockSpec((seq, LANES), lambda b, hp, kj: (b, hp))
    c_spec = pl.BlockSpec((1, 2, seq), lambda b, hp, kj: (b * pairs + hp, 0, 0))
    dv_spec = pl.BlockSpec((tk, LANES), lambda b, hp, kj: (b * nk + kj, hp + dv_col // LANES))
    in_specs = [kv_spec, kv_spec, full_spec, c_spec, full_spec, full_spec, full_spec, ANY]
    out_specs = [kv_spec, dv_spec, full_spec, c_spec, full_spec]
    out_shape = [jax.ShapeDtypeStruct((t, width), F32), jax.ShapeDtypeStruct(dz.shape, BF16),
                 jax.ShapeDtypeStruct((t, width), F32), jax.ShapeDtypeStruct(cks.shape, F32),
                 jax.ShapeDtypeStruct((t, width), F32)]
    res_all = pl.pallas_call(
        kern, name=name, grid=(batch, pairs, nk), in_specs=in_specs + [ANY] * len(comm.srcs),
        out_specs=out_specs + [ANY] * len(comm.outs), out_shape=out_shape + comm.outs,
        scratch_shapes=comm.scratch(), input_output_aliases={7: 1},
        compiler_params=_params(("arbitrary", "arbitrary", "arbitrary")),
    )(kn, vb, qn, cks, o, lse, do, dz, *comm.srcs)
    return list(res_all[:5]), list(res_all[5:])


def _mem_fwd(qn, kn, vb, batch, seq, mlen, name):
    t, width = qn.shape
    heads = width // LANES
    tq = _tile(seq, 512, LANES)
    nq = seq // tq
    scale = MEM_HEAD_DIM ** -0.5

    def kern(q_ref, k_ref, v_ref, o_ref):
        s = lax.dot_general(q_ref[...], k_ref[...], _DIMS["nt"], preferred_element_type=F32) * scale
        e = jnp.exp(s - jnp.max(s, axis=-1, keepdims=True))
        p = e / jnp.sum(e, axis=-1, keepdims=True)
        o_ref[...] = lax.dot_general(p.astype(BF16), v_ref[...], _DIMS["nn"], preferred_element_type=F32)

    q_spec = pl.BlockSpec((tq, LANES), lambda b, h, qi: (b * nq + qi, h))
    kv_spec = pl.BlockSpec((mlen, LANES), lambda b, h, qi: (b, h))
    return pl.pallas_call(
        kern, name=name, grid=(batch, heads, nq), in_specs=[q_spec, kv_spec, kv_spec], out_specs=q_spec,
        out_shape=jax.ShapeDtypeStruct((t, width), F32),
        compiler_params=_params(("parallel", "parallel", "parallel")),
    )(qn, kn, vb)


def _mem_bwd(qn, kn, vb, do, batch, seq, mlen, name):
    t, width = qn.shape
    heads = width // LANES
    tq = _tile(seq, 512, LANES)
    nq = seq // tq
    scale = MEM_HEAD_DIM ** -0.5

    def kern(q_ref, k_ref, v_ref, do_ref, dq_ref, dk_ref, dv_ref):
        qi = pl.program_id(2)
        q, k, v = q_ref[...], k_ref[...], v_ref[...]
        dob = do_ref[...].astype(BF16)
        s = lax.dot_general(q, k, _DIMS["nt"], preferred_element_type=F32) * scale
        e = jnp.exp(s - jnp.max(s, axis=-1, keepdims=True))
        p = e / jnp.sum(e, axis=-1, keepdims=True)
        dp = lax.dot_general(dob, v, _DIMS["nt"], preferred_element_type=F32)
        ds = p * (dp - jnp.sum(p * dp, axis=-1, keepdims=True))
        dsb = (ds * scale).astype(BF16)
        dq_ref[...] = lax.dot_general(dsb, k, _DIMS["nn"], preferred_element_type=F32)
        dk = lax.dot_general(dsb, q, _DIMS["tn"], preferred_element_type=F32)
        dv = lax.dot_general(p.astype(BF16), dob, _DIMS["tn"], preferred_element_type=F32)

        @pl.when(qi == 0)
        def _():
            dk_ref[...] = dk
            dv_ref[...] = dv

        @pl.when(qi > 0)
        def _():
            dk_ref[...] += dk
            dv_ref[...] += dv

    q_spec = pl.BlockSpec((tq, LANES), lambda b, h, qi: (b * nq + qi, h))
    kv_spec = pl.BlockSpec((mlen, LANES), lambda b, h, qi: (b, h))
    return pl.pallas_call(
        kern, name=name, grid=(batch, heads, nq), in_specs=[q_spec, kv_spec, kv_spec, q_spec],
        out_specs=[q_spec, kv_spec, kv_spec],
        out_shape=[jax.ShapeDtypeStruct((t, width), F32), jax.ShapeDtypeStruct(kn.shape, F32),
                   jax.ShapeDtypeStruct(kn.shape, F32)],
        compiler_params=_params(("parallel", "parallel", "arbitrary")),
    )(qn, kn, vb, do)


def _position():
    return lax.axis_index("x"), lax.axis_index("y"), lax.axis_index("c")


def _other_chips(x, y):
    return [(1 - x, y), (x, 1 - y), (1 - x, 1 - y)]


def _remote(src, dst, sems, to):
    send, recv = sems.remote()
    return pltpu.make_async_remote_copy(src_ref=src, dst_ref=dst, send_sem=send, recv_sem=recv,
                                        device_id=to, device_id_type=MESH)


def _gather_first(shards):
    n = len(shards)

    def plan(src, out, sems):
        x, y, c = _position()
        me = 4 * x + 2 * y + c
        peers = [(x, y, 1 - c)] + [(px, py, c) for px, py in _other_chips(x, y)]
        copies = []
        for i in range(n):
            copies.append(pltpu.make_async_copy(src[i], out[i].at[me], sems.one_local()))
            copies += [_remote(src[i], out[i].at[me], sems, to) for to in peers]
        return copies

    outs = [jax.ShapeDtypeStruct((N_DEV,) + s.shape, s.dtype) for s in shards]
    return _Comm(shards, outs, 4 * n, n, plan)


def _gather_second(bufs):
    n = len(bufs)

    def plan(src, out, sems):
        x, y, c = _position()
        copies = []
        for i in range(n):
            for px, py in _other_chips(x, y):
                blk = out[i].at[4 * px + 2 * py + c]
                copies.append(_remote(blk, blk, sems, (x, y, 1 - c)))
        return copies

    outs = [jax.ShapeDtypeStruct(b.shape, b.dtype) for b in bufs]
    return _Comm(bufs, outs, 3 * n, 0, plan, in_place=True)


def _scatter_first(partials):
    n = len(partials)

    def plan(src, out, sems):
        x, y, c = _position()
        return [_remote(src[i].at[2 * q + (1 - c)], out[i].at[q], sems, (x, y, 1 - c))
                for i in range(n) for q in range(4)]

    outs = [jax.ShapeDtypeStruct((4,) + g.shape[1:], g.dtype) for g in partials]
    return _Comm(partials, outs, 4 * n, 0, plan)


def _scatter_second(halves):
    n = len(halves)

    def plan(src, out, sems):
        x, y, c = _position()
        return [_remote(src[i].at[2 * px + py], out[i].at[r], sems, (px, py, c))
                for i in range(n) for r, (px, py) in enumerate(_other_chips(x, y))]

    outs = [jax.ShapeDtypeStruct((3,) + h.shape[1:], h.dtype) for h in halves]
    return _Comm(halves, outs, 3 * n, 0, plan)


def _comm_call(comm, name):
    ns, no = len(comm.srcs), len(comm.outs)

    def body(*refs):
        copies = comm.plan(list(refs[:ns]), list(refs[ns:ns + no]), _Sems(*refs[ns + no:]))
        for cp in copies:
            cp.start()
        for cp in copies:
            cp.wait()

    return pl.pallas_call(
        body, name=name, in_specs=[ANY] * ns, out_specs=[ANY] * no, out_shape=comm.outs,
        scratch_shapes=comm.scratch(), input_output_aliases=comm.aliases(0, 0),
    )(*comm.srcs)


def _gather_first_relayed(shard, name):
    def body(x_ref, out_ref, send, recv, local):
        sems = _Sems(send, recv, local)
        x, y, c = _position()

        def slot(px, py):
            return out_ref.at[4 * px + 2 * py + c]

        mine = pltpu.make_async_copy(x_ref, slot(x, y), sems.one_local())
        to_sibling = _remote(x_ref, slot(x, y), sems, (x, y, 1 - c))
        to_x = _remote(x_ref, slot(x, y), sems, (1 - x, y, c))
        to_y = _remote(x_ref, slot(x, y), sems, (x, 1 - y, c))
        for cp in (mine, to_sibling, to_x, to_y):
            cp.start()
        relay_send, relay_recv = sems.remote()

        def relay(arrived, unused, origin, to):
            arrived.wait_recv()
            blk = slot(*origin)
            fwd = pltpu.make_async_remote_copy(src_ref=blk, dst_ref=blk, send_sem=relay_send, recv_sem=relay_recv,
                                               device_id=(*to, c), device_id_type=MESH)
            fwd.start()
            fwd.wait()
            arrived.wait_send()
            unused.wait()

        _when(c == 0, lambda: relay(to_y, to_x, (x, 1 - y), (1 - x, y)))
        _when(c == 1, lambda: relay(to_x, to_y, (1 - x, y), (x, 1 - y)))
        mine.wait()
        to_sibling.wait()

    return pl.pallas_call(
        body, name=name, in_specs=[ANY], out_specs=ANY,
        out_shape=jax.ShapeDtypeStruct((N_DEV,) + shard.shape, shard.dtype),
        scratch_shapes=[pltpu.SemaphoreType.DMA((4,)), pltpu.SemaphoreType.DMA((4,)), pltpu.SemaphoreType.DMA((1,))],
    )(shard)


def _all_gather_vmem(shard, name):
    first = _gather_first([shard])

    def body(x_ref, out_ref, send, recv, local):
        sems = _Sems(send, recv, local)
        copies = first.plan([x_ref], [out_ref], sems)
        for cp in copies:
            cp.start()
        x, y, c = _position()
        passed = []
        for j, (px, py) in enumerate(_other_chips(x, y)):
            copies[2 + j].wait_recv()
            blk = out_ref.at[4 * px + 2 * py + c]
            fwd = _remote(blk, blk, sems, (x, y, 1 - c))
            fwd.start()
            passed.append(fwd)
        copies[0].wait()
        copies[1].wait()
        for cp in copies[2:]:
            cp.wait_send()
        for cp in passed:
            cp.wait()

    vm = pl.BlockSpec(memory_space=pltpu.VMEM)
    return pl.pallas_call(
        body, name=name, in_specs=[vm], out_specs=vm,
        out_shape=jax.ShapeDtypeStruct((N_DEV,) + shard.shape, shard.dtype),
        scratch_shapes=[pltpu.SemaphoreType.DMA((7,)), pltpu.SemaphoreType.DMA((7,)), pltpu.SemaphoreType.DMA((1,))],
    )(shard)


def _rs_pair_add(partial, landed, where, name):
    _, r, c = partial.shape
    tm = _tile(r, 256, 16)

    def kern(where_ref, g_ref, l_ref, own_ref, hb_ref):
        s = g_ref[...] + l_ref[...]
        hb_ref[...] = s.astype(BF16)

        @pl.when(pl.program_id(1) == where_ref[1])
        def _():
            own_ref[...] = s

    grid_spec = pltpu.PrefetchScalarGridSpec(
        num_scalar_prefetch=1, grid=(r // tm, 4),
        in_specs=[pl.BlockSpec((None, tm, c), lambda i, q, wr: (2 * q + wr[0], i, 0)),
                  pl.BlockSpec((None, tm, c), lambda i, q, wr: (q, i, 0))],
        out_specs=[pl.BlockSpec((tm, c), lambda i, q, wr: (i, 0)),
                   pl.BlockSpec((None, tm, c), lambda i, q, wr: (q, i, 0))])
    return pl.pallas_call(
        kern, name=name, grid_spec=grid_spec,
        out_shape=[jax.ShapeDtypeStruct((r, c), F32), jax.ShapeDtypeStruct((4, r, c), BF16)],
        compiler_params=_params(("parallel", "arbitrary")),
    )(where, partial, landed)


def _adam_math(g, w, m, v):
    m = ADAM_B1 * m + (1.0 - ADAM_B1) * g
    v = ADAM_B2 * v + (1.0 - ADAM_B2) * (g * g)
    m_hat = m / (1.0 - ADAM_B1 ** ADAM_STEP)
    v_hat = v / (1.0 - ADAM_B2 ** ADAM_STEP)
    delta = -ADAM_LR * (m_hat / (jnp.sqrt(v_hat) + ADAM_EPS) + ADAM_WD * w)
    return delta, m, v


def _rs_finish_adam(own, landed, w, m, v, name):
    r, c = own.shape
    tm = _tile(r, 128, 16)

    def kern(h_ref, l_ref, w_ref, m_ref, v_ref, *outs):
        g = ((h_ref[...] + l_ref[0].astype(F32)) + l_ref[1].astype(F32)) + l_ref[2].astype(F32)
        delta, m_new, v_new = _adam_math(g, w_ref[...], m_ref[...], v_ref[...])
        for ref, val in zip(outs, (g, delta, m_new, v_new)):
            ref[...] = val

    flat = pl.BlockSpec((tm, c), lambda i: (i, 0))
    return pl.pallas_call(
        kern, name=name, grid=(r // tm,), in_specs=[flat, pl.BlockSpec((3, tm, c), lambda i: (0, i, 0)), flat, flat, flat],
        out_specs=[flat] * 4, out_shape=[jax.ShapeDtypeStruct((r, c), F32)] * 4,
        compiler_params=_params(("parallel",)))(own, landed, w, m, v)


def _allreduce_adam(gathered, w, m, v, name):
    _, r, c = gathered.shape

    def kern(a_ref, w_ref, m_ref, v_ref, g_out, d_out, m_out, v_out):
        g = a_ref[0]
        for j in range(1, N_DEV):
            g = g + a_ref[j]
        delta, m_new, v_new = _adam_math(g, w_ref[...], m_ref[...], v_ref[...])
        g_out[...] = g
        d_out[...] = delta
        m_out[...] = m_new
        v_out[...] = v_new

    return pl.pallas_call(
        kern, name=name, out_shape=[jax.ShapeDtypeStruct((r, c), F32)] * 4,
        compiler_params=_params(),
    )(gathered, w, m, v)


def _cols_from_blocks(g):
    n, k, nb = g.shape
    return jnp.transpose(g, (1, 0, 2)).reshape(k, n * nb)


def _blocks_from_cols(w):
    k, n = w.shape
    return jnp.transpose(w.reshape(k, N_DEV, n // N_DEV), (1, 0, 2))


def _pack_small(parts):
    flat = []
    for p in parts:
        v = p.reshape(-1)
        flat.append(jnp.pad(v, (0, (-v.shape[0]) % (8 * LANES))))
    return jnp.concatenate(flat).reshape(-1, LANES)


def _unpack_small(buf, like):
    out, pos = [], 0
    flat = buf.reshape(-1)
    for p in like:
        size = p.size
        out.append(flat[pos:pos + size].reshape(p.shape))
        pos += size + (-size) % (8 * LANES)
    return out


def kernel(x, mem, ffn1_norm, ffn1_w_gate_up, ffn1_w_down, mix_norm, mem_norm, w_in, b_forget, pool_w, pool_scale, w_pool_up, fox_q_norm, fox_k_norm, w_fox_o, w_mem_kv, mem_q_norm, mem_k_norm, w_mem_o, w_out, ffn2_norm, ffn2_w_gate_up, ffn2_w_down, loss_target, m_ffn1_norm, m_ffn1_w_gate_up, m_ffn1_w_down, m_mix_norm, m_mem_norm, m_w_in, m_b_forget, m_pool_w, m_pool_scale, m_w_pool_up, m_fox_q_norm, m_fox_k_norm, m_w_fox_o, m_w_mem_kv, m_mem_q_norm, m_mem_k_norm, m_w_mem_o, m_w_out, m_ffn2_norm, m_ffn2_w_gate_up, m_ffn2_w_down, v_ffn1_norm, v_ffn1_w_gate_up, v_ffn1_w_down, v_mix_norm, v_mem_norm, v_w_in, v_b_forget, v_pool_w, v_pool_scale, v_w_pool_up, v_fox_q_norm, v_fox_k_norm, v_w_fox_o, v_w_mem_kv, v_mem_q_norm, v_mem_k_norm, v_w_mem_o, v_w_out, v_ffn2_norm, v_ffn2_w_gate_up, v_ffn2_w_down):
    names = ["ffn1_norm", "ffn1_w_gate_up", "ffn1_w_down", "mix_norm", "mem_norm", "w_in", "b_forget", "pool_w",
             "pool_scale", "w_pool_up", "fox_q_norm", "fox_k_norm", "w_fox_o", "w_mem_kv", "mem_q_norm",
             "mem_k_norm", "w_mem_o", "w_out", "ffn2_norm", "ffn2_w_gate_up", "ffn2_w_down"]
    w_args = [ffn1_norm, ffn1_w_gate_up, ffn1_w_down, mix_norm, mem_norm, w_in, b_forget, pool_w, pool_scale,
              w_pool_up, fox_q_norm, fox_k_norm, w_fox_o, w_mem_kv, mem_q_norm, mem_k_norm, w_mem_o, w_out,
              ffn2_norm, ffn2_w_gate_up, ffn2_w_down]
    m_args = [m_ffn1_norm, m_ffn1_w_gate_up, m_ffn1_w_down, m_mix_norm, m_mem_norm, m_w_in, m_b_forget, m_pool_w,
              m_pool_scale, m_w_pool_up, m_fox_q_norm, m_fox_k_norm, m_w_fox_o, m_w_mem_kv, m_mem_q_norm,
              m_mem_k_norm, m_w_mem_o, m_w_out, m_ffn2_norm, m_ffn2_w_gate_up, m_ffn2_w_down]
    v_args = [v_ffn1_norm, v_ffn1_w_gate_up, v_ffn1_w_down, v_mix_norm, v_mem_norm, v_w_in, v_b_forget, v_pool_w,
              v_pool_scale, v_w_pool_up, v_fox_q_norm, v_fox_k_norm, v_w_fox_o, v_w_mem_kv, v_mem_q_norm,
              v_mem_k_norm, v_w_mem_o, v_w_out, v_ffn2_norm, v_ffn2_w_gate_up, v_ffn2_w_down]
    W = dict(zip(names, w_args))
    M = dict(zip(names, m_args))
    V = dict(zip(names, v_args))

    batch, seq, d = x.shape
    mlen = mem.shape[1]
    t = batch * seq
    gate_w = 3 * d
    z_gate, z_q = 0, gate_w
    z_k, z_v = z_q + FOX_WIDTH, z_q + 2 * FOX_WIDTH
    z_u = z_q + 3 * FOX_WIDTH
    z_qm = z_u + POOL_WIDTH
    z_f = z_qm + MEM_WIDTH
    z_width = -(-(z_f + F_PAD) // 512) * 512

    x2d = x.reshape(t, d)
    mem2d = mem.reshape(batch * mlen, d)
    tgt2d = loss_target.reshape(t, d)

    big = ["ffn1_w_gate_up", "ffn1_w_down", "w_in", "w_pool_up", "w_fox_o", "w_mem_kv", "w_mem_o", "w_out",
           "ffn2_w_gate_up", "ffn2_w_down"]
    col_sharded = {"ffn1_w_gate_up", "ffn2_w_gate_up", "w_in", "w_pool_up", "w_fox_o", "w_mem_o"}
    mixer_small = ["w_pool_up", "w_fox_o", "w_mem_kv", "w_mem_o", "w_out"]
    shard = {n: W[n][0].astype(BF16) for n in big}

    def rows_of(g):
        return g.reshape(-1, g.shape[2])

    gu_caps_nt, gu_caps_tn = (1024, 2048, 1408), (1024, 1408, 2048)
    down_caps, dwd_caps = (1024, 1024, 2816), (1408, 1024, 2048)

    wgu1 = _gather_first_relayed(shard["ffn1_w_gate_up"], "ag_first")
    h1, (wgu1,) = _rms_fwd(x2d, W["ffn1_norm"], "ffn1_rms", comm=_gather_second([wgu1]))
    w_in_top, w_in_bot = shard["w_in"][:d // 2], shard["w_in"][d // 2:]
    (gu1, a1), bufs = _gate_up_swiglu(h1, wgu1, "ffn1_gu", comm=_gather_first([shard["ffn1_w_down"], w_in_top]))
    wd1_g, w_in_top_g = _comm_call(_gather_second(bufs), "ag_second")
    wd1 = rows_of(wd1_g)
    x1, (w_in_bot_g,) = _mm(a1, wd1, "nn", "ffn1_down", scale=0.5, res=x2d, caps=down_caps,
                            comm=_gather_first([w_in_bot]))
    h2, (w_in_bot_g,) = _rms_fwd(x1, W["mix_norm"], "mix_rms", comm=_gather_second([w_in_bot_g]))

    o_u, o_q, o_v = 0, POOL_WIDTH, POOL_WIDTH + 2 * FOX_WIDTH
    o_f = o_v + FOX_WIDTH
    o_qm = o_f + FOX_HEADS
    o_g = o_qm + MEM_WIDTH
    wi = jnp.concatenate([_cols_from_blocks(w_in_top_g), _cols_from_blocks(w_in_bot_g)], axis=0)
    w_in_pad = jnp.concatenate(
        [wi[:, o_g:o_g + gate_w], wi[:, o_q:o_q + 3 * FOX_WIDTH], wi[:, o_u:o_u + POOL_WIDTH],
         wi[:, o_qm:o_qm + MEM_WIDTH], wi[:, o_f:o_f + FOX_HEADS],
         jnp.zeros((d, z_width - z_f - FOX_HEADS), BF16)], axis=1)
    group_b = mixer_small + ["ffn2_w_down"]
    z, bufs = _mm(h2, w_in_pad, "nn", "mix_in", caps=(1024, 1536, 2048), comm=_gather_first([shard[n] for n in group_b]))

    pool_w_b = W["pool_w"][0].astype(BF16)

    def pool_fwd_body(u, pw, ps):
        row = lax.broadcasted_iota(jnp.int32, (seq, 1), 0)
        diffs, mixed = [], []
        for g in range(POOL_GROUPS):
            ug = u[:, g * POOL_GROUP_DIM:(g + 1) * POOL_GROUP_DIM]
            cnt = jnp.minimum(row + 1, POOL_WINDOWS[g]).astype(F32)
            diff = _window_sum(ug, g + 1, True) / cnt - ug
            diffs.append(diff)
            mixed.append(lax.dot_general(diff.astype(BF16), pw[g], _DIMS["nn"], preferred_element_type=F32))
        diffs = jnp.concatenate(diffs, axis=1)
        mixed = jnp.concatenate(mixed, axis=1)
        return mixed * ps, diffs, mixed

    ypp, pool_diff, pool_mixed = _rowwise(
        pool_fwd_body, [(z, z_u, POOL_WIDTH)], [pool_w_b, W["pool_scale"]],
        [(POOL_WIDTH, BF16), (POOL_WIDTH, BF16), (POOL_WIDTH, F32)], [], "pool_fwd", seq)

    gq = jnp.tile(W["fox_q_norm"], (1, FOX_HEADS))
    gk = jnp.tile(W["fox_k_norm"], (1, FOX_HEADS))
    b_pad = jnp.pad(W["b_forget"], ((0, 0), (0, F_PAD - FOX_HEADS)))

    def fox_prep_body(q, k, v, f, gqv, gkv, bv):
        qn_scaled = _head_norm(q, gqv, FOX_HEAD_DIM).astype(BF16) * FOX_HEAD_DIM ** -0.5
        return qn_scaled, _head_norm(k, gkv, FOX_HEAD_DIM), v, _log_sigmoid(f + bv)

    tm_e = _tile(t, 256, 16)
    (qn, kn, vb, logf), bufs = _rowwise(
        fox_prep_body, [(z, z_q, FOX_WIDTH), (z, z_k, FOX_WIDTH), (z, z_v, FOX_WIDTH), (z, z_f, F_PAD)],
        [gq, gk, b_pad], [(FOX_WIDTH, BF16), (FOX_WIDTH, BF16), (FOX_WIDTH, BF16), (F_PAD, F32)], [], "fox_prep", tm_e,
        comm=_gather_second(bufs))
    full = {n: (_cols_from_blocks(g) if n in col_sharded else rows_of(g)) for n, g in zip(group_b, bufs)}
    wd2 = full["ffn2_w_down"]
    y_pool = _mm(ypp, full["w_pool_up"], "nn", "pool_up")
    csum = _rowwise(lambda v: (_scan_rows(v, True),), [_whole(logf)], [], [(F_PAD, F32)], [], "fox_cumsum", seq)[0]
    cks = jnp.transpose(csum.reshape(batch, seq, F_PAD)[:, :, :FOX_HEADS], (0, 2, 1)).reshape(
        batch * FOX_HEADS // 2, 2, seq)
    (o_fox, lse), (wgu2,) = _fox_fwd(qn, kn, vb, cks, batch, seq, "fox_fwd",
                                     comm=_gather_first([shard["ffn2_w_gate_up"]]))
    y_fox = _mm(o_fox, full["w_fox_o"], "nn", "fox_o")

    memn = _rms_fwd(mem2d, W["mem_norm"], "mem_rms")
    kv = _mm(memn, full["w_mem_kv"], "nn", "mem_kv")
    gqm = jnp.tile(W["mem_q_norm"], (1, MEM_HEADS))
    gkm = jnp.tile(W["mem_k_norm"], (1, MEM_HEADS))
    qmn = _rowwise(lambda q, g: (_head_norm(q, g, MEM_HEAD_DIM),), [(z, z_qm, MEM_WIDTH)], [gqm],
                   [(MEM_WIDTH, BF16)], [], "memq_prep", tm_e)[0]
    kmn, vmb = _rowwise(lambda k, v, g: (_head_norm(k, g, MEM_HEAD_DIM), v),
                        [(kv, 0, MEM_WIDTH), (kv, MEM_WIDTH, MEM_WIDTH)], [gkm],
                        [(MEM_WIDTH, BF16), (MEM_WIDTH, BF16)], [], "memk_prep", _tile(batch * mlen, 256, 16))
    o_mem = _mem_fwd(qmn, kmn, vmb, batch, seq, mlen, "mem_fwd")
    y_mem = _mm(o_mem, full["w_mem_o"], "nn", "mem_o")

    def gate_fwd_body(gp, gf, gm, yp, yf, ym):
        return ((_sigmoid(gp) * yp + _sigmoid(gf) * yf) + _sigmoid(gm) * ym,)

    tm_g = _tile(t, 128, 16)
    (merged,), (wgu2,) = _rowwise(
        gate_fwd_body, [(z, 0, d), (z, d, d), (z, 2 * d, d), _whole(y_pool), _whole(y_fox), _whole(y_mem)],
        [], [(d, BF16)], [], "gate_fwd", tm_g, comm=_gather_second([wgu2]))
    x2 = _mm(merged, full["w_out"], "nn", "mix_out", res=x1)

    h3 = _rms_fwd(x2, W["ffn2_norm"], "ffn2_rms")
    gu2, a2 = _gate_up_swiglu(h3, wgu2, "ffn2_gu")
    x3 = _mm(a2, wd2, "nn", "ffn2_down", scale=0.5, res=x2, caps=down_caps)
    dy, dy_b, loss_part = _loss_head(x3, tgt2d, "loss")
    loss = lax.psum(loss_part[0, 0], ("x", "y", "c"))

    cx, cy, cc = _position()
    where = jnp.stack([cc, 2 * cx + cy]).astype(jnp.int32)
    G, own, landed = {}, {}, {}

    def row_blocks(g):
        return g.reshape(N_DEV, g.shape[0] // N_DEV, g.shape[1])

    def pair_add(n, partial, from_core):
        own[n], chip_sums = _rs_pair_add(partial, from_core, where, "rs_add_" + n)
        return chip_sums

    p_wd2 = row_blocks(_mm(a2, dy_b, "tn", "ffn2_dwd", scale=0.5, caps=dwd_caps))
    dgu2, (l_wd2,) = _swiglu_bwd_from_out(dy_b, wd2, gu2, "ffn2_dgu", comm=_scatter_first([p_wd2]))
    s_wd2 = pair_add("ffn2_w_down", p_wd2, l_wd2)
    p_wgu2, (landed["ffn2_w_down"],) = _mm(h3, dgu2, "tn", "ffn2_dwgu", caps=gu_caps_tn, out_blocks=True, halves="b",
                                           comm=_scatter_second([s_wd2]))
    dh3, (l_wgu2,) = _mm(dgu2, wgu2, "nt", "ffn2_dh", caps=gu_caps_nt, b_blocks=True, halves="a",
                         comm=_scatter_first([p_wgu2]))
    s_wgu2 = pair_add("ffn2_w_gate_up", p_wgu2, l_wgu2)
    dx2, dx2_b, G["ffn2_norm"] = _rms_bwd(x2, W["ffn2_norm"], dh3, dy, "ffn2_drms")

    dmerged = _mm(dx2_b, full["w_out"], "nt", "mix_out_dx")
    P = {"w_out": row_blocks(_mm(merged, dx2_b, "tn", "mix_out_dw"))}

    def gate_bwd_body(gp, gf, gm, yp, yf, ym, dm):
        outs_dl, outs_dy = [], []
        for gl, yv in ((gp, yp), (gf, yf), (gm, ym)):
            s = _sigmoid(gl)
            outs_dl.append((dm * yv) * (s * (1.0 - s)))
            outs_dy.append(dm * s)
        return (jnp.concatenate(outs_dl, axis=1), *outs_dy)

    dz, dy_pool, dy_fox, dy_mem = _rowwise(
        gate_bwd_body, [(z, 0, d), (z, d, d), (z, 2 * d, d), _whole(y_pool), _whole(y_fox), _whole(y_mem), _whole(dmerged)],
        [], [(gate_w, BF16), (d, BF16), (d, BF16), (d, BF16)], [], "gate_bwd", tm_g, into=(None, z_width, z_gate))

    dypp = _mm(dy_pool, full["w_pool_up"], "nt", "pool_up_dx")
    P["w_pool_up"] = _blocks_from_cols(_mm(ypp, dy_pool, "tn", "pool_up_dw"))

    def pool_bwd_body(dyv, mixed, diff, pw, ps):
        row = lax.broadcasted_iota(jnp.int32, (seq, 1), 0)
        d_scale = jnp.sum(dyv * mixed, axis=0, keepdims=True)
        dmix = (dyv * ps).astype(BF16)
        du, dpw = [], []
        for g in range(POOL_GROUPS):
            sl = slice(g * POOL_GROUP_DIM, (g + 1) * POOL_GROUP_DIM)
            dmg = dmix[:, sl]
            ddiff = lax.dot_general(dmg, pw[g], _DIMS["nt"], preferred_element_type=F32)
            dpw.append(lax.dot_general(diff[:, sl], dmg, _DIMS["tn"], preferred_element_type=F32))
            cnt = jnp.minimum(row + 1, POOL_WINDOWS[g]).astype(F32)
            du.append(_window_sum(ddiff / cnt, g + 1, False) - ddiff)
        return jnp.concatenate(du, axis=1), d_scale, jnp.concatenate(dpw, axis=0)

    dz, G["pool_scale"], d_pool_w = _rowwise(
        pool_bwd_body, [_whole(dypp), _whole(pool_mixed), _whole(pool_diff)], [pool_w_b, W["pool_scale"]],
        [(POOL_WIDTH, BF16)], [((1, POOL_WIDTH), F32), ((POOL_WIDTH, POOL_GROUP_DIM), F32)],
        "pool_bwd", seq, into=(dz, z_width, z_u))
    G["pool_w"] = d_pool_w.reshape(1, POOL_GROUPS, POOL_GROUP_DIM, POOL_GROUP_DIM)

    do_fox = _mm(dy_fox, full["w_fox_o"], "nt", "fox_o_dx")
    P["w_fox_o"] = _blocks_from_cols(_mm(o_fox, dy_fox, "tn", "fox_o_dw"))
    (dkn, dz, dqn, dcks, dcq), (landed["ffn2_w_gate_up"],) = _fox_bwd(
        qn, kn, vb, cks, o_fox, lse, do_fox, dz, z_v, batch, seq, "fox_bwd", comm=_scatter_second([s_wgu2]))
    dcs = jnp.transpose(dcks.reshape(batch, FOX_HEADS, seq), (0, 2, 1)).reshape(t, FOX_HEADS)
    dcs = jnp.pad(dcs, ((0, 0), (0, F_PAD - FOX_HEADS)))
    dcq = jnp.pad(dcq.reshape(t, FOX_HEADS, FOX_HEAD_DIM)[:, :, 0], ((0, 0), (0, F_PAD - FOX_HEADS)))

    def fox_f_bwd_body(dc, dc_rows, f, bv):
        lane = lax.broadcasted_iota(jnp.int32, (1, F_PAD), 1)
        dlogf = _scan_rows(dc + dc_rows, False)
        df = jnp.where(lane < FOX_HEADS, dlogf * _sigmoid(-(f + bv)), 0.0)
        behind = jnp.zeros((seq, z_width - z_f - F_PAD), F32)
        return jnp.concatenate([df, behind], axis=1), jnp.sum(df, axis=0, keepdims=True)

    dz, db_pad = _rowwise(fox_f_bwd_body, [_whole(dcs), _whole(dcq), (z, z_f, F_PAD)], [b_pad],
                          [(z_width - z_f, BF16)], [((1, F_PAD), F32)], "fox_f_bwd", seq, into=(dz, z_width, z_f))
    G["b_forget"] = db_pad[:, :FOX_HEADS]

    def fox_head_bwd(dz_in, col, dyn, gain, name):
        return _rowwise(lambda v, dyv, g: _head_norm_bwd(v, g, dyv, FOX_HEAD_DIM), [(z, col, FOX_WIDTH), _whole(dyn)],
                        [gain], [(FOX_WIDTH, BF16)], [((1, FOX_WIDTH), F32)], name, tm_e, into=(dz_in, z_width, col))

    dz, dgq_t = fox_head_bwd(dz, z_q, dqn, gq, "fox_q_bwd")
    dz, dgk_t = fox_head_bwd(dz, z_k, dkn, gk, "fox_k_bwd")
    G["fox_q_norm"] = jnp.sum(dgq_t.reshape(FOX_HEADS, FOX_HEAD_DIM), axis=0, keepdims=True)
    G["fox_k_norm"] = jnp.sum(dgk_t.reshape(FOX_HEADS, FOX_HEAD_DIM), axis=0, keepdims=True)

    do_mem = _mm(dy_mem, full["w_mem_o"], "nt", "mem_o_dx")
    P["w_mem_o"] = _blocks_from_cols(_mm(o_mem, dy_mem, "tn", "mem_o_dw"))
    dqmn, dkmn, dvm = _mem_bwd(qmn, kmn, vmb, do_mem, batch, seq, mlen, "mem_bwd")
    dz, dgqm_t = _rowwise(lambda q, dq, g: _head_norm_bwd(q, g, dq, MEM_HEAD_DIM),
                          [(z, z_qm, MEM_WIDTH), _whole(dqmn)], [gqm], [(MEM_WIDTH, BF16)],
                          [((1, MEM_WIDTH), F32)], "memq_bwd", tm_e, into=(dz, z_width, z_qm))

    def memk_bwd_body(k, dk, dv, g):
        dkr, dg = _head_norm_bwd(k, g, dk, MEM_HEAD_DIM)
        return jnp.concatenate([dkr, dv], axis=1), dg

    dkv, dgkm_t = _rowwise(memk_bwd_body, [(kv, 0, MEM_WIDTH), _whole(dkmn), _whole(dvm)], [gkm],
                           [(2 * MEM_WIDTH, BF16)], [((1, MEM_WIDTH), F32)], "memk_bwd", _tile(batch * mlen, 256, 16))
    G["mem_q_norm"] = jnp.sum(dgqm_t.reshape(MEM_HEADS, MEM_HEAD_DIM), axis=0, keepdims=True)
    G["mem_k_norm"] = jnp.sum(dgkm_t.reshape(MEM_HEADS, MEM_HEAD_DIM), axis=0, keepdims=True)
    P["w_mem_kv"] = row_blocks(_mm(memn, dkv, "tn", "mem_kv_dw"))
    dmemn = _mm(dkv, full["w_mem_kv"], "nt", "mem_kv_dx")
    _, _, G["mem_norm"] = _rms_bwd(mem2d, W["mem_norm"], dmemn, None, "mem_drms")

    d_w_in_pad, l_small = _mm(h2, dz, "tn", "mix_in_dw", caps=(1024, 1536, 2048), comm=_scatter_first([P[n] for n in mixer_small]))
    s_small = [pair_add(n, P[n], l) for n, l in zip(mixer_small, l_small)]
    p = d_w_in_pad
    p_w_in = _blocks_from_cols(jnp.concatenate(
        [p[:, z_u:z_u + POOL_WIDTH], p[:, z_q:z_q + 3 * FOX_WIDTH], p[:, z_f:z_f + FOX_HEADS],
         p[:, z_qm:z_qm + MEM_WIDTH], p[:, z_gate:z_gate + gate_w]], axis=1))
    dh2, rest = _mm(dz, w_in_pad, "nt", "mix_in_dx", caps=(1024, 2048, 1792),
                    comm=_join(_scatter_second(s_small), _scatter_first([p_w_in])))
    for n, l in zip(mixer_small, rest[:len(mixer_small)]):
        landed[n] = l
    s_w_in = pair_add("w_in", p_w_in, rest[-1])
    dx1, dx1_b, G["mix_norm"] = _rms_bwd(x1, W["mix_norm"], dh2, dx2, "mix_drms")

    dgu1 = _swiglu_bwd_from_out(dx1_b, wd1, gu1, "ffn1_dgu")
    p_wgu1, (landed["w_in"],) = _mm(h1, dgu1, "tn", "ffn1_dwgu", caps=gu_caps_tn, out_blocks=True, halves="b",
                                    comm=_scatter_second([s_w_in]))
    d_wd1, (l_wgu1,) = _mm(a1, dx1_b, "tn", "ffn1_dwd", scale=0.5, caps=dwd_caps, comm=_scatter_first([p_wgu1]))
    p_wd1 = row_blocks(d_wd1)
    s_wgu1 = pair_add("ffn1_w_gate_up", p_wgu1, l_wgu1)
    dh1, (landed["ffn1_w_gate_up"], l_wd1) = _mm(dgu1, wgu1, "nt", "ffn1_dh", caps=gu_caps_nt, b_blocks=True, halves="a",
                                                 comm=_join(_scatter_second([s_wgu1]), _scatter_first([p_wd1])))
    s_wd1 = pair_add("ffn1_w_down", p_wd1, l_wd1)
    (dx0, _, G["ffn1_norm"]), (landed["ffn1_w_down"],) = _rms_bwd(x2d, W["ffn1_norm"], dh1, dx1, "ffn1_drms",
                                                                  comm=_scatter_second([s_wd1]))
    grad_x = dx0.reshape(batch, seq, d)

    out_g, out_d, out_m, out_v = {}, {}, {}, {}
    for n in big:
        res = _rs_finish_adam(own[n], landed[n], W[n][0], M[n][0], V[n][0], "adam_" + n)
        out_g[n], out_d[n], out_m[n], out_v[n] = [r[None] for r in res]

    small = [n for n in names if n not in big]
    g_small = _pack_small([G[n].reshape(W[n].shape) for n in small])
    all_small = _all_gather_vmem(g_small, "ag_small_grads")
    res = _allreduce_adam(all_small, _pack_small([W[n] for n in small]), _pack_small([M[n] for n in small]),
                          _pack_small([V[n] for n in small]), "adam_small")
    like = [W[n] for n in small]
    for dst, buf in zip((out_g, out_d, out_m, out_v), res):
        for n, a in zip(small, _unpack_small(buf, like)):
            dst[n] = a

    return (loss, grad_x, *[out_g[n] for n in names], *[out_d[n] for n in names],
            *[out_m[n] for n in names], *[out_v[n] for n in names])
```

```python
import functools

import jax
import jax.numpy as jnp
from jax import lax
from jax.experimental import pallas as pl
from jax.experimental.pallas import tpu as pltpu

F32 = jnp.float32
BF16 = jnp.bfloat16
MESH = pl.DeviceIdType.MESH

N_DEV = 8
EPS = 1e-6
FOX_HEADS = 16
FOX_HEAD_DIM = 64
FOX_WIDTH = FOX_HEADS * FOX_HEAD_DIM
MEM_HEADS = 4
MEM_HEAD_DIM = 128
MEM_WIDTH = MEM_HEADS * MEM_HEAD_DIM
POOL_GROUPS = 4
POOL_GROUP_DIM = 128
POOL_WIDTH = POOL_GROUPS * POOL_GROUP_DIM
POOL_WINDOWS = (2, 4, 8, 16)
LANES = 128
F_PAD = LANES

ADAM_LR = 0.001
ADAM_B1 = 0.9
ADAM_B2 = 0.999
ADAM_EPS = 1e-08
ADAM_WD = 0.01
ADAM_STEP = 10

VMEM_LIMIT = 56 * 1024 * 1024
STREAM_BUFFERS = 3
NEG = -1e30

ANY = pl.BlockSpec(memory_space=pl.ANY)


def _params(sem=None):
    return pltpu.CompilerParams(dimension_semantics=sem, vmem_limit_bytes=VMEM_LIMIT)


MXU_DIM = 256


def _tile(dim, cap, align):
    best = None
    t = align
    while t <= min(dim, cap):
        if dim % t == 0:
            best = t
        t += align
    return dim if best is None else best


def _mxu_tile(dim, cap):
    wide, fine = _tile(dim, cap, MXU_DIM), _tile(dim, cap, LANES)
    whole_widths = wide % MXU_DIM == 0 and wide <= cap
    return wide if whole_widths and fine < 2 * wide else fine


class _Sems:
    def __init__(self, send, recv, local):
        self.send, self.recv, self.local = send, recv, local
        self.n_remote = self.n_local = 0

    def remote(self):
        i = self.n_remote
        self.n_remote += 1
        return self.send.at[i], self.recv.at[i]

    def one_local(self):
        i = self.n_local
        self.n_local += 1
        return self.local.at[i]


class _Comm:
    def __init__(self, srcs, outs, n_remote, n_local, plan, in_place=False):
        self.srcs, self.outs, self.n_remote, self.n_local = list(srcs), list(outs), n_remote, n_local
        self.plan, self.in_place = plan, in_place

    def aliases(self, n_in, n_out):
        return {n_in + i: n_out + i for i in range(len(self.srcs))} if self.in_place else {}

    def scratch(self):
        return [pltpu.SemaphoreType.DMA((self.n_remote,)), pltpu.SemaphoreType.DMA((self.n_remote,)),
                pltpu.SemaphoreType.DMA((max(self.n_local, 1),))]

    def run(self, src_refs, out_refs, sem_refs, first, last):
        def copies():
            return self.plan(list(src_refs), list(out_refs), _Sems(*sem_refs))

        return (lambda: _when(first, lambda: [cp.start() for cp in copies()]),
                lambda: _when(last, lambda: [cp.wait() for cp in copies()]))


def _when(cond, fn):
    @pl.when(cond)
    def _():
        fn()


def _join(*comms):
    comms = [c for c in comms if c is not None]
    assert all(not c.in_place for c in comms)

    def plan(src_refs, out_refs, sems):
        copies, si, oi = [], 0, 0
        for c in comms:
            copies += c.plan(src_refs[si:si + len(c.srcs)], out_refs[oi:oi + len(c.outs)], sems)
            si += len(c.srcs)
            oi += len(c.outs)
        return copies

    return _Comm(sum((c.srcs for c in comms), []), sum((c.outs for c in comms), []),
                 sum(c.n_remote for c in comms), sum(c.n_local for c in comms), plan)


def _split_refs(refs, n_in, n_out, comm):
    if comm is None:
        return refs[:n_in], refs[n_in:n_in + n_out], refs[n_in + n_out:], (), (), ()
    ns, no = len(comm.srcs), len(comm.outs)
    ins = refs[:n_in]
    srcs = refs[n_in:n_in + ns]
    outs = refs[n_in + ns:n_in + ns + n_out]
    couts = refs[n_in + ns + n_out:n_in + ns + n_out + no]
    rest = refs[n_in + ns + n_out + no:]
    return ins, outs, rest[:-3], srcs, couts, rest[-3:]


def _sigmoid(x):
    return 1.0 / (1.0 + jnp.exp(-x))


_DIMS = {"nn": (((1,), (0,)), ((), ())), "nt": (((1,), (1,)), ((), ())), "tn": (((0,), (0,)), ((), ()))}


def _mm(a, b, mode, name, out_dtype=F32, scale=1.0, res=None, caps=None, b_blocks=False, out_blocks=False,
        halves=None, comm=None):
    nblk = N_DEV
    a_shape, b_shape = a.shape, b.shape
    if b_blocks:
        _, r0, c0 = b.shape
        b_shape = (r0, nblk * c0)
    if halves == "a":
        assert mode == "nt"
        a_shape = (a.shape[1], 2 * a.shape[2])
    elif halves == "b":
        assert mode == "tn"
        b_shape = (b.shape[1], 2 * b.shape[2])
    if mode == "nn":
        (m, k), (k2, n) = a_shape, b_shape
    elif mode == "nt":
        (m, k), (n, k2) = a_shape, b_shape
    else:
        (k, m), (k2, n) = a_shape, b_shape
    assert k == k2, (name, a.shape, b.shape)
    cm, cn, ck = caps or ((1024, 512, 2048) if k <= 2048 else (1024, 1024, 2048))
    n_unit = n // nblk if (out_blocks or (b_blocks and mode == "nn")) else n
    k_unit = k // nblk if (b_blocks and mode == "nt") else k
    tm, tn, tk = _mxu_tile(m, cm), _mxu_tile(n_unit, cn), _mxu_tile(k_unit, ck)
    nk = k // tk
    npb, kpb = n_unit // tn, k_unit // tk
    n_half, k_half = (n // 2) // tn, (k // 2) // tk
    if mode == "nn":
        a_spec = pl.BlockSpec((tm, tk), lambda i, j, kk: (i, kk))
        b_spec = pl.BlockSpec((tk, tn), lambda i, j, kk: (kk, j))
        if b_blocks:
            b_spec = pl.BlockSpec((None, tk, tn), lambda i, j, kk: (j // npb, kk, j % npb))
    elif mode == "nt":
        a_spec = pl.BlockSpec((tm, tk), lambda i, j, kk: (i, kk))
        b_spec = pl.BlockSpec((tn, tk), lambda i, j, kk: (j, kk))
        if b_blocks:
            b_spec = pl.BlockSpec((None, tn, tk), lambda i, j, kk: (kk // kpb, j, kk % kpb))
        if halves == "a":
            assert (k // 2) % tk == 0
            a_spec = pl.BlockSpec((None, tm, tk), lambda i, j, kk: (kk // k_half, i, kk % k_half))
    else:
        assert not b_blocks
        a_spec = pl.BlockSpec((tk, tm), lambda i, j, kk: (kk, i))
        b_spec = pl.BlockSpec((tk, tn), lambda i, j, kk: (kk, j))
        if halves == "b":
            assert (n // 2) % tn == 0
            b_spec = pl.BlockSpec((None, tk, tn), lambda i, j, kk: (j // n_half, kk, j % n_half))
    if out_blocks:
        assert res is None
        o_spec = pl.BlockSpec((None, tm, tn), lambda i, j, kk: (j // npb, i, j % npb))
        o_shape = jax.ShapeDtypeStruct((nblk, m, n // nblk), out_dtype)
    else:
        o_spec = pl.BlockSpec((tm, tn), lambda i, j, kk: (i, j))
        o_shape = jax.ShapeDtypeStruct((m, n), out_dtype)
    in_specs = [a_spec, b_spec] + ([o_spec] if res is not None else [])
    n_in = len(in_specs)
    dims = _DIMS[mode]
    gm, gn = m // tm, n // tn

    def kern(*refs):
        ins, outs, scratch, c_src, c_out, c_sem = _split_refs(refs, n_in, 1, comm)
        a_ref, b_ref = ins[0], ins[1]
        res_ref = ins[2] if res is not None else None
        o_ref = outs[0]
        if comm is not None:
            i, j, kq = pl.program_id(0), pl.program_id(1), pl.program_id(2)
            first = jnp.logical_and(jnp.logical_and(i == 0, j == 0), kq == 0)
            last = jnp.logical_and(jnp.logical_and(i == gm - 1, j == gn - 1), kq == nk - 1)
            start_comm, wait_comm = comm.run(c_src, c_out, c_sem, first, last)
            start_comm()
        a_tile = a_ref[...].astype(BF16)
        if scale != 1.0:
            a_tile = a_tile * scale

        def product():
            return lax.dot_general(a_tile, b_ref[...].astype(BF16), dims, preferred_element_type=F32)

        if nk == 1:
            r = product()
            if res_ref is not None:
                r = res_ref[...] + r
            o_ref[...] = r.astype(out_dtype)
        else:
            acc_ref = scratch[0] if scratch else o_ref
            kk = pl.program_id(2)

            @pl.when(kk == 0)
            def _():
                acc_ref[...] = jnp.zeros_like(acc_ref) if res_ref is None else res_ref[...]

            acc_ref[...] += product()
            if scratch:
                @pl.when(kk == nk - 1)
                def _():
                    o_ref[...] = acc_ref[...].astype(out_dtype)
        if comm is not None:
            wait_comm()

    assert scale in (1.0, 0.5)
    args = (a, b) + ((res,) if res is not None else ())
    scratch_shapes = [pltpu.VMEM((tm, tn), F32)] if (nk > 1 and out_dtype != F32) else []
    if comm is None:
        return pl.pallas_call(
            kern, name=name, grid=(gm, gn, nk), in_specs=in_specs, out_specs=o_spec, out_shape=o_shape,
            scratch_shapes=scratch_shapes, compiler_params=_params(("parallel", "parallel", "arbitrary")),
        )(*args)
    res_all = pl.pallas_call(
        kern, name=name, grid=(gm, gn, nk), in_specs=in_specs + [ANY] * len(comm.srcs),
        out_specs=[o_spec] + [ANY] * len(comm.outs), out_shape=[o_shape] + comm.outs,
        scratch_shapes=scratch_shapes + comm.scratch(), input_output_aliases=comm.aliases(n_in, 1),
        compiler_params=_params(("arbitrary", "arbitrary", "arbitrary")),
    )(*args, *comm.srcs)
    return res_all[0], list(res_all[1:])


def _gate_up_swiglu(h, w_blocks, name, comm=None):
    t, d = h.shape
    nblk, _, nb = w_blocks.shape
    half = nblk // 2
    tm = _mxu_tile(t, 512)
    steps = t // tm

    def kern(*refs):
        (h_ref, wg_ref, wu_ref), (gu_ref, a_ref), _, c_src, c_out, c_sem = _split_refs(refs, 3, 2, comm)
        if comm is not None:
            j, i = pl.program_id(0), pl.program_id(1)
            start_comm, wait_comm = comm.run(c_src, c_out, c_sem, jnp.logical_and(j == 0, i == 0),
                                             jnp.logical_and(j == half - 1, i == steps - 1))
            start_comm()
        hv = h_ref[...]
        g = lax.dot_general(hv, wg_ref[...], _DIMS["nn"], preferred_element_type=F32)
        u = lax.dot_general(hv, wu_ref[...], _DIMS["nn"], preferred_element_type=F32)
        gu_ref[0] = g
        gu_ref[1] = u
        a_ref[...] = ((g * _sigmoid(g)) * u).astype(BF16)
        if comm is not None:
            wait_comm()

    in_specs = [pl.BlockSpec((tm, d), lambda j, i: (i, 0)), pl.BlockSpec((None, d, nb), lambda j, i: (j, 0, 0)),
                pl.BlockSpec((None, d, nb), lambda j, i: (j + half, 0, 0))]
    out_specs = [pl.BlockSpec((2, tm, nb), lambda j, i: (0, i, j)), pl.BlockSpec((tm, nb), lambda j, i: (i, j))]
    out_shape = [jax.ShapeDtypeStruct((2, t, half * nb), F32), jax.ShapeDtypeStruct((t, half * nb), BF16)]
    if comm is None:
        return pl.pallas_call(kern, name=name, grid=(half, steps), in_specs=in_specs, out_specs=out_specs,
                              out_shape=out_shape, compiler_params=_params(("parallel", "parallel")))(h, w_blocks, w_blocks)
    res_all = pl.pallas_call(
        kern, name=name, grid=(half, steps), in_specs=in_specs + [ANY] * len(comm.srcs),
        out_specs=out_specs + [ANY] * len(comm.outs), out_shape=out_shape + comm.outs,
        scratch_shapes=comm.scratch(), compiler_params=_params(("arbitrary", "arbitrary")),
    )(h, w_blocks, w_blocks, *comm.srcs)
    return list(res_all[:2]), list(res_all[2:])


def _swiglu_bwd_from_out(dy, wd, gu, name, comm=None):
    t, d = dy.shape
    f = wd.shape[0]
    tm, tn = _mxu_tile(t, 1024), _mxu_tile(f, 512)
    gi, gj = t // tm, f // tn

    def kern(*refs):
        (dy_ref, wd_ref, gu_ref), (out_ref,), _, c_src, c_out, c_sem = _split_refs(refs, 3, 1, comm)
        if comm is not None:
            i, j = pl.program_id(0), pl.program_id(1)
            start_comm, wait_comm = comm.run(c_src, c_out, c_sem, jnp.logical_and(i == 0, j == 0),
                                             jnp.logical_and(i == gi - 1, j == gj - 1))
            start_comm()
        da = lax.dot_general(dy_ref[...] * 0.5, wd_ref[...], _DIMS["nt"], preferred_element_type=F32)
        g, u = gu_ref[0], gu_ref[1]
        s = _sigmoid(g)
        out_ref[0] = (da * u * (s * (1.0 + g * (1.0 - s)))).astype(BF16)
        out_ref[1] = (da * (g * s)).astype(BF16)
        if comm is not None:
            wait_comm()

    pair = pl.BlockSpec((2, tm, tn), lambda i, j: (0, i, j))
    in_specs = [pl.BlockSpec((tm, d), lambda i, j: (i, 0)), pl.BlockSpec((tn, d), lambda i, j: (j, 0)), pair]
    out_shape = jax.ShapeDtypeStruct((2, t, f), BF16)
    if comm is None:
        return pl.pallas_call(kern, name=name, grid=(gi, gj), in_specs=in_specs, out_specs=pair, out_shape=out_shape,
                              compiler_params=_params(("parallel", "parallel")))(dy, wd, gu)
    res_all = pl.pallas_call(
        kern, name=name, grid=(gi, gj), in_specs=in_specs + [ANY] * len(comm.srcs),
        out_specs=[pair] + [ANY] * len(comm.outs), out_shape=[out_shape] + comm.outs,
        scratch_shapes=comm.scratch(), compiler_params=_params(("arbitrary", "arbitrary")),
    )(dy, wd, gu, *comm.srcs)
    return res_all[0], list(res_all[1:])


def _rowwise(body, ins, params, outs, accs, name, tm, comm=None, into=None):
    rows = ins[0][0].shape[0]
    assert rows % tm == 0, (name, rows, tm)
    in_specs = []
    for arr, off, width in ins:
        assert off % width == 0 and arr.shape[0] == rows, (name, arr.shape, off, width)
        in_specs.append(pl.BlockSpec((tm, width), functools.partial(lambda i, c: (i, c), c=off // width)))
    for p in params:
        in_specs.append(pl.BlockSpec(p.shape, functools.partial(lambda i, nd: (0,) * nd, nd=p.ndim)))
    out_specs = [pl.BlockSpec((tm, w), lambda i: (i, 0)) for w, _ in outs]
    out_specs += [pl.BlockSpec(s, functools.partial(lambda i, nd: (0,) * nd, nd=len(s))) for s, _ in accs]
    out_shape = [jax.ShapeDtypeStruct((rows, w), d) for w, d in outs]
    out_shape += [jax.ShapeDtypeStruct(s, d) for s, d in accs]
    n_in, n_par, n_out = len(ins), len(params), len(outs)
    carried = []
    if into is not None:
        buf, total, col = into
        assert comm is None and col % outs[0][0] == 0, (name, col, outs[0])
        out_specs[0] = pl.BlockSpec((tm, outs[0][0]), functools.partial(lambda i, c: (i, c), c=col // outs[0][0]))
        out_shape[0] = jax.ShapeDtypeStruct((rows, total), outs[0][1])
        carried = [] if buf is None else [buf]

    steps = rows // tm

    def kern(*refs):
        in_refs, out_refs, _, c_src, c_out, c_sem = _split_refs(refs, n_in + n_par + len(carried),
                                                                 n_out + len(accs), comm)
        in_refs = in_refs[:n_in + n_par]
        first = pl.program_id(0) == 0
        if comm is not None:
            start_comm, wait_comm = comm.run(c_src, c_out, c_sem, first, pl.program_id(0) == steps - 1)
            start_comm()
        res = body(*[r[...] for r in in_refs])
        for r, v in zip(out_refs[:n_out], res[:n_out]):
            r[...] = v.astype(r.dtype)
        for r, v in zip(out_refs[n_out:], res[n_out:]):
            @pl.when(first)
            def _(r=r, v=v):
                r[...] = v.astype(r.dtype)

            @pl.when(jnp.logical_not(first))
            def _(r=r, v=v):
                r[...] += v.astype(r.dtype)
        if comm is not None:
            wait_comm()

    args = [a for a, _, _ in ins] + list(params)
    if comm is None:
        return pl.pallas_call(
            kern, name=name, grid=(steps,), in_specs=in_specs + [ANY] * len(carried), out_specs=out_specs,
            out_shape=out_shape, input_output_aliases={len(args): 0} if carried else {},
            compiler_params=_params(("arbitrary",) if accs else ("parallel",)),
        )(*args, *carried)
    res_all = pl.pallas_call(
        kern, name=name, grid=(steps,), in_specs=in_specs + [ANY] * len(comm.srcs),
        out_specs=out_specs + [ANY] * len(comm.outs), out_shape=out_shape + comm.outs,
        scratch_shapes=comm.scratch(), input_output_aliases=comm.aliases(len(args), len(out_shape)),
        compiler_params=_params(("arbitrary",)),
    )(*args, *comm.srcs)
    return list(res_all[:len(out_shape)]), list(res_all[len(out_shape):])


def _whole(x):
    return (x, 0, x.shape[1])


def _first(res, comm):
    return res[0] if comm is None else (res[0][0], res[1])


def _rms_fwd(x, gain, name, comm=None):
    def body(xv, g):
        r = lax.rsqrt(jnp.mean(xv * xv, axis=-1, keepdims=True) + EPS)
        return ((xv * r) * g,)

    return _first(_rowwise(body, [_whole(x)], [gain], [(x.shape[1], BF16)], [], name, _tile(x.shape[0], 256, 16),
                           comm=comm), comm)


def _rms_bwd(x, gain, dh, dres, name, comm=None):
    d = x.shape[1]

    def body(*vals):
        if dres is None:
            xv, dhv, g = vals
        else:
            xv, dhv, drv, g = vals
        r = lax.rsqrt(jnp.mean(xv * xv, axis=-1, keepdims=True) + EPS)
        xh = xv * r
        w = dhv * g
        dx = r * (w - xh * jnp.mean(w * xh, axis=-1, keepdims=True))
        if dres is not None:
            dx = drv + dx
        return dx, dx, jnp.sum(dhv * xh, axis=0, keepdims=True)

    ins = [_whole(x), _whole(dh)] + ([_whole(dres)] if dres is not None else [])
    return _rowwise(body, ins, [gain], [(d, F32), (d, BF16)], [((1, d), F32)], name, _tile(x.shape[0], 256, 16),
                    comm=comm)


def _loss_head(y, target, name):
    d = y.shape[1]

    def body(yv, tv):
        e = yv - tv
        part = jnp.sum(jnp.sum(e * e, axis=1, keepdims=True), axis=0, keepdims=True)
        return e / d, e / d, jnp.broadcast_to((0.5 / d) * part, (1, LANES))

    return _rowwise(body, [_whole(y), _whole(target)], [], [(d, F32), (d, BF16)], [((1, LANES), F32)], name,
                    _tile(y.shape[0], 256, 16))


def _head_mean(v, head_dim):
    cols = []
    lane = lax.broadcasted_iota(jnp.int32, (1, LANES), 1)
    for j in range(v.shape[1] // LANES):
        blk = v[:, j * LANES:(j + 1) * LANES]
        if head_dim == LANES:
            m = jnp.sum(blk, axis=-1, keepdims=True)
            cols.append(jnp.broadcast_to(m, blk.shape))
        else:
            lo = jnp.sum(jnp.where(lane < head_dim, blk, 0.0), axis=-1, keepdims=True)
            hi = jnp.sum(jnp.where(lane >= head_dim, blk, 0.0), axis=-1, keepdims=True)
            cols.append(jnp.where(lane < head_dim, lo, hi))
    return jnp.concatenate(cols, axis=1) / head_dim


def _head_norm(xv, g, head_dim):
    r = lax.rsqrt(_head_mean(xv * xv, head_dim) + EPS)
    return (xv * r) * g


def _head_norm_bwd(xv, g, dy, head_dim):
    r = lax.rsqrt(_head_mean(xv * xv, head_dim) + EPS)
    xh = xv * r
    w = dy * g
    dx = r * (w - xh * _head_mean(w * xh, head_dim))
    return dx, jnp.sum(dy * xh, axis=0, keepdims=True)


def _log_sigmoid(x):
    return jnp.minimum(x, 0.0) - jnp.log(1.0 + jnp.exp(-jnp.abs(x)))


def _shift_rows(v, sh, down):
    n = v.shape[0]
    row = lax.broadcasted_iota(jnp.int32, (n, 1), 0)
    if down:
        return jnp.where(row >= sh, pltpu.roll(v, sh, 0), 0.0)
    return jnp.where(row < n - sh, pltpu.roll(v, n - sh, 0), 0.0)


def _scan_rows(v, down):
    sh = 1
    while sh < v.shape[0]:
        v = v + _shift_rows(v, sh, down)
        sh *= 2
    return v


def _window_sum(v, steps, down):
    for s in range(steps):
        v = v + _shift_rows(v, 2 ** s, down)
    return v


def _fox_fwd(qn, kn, vb, cks, batch, seq, name, comm=None):
    t, width = qn.shape
    pairs = width // LANES
    tq = _tile(seq, 512, LANES)
    nq = seq // tq

    def kern(*refs):
        (q_ref, k_ref, v_ref, c_ref), (o_ref, lse_ref), _, c_src, c_out, c_sem = _split_refs(refs, 4, 2, comm)
        qi = pl.program_id(2)
        if comm is not None:
            b_id, p_id = pl.program_id(0), pl.program_id(1)
            first = jnp.logical_and(jnp.logical_and(b_id == 0, p_id == 0), qi == 0)
            last = jnp.logical_and(jnp.logical_and(b_id == batch - 1, p_id == pairs - 1), qi == nq - 1)
            start_comm, wait_comm = comm.run(c_src, c_out, c_sem, first, last)
            start_comm()
        lane = lax.broadcasted_iota(jnp.int32, (1, LANES), 1)
        rowi = lax.broadcasted_iota(jnp.int32, (tq, tq), 0)
        coli = lax.broadcasted_iota(jnp.int32, (tq, tq), 1)
        q_all = q_ref[...]
        o_heads, lse_heads = [], []
        for h in range(2):
            hm = (lane < FOX_HEAD_DIM) if h == 0 else (lane >= FOX_HEAD_DIM)
            q = jnp.where(hm, q_all, jnp.zeros_like(q_all))

            def step(j, carry, h=h, q=q, diagonal=False):
                m, l, acc = carry
                start = pl.multiple_of(j * tq, tq)
                k = k_ref[pl.ds(start, tq), :]
                v = v_ref[pl.ds(start, tq), :]
                s = lax.dot_general(q, k, _DIMS["nt"], preferred_element_type=F32)
                s = s - c_ref[0, h:h + 1, pl.ds(start, tq)]
                if diagonal:
                    s = jnp.where(rowi >= coli, s, NEG)
                m_new = jnp.maximum(m, jnp.max(s, axis=-1, keepdims=True))
                alpha = jnp.exp(m - m_new)
                p = jnp.exp(s - m_new)
                l = alpha * l + jnp.sum(p, axis=-1, keepdims=True)
                acc = alpha * acc + lax.dot_general(p.astype(BF16), v, _DIMS["nn"], preferred_element_type=F32)
                return m_new, l, acc

            init = (jnp.full((tq, 1), NEG, F32), jnp.zeros((tq, 1), F32), jnp.zeros((tq, LANES), F32))
            m, l, acc = step(qi, lax.fori_loop(0, qi, step, init), diagonal=True)
            o_heads.append(acc / l)
            lse_heads.append(jnp.broadcast_to(m + jnp.log(l), (tq, LANES)))
        o_ref[...] = jnp.where(lane < FOX_HEAD_DIM, o_heads[0], o_heads[1])
        lse_ref[...] = jnp.where(lane < FOX_HEAD_DIM, lse_heads[0], lse_heads[1])
        if comm is not None:
            wait_comm()

    q_spec = pl.BlockSpec((tq, LANES), lambda b, hp, qi: (b * nq + qi, hp))
    kv_spec = pl.BlockSpec((seq, LANES), lambda b, hp, qi: (b, hp))
    c_spec = pl.BlockSpec((1, 2, seq), lambda b, hp, qi: (b * pairs + hp, 0, 0))
    in_specs = [q_spec, kv_spec, kv_spec, c_spec]
    out_shape = [jax.ShapeDtypeStruct((t, width), F32), jax.ShapeDtypeStruct((t, width), F32)]
    if comm is None:
        return pl.pallas_call(
            kern, name=name, grid=(batch, pairs, nq), in_specs=in_specs, out_specs=[q_spec, q_spec],
            out_shape=out_shape, compiler_params=_params(("parallel", "parallel", "arbitrary")),
        )(qn, kn, vb, cks)
    res_all = pl.pallas_call(
        kern, name=name, grid=(batch, pairs, nq), in_specs=in_specs + [ANY] * len(comm.srcs),
        out_specs=[q_spec, q_spec] + [ANY] * len(comm.outs), out_shape=out_shape + comm.outs,
        scratch_shapes=comm.scratch(), compiler_params=_params(("arbitrary", "arbitrary", "arbitrary")),
    )(qn, kn, vb, cks, *comm.srcs)
    return list(res_all[:2]), list(res_all[2:])


def _fox_bwd(qn, kn, vb, cks, o, lse, do, dz, dv_col, batch, seq, name, comm):
    t, width = qn.shape
    pairs = width // LANES
    tk = _tile(seq, 512, LANES)
    nk = seq // tk
    scale = FOX_HEAD_DIM ** -0.5

    def kern(*refs):
        ins, outs, _, c_src, c_out, c_sem = _split_refs(refs, 8, 5, comm)
        k_ref, v_ref, q_ref, c_ref, o_ref, lse_ref, do_ref, _ = ins
        dk_ref, dv_ref, dq_ref, dc_ref, dcq_ref = outs
        kj = pl.program_id(2)
        if comm is not None:
            b_id, p_id = pl.program_id(0), pl.program_id(1)
            first = jnp.logical_and(jnp.logical_and(b_id == 0, p_id == 0), kj == 0)
            last = jnp.logical_and(jnp.logical_and(b_id == batch - 1, p_id == pairs - 1), kj == nk - 1)
            start_comm, wait_comm = comm.run(c_src, c_out, c_sem, first, last)
            start_comm()
        lane = lax.broadcasted_iota(jnp.int32, (1, LANES), 1)
        rowi = lax.broadcasted_iota(jnp.int32, (tk, tk), 0)
        coli = lax.broadcasted_iota(jnp.int32, (tk, tk), 1)

        @pl.when(kj == 0)
        def _():
            dq_ref[...] = jnp.zeros_like(dq_ref)
            dcq_ref[...] = jnp.zeros_like(dcq_ref)

        k_all = k_ref[...]
        v_all = v_ref[...]
        kstart = pl.multiple_of(kj * tk, tk)
        dk_heads, dv_heads = [], []
        for h in range(2):
            hm = (lane < FOX_HEAD_DIM) if h == 0 else (lane >= FOX_HEAD_DIM)
            kh = jnp.where(hm, k_all, jnp.zeros_like(k_all))
            vh = jnp.where(hm, v_all, jnp.zeros_like(v_all))
            c_row = c_ref[0, h:h + 1, pl.ds(kstart, tk)]

            def step(qi, carry, h=h, hm=hm, kh=kh, vh=vh, c_row=c_row, diagonal=False):
                dk_acc, dv_acc, dc_acc = carry
                start = pl.multiple_of(qi * tk, tk)
                q = q_ref[pl.ds(start, tk), :]
                dov = do_ref[pl.ds(start, tk), :]
                ov = o_ref[pl.ds(start, tk), :]
                lse_col = jnp.max(jnp.where(hm, lse_ref[pl.ds(start, tk), :], NEG), axis=-1, keepdims=True)
                dob = jnp.where(hm, dov, 0.0).astype(BF16)
                dcol = jnp.sum(dob.astype(F32) * ov, axis=-1, keepdims=True)
                s = lax.dot_general(q, kh, _DIMS["nt"], preferred_element_type=F32) - c_row
                p = jnp.exp(s - lse_col)
                if diagonal:
                    p = jnp.where(rowi >= coli, p, 0.0)
                dp = lax.dot_general(dob, vh, _DIMS["nt"], preferred_element_type=F32)
                ds = p * (dp - dcol)
                ds_b = ds.astype(BF16)
                dv_acc = dv_acc + lax.dot_general(p.astype(BF16), dob, _DIMS["tn"], preferred_element_type=F32)
                dk_acc = dk_acc + lax.dot_general(ds_b, q, _DIMS["tn"], preferred_element_type=F32)
                dq_part = lax.dot_general(ds_b * scale, kh, _DIMS["nn"], preferred_element_type=F32)
                dq_ref[pl.ds(start, tk), :] += dq_part
                dcq_ref[pl.ds(start, tk), :] += jnp.where(hm, jnp.sum(ds, axis=-1, keepdims=True), 0.0)
                dc_acc = dc_acc - jnp.sum(ds, axis=0, keepdims=True)
                return dk_acc, dv_acc, dc_acc

            init = (jnp.zeros((tk, LANES), F32), jnp.zeros((tk, LANES), F32), jnp.zeros((1, tk), F32))
            dk_acc, dv_acc, dc_acc = lax.fori_loop(kj + 1, nk, step, step(kj, init, diagonal=True))
            dk_heads.append(dk_acc)
            dv_heads.append(dv_acc)
            dc_ref[0, h:h + 1, pl.ds(kstart, tk)] = dc_acc
        dk_ref[...] = jnp.where(lane < FOX_HEAD_DIM, dk_heads[0], dk_heads[1])
        dv_ref[...] = jnp.where(lane < FOX_HEAD_DIM, dv_heads[0], dv_heads[1]).astype(BF16)
        if comm is not None:
            wait_comm()

    kv_spec = pl.BlockSpec((tk, LANES), lambda b, hp, kj: (b * nk + kj, hp))
    full_spec = pl.BlockSpec((seq, LANES), lambda b, hp, kj: (b, hp))
    c_spec = pl.BlockSpec((1, 2, seq), lambda b, hp, kj: (b * pairs + hp, 0, 0))
    dv_spec = pl.BlockSpec((tk, LANES), lambda b, hp, kj: (b * nk + kj, hp + dv_col // LANES))
    in_specs = [kv_spec, kv_spec, full_spec, c_spec, full_spec, full_spec, full_spec, ANY]
    out_specs = [kv_spec, dv_spec, full_spec, c_spec, full_spec]
    out_shape = [jax.ShapeDtypeStruct((t, width), F32), jax.ShapeDtypeStruct(dz.shape, BF16),
                 jax.ShapeDtypeStruct((t, width), F32), jax.ShapeDtypeStruct(cks.shape, F32),
                 jax.ShapeDtypeStruct((t, width), F32)]
    res_all = pl.pallas_call(
        kern, name=name, grid=(batch, pairs, nk), in_specs=in_specs + [ANY] * len(comm.srcs),
        out_specs=out_specs + [ANY] * len(comm.outs), out_shape=out_shape + comm.outs,
        scratch_shapes=comm.scratch(), input_output_aliases={7: 1},
        compiler_params=_params(("arbitrary", "arbitrary", "arbitrary")),
    )(kn, vb, qn, cks, o, lse, do, dz, *comm.srcs)
    return list(res_all[:5]), list(res_all[5:])


def _mem_fwd(qn, kn, vb, batch, seq, mlen, name):
    t, width = qn.shape
    heads = width // LANES
    tq = _tile(seq, 512, LANES)
    nq = seq // tq
    scale = MEM_HEAD_DIM ** -0.5

    def kern(q_ref, k_ref, v_ref, o_ref):
        s = lax.dot_general(q_ref[...], k_ref[...], _DIMS["nt"], preferred_element_type=F32) * scale
        e = jnp.exp(s - jnp.max(s, axis=-1, keepdims=True))
        p = e / jnp.sum(e, axis=-1, keepdims=True)
        o_ref[...] = lax.dot_general(p.astype(BF16), v_ref[...], _DIMS["nn"], preferred_element_type=F32)

    q_spec = pl.BlockSpec((tq, LANES), lambda b, h, qi: (b * nq + qi, h))
    kv_spec = pl.BlockSpec((mlen, LANES), lambda b, h, qi: (b, h))
    return pl.pallas_call(
        kern, name=name, grid=(batch, heads, nq), in_specs=[q_spec, kv_spec, kv_spec], out_specs=q_spec,
        out_shape=jax.ShapeDtypeStruct((t, width), F32),
        compiler_params=_params(("parallel", "parallel", "parallel")),
    )(qn, kn, vb)


def _mem_bwd(qn, kn, vb, do, batch, seq, mlen, name):
    t, width = qn.shape
    heads = width // LANES
    tq = _tile(seq, 512, LANES)
    nq = seq // tq
    scale = MEM_HEAD_DIM ** -0.5

    def kern(q_ref, k_ref, v_ref, do_ref, dq_ref, dk_ref, dv_ref):
        qi = pl.program_id(2)
        q, k, v = q_ref[...], k_ref[...], v_ref[...]
        dob = do_ref[...].astype(BF16)
        s = lax.dot_general(q, k, _DIMS["nt"], preferred_element_type=F32) * scale
        e = jnp.exp(s - jnp.max(s, axis=-1, keepdims=True))
        p = e / jnp.sum(e, axis=-1, keepdims=True)
        dp = lax.dot_general(dob, v, _DIMS["nt"], preferred_element_type=F32)
        ds = p * (dp - jnp.sum(p * dp, axis=-1, keepdims=True))
        dsb = (ds * scale).astype(BF16)
        dq_ref[...] = lax.dot_general(dsb, k, _DIMS["nn"], preferred_element_type=F32)
        dk = lax.dot_general(dsb, q, _DIMS["tn"], preferred_element_type=F32)
        dv = lax.dot_general(p.astype(BF16), dob, _DIMS["tn"], preferred_element_type=F32)

        @pl.when(qi == 0)
        def _():
            dk_ref[...] = dk
            dv_ref[...] = dv

        @pl.when(qi > 0)
        def _():
            dk_ref[...] += dk
            dv_ref[...] += dv

    q_spec = pl.BlockSpec((tq, LANES), lambda b, h, qi: (b * nq + qi, h))
    kv_spec = pl.BlockSpec((mlen, LANES), lambda b, h, qi: (b, h))
    return pl.pallas_call(
        kern, name=name, grid=(batch, heads, nq), in_specs=[q_spec, kv_spec, kv_spec, q_spec],
        out_specs=[q_spec, kv_spec, kv_spec],
        out_shape=[jax.ShapeDtypeStruct((t, width), F32), jax.ShapeDtypeStruct(kn.shape, F32),
                   jax.ShapeDtypeStruct(kn.shape, F32)],
        compiler_params=_params(("parallel", "parallel", "arbitrary")),
    )(qn, kn, vb, do)


def _position():
    return lax.axis_index("x"), lax.axis_index("y"), lax.axis_index("c")


def _other_chips(x, y):
    return [(1 - x, y), (x, 1 - y), (1 - x, 1 - y)]


def _remote(src, dst, sems, to):
    send, recv = sems.remote()
    return pltpu.make_async_remote_copy(src_ref=src, dst_ref=dst, send_sem=send, recv_sem=recv,
                                        device_id=to, device_id_type=MESH)


def _gather_first(shards):
    n = len(shards)

    def plan(src, out, sems):
        x, y, c = _position()
        me = 4 * x + 2 * y + c
        peers = [(x, y, 1 - c)] + [(px, py, c) for px, py in _other_chips(x, y)]
        copies = []
        for i in range(n):
            copies.append(pltpu.make_async_copy(src[i], out[i].at[me], sems.one_local()))
            copies += [_remote(src[i], out[i].at[me], sems, to) for to in peers]
        return copies

    outs = [jax.ShapeDtypeStruct((N_DEV,) + s.shape, s.dtype) for s in shards]
    return _Comm(shards, outs, 4 * n, n, plan)


def _gather_second(bufs):
    n = len(bufs)

    def plan(src, out, sems):
        x, y, c = _position()
        copies = []
        for i in range(n):
            for px, py in _other_chips(x, y):
                blk = out[i].at[4 * px + 2 * py + c]
                copies.append(_remote(blk, blk, sems, (x, y, 1 - c)))
        return copies

    outs = [jax.ShapeDtypeStruct(b.shape, b.dtype) for b in bufs]
    return _Comm(bufs, outs, 3 * n, 0, plan, in_place=True)


def _scatter_first(partials):
    n = len(partials)

    def plan(src, out, sems):
        x, y, c = _position()
        return [_remote(src[i].at[2 * q + (1 - c)], out[i].at[q], sems, (x, y, 1 - c))
                for i in range(n) for q in range(4)]

    outs = [jax.ShapeDtypeStruct((4,) + g.shape[1:], g.dtype) for g in partials]
    return _Comm(partials, outs, 4 * n, 0, plan)


def _scatter_second(halves):
    n = len(halves)

    def plan(src, out, sems):
        x, y, c = _position()
        return [_remote(src[i].at[2 * px + py], out[i].at[r], sems, (px, py, c))
                for i in range(n) for r, (px, py) in enumerate(_other_chips(x, y))]

    outs = [jax.ShapeDtypeStruct((3,) + h.shape[1:], h.dtype) for h in halves]
    return _Comm(halves, outs, 3 * n, 0, plan)


def _comm_call(comm, name):
    ns, no = len(comm.srcs), len(comm.outs)

    def body(*refs):
        copies = comm.plan(list(refs[:ns]), list(refs[ns:ns + no]), _Sems(*refs[ns + no:]))
        for cp in copies:
            cp.start()
        for cp in copies:
            cp.wait()

    return pl.pallas_call(
        body, name=name, in_specs=[ANY] * ns, out_specs=[ANY] * no, out_shape=comm.outs,
        scratch_shapes=comm.scratch(), input_output_aliases=comm.aliases(0, 0),
    )(*comm.srcs)


def _gather_first_relayed(shard, name):
    def body(x_ref, out_ref, send, recv, local):
        sems = _Sems(send, recv, local)
        x, y, c = _position()

        def slot(px, py):
            return out_ref.at[4 * px + 2 * py + c]

        mine = pltpu.make_async_copy(x_ref, slot(x, y), sems.one_local())
        to_sibling = _remote(x_ref, slot(x, y), sems, (x, y, 1 - c))
        to_x = _remote(x_ref, slot(x, y), sems, (1 - x, y, c))
        to_y = _remote(x_ref, slot(x, y), sems, (x, 1 - y, c))
        for cp in (mine, to_sibling, to_x, to_y):
            cp.start()
        relay_send, relay_recv = sems.remote()

        def relay(arrived, unused, origin, to):
            arrived.wait_recv()
            blk = slot(*origin)
            fwd = pltpu.make_async_remote_copy(src_ref=blk, dst_ref=blk, send_sem=relay_send, recv_sem=relay_recv,
                                               device_id=(*to, c), device_id_type=MESH)
            fwd.start()
            fwd.wait()
            arrived.wait_send()
            unused.wait()

        _when(c == 0, lambda: relay(to_y, to_x, (x, 1 - y), (1 - x, y)))
        _when(c == 1, lambda: relay(to_x, to_y, (1 - x, y), (x, 1 - y)))
        mine.wait()
        to_sibling.wait()

    return pl.pallas_call(
        body, name=name, in_specs=[ANY], out_specs=ANY,
        out_shape=jax.ShapeDtypeStruct((N_DEV,) + shard.shape, shard.dtype),
        scratch_shapes=[pltpu.SemaphoreType.DMA((4,)), pltpu.SemaphoreType.DMA((4,)), pltpu.SemaphoreType.DMA((1,))],
    )(shard)


def _all_gather_vmem(shard, name):
    first = _gather_first([shard])

    def body(x_ref, out_ref, send, recv, local):
        sems = _Sems(send, recv, local)
        copies = first.plan([x_ref], [out_ref], sems)
        for cp in copies:
            cp.start()
        x, y, c = _position()
        passed = []
        for j, (px, py) in enumerate(_other_chips(x, y)):
            copies[2 + j].wait_recv()
            blk = out_ref.at[4 * px + 2 * py + c]
            fwd = _remote(blk, blk, sems, (x, y, 1 - c))
            fwd.start()
            passed.append(fwd)
        copies[0].wait()
        copies[1].wait()
        for cp in copies[2:]:
            cp.wait_send()
        for cp in passed:
            cp.wait()

    vm = pl.BlockSpec(memory_space=pltpu.VMEM)
    return pl.pallas_call(
        body, name=name, in_specs=[vm], out_specs=vm,
        out_shape=jax.ShapeDtypeStruct((N_DEV,) + shard.shape, shard.dtype),
        scratch_shapes=[pltpu.SemaphoreType.DMA((7,)), pltpu.SemaphoreType.DMA((7,)), pltpu.SemaphoreType.DMA((1,))],
    )(shard)


def _rs_pair_add(partial, landed, where, name):
    _, r, c = partial.shape
    tm = _tile(r, 256, 16)

    def kern(where_ref, g_ref, l_ref, own_ref, hb_ref):
        s = g_ref[...] + l_ref[...]
        hb_ref[...] = s.astype(BF16)

        @pl.when(pl.program_id(1) == where_ref[1])
        def _():
            own_ref[...] = s

    grid_spec = pltpu.PrefetchScalarGridSpec(
        num_scalar_prefetch=1, grid=(r // tm, 4),
        in_specs=[pl.BlockSpec((None, tm, c), lambda i, q, wr: (2 * q + wr[0], i, 0)),
                  pl.BlockSpec((None, tm, c), lambda i, q, wr: (q, i, 0))],
        out_specs=[pl.BlockSpec((tm, c), lambda i, q, wr: (i, 0)),
                   pl.BlockSpec((None, tm, c), lambda i, q, wr: (q, i, 0))])
    return pl.pallas_call(
        kern, name=name, grid_spec=grid_spec,
        out_shape=[jax.ShapeDtypeStruct((r, c), F32), jax.ShapeDtypeStruct((4, r, c), BF16)],
        compiler_params=_params(("parallel", "arbitrary")),
    )(where, partial, landed)


def _adam_math(g, w, m, v):
    m = ADAM_B1 * m + (1.0 - ADAM_B1) * g
    v = ADAM_B2 * v + (1.0 - ADAM_B2) * (g * g)
    m_hat = m / (1.0 - ADAM_B1 ** ADAM_STEP)
    v_hat = v / (1.0 - ADAM_B2 ** ADAM_STEP)
    delta = -ADAM_LR * (m_hat / (jnp.sqrt(v_hat) + ADAM_EPS) + ADAM_WD * w)
    return delta, m, v


def _rs_finish_adam(own, landed, w, m, v, name):
    r, c = own.shape
    tm = _tile(r, 128, 16)

    def tile(h_ref, l_ref, w_ref, m_ref, v_ref, *outs):
        g = ((h_ref[...] + l_ref[0].astype(F32)) + l_ref[1].astype(F32)) + l_ref[2].astype(F32)
        delta, m_new, v_new = _adam_math(g, w_ref[...], m_ref[...], v_ref[...])
        for ref, val in zip(outs, (g, delta, m_new, v_new)):
            ref[...] = val

    flat = pl.BlockSpec((tm, c), lambda i: (i, 0))
    deep = pl.Buffered(STREAM_BUFFERS) if r // tm > STREAM_BUFFERS else None
    stream = pl.BlockSpec((tm, c), lambda i: (i, 0), pipeline_mode=deep)
    pipeline = pltpu.emit_pipeline(
        tile, grid=(r // tm,),
        in_specs=[stream, pl.BlockSpec((3, tm, c), lambda i: (0, i, 0), pipeline_mode=deep), stream, stream, stream],
        out_specs=[flat] * 4)

    def kern(*refs):
        pipeline(*refs)

    return pl.pallas_call(
        kern, name=name, in_specs=[ANY] * 5, out_specs=[ANY] * 4, out_shape=[jax.ShapeDtypeStruct((r, c), F32)] * 4,
        compiler_params=_params())(own, landed, w, m, v)


def _allreduce_adam(gathered, w, m, v, name):
    _, r, c = gathered.shape

    def kern(a_ref, w_ref, m_ref, v_ref, g_out, d_out, m_out, v_out):
        g = a_ref[0]
        for j in range(1, N_DEV):
            g = g + a_ref[j]
        delta, m_new, v_new = _adam_math(g, w_ref[...], m_ref[...], v_ref[...])
        g_out[...] = g
        d_out[...] = delta
        m_out[...] = m_new
        v_out[...] = v_new

    return pl.pallas_call(
        kern, name=name, out_shape=[jax.ShapeDtypeStruct((r, c), F32)] * 4,
        compiler_params=_params(),
    )(gathered, w, m, v)


def _cols_from_blocks(g):
    n, k, nb = g.shape
    return jnp.transpose(g, (1, 0, 2)).reshape(k, n * nb)


def _blocks_from_cols(w):
    k, n = w.shape
    return jnp.transpose(w.reshape(k, N_DEV, n // N_DEV), (1, 0, 2))


def _pack_small(parts):
    flat = []
    for p in parts:
        v = p.reshape(-1)
        flat.append(jnp.pad(v, (0, (-v.shape[0]) % (8 * LANES))))
    return jnp.concatenate(flat).reshape(-1, LANES)


def _unpack_small(buf, like):
    out, pos = [], 0
    flat = buf.reshape(-1)
    for p in like:
        size = p.size
        out.append(flat[pos:pos + size].reshape(p.shape))
        pos += size + (-size) % (8 * LANES)
    return out


def kernel(x, mem, ffn1_norm, ffn1_w_gate_up, ffn1_w_down, mix_norm, mem_norm, w_in, b_forget, pool_w, pool_scale, w_pool_up, fox_q_norm, fox_k_norm, w_fox_o, w_mem_kv, mem_q_norm, mem_k_norm, w_mem_o, w_out, ffn2_norm, ffn2_w_gate_up, ffn2_w_down, loss_target, m_ffn1_norm, m_ffn1_w_gate_up, m_ffn1_w_down, m_mix_norm, m_mem_norm, m_w_in, m_b_forget, m_pool_w, m_pool_scale, m_w_pool_up, m_fox_q_norm, m_fox_k_norm, m_w_fox_o, m_w_mem_kv, m_mem_q_norm, m_mem_k_norm, m_w_mem_o, m_w_out, m_ffn2_norm, m_ffn2_w_gate_up, m_ffn2_w_down, v_ffn1_norm, v_ffn1_w_gate_up, v_ffn1_w_down, v_mix_norm, v_mem_norm, v_w_in, v_b_forget, v_pool_w, v_pool_scale, v_w_pool_up, v_fox_q_norm, v_fox_k_norm, v_w_fox_o, v_w_mem_kv, v_mem_q_norm, v_mem_k_norm, v_w_mem_o, v_w_out, v_ffn2_norm, v_ffn2_w_gate_up, v_ffn2_w_down):
    names = ["ffn1_norm", "ffn1_w_gate_up", "ffn1_w_down", "mix_norm", "mem_norm", "w_in", "b_forget", "pool_w",
             "pool_scale", "w_pool_up", "fox_q_norm", "fox_k_norm", "w_fox_o", "w_mem_kv", "mem_q_norm",
             "mem_k_norm", "w_mem_o", "w_out", "ffn2_norm", "ffn2_w_gate_up", "ffn2_w_down"]
    w_args = [ffn1_norm, ffn1_w_gate_up, ffn1_w_down, mix_norm, mem_norm, w_in, b_forget, pool_w, pool_scale,
              w_pool_up, fox_q_norm, fox_k_norm, w_fox_o, w_mem_kv, mem_q_norm, mem_k_norm, w_mem_o, w_out,
              ffn2_norm, ffn2_w_gate_up, ffn2_w_down]
    m_args = [m_ffn1_norm, m_ffn1_w_gate_up, m_ffn1_w_down, m_mix_norm, m_mem_norm, m_w_in, m_b_forget, m_pool_w,
              m_pool_scale, m_w_pool_up, m_fox_q_norm, m_fox_k_norm, m_w_fox_o, m_w_mem_kv, m_mem_q_norm,
              m_mem_k_norm, m_w_mem_o, m_w_out, m_ffn2_norm, m_ffn2_w_gate_up, m_ffn2_w_down]
    v_args = [v_ffn1_norm, v_ffn1_w_gate_up, v_ffn1_w_down, v_mix_norm, v_mem_norm, v_w_in, v_b_forget, v_pool_w,
              v_pool_scale, v_w_pool_up, v_fox_q_norm, v_fox_k_norm, v_w_fox_o, v_w_mem_kv, v_mem_q_norm,
              v_mem_k_norm, v_w_mem_o, v_w_out, v_ffn2_norm, v_ffn2_w_gate_up, v_ffn2_w_down]
    W = dict(zip(names, w_args))
    M = dict(zip(names, m_args))
    V = dict(zip(names, v_args))

    batch, seq, d = x.shape
    mlen = mem.shape[1]
    t = batch * seq
    gate_w = 3 * d
    z_gate, z_q = 0, gate_w
    z_k, z_v = z_q + FOX_WIDTH, z_q + 2 * FOX_WIDTH
    z_u = z_q + 3 * FOX_WIDTH
    z_qm = z_u + POOL_WIDTH
    z_f = z_qm + MEM_WIDTH
    z_width = -(-(z_f + F_PAD) // 512) * 512

    x2d = x.reshape(t, d)
    mem2d = mem.reshape(batch * mlen, d)
    tgt2d = loss_target.reshape(t, d)

    big = ["ffn1_w_gate_up", "ffn1_w_down", "w_in", "w_pool_up", "w_fox_o", "w_mem_kv", "w_mem_o", "w_out",
           "ffn2_w_gate_up", "ffn2_w_down"]
    col_sharded = {"ffn1_w_gate_up", "ffn2_w_gate_up", "w_in", "w_pool_up", "w_fox_o", "w_mem_o"}
    mixer_small = ["w_pool_up", "w_fox_o", "w_mem_kv", "w_mem_o", "w_out"]
    shard = {n: W[n][0].astype(BF16) for n in big}

    def rows_of(g):
        return g.reshape(-1, g.shape[2])

    gu_caps_nt, gu_caps_tn = (1024, 2048, 1408), (1024, 1408, 2048)
    down_caps, dwd_caps = (1024, 1024, 2816), (1408, 1024, 2048)

    wgu1 = _gather_first_relayed(shard["ffn1_w_gate_up"], "ag_first")
    h1, (wgu1,) = _rms_fwd(x2d, W["ffn1_norm"], "ffn1_rms", comm=_gather_second([wgu1]))
    w_in_top, w_in_bot = shard["w_in"][:d // 2], shard["w_in"][d // 2:]
    (gu1, a1), bufs = _gate_up_swiglu(h1, wgu1, "ffn1_gu", comm=_gather_first([shard["ffn1_w_down"], w_in_top]))
    wd1_g, w_in_top_g = _comm_call(_gather_second(bufs), "ag_second")
    wd1 = rows_of(wd1_g)
    x1, (w_in_bot_g,) = _mm(a1, wd1, "nn", "ffn1_down", scale=0.5, res=x2d, caps=down_caps,
                            comm=_gather_first([w_in_bot]))
    h2, (w_in_bot_g,) = _rms_fwd(x1, W["mix_norm"], "mix_rms", comm=_gather_second([w_in_bot_g]))

    o_u, o_q, o_v = 0, POOL_WIDTH, POOL_WIDTH + 2 * FOX_WIDTH
    o_f = o_v + FOX_WIDTH
    o_qm = o_f + FOX_HEADS
    o_g = o_qm + MEM_WIDTH
    wi = jnp.concatenate([_cols_from_blocks(w_in_top_g), _cols_from_blocks(w_in_bot_g)], axis=0)
    w_in_pad = jnp.concatenate(
        [wi[:, o_g:o_g + gate_w], wi[:, o_q:o_q + 3 * FOX_WIDTH], wi[:, o_u:o_u + POOL_WIDTH],
         wi[:, o_qm:o_qm + MEM_WIDTH], wi[:, o_f:o_f + FOX_HEADS],
         jnp.zeros((d, z_width - z_f - FOX_HEADS), BF16)], axis=1)
    group_b = mixer_small + ["ffn2_w_down"]
    z, bufs = _mm(h2, w_in_pad, "nn", "mix_in", caps=(1024, 1536, 2048), comm=_gather_first([shard[n] for n in group_b]))

    pool_w_b = W["pool_w"][0].astype(BF16)

    def pool_fwd_body(u, pw, ps):
        row = lax.broadcasted_iota(jnp.int32, (seq, 1), 0)
        diffs, mixed = [], []
        for g in range(POOL_GROUPS):
            ug = u[:, g * POOL_GROUP_DIM:(g + 1) * POOL_GROUP_DIM]
            cnt = jnp.minimum(row + 1, POOL_WINDOWS[g]).astype(F32)
            diff = _window_sum(ug, g + 1, True) / cnt - ug
            diffs.append(diff)
            mixed.append(lax.dot_general(diff.astype(BF16), pw[g], _DIMS["nn"], preferred_element_type=F32))
        diffs = jnp.concatenate(diffs, axis=1)
        mixed = jnp.concatenate(mixed, axis=1)
        return mixed * ps, diffs, mixed

    ypp, pool_diff, pool_mixed = _rowwise(
        pool_fwd_body, [(z, z_u, POOL_WIDTH)], [pool_w_b, W["pool_scale"]],
        [(POOL_WIDTH, BF16), (POOL_WIDTH, BF16), (POOL_WIDTH, F32)], [], "pool_fwd", seq)

    gq = jnp.tile(W["fox_q_norm"], (1, FOX_HEADS))
    gk = jnp.tile(W["fox_k_norm"], (1, FOX_HEADS))
    b_pad = jnp.pad(W["b_forget"], ((0, 0), (0, F_PAD - FOX_HEADS)))

    def fox_prep_body(q, k, v, f, gqv, gkv, bv):
        qn_scaled = _head_norm(q, gqv, FOX_HEAD_DIM).astype(BF16) * FOX_HEAD_DIM ** -0.5
        return qn_scaled, _head_norm(k, gkv, FOX_HEAD_DIM), v, _log_sigmoid(f + bv)

    tm_e = _tile(t, 256, 16)
    (qn, kn, vb, logf), bufs = _rowwise(
        fox_prep_body, [(z, z_q, FOX_WIDTH), (z, z_k, FOX_WIDTH), (z, z_v, FOX_WIDTH), (z, z_f, F_PAD)],
        [gq, gk, b_pad], [(FOX_WIDTH, BF16), (FOX_WIDTH, BF16), (FOX_WIDTH, BF16), (F_PAD, F32)], [], "fox_prep", tm_e,
        comm=_gather_second(bufs))
    full = {n: (_cols_from_blocks(g) if n in col_sharded else rows_of(g)) for n, g in zip(group_b, bufs)}
    wd2 = full["ffn2_w_down"]
    y_pool = _mm(ypp, full["w_pool_up"], "nn", "pool_up")
    csum = _rowwise(lambda v: (_scan_rows(v, True),), [_whole(logf)], [], [(F_PAD, F32)], [], "fox_cumsum", seq)[0]
    cks = jnp.transpose(csum.reshape(batch, seq, F_PAD)[:, :, :FOX_HEADS], (0, 2, 1)).reshape(
        batch * FOX_HEADS // 2, 2, seq)
    (o_fox, lse), (wgu2,) = _fox_fwd(qn, kn, vb, cks, batch, seq, "fox_fwd",
                                     comm=_gather_first([shard["ffn2_w_gate_up"]]))
    y_fox = _mm(o_fox, full["w_fox_o"], "nn", "fox_o")

    memn = _rms_fwd(mem2d, W["mem_norm"], "mem_rms")
    kv = _mm(memn, full["w_mem_kv"], "nn", "mem_kv")
    gqm = jnp.tile(W["mem_q_norm"], (1, MEM_HEADS))
    gkm = jnp.tile(W["mem_k_norm"], (1, MEM_HEADS))
    qmn = _rowwise(lambda q, g: (_head_norm(q, g, MEM_HEAD_DIM),), [(z, z_qm, MEM_WIDTH)], [gqm],
                   [(MEM_WIDTH, BF16)], [], "memq_prep", tm_e)[0]
    kmn, vmb = _rowwise(lambda k, v, g: (_head_norm(k, g, MEM_HEAD_DIM), v),
                        [(kv, 0, MEM_WIDTH), (kv, MEM_WIDTH, MEM_WIDTH)], [gkm],
                        [(MEM_WIDTH, BF16), (MEM_WIDTH, BF16)], [], "memk_prep", _tile(batch * mlen, 256, 16))
    o_mem = _mem_fwd(qmn, kmn, vmb, batch, seq, mlen, "mem_fwd")
    y_mem = _mm(o_mem, full["w_mem_o"], "nn", "mem_o")

    def gate_fwd_body(gp, gf, gm, yp, yf, ym):
        return ((_sigmoid(gp) * yp + _sigmoid(gf) * yf) + _sigmoid(gm) * ym,)

    tm_g = _tile(t, 128, 16)
    (merged,), (wgu2,) = _rowwise(
        gate_fwd_body, [(z, 0, d), (z, d, d), (z, 2 * d, d), _whole(y_pool), _whole(y_fox), _whole(y_mem)],
        [], [(d, BF16)], [], "gate_fwd", tm_g, comm=_gather_second([wgu2]))
    x2 = _mm(merged, full["w_out"], "nn", "mix_out", res=x1)

    h3 = _rms_fwd(x2, W["ffn2_norm"], "ffn2_rms")
    gu2, a2 = _gate_up_swiglu(h3, wgu2, "ffn2_gu")
    x3 = _mm(a2, wd2, "nn", "ffn2_down", scale=0.5, res=x2, caps=down_caps)
    dy, dy_b, loss_part = _loss_head(x3, tgt2d, "loss")
    loss = lax.psum(loss_part[0, 0], ("x", "y", "c"))

    cx, cy, cc = _position()
    where = jnp.stack([cc, 2 * cx + cy]).astype(jnp.int32)
    G, own, landed = {}, {}, {}

    def row_blocks(g):
        return g.reshape(N_DEV, g.shape[0] // N_DEV, g.shape[1])

    def pair_add(n, partial, from_core):
        own[n], chip_sums = _rs_pair_add(partial, from_core, where, "rs_add_" + n)
        return chip_sums

    p_wd2 = row_blocks(_mm(a2, dy_b, "tn", "ffn2_dwd", scale=0.5, caps=dwd_caps))
    dgu2, (l_wd2,) = _swiglu_bwd_from_out(dy_b, wd2, gu2, "ffn2_dgu", comm=_scatter_first([p_wd2]))
    s_wd2 = pair_add("ffn2_w_down", p_wd2, l_wd2)
    p_wgu2, (landed["ffn2_w_down"],) = _mm(h3, dgu2, "tn", "ffn2_dwgu", caps=gu_caps_tn, out_blocks=True, halves="b",
                                           comm=_scatter_second([s_wd2]))
    dh3, (l_wgu2,) = _mm(dgu2, wgu2, "nt", "ffn2_dh", caps=gu_caps_nt, b_blocks=True, halves="a",
                         comm=_scatter_first([p_wgu2]))
    s_wgu2 = pair_add("ffn2_w_gate_up", p_wgu2, l_wgu2)
    dx2, dx2_b, G["ffn2_norm"] = _rms_bwd(x2, W["ffn2_norm"], dh3, dy, "ffn2_drms")

    dmerged = _mm(dx2_b, full["w_out"], "nt", "mix_out_dx")
    P = {"w_out": row_blocks(_mm(merged, dx2_b, "tn", "mix_out_dw"))}

    def gate_bwd_body(gp, gf, gm, yp, yf, ym, dm):
        outs_dl, outs_dy = [], []
        for gl, yv in ((gp, yp), (gf, yf), (gm, ym)):
            s = _sigmoid(gl)
            outs_dl.append((dm * yv) * (s * (1.0 - s)))
            outs_dy.append(dm * s)
        return (jnp.concatenate(outs_dl, axis=1), *outs_dy)

    dz, dy_pool, dy_fox, dy_mem = _rowwise(
        gate_bwd_body, [(z, 0, d), (z, d, d), (z, 2 * d, d), _whole(y_pool), _whole(y_fox), _whole(y_mem), _whole(dmerged)],
        [], [(gate_w, BF16), (d, BF16), (d, BF16), (d, BF16)], [], "gate_bwd", tm_g, into=(None, z_width, z_gate))

    dypp = _mm(dy_pool, full["w_pool_up"], "nt", "pool_up_dx")
    P["w_pool_up"] = _blocks_from_cols(_mm(ypp, dy_pool, "tn", "pool_up_dw"))

    def pool_bwd_body(dyv, mixed, diff, pw, ps):
        row = lax.broadcasted_iota(jnp.int32, (seq, 1), 0)
        d_scale = jnp.sum(dyv * mixed, axis=0, keepdims=True)
        dmix = (dyv * ps).astype(BF16)
        du, dpw = [], []
        for g in range(POOL_GROUPS):
            sl = slice(g * POOL_GROUP_DIM, (g + 1) * POOL_GROUP_DIM)
            dmg = dmix[:, sl]
            ddiff = lax.dot_general(dmg, pw[g], _DIMS["nt"], preferred_element_type=F32)
            dpw.append(lax.dot_general(diff[:, sl], dmg, _DIMS["tn"], preferred_element_type=F32))
            cnt = jnp.minimum(row + 1, POOL_WINDOWS[g]).astype(F32)
            du.append(_window_sum(ddiff / cnt, g + 1, False) - ddiff)
        return jnp.concatenate(du, axis=1), d_scale, jnp.concatenate(dpw, axis=0)

    dz, G["pool_scale"], d_pool_w = _rowwise(
        pool_bwd_body, [_whole(dypp), _whole(pool_mixed), _whole(pool_diff)], [pool_w_b, W["pool_scale"]],
        [(POOL_WIDTH, BF16)], [((1, POOL_WIDTH), F32), ((POOL_WIDTH, POOL_GROUP_DIM), F32)],
        "pool_bwd", seq, into=(dz, z_width, z_u))
    G["pool_w"] = d_pool_w.reshape(1, POOL_GROUPS, POOL_GROUP_DIM, POOL_GROUP_DIM)

    do_fox = _mm(dy_fox, full["w_fox_o"], "nt", "fox_o_dx")
    P["w_fox_o"] = _blocks_from_cols(_mm(o_fox, dy_fox, "tn", "fox_o_dw"))
    (dkn, dz, dqn, dcks, dcq), (landed["ffn2_w_gate_up"],) = _fox_bwd(
        qn, kn, vb, cks, o_fox, lse, do_fox, dz, z_v, batch, seq, "fox_bwd", comm=_scatter_second([s_wgu2]))
    dcs = jnp.transpose(dcks.reshape(batch, FOX_HEADS, seq), (0, 2, 1)).reshape(t, FOX_HEADS)
    dcs = jnp.pad(dcs, ((0, 0), (0, F_PAD - FOX_HEADS)))
    dcq = jnp.pad(dcq.reshape(t, FOX_HEADS, FOX_HEAD_DIM)[:, :, 0], ((0, 0), (0, F_PAD - FOX_HEADS)))

    def fox_f_bwd_body(dc, dc_rows, f, bv):
        lane = lax.broadcasted_iota(jnp.int32, (1, F_PAD), 1)
        dlogf = _scan_rows(dc + dc_rows, False)
        df = jnp.where(lane < FOX_HEADS, dlogf * _sigmoid(-(f + bv)), 0.0)
        behind = jnp.zeros((seq, z_width - z_f - F_PAD), F32)
        return jnp.concatenate([df, behind], axis=1), jnp.sum(df, axis=0, keepdims=True)

    dz, db_pad = _rowwise(fox_f_bwd_body, [_whole(dcs), _whole(dcq), (z, z_f, F_PAD)], [b_pad],
                          [(z_width - z_f, BF16)], [((1, F_PAD), F32)], "fox_f_bwd", seq, into=(dz, z_width, z_f))
    G["b_forget"] = db_pad[:, :FOX_HEADS]

    def fox_head_bwd(dz_in, col, dyn, gain, name):
        return _rowwise(lambda v, dyv, g: _head_norm_bwd(v, g, dyv, FOX_HEAD_DIM), [(z, col, FOX_WIDTH), _whole(dyn)],
                        [gain], [(FOX_WIDTH, BF16)], [((1, FOX_WIDTH), F32)], name, tm_e, into=(dz_in, z_width, col))

    dz, dgq_t = fox_head_bwd(dz, z_q, dqn, gq, "fox_q_bwd")
    dz, dgk_t = fox_head_bwd(dz, z_k, dkn, gk, "fox_k_bwd")
    G["fox_q_norm"] = jnp.sum(dgq_t.reshape(FOX_HEADS, FOX_HEAD_DIM), axis=0, keepdims=True)
    G["fox_k_norm"] = jnp.sum(dgk_t.reshape(FOX_HEADS, FOX_HEAD_DIM), axis=0, keepdims=True)

    do_mem = _mm(dy_mem, full["w_mem_o"], "nt", "mem_o_dx")
    P["w_mem_o"] = _blocks_from_cols(_mm(o_mem, dy_mem, "tn", "mem_o_dw"))
    dqmn, dkmn, dvm = _mem_bwd(qmn, kmn, vmb, do_mem, batch, seq, mlen, "mem_bwd")
    dz, dgqm_t = _rowwise(lambda q, dq, g: _head_norm_bwd(q, g, dq, MEM_HEAD_DIM),
                          [(z, z_qm, MEM_WIDTH), _whole(dqmn)], [gqm], [(MEM_WIDTH, BF16)],
                          [((1, MEM_WIDTH), F32)], "memq_bwd", tm_e, into=(dz, z_width, z_qm))

    def memk_bwd_body(k, dk, dv, g):
        dkr, dg = _head_norm_bwd(k, g, dk, MEM_HEAD_DIM)
        return jnp.concatenate([dkr, dv], axis=1), dg

    dkv, dgkm_t = _rowwise(memk_bwd_body, [(kv, 0, MEM_WIDTH), _whole(dkmn), _whole(dvm)], [gkm],
                           [(2 * MEM_WIDTH, BF16)], [((1, MEM_WIDTH), F32)], "memk_bwd", _tile(batch * mlen, 256, 16))
    G["mem_q_norm"] = jnp.sum(dgqm_t.reshape(MEM_HEADS, MEM_HEAD_DIM), axis=0, keepdims=True)
    G["mem_k_norm"] = jnp.sum(dgkm_t.reshape(MEM_HEADS, MEM_HEAD_DIM), axis=0, keepdims=True)
    P["w_mem_kv"] = row_blocks(_mm(memn, dkv, "tn", "mem_kv_dw"))
    dmemn = _mm(dkv, full["w_mem_kv"], "nt", "mem_kv_dx")
    _, _, G["mem_norm"] = _rms_bwd(mem2d, W["mem_norm"], dmemn, None, "mem_drms")

    d_w_in_pad, l_small = _mm(h2, dz, "tn", "mix_in_dw", caps=(1024, 1536, 2048), comm=_scatter_first([P[n] for n in mixer_small]))
    s_small = [pair_add(n, P[n], l) for n, l in zip(mixer_small, l_small)]
    p = d_w_in_pad
    p_w_in = _blocks_from_cols(jnp.concatenate(
        [p[:, z_u:z_u + POOL_WIDTH], p[:, z_q:z_q + 3 * FOX_WIDTH], p[:, z_f:z_f + FOX_HEADS],
         p[:, z_qm:z_qm + MEM_WIDTH], p[:, z_gate:z_gate + gate_w]], axis=1))
    dh2, rest = _mm(dz, w_in_pad, "nt", "mix_in_dx", caps=(1024, 2048, 1792),
                    comm=_join(_scatter_second(s_small), _scatter_first([p_w_in])))
    for n, l in zip(mixer_small, rest[:len(mixer_small)]):
        landed[n] = l
    s_w_in = pair_add("w_in", p_w_in, rest[-1])
    dx1, dx1_b, G["mix_norm"] = _rms_bwd(x1, W["mix_norm"], dh2, dx2, "mix_drms")

    dgu1 = _swiglu_bwd_from_out(dx1_b, wd1, gu1, "ffn1_dgu")
    p_wgu1, (landed["w_in"],) = _mm(h1, dgu1, "tn", "ffn1_dwgu", caps=gu_caps_tn, out_blocks=True, halves="b",
                                    comm=_scatter_second([s_w_in]))
    d_wd1, (l_wgu1,) = _mm(a1, dx1_b, "tn", "ffn1_dwd", scale=0.5, caps=dwd_caps, comm=_scatter_first([p_wgu1]))
    p_wd1 = row_blocks(d_wd1)
    s_wgu1 = pair_add("ffn1_w_gate_up", p_wgu1, l_wgu1)
    dh1, (landed["ffn1_w_gate_up"], l_wd1) = _mm(dgu1, wgu1, "nt", "ffn1_dh", caps=gu_caps_nt, b_blocks=True, halves="a",
                                                 comm=_join(_scatter_second([s_wgu1]), _scatter_first([p_wd1])))
    s_wd1 = pair_add("ffn1_w_down", p_wd1, l_wd1)
    (dx0, _, G["ffn1_norm"]), (landed["ffn1_w_down"],) = _rms_bwd(x2d, W["ffn1_norm"], dh1, dx1, "ffn1_drms",
                                                                  comm=_scatter_second([s_wd1]))
    grad_x = dx0.reshape(batch, seq, d)

    out_g, out_d, out_m, out_v = {}, {}, {}, {}
    for n in big:
        res = _rs_finish_adam(own[n], landed[n], W[n][0], M[n][0], V[n][0], "adam_" + n)
        out_g[n], out_d[n], out_m[n], out_v[n] = [r[None] for r in res]

    small = [n for n in names if n not in big]
    g_small = _pack_small([G[n].reshape(W[n].shape) for n in small])
    all_small = _all_gather_vmem(g_small, "ag_small_grads")
    res = _allreduce_adam(all_small, _pack_small([W[n] for n in small]), _pack_small([M[n] for n in small]),
                          _pack_small([V[n] for n in small]), "adam_small")
    like = [W[n] for n in small]
    for dst, buf in zip((out_g, out_d, out_m, out_v), res):
        for n, a in zip(small, _unpack_small(buf, like)):
            dst[n] = a

    return (loss, grad_x, *[out_g[n] for n in names], *[out_d[n] for n in names],
            *[out_m[n] for n in names], *[out_v[n] for n in names])
```
